```python
import math
import jax, jax.numpy as jnp
from jax import lax
import numpy as np

D_MODEL = 1024
BATCH = 16
SEQ = 2048
DEPTH = 1

HG_HEADS = 4
HG_KDIM = 128
HG_VDIM = 128
HG_F = HG_HEADS * HG_KDIM
HG_V = HG_HEADS * HG_VDIM
HG_CHUNK = 64
MLA_HEADS = 8
MLA_NOPE = 64
MLA_ROPE = 32
MLA_VDIM = 64
MLA_QK = MLA_NOPE + MLA_ROPE
MLA_Q_RANK = 256
MLA_KV_RANK = 128
ATTN_BLOCK = 128
ROPE_THETA = 10000.0
N_GROUPS = 8
EXPERTS_PER_GROUP = 8
N_EXPERTS = N_GROUPS * EXPERTS_PER_GROUP
TOP_K = 2
D_EXPERT = 256
MOE_BLOCK = 128
PLE_DIM = 256
EPS = 1e-6
IN_WIDTHS = (HG_F, HG_F, HG_F, HG_V, HG_V, MLA_Q_RANK, MLA_KV_RANK, MLA_ROPE, D_MODEL, D_MODEL)
D_IN = HG_F * 3 + HG_V * 2 + MLA_Q_RANK + MLA_KV_RANK + MLA_ROPE + 2 * D_MODEL

kernel_name = "hybrid_hgrn2_mla_hmoe_block"


def rmsnorm(x, g):
    xf = x.astype(jnp.float32)
    y = xf * lax.rsqrt(jnp.mean(xf * xf, axis=-1, keepdims=True) + EPS)
    return (y * g.astype(jnp.float32)).astype(x.dtype)


def rope(x, positions):
    half = MLA_ROPE // 2
    inv_freq = ROPE_THETA ** (-jnp.arange(half, dtype=jnp.float32) / half)
    ang = positions.astype(jnp.float32)[..., None] * inv_freq
    cos = jnp.cos(ang)[:, :, None, :]
    sin = jnp.sin(ang)[:, :, None, :]
    x1 = x[..., :half].astype(jnp.float32)
    x2 = x[..., half:].astype(jnp.float32)
    out = jnp.concatenate([x1 * cos - x2 * sin, x2 * cos + x1 * sin], axis=-1)
    return out.astype(x.dtype)


def gla_chunkwise(q, k, v, log_f):
    B, H, S, K = q.shape
    V = v.shape[-1]
    n = S // HG_CHUNK

    def to_chunks(a):
        return jnp.moveaxis(a.reshape(B, H, n, HG_CHUNK, a.shape[-1]), 2, 0)

    qc, kc, vc, gc = to_chunks(q), to_chunks(k), to_chunks(v), to_chunks(log_f)
    lower = jnp.tril(jnp.ones((HG_CHUNK, HG_CHUNK), dtype=bool))

    def step(state, inp):
        qb, kb, vb, gb = inp
        g = jnp.cumsum(gb, axis=2)
        o_inter = jnp.einsum('bhtk,bhkv->bhtv', qb * jnp.exp(g), state)
        diff = g[:, :, :, None, :] - g[:, :, None, :, :]
        decay = jnp.exp(jnp.where(lower[:, :, None], diff, -jnp.inf))
        scores = jnp.einsum('bhtk,bhsk,bhtsk->bhts', qb, kb, decay)
        o_intra = jnp.einsum('bhts,bhsv->bhtv', scores, vb)
        g_last = g[:, :, -1]
        k_dec = kb * jnp.exp(g_last[:, :, None, :] - g)
        new_state = jnp.exp(g_last)[..., None] * state + jnp.einsum('bhck,bhcv->bhkv', k_dec, vb)
        return new_state, o_inter + o_intra

    s0 = jnp.zeros((B, H, K, V), jnp.float32)
    _, o = lax.scan(step, s0, (qc, kc, vc, gc))
    return jnp.moveaxis(o, 0, 2).reshape(B, H, S, V)


def block_dense_attention(q, k, v):
    B, H, S, Dq = q.shape
    nb = S // ATTN_BLOCK
    qb = jnp.moveaxis(q.reshape(B, H, nb, ATTN_BLOCK, Dq), 2, 0)
    scale = MLA_QK ** -0.5

    def one_block(qblk):
        s = jnp.einsum('bhqd,bhkd->bhqk', qblk, k).astype(jnp.float32) * scale
        pr = jax.nn.softmax(s, axis=-1).astype(v.dtype)
        return jnp.einsum('bhqk,bhkd->bhqd', pr, v)

    o = lax.map(one_block, qb)
    return jnp.moveaxis(o, 0, 2).reshape(B, H, S, v.shape[-1])


def hierarchical_moe(h, w_rg, b_rg, w_re, b_re, w1, w3, w2):
    T, D = h.shape
    g_logits = (h @ w_rg).astype(jnp.float32) + b_rg.astype(jnp.float32)
    g_prob = jax.nn.softmax(g_logits, axis=-1)
    g_sel = jnp.argmax(g_prob, axis=-1).astype(jnp.int32)
    p_group = jnp.max(g_prob, axis=-1)
    e_logits = ((h @ w_re).astype(jnp.float32) + b_re.astype(jnp.float32)).reshape(T, N_GROUPS, EXPERTS_PER_GROUP)
    e_in_group = jnp.take_along_axis(e_logits, g_sel[:, None, None], axis=1)[:, 0]
    top_v, top_i = lax.top_k(e_in_group, TOP_K)
    w_local = jax.nn.softmax(top_v, axis=-1)
    expert = (g_sel[:, None] * EXPERTS_PER_GROUP + top_i.astype(jnp.int32)).reshape(-1)
    weight = (p_group[:, None] * w_local).reshape(-1)
    token = jnp.repeat(jnp.arange(T, dtype=jnp.int32), TOP_K)
    A = T * TOP_K

    counts = jnp.zeros((N_EXPERTS,), jnp.int32).at[expert].add(1)
    starts = jnp.cumsum(counts) - counts
    pcounts = (counts + MOE_BLOCK - 1) // MOE_BLOCK * MOE_BLOCK
    pends = jnp.cumsum(pcounts)
    pstarts = pends - pcounts
    order = jnp.argsort(expert)
    e_sorted = expert[order]
    rank = jnp.arange(A, dtype=jnp.int32) - starts[e_sorted]
    dest = pstarts[e_sorted] + rank
    n_blocks = (A + N_EXPERTS * (MOE_BLOCK - 1) + MOE_BLOCK - 1) // MOE_BLOCK
    P = n_blocks * MOE_BLOCK
    buf_tok = jnp.full((P,), T, jnp.int32).at[dest].set(token[order])
    buf_w = jnp.zeros((P,), h.dtype).at[dest].set(weight[order].astype(h.dtype))
    block_expert = jnp.minimum(
        jnp.searchsorted(pends, jnp.arange(n_blocks, dtype=jnp.int32) * MOE_BLOCK, side='right'),
        N_EXPERTS - 1)
    h_pad = jnp.concatenate([h, jnp.zeros((1, D), h.dtype)], axis=0)
    xb = h_pad[buf_tok].reshape(n_blocks, MOE_BLOCK, D)

    def expert_block(args):
        xblk, e = args
        return (jax.nn.silu(xblk @ w1[e]) * (xblk @ w3[e])) @ w2[e]

    yb = lax.map(expert_block, (xb, block_expert)).reshape(P, D)
    out = jax.ops.segment_sum(yb * buf_w[:, None], buf_tok, num_segments=T + 1)
    return out[:T]


def setup_inputs(seed: int = 0) -> dict:
    key = jax.random.key(seed)
    ks = jax.random.split(key, 32)
    f32 = jnp.float32

    def w(k, shape, fan_in):
        return jax.random.normal(k, shape, f32) * (fan_in ** -0.5)

    def gain(k, shape):
        return 1.0 + 0.1 * jax.random.normal(k, shape, f32)

    L = DEPTH
    offsets = jax.random.randint(ks[2], (BATCH, 1), 0, SEQ, dtype=jnp.int32)
    positions = jnp.arange(SEQ, dtype=jnp.int32)[None, :] + offsets
    return {
        "x": jax.random.normal(ks[0], (BATCH, SEQ, D_MODEL), f32),
        "p": jax.random.normal(ks[1], (DEPTH, BATCH, SEQ, PLE_DIM), f32),
        "positions": positions,
        "ln_mix": gain(ks[3], (L, D_MODEL)),
        "w_in": w(ks[4], (L, D_MODEL, D_IN), D_MODEL),
        "hg_lb": 0.5 * jax.random.normal(ks[5], (2, DEPTH + 1, HG_F), f32),
        "hg_onorm": gain(ks[6], (L, HG_VDIM)),
        "w_oA": w(ks[7], (L, HG_V, D_MODEL), HG_V),
        "mla_qa_norm": gain(ks[8], (L, MLA_Q_RANK)),
        "mla_kva_norm": gain(ks[9], (L, MLA_KV_RANK)),
        "w_uq": w(ks[10], (L, MLA_Q_RANK, MLA_HEADS * MLA_QK), MLA_Q_RANK),
        "w_ukv": w(ks[11], (L, MLA_KV_RANK, MLA_HEADS * (MLA_NOPE + MLA_VDIM)), MLA_KV_RANK),
        "q_norm": gain(ks[12], (L, MLA_QK)),
        "k_norm": gain(ks[13], (L, MLA_QK)),
        "w_oB": w(ks[14], (L, MLA_HEADS * MLA_VDIM, D_MODEL), MLA_HEADS * MLA_VDIM),
        "w_out": w(ks[15], (L, D_MODEL, D_MODEL), D_MODEL),
        "ln_moe": gain(ks[16], (L, D_MODEL)),
        "w_rg": w(ks[17], (L, D_MODEL, N_GROUPS), D_MODEL),
        "b_rg": 0.01 * jax.random.normal(ks[18], (L, N_GROUPS), f32),
        "w_re": w(ks[19], (L, D_MODEL, N_EXPERTS), D_MODEL),
        "b_re": 0.01 * jax.random.normal(ks[20], (L, N_EXPERTS), f32),
        "w1": w(ks[21], (L, N_EXPERTS, D_MODEL, D_EXPERT), D_MODEL),
        "w3": w(ks[22], (L, N_EXPERTS, D_MODEL, D_EXPERT), D_MODEL),
        "w2": w(ks[23], (L, N_EXPERTS, D_EXPERT, D_MODEL), D_EXPERT),
        "ln_ple": gain(ks[24], (L, D_MODEL)),
        "w_ple_gate": w(ks[25], (L, D_MODEL, D_MODEL), D_MODEL),
        "w_ple_proj": w(ks[26], (L, PLE_DIM, D_MODEL), PLE_DIM),
    }


def reference(x, p, positions, ln_mix, w_in, hg_lb, hg_onorm, w_oA,
              mla_qa_norm, mla_kva_norm, w_uq, w_ukv, q_norm, k_norm, w_oB,
              w_out, ln_moe, w_rg, b_rg, w_re, b_re, w1, w3, w2,
              ln_ple, w_ple_gate, w_ple_proj):
    B, S, D = x.shape
    f32 = jnp.float32
    split_points = [int(v) for v in np.cumsum(IN_WIDTHS)[:-1]]
    lb_all = jnp.cumsum(jax.nn.softmax(hg_lb.astype(f32), axis=1), axis=1)

    def heads(a, hd):
        return a.reshape(B, S, -1, hd).transpose(0, 2, 1, 3)

    for layer in range(DEPTH):
        h = rmsnorm(x, ln_mix[layer])
        z = h @ w_in[layer]
        q_a, f_fw, f_bw, i_a, og_a, c_q, c_kv, k_rope, gate_a, gate_b = jnp.split(z, split_points, axis=-1)

        qh = heads(jax.nn.silu(q_a.astype(f32)), HG_KDIM)
        vh = heads(i_a.astype(f32), HG_VDIM)

        def hgrn_direction(f_logit, lb, reverse):
            f = lb + (1.0 - lb) * jax.nn.sigmoid(f_logit.astype(f32))
            log_f = heads(jnp.log(f), HG_KDIM)
            kh = heads(1.0 - f, HG_KDIM)
            if reverse:
                o = gla_chunkwise(jnp.flip(qh, 2), jnp.flip(kh, 2), jnp.flip(vh, 2), jnp.flip(log_f, 2))
                return jnp.flip(o, 2)
            return gla_chunkwise(qh, kh, vh, log_f)

        o_hg = hgrn_direction(f_fw, lb_all[0, layer], False) + hgrn_direction(f_bw, lb_all[1, layer], True)
        o_hg = o_hg.transpose(0, 2, 1, 3)
        o_hg = rmsnorm(o_hg, hg_onorm[layer]) * jax.nn.silu(og_a.astype(f32).reshape(B, S, HG_HEADS, HG_VDIM))
        y_a = o_hg.reshape(B, S, HG_V).astype(x.dtype) @ w_oA[layer]

        cq = rmsnorm(c_q, mla_qa_norm[layer])
        qm = (cq @ w_uq[layer]).reshape(B, S, MLA_HEADS, MLA_QK)
        ckv = rmsnorm(c_kv, mla_kva_norm[layer])
        kv = (ckv @ w_ukv[layer]).reshape(B, S, MLA_HEADS, MLA_NOPE + MLA_VDIM)
        k_nope, vm = kv[..., :MLA_NOPE], kv[..., MLA_NOPE:]
        km = jnp.concatenate(
            [k_nope, jnp.broadcast_to(k_rope[:, :, None, :], (B, S, MLA_HEADS, MLA_ROPE))], axis=-1)
        qm = rmsnorm(qm, q_norm[layer])
        km = rmsnorm(km, k_norm[layer])
        qm = jnp.concatenate([qm[..., :MLA_NOPE], rope(qm[..., MLA_NOPE:], positions)], axis=-1)
        km = jnp.concatenate([km[..., :MLA_NOPE], rope(km[..., MLA_NOPE:], positions)], axis=-1)
        o_mla = block_dense_attention(qm.transpose(0, 2, 1, 3), km.transpose(0, 2, 1, 3), vm.transpose(0, 2, 1, 3))
        y_b = o_mla.transpose(0, 2, 1, 3).reshape(B, S, MLA_HEADS * MLA_VDIM) @ w_oB[layer]

        merged = jax.nn.sigmoid(gate_a) * y_a + jax.nn.sigmoid(gate_b) * y_b
        x = x + merged @ w_out[layer]

        hm = rmsnorm(x, ln_moe[layer]).reshape(B * S, D)
        x = x + hierarchical_moe(hm, w_rg[layer], b_rg[layer], w_re[layer], b_re[layer],
                                 w1[layer], w3[layer], w2[layer]).reshape(B, S, D)

        ple_gate = jax.nn.sigmoid(rmsnorm(x, ln_ple[layer]) @ w_ple_gate[layer])
        x = x + (p[layer] @ w_ple_proj[layer]) * ple_gate
    return x
```

```python
import functools

import numpy as np
import jax
import jax.numpy as jnp
from jax import lax
from jax.experimental import pallas as pl
from jax.experimental.pallas import tpu as pltpu

F32 = jnp.float32
BF16 = jnp.bfloat16

HG_HEADS = 4
HG_KDIM = 128
HG_VDIM = 128
HG_F = HG_HEADS * HG_KDIM
HG_V = HG_HEADS * HG_VDIM
MLA_HEADS = 8
MLA_NOPE = 64
MLA_ROPE = 32
MLA_VDIM = 64
MLA_QK = MLA_NOPE + MLA_ROPE
MLA_Q_RANK = 256
MLA_KV_RANK = 128
ROPE_THETA = 10000.0
N_GROUPS = 8
EXPERTS_PER_GROUP = 8
N_EXPERTS = N_GROUPS * EXPERTS_PER_GROUP
D_EXPERT = 256
EPS = 1e-6

LANES = 128
VMEM_LIMIT_BYTES = 56 * 1024 * 1024

ROW_TILE = 256
HG_CHUNK = 64
HG_HEADS_PER_STEP = 2
ATTN_Q_TILE = 256
MOE_ROWS = 256
HG_SAFE_LOG_DECAY = -60.0


def _cparams(sem):
    return pltpu.CompilerParams(dimension_semantics=sem, vmem_limit_bytes=VMEM_LIMIT_BYTES)


def _const_spec(shape):
    nd = len(shape)
    return pl.BlockSpec(shape, lambda *_: (0,) * nd)


def _sigmoid(x):
    return 1.0 / (1.0 + jnp.exp(-x))


def _silu(x):
    return x * _sigmoid(x)


def _rms_scale(x, n):
    return lax.rsqrt(jnp.sum(x * x, axis=-1, keepdims=True) * (1.0 / n) + EPS)


def _dot(a, b):
    return jnp.dot(a, b, preferred_element_type=F32)


def _dot_nt(a, b):
    return lax.dot_general(a, b, (((1,), (1,)), ((), ())), preferred_element_type=F32)


def _dot_tn(a, b):
    return lax.dot_general(a, b, (((0,), (0,)), ((), ())), preferred_element_type=F32)


def _inproj_kernel(x_ref, g_ref, whg_ref, wmla_ref, wgate_ref, lb_ref, nq_ref, nkv_ref,
                   qs_ref, lf_ref, kk_ref, v_ref, og_ref, mla_ref, gates_ref):
    x = x_ref[...]
    d = x.shape[-1]
    h = (x * _rms_scale(x, d) * g_ref[...]).astype(BF16)

    qs_ref[...] = _silu(_dot(h, whg_ref[:, 0:HG_F])).astype(BF16)
    for direction in range(2):
        cols = slice(HG_F * (1 + direction), HG_F * (2 + direction))
        out_cols = slice(HG_F * direction, HG_F * (direction + 1))
        lb = lb_ref[direction:direction + 1, :]
        f = lb + (1.0 - lb) * _sigmoid(_dot(h, whg_ref[:, cols]))
        lf_ref[:, out_cols] = jnp.log(f)
        kk_ref[:, out_cols] = (1.0 - f).astype(BF16)
    v_ref[...] = _dot(h, whg_ref[:, 3 * HG_F:3 * HG_F + HG_V]).astype(BF16)
    og_ref[...] = _silu(_dot(h, whg_ref[:, 3 * HG_F + HG_V:3 * HG_F + 2 * HG_V])).astype(BF16)

    zm = _dot(h, wmla_ref[...])
    cq = zm[:, 0:MLA_Q_RANK]
    mla_ref[:, 0:MLA_Q_RANK] = (cq * _rms_scale(cq, MLA_Q_RANK) * nq_ref[...]).astype(BF16)
    ckv = zm[:, MLA_Q_RANK:MLA_Q_RANK + MLA_KV_RANK]
    mla_ref[:, MLA_Q_RANK:MLA_Q_RANK + MLA_KV_RANK] = (
        ckv * _rms_scale(ckv, MLA_KV_RANK) * nkv_ref[...]).astype(BF16)
    mla_ref[:, MLA_Q_RANK + MLA_KV_RANK:] = zm[:, MLA_Q_RANK + MLA_KV_RANK:].astype(BF16)

    for half in range(2):
        cols = slice(d * half, d * (half + 1))
        gates_ref[:, cols] = _sigmoid(_dot(h, wgate_ref[:, cols])).astype(BF16)


def _inproj(x2, ln_mix, whg, wmla, wgate, lb, nq, nkv):
    t, d = x2.shape
    tm = ROW_TILE
    mla_w = wmla.shape[1]
    row = lambda w: pl.BlockSpec((tm, w), lambda i: (i, 0))
    out_shape = (
        jax.ShapeDtypeStruct((t, HG_F), BF16),
        jax.ShapeDtypeStruct((t, 2 * HG_F), F32),
        jax.ShapeDtypeStruct((t, 2 * HG_F), BF16),
        jax.ShapeDtypeStruct((t, HG_V), BF16),
        jax.ShapeDtypeStruct((t, HG_V), BF16),
        jax.ShapeDtypeStruct((t, mla_w), BF16),
        jax.ShapeDtypeStruct((t, 2 * d), BF16),
    )
    return pl.pallas_call(
        _inproj_kernel,
        grid=(t // tm,),
        in_specs=[row(d), _const_spec((1, d)), _const_spec(whg.shape), _const_spec(wmla.shape),
                  _const_spec(wgate.shape), _const_spec(lb.shape), _const_spec(nq.shape),
                  _const_spec(nkv.shape)],
        out_specs=(row(HG_F), row(2 * HG_F), row(2 * HG_F), row(HG_V), row(HG_V), row(mla_w),
                   row(2 * d)),
        out_shape=out_shape,
        compiler_params=_cparams(("parallel",)),
        name="inproj",
    )(x2, ln_mix, whg, wmla, wgate, lb, nq, nkv)


def _hgrn_kernel(qs_ref, lff_ref, lfb_ref, kf_ref, kb_ref, v_ref, og_ref, onorm_ref, out_ref,
                 state_ref, ofw_ref, obw_ref, g_scr, k_scr, v_scr):
    c = HG_CHUNK
    s_len = qs_ref.shape[1]
    n_chunks = s_len // c
    hb = HG_HEADS_PER_STEP
    kd = HG_KDIM

    row = lax.broadcasted_iota(jnp.int32, (c, c), 0)
    col = lax.broadcasted_iota(jnp.int32, (c, c), 1)
    masks = (row >= col, row <= col)
    tris = tuple(jnp.where(m, 1.0, 0.0).astype(BF16) for m in masks)
    rowc = lax.broadcasted_iota(jnp.int32, (c, kd), 0)

    state_ref[...] = jnp.zeros_like(state_ref)

    def load(j):
        chains = []
        for hh in range(hb):
            lanes = slice(hh * kd, (hh + 1) * kd)
            for direction in range(2):
                chunk = j if direction == 0 else n_chunks - 1 - j
                rows = pl.ds(pl.multiple_of(chunk * c, c), c)
                lf = (lff_ref, lfb_ref)[direction][0, rows, lanes]
                hi = lf.astype(BF16)
                r1 = lf - hi.astype(F32)
                mid = r1.astype(BF16)
                lo = (r1 - mid.astype(F32)).astype(BF16)
                tri = tris[direction]
                g = _dot(tri, hi) + _dot(tri, mid) + _dot(tri, lo)
                total = g[c - 1:c, :] if direction == 0 else g[0:1, :]
                chains.append(dict(
                    idx=hh * 2 + direction, direction=direction, rows=rows, lanes=lanes, g=g,
                    total=total,
                    q=qs_ref[0, rows, lanes].astype(F32),
                    k=(kf_ref, kb_ref)[direction][0, rows, lanes].astype(F32),
                    v=v_ref[0, rows, lanes]))
        return chains

    def finish(ch, o, kdec_t_v_scaled):
        (ofw_ref, obw_ref)[ch["direction"]][ch["rows"], ch["lanes"]] = o
        state_ref[ch["idx"]] = kdec_t_v_scaled

    def fast(ch):
        g = ch["g"]
        qt = (ch["q"] * jnp.exp(g)).astype(BF16)
        kt = (ch["k"] * jnp.exp(-g)).astype(BF16)
        st = state_ref[ch["idx"]]
        sc = jnp.where(masks[ch["direction"]], _dot_nt(qt, kt), 0.0)
        o = _dot_nt(qt, st.astype(BF16)) + _dot(sc.astype(BF16), ch["v"])
        finish(ch, o, (st + _dot_tn(ch["v"], kt)) * jnp.exp(ch["total"]))

    def robust(ch):
        g = ch["g"]
        direction = ch["direction"]
        st = state_ref[ch["idx"]]
        o0 = _dot_nt((ch["q"] * jnp.exp(g)).astype(BF16), st.astype(BF16))
        slot = ch["idx"]
        g_scr[slot] = g
        k_scr[slot] = ch["k"]
        v_scr[slot] = ch["v"].astype(F32)
        q = ch["q"]

        def body(s, acc):
            g_s = g_scr[slot, pl.ds(s, 1), :]
            seen = (rowc >= s) if direction == 0 else (rowc <= s)
            decay = jnp.where(seen, jnp.exp(jnp.minimum(g - g_s, 0.0)), 0.0)
            a = jnp.sum(q * decay * k_scr[slot, pl.ds(s, 1), :], axis=-1, keepdims=True)
            return acc + a * v_scr[slot, pl.ds(s, 1), :]

        o = lax.fori_loop(0, c, body, o0)
        kdec = (ch["k"] * jnp.exp(ch["total"] - g)).astype(BF16)
        finish(ch, o, st * jnp.exp(ch["total"]) + _dot_tn(ch["v"], kdec))

    def step(j, carry):
        chains = load(j)
        lowest = chains[0]["total"]
        for ch in chains[1:]:
            lowest = jnp.minimum(lowest, ch["total"])
        safe = jnp.min(lowest) >= HG_SAFE_LOG_DECAY

        @pl.when(safe)
        def _():
            for ch in chains:
                fast(ch)

        @pl.when(jnp.logical_not(safe))
        def _():
            for ch in chains:
                robust(ch)

        return carry

    lax.fori_loop(0, n_chunks, step, 0)

    blk = 256 if s_len % 256 == 0 else c

    def epilogue(i, carry):
        rows = pl.ds(pl.multiple_of(i * blk, blk), blk)
        for hh in range(hb):
            lanes = slice(hh * kd, (hh + 1) * kd)
            o = ofw_ref[rows, lanes] + obw_ref[rows, lanes]
            y = o * _rms_scale(o, HG_VDIM) * onorm_ref[...]
            out_ref[0, rows, lanes] = (y * og_ref[0, rows, lanes].astype(F32)).astype(BF16)
        return carry

    lax.fori_loop(0, s_len // blk, epilogue, 0)


def _hgrn(qs, lf, kk, v, og, onorm):
    b, s, _ = qs.shape
    hb = HG_HEADS_PER_STEP
    w = hb * HG_KDIM
    n_hsteps = HG_HEADS // hb
    fwd = pl.BlockSpec((1, s, w), lambda bi, hi: (bi, 0, hi))
    bwd = pl.BlockSpec((1, s, w), lambda bi, hi: (bi, 0, n_hsteps + hi))
    return pl.pallas_call(
        _hgrn_kernel,
        grid=(b, n_hsteps),
        in_specs=[fwd, fwd, bwd, fwd, bwd, fwd, fwd, _const_spec(onorm.shape)],
        out_specs=fwd,
        out_shape=jax.ShapeDtypeStruct((b, s, HG_V), BF16),
        scratch_shapes=[
            pltpu.VMEM((2 * hb, HG_VDIM, HG_KDIM), F32),
            pltpu.VMEM((s, w), F32),
            pltpu.VMEM((s, w), F32),
            pltpu.VMEM((2 * hb, HG_CHUNK, HG_KDIM), F32),
            pltpu.VMEM((2 * hb, HG_CHUNK, HG_KDIM), F32),
            pltpu.VMEM((2 * hb, HG_CHUNK, HG_VDIM), F32),
        ],
        compiler_params=_cparams(("parallel", "parallel")),
        name="hgrn",
    )(qs, lf, lf, kk, kk, v, og, onorm)


def _mlaprep_kernel(mla_ref, pos_ref, wq_ref, wk_ref, wv_ref, gq_ref, gk_ref, rope_ref,
                    q_ref, k_ref, v_ref):
    cq = mla_ref[:, 0:MLA_Q_RANK]
    ckr = mla_ref[:, MLA_Q_RANK:]
    ang = pos_ref[...] * rope_ref[0:1, :]
    cos = jnp.cos(ang)
    sin = jnp.sin(ang)
    sin_lo = sin * rope_ref[1:2, :]
    sin_hi = sin * rope_ref[2:3, :]
    half = MLA_ROPE // 2

    def norm_rope(x, gain, scale):
        y = x * (_rms_scale(x, MLA_QK) * scale) * gain
        return y * cos + pltpu.roll(y, half, 1) * sin_lo + pltpu.roll(y, LANES - half, 1) * sin_hi

    qm = _dot(cq, wq_ref[...])
    km = _dot(ckr, wk_ref[...])
    for hd in range(MLA_HEADS):
        lanes = slice(hd * LANES, (hd + 1) * LANES)
        q_ref[:, lanes] = norm_rope(qm[:, lanes], gq_ref[...], MLA_QK ** -0.5).astype(BF16)
        k_ref[:, lanes] = norm_rope(km[:, lanes], gk_ref[...], 1.0).astype(BF16)
    v_ref[...] = _dot(ckr, wv_ref[...]).astype(BF16)


def _mlaprep(mla_in, posf, wq, wk, wv, gq, gk, rope_tab):
    t = mla_in.shape[0]
    tm = ROW_TILE
    row = lambda w: pl.BlockSpec((tm, w), lambda i: (i, 0))
    hw = MLA_HEADS * LANES
    vw = MLA_HEADS * MLA_VDIM
    return pl.pallas_call(
        _mlaprep_kernel,
        grid=(t // tm,),
        in_specs=[row(mla_in.shape[1]), row(1), _const_spec(wq.shape), _const_spec(wk.shape),
                  _const_spec(wv.shape), _const_spec(gq.shape), _const_spec(gk.shape),
                  _const_spec(rope_tab.shape)],
        out_specs=(row(hw), row(hw), row(vw)),
        out_shape=(jax.ShapeDtypeStruct((t, hw), BF16), jax.ShapeDtypeStruct((t, hw), BF16),
                   jax.ShapeDtypeStruct((t, vw), BF16)),
        compiler_params=_cparams(("parallel",)),
        name="mlaprep",
    )(mla_in, posf, wq, wk, wv, gq, gk, rope_tab)


def _attn_kernel(q_ref, k_ref, v_ref, o_ref):
    for j in range(2):
        lanes = slice(j * LANES, (j + 1) * LANES)
        s = _dot_nt(q_ref[0, :, lanes], k_ref[0, :, lanes])
        p = jnp.exp(s - jnp.max(s, axis=-1, keepdims=True))
        l = jnp.sum(p, axis=-1, keepdims=True)
        vl = slice(j * MLA_VDIM, (j + 1) * MLA_VDIM)
        o = _dot(p.astype(BF16), v_ref[0, :, vl])
        o_ref[0, :, vl] = (o / l).astype(BF16)


def _attn(q, k, v):
    b, s, _ = q.shape
    tq = min(ATTN_Q_TILE, s)
    return pl.pallas_call(
        _attn_kernel,
        grid=(b, MLA_HEADS // 2, s // tq),
        in_specs=[pl.BlockSpec((1, tq, 2 * LANES), lambda bi, hp, i: (bi, i, hp)),
                  pl.BlockSpec((1, s, 2 * LANES), lambda bi, hp, i: (bi, 0, hp)),
                  pl.BlockSpec((1, s, 2 * MLA_VDIM), lambda bi, hp, i: (bi, 0, hp))],
        out_specs=pl.BlockSpec((1, tq, 2 * MLA_VDIM), lambda bi, hp, i: (bi, i, hp)),
        out_shape=jax.ShapeDtypeStruct((b, s, MLA_HEADS * MLA_VDIM), BF16),
        compiler_params=_cparams(("parallel", "parallel", "arbitrary")),
        name="attn",
    )(q, k, v)


def _merge_kernel(x_ref, ya_ref, yb_ref, gates_ref, woa_ref, wob_ref, wout_ref, lnm_ref, wr_ref,
                  br_ref, x1_ref, hm_ref, route_ref):
    d = x_ref.shape[-1]
    y_a = _dot(ya_ref[...], woa_ref[...])
    y_b = _dot(yb_ref[...], wob_ref[...])
    merged = gates_ref[:, 0:d].astype(F32) * y_a + gates_ref[:, d:2 * d].astype(F32) * y_b
    x1 = x_ref[...] + _dot(merged.astype(BF16), wout_ref[...])
    x1_ref[...] = x1
    hm = x1 * _rms_scale(x1, d) * lnm_ref[...]
    hm_ref[...] = hm

    logits = jnp.dot(hm, wr_ref[...], preferred_element_type=F32,
                     precision=lax.Precision.HIGHEST) + br_ref[...]
    lane = lax.broadcasted_iota(jnp.int32, logits.shape, 1)
    neg = -jnp.inf
    big = jnp.int32(2 ** 30)
    is_group = lane < N_GROUPS
    gl = jnp.where(is_group, logits, neg)
    gmax = jnp.max(gl, axis=-1, keepdims=True)
    p_group = 1.0 / jnp.sum(jnp.where(is_group, jnp.exp(gl - gmax), 0.0), axis=-1, keepdims=True)
    g_sel = jnp.min(jnp.where(gl == gmax, lane, big), axis=-1, keepdims=True)
    lo = N_GROUPS + g_sel * EXPERTS_PER_GROUP
    in_group = (lane >= lo) & (lane < lo + EXPERTS_PER_GROUP)
    el = jnp.where(in_group, logits, neg)
    v1 = jnp.max(el, axis=-1, keepdims=True)
    i1 = jnp.min(jnp.where(el == v1, lane, big), axis=-1, keepdims=True)
    el2 = jnp.where(lane == i1, neg, el)
    v2 = jnp.max(el2, axis=-1, keepdims=True)
    i2 = jnp.min(jnp.where(el2 == v2, lane, big), axis=-1, keepdims=True)
    e21 = jnp.exp(v2 - v1)
    w1 = p_group / (1.0 + e21)
    w2 = w1 * e21
    route = jnp.where(lane == 0, (i1 - N_GROUPS).astype(F32),
                      jnp.where(lane == 1, (i2 - N_GROUPS).astype(F32),
                                jnp.where(lane == 2, w1, jnp.where(lane == 3, w2, 0.0))))
    route_ref[...] = route


def _merge(x2, ya, yb, gates, woa, wob, wout, lnm, wr, br):
    t, d = x2.shape
    tm = ROW_TILE
    row = lambda w: pl.BlockSpec((tm, w), lambda i: (i, 0))
    return pl.pallas_call(
        _merge_kernel,
        grid=(t // tm,),
        in_specs=[row(d), row(ya.shape[1]), row(yb.shape[1]), row(2 * d), _const_spec(woa.shape),
                  _const_spec(wob.shape), _const_spec(wout.shape), _const_spec(lnm.shape),
                  _const_spec(wr.shape), _const_spec(br.shape)],
        out_specs=(row(d), row(d), row(LANES)),
        out_shape=(jax.ShapeDtypeStruct((t, d), F32), jax.ShapeDtypeStruct((t, d), F32),
                   jax.ShapeDtypeStruct((t, LANES), F32)),
        compiler_params=_cparams(("parallel",)),
        name="merge",
    )(x2, ya, yb, gates, woa, wob, wout, lnm, wr, br)


def _dispatch_kernel(dest_ref, hm_ref, xb_in_ref, xb_ref, sem):
    del xb_in_ref
    tm = hm_ref.shape[0]

    def row_copy(r, slot):
        return pltpu.make_async_copy(hm_ref.at[pl.ds(r, 1)],
                                     xb_ref.at[pl.ds(dest_ref[2 * r + slot], 1)], sem)

    def issue(r, carry):
        row_copy(r, 0).start()
        row_copy(r, 1).start()
        return carry

    lax.fori_loop(0, tm, issue, 0)

    def drain(r, carry):
        row_copy(r, 0).wait()
        row_copy(r, 1).wait()
        return carry

    lax.fori_loop(0, tm, drain, 0)


def _dispatch(dest_flat, hm, xb_init):
    t, d = hm.shape
    tm = ROW_TILE
    return pl.pallas_call(
        _dispatch_kernel,
        grid=(t // tm,),
        in_specs=[pl.BlockSpec((2 * tm,), lambda i: (i,), memory_space=pltpu.SMEM),
                  pl.BlockSpec((tm, d), lambda i: (i, 0)),
                  pl.BlockSpec(memory_space=pl.ANY)],
        out_specs=pl.BlockSpec(memory_space=pl.ANY),
        out_shape=jax.ShapeDtypeStruct(xb_init.shape, xb_init.dtype),
        scratch_shapes=[pltpu.SemaphoreType.DMA(())],
        input_output_aliases={2: 0},
        compiler_params=_cparams(("arbitrary",)),
        name="dispatch",
    )(dest_flat, hm, xb_init)


def _experts_kernel(be_ref, nu_ref, xb_ref, w1_ref, w3_ref, w2_ref, yb_ref, w1b, w3b, w2b):
    i = pl.program_id(0)
    used = i < nu_ref[0]
    new_expert = jnp.logical_or(i == 0, be_ref[i] != be_ref[jnp.maximum(i - 1, 0)])

    @pl.when(jnp.logical_and(used, new_expert))
    def _():
        w1b[...] = w1_ref[0].astype(BF16)
        w3b[...] = w3_ref[0].astype(BF16)
        w2b[...] = w2_ref[0].astype(BF16)

    @pl.when(used)
    def _():
        x = xb_ref[...].astype(BF16)
        hmid = (_silu(_dot(x, w1b[...])) * _dot(x, w3b[...])).astype(BF16)
        yb_ref[...] = _dot(hmid, w2b[...])

    @pl.when(jnp.logical_not(used))
    def _():
        yb_ref[...] = jnp.zeros_like(yb_ref)


def _experts(block_expert, n_used, xb, w1, w3, w2):
    nblk = block_expert.shape[0]
    d = xb.shape[1]
    de = w1.shape[2]
    rows_in = lambda i, be, nu: (jnp.minimum(i, nu[0] - 1), 0)
    rows_out = lambda i, be, nu: (i, 0)
    wsel = lambda i, be, nu: (be[i], 0, 0)
    return pl.pallas_call(
        _experts_kernel,
        grid_spec=pltpu.PrefetchScalarGridSpec(
            num_scalar_prefetch=2,
            grid=(nblk,),
            in_specs=[pl.BlockSpec((MOE_ROWS, d), rows_in),
                      pl.BlockSpec((1, d, de), wsel), pl.BlockSpec((1, d, de), wsel),
                      pl.BlockSpec((1, de, d), wsel)],
            out_specs=pl.BlockSpec((MOE_ROWS, d), rows_out),
            scratch_shapes=[pltpu.VMEM((d, de), BF16), pltpu.VMEM((d, de), BF16),
                            pltpu.VMEM((de, d), BF16)]),
        out_shape=jax.ShapeDtypeStruct((nblk * MOE_ROWS, d), F32),
        compiler_params=_cparams(("arbitrary",)),
        name="experts",
    )(block_expert, n_used, xb, w1, w3, w2)


def _ple_kernel(dest_ref, x1_ref, route_ref, p_ref, ln_ref, wg_ref, wp_ref, yb_ref, out_ref,
                ybuf, sem):
    tm, d = x1_ref.shape

    def row_copy(r, slot):
        return pltpu.make_async_copy(yb_ref.at[pl.ds(dest_ref[2 * r + slot], 1)],
                                     ybuf.at[slot, pl.ds(r, 1)], sem)

    def issue(r, carry):
        row_copy(r, 0).start()
        row_copy(r, 1).start()
        return carry

    lax.fori_loop(0, tm, issue, 0)

    pp = _dot(p_ref[...].astype(BF16), wp_ref[...])

    def drain(r, carry):
        row_copy(r, 0).wait()
        row_copy(r, 1).wait()
        return carry

    lax.fori_loop(0, tm, drain, 0)

    x2 = x1_ref[...] + route_ref[:, 2:3] * ybuf[0] + route_ref[:, 3:4] * ybuf[1]
    h = (x2 * _rms_scale(x2, d) * ln_ref[...]).astype(BF16)
    out_ref[...] = x2 + pp * _sigmoid(_dot(h, wg_ref[...]))


def _ple(dest_flat, x1, route, p2, ln, wg, wp, yb):
    t, d = x1.shape
    tm = ROW_TILE
    row = lambda w: pl.BlockSpec((tm, w), lambda i: (i, 0))
    return pl.pallas_call(
        _ple_kernel,
        grid=(t // tm,),
        in_specs=[pl.BlockSpec((2 * tm,), lambda i: (i,), memory_space=pltpu.SMEM),
                  row(d), row(LANES), row(p2.shape[1]), _const_spec(ln.shape),
                  _const_spec(wg.shape), _const_spec(wp.shape),
                  pl.BlockSpec(memory_space=pl.ANY)],
        out_specs=row(d),
        out_shape=jax.ShapeDtypeStruct((t, d), F32),
        scratch_shapes=[pltpu.VMEM((2, tm, d), F32), pltpu.SemaphoreType.DMA(())],
        compiler_params=_cparams(("arbitrary",)),
        name="ple",
    )(dest_flat, x1, route, p2, ln, wg, wp, yb)


def _head_pad(w, width):
    r = w.shape[0]
    w = w.reshape(r, MLA_HEADS, width)
    return jnp.pad(w, ((0, 0), (0, 0), (0, LANES - width))).reshape(r, MLA_HEADS * LANES)


def _rope_table():
    half = MLA_ROPE // 2
    inv_freq = ROPE_THETA ** (-np.arange(half, dtype=np.float32) / half)
    tab = np.zeros((8, LANES), np.float32)
    tab[0, MLA_NOPE:MLA_NOPE + half] = inv_freq
    tab[0, MLA_NOPE + half:MLA_QK] = inv_freq
    tab[1, MLA_NOPE + half:MLA_QK] = 1.0
    tab[2, MLA_NOPE:MLA_NOPE + half] = -1.0
    return jnp.asarray(tab)


def _route_tables(ids):
    a = ids.shape[0] * 2
    e = ids.reshape(a)
    onehot = (e[:, None] == jnp.arange(N_EXPERTS, dtype=jnp.int32)[None, :]).astype(jnp.int32)
    csum = jnp.cumsum(onehot, axis=0)
    counts = csum[-1]
    rank = jnp.sum(csum * onehot, axis=1) - 1
    pcounts = (counts + MOE_ROWS - 1) // MOE_ROWS * MOE_ROWS
    pends = jnp.cumsum(pcounts)
    pstarts = pends - pcounts
    dest = jnp.sum(onehot * pstarts[None, :], axis=1) + rank
    nblk = (a + N_EXPERTS * (MOE_ROWS - 1)) // MOE_ROWS
    n_used = (pends[-1] // MOE_ROWS).astype(jnp.int32)
    blk_start = jnp.arange(nblk, dtype=jnp.int32) * MOE_ROWS
    block_expert = jnp.searchsorted(pends, jnp.minimum(blk_start, pends[-1] - 1), side='right')
    block_expert = jnp.minimum(block_expert, N_EXPERTS - 1).astype(jnp.int32)
    return dest.astype(jnp.int32), block_expert, n_used.reshape(1), nblk


def kernel(x, p, positions, ln_mix, w_in, hg_lb, hg_onorm, w_oA, mla_qa_norm, mla_kva_norm, w_uq,
           w_ukv, q_norm, k_norm, w_oB, w_out, ln_moe, w_rg, b_rg, w_re, b_re, w1, w3, w2, ln_ple,
           w_ple_gate, w_ple_proj):
    b, s, d = x.shape
    t = b * s
    depth = w_in.shape[0]
    lb_all = jnp.cumsum(jax.nn.softmax(hg_lb.astype(F32), axis=1), axis=1)
    posf = positions.astype(F32).reshape(t, 1)
    rope_tab = _rope_table()
    xc = x.reshape(t, d)

    for layer in range(depth):
        wi = w_in[layer]
        n_hg = 3 * HG_F + 2 * HG_V
        n_mla = MLA_Q_RANK + MLA_KV_RANK + MLA_ROPE
        mla_w = -(-n_mla // LANES) * LANES
        whg = wi[:, :n_hg].astype(BF16)
        wmla = jnp.pad(wi[:, n_hg:n_hg + n_mla], ((0, 0), (0, mla_w - n_mla))).astype(BF16)
        wgate = wi[:, n_hg + n_mla:].astype(BF16)
        lb = lb_all[:, layer, :]

        qs, lf, kk, v, og, mla_in, gates = _inproj(
            xc, ln_mix[layer][None, :], whg, wmla, wgate, lb, mla_qa_norm[layer][None, :],
            mla_kva_norm[layer][None, :])

        r3 = lambda a: a.reshape(b, s, a.shape[-1])
        ya = _hgrn(r3(qs), r3(lf), r3(kk), r3(v), r3(og), hg_onorm[layer][None, :]).reshape(t, HG_V)

        kv_w = MLA_NOPE + MLA_VDIM
        wkv = w_ukv[layer].reshape(MLA_KV_RANK, MLA_HEADS, kv_w)
        ckr_w = mla_w - MLA_Q_RANK
        wq = _head_pad(w_uq[layer], MLA_QK).astype(BF16)
        wk_nope = _head_pad(wkv[:, :, :MLA_NOPE].reshape(MLA_KV_RANK, MLA_HEADS * MLA_NOPE), MLA_NOPE)
        place = np.zeros((ckr_w - MLA_KV_RANK, MLA_HEADS, LANES), np.float32)
        for j in range(MLA_ROPE):
            place[j, :, MLA_NOPE + j] = 1.0
        wk = jnp.concatenate([wk_nope, jnp.asarray(place.reshape(ckr_w - MLA_KV_RANK, -1))],
                             axis=0).astype(BF16)
        wv = jnp.pad(wkv[:, :, MLA_NOPE:].reshape(MLA_KV_RANK, MLA_HEADS * MLA_VDIM),
                     ((0, ckr_w - MLA_KV_RANK), (0, 0))).astype(BF16)
        gq = jnp.pad(q_norm[layer], (0, LANES - MLA_QK))[None, :]
        gk = jnp.pad(k_norm[layer], (0, LANES - MLA_QK))[None, :]
        qh, kh, vh = _mlaprep(mla_in, posf, wq, wk, wv, gq, gk, rope_tab)
        yb = _attn(r3(qh), r3(kh), r3(vh)).reshape(t, MLA_HEADS * MLA_VDIM)

        wr = jnp.pad(jnp.concatenate([w_rg[layer], w_re[layer]], axis=1),
                     ((0, 0), (0, LANES - N_GROUPS - N_EXPERTS)))
        br = jnp.pad(jnp.concatenate([b_rg[layer], b_re[layer]]),
                     (0, LANES - N_GROUPS - N_EXPERTS))[None, :]
        x1, hm, route = _merge(xc, ya, yb, gates, w_oA[layer].astype(BF16), w_oB[layer].astype(BF16),
                               w_out[layer].astype(BF16), ln_moe[layer][None, :], wr, br)

        ids = route[:, 0:2].astype(jnp.int32)
        dest, block_expert, n_used, nblk = _route_tables(ids)
        xb = _dispatch(dest, hm, jnp.zeros((nblk * MOE_ROWS, d), F32))
        yexp = _experts(block_expert, n_used, xb, w1[layer], w3[layer], w2[layer])

        xc = _ple(dest, x1, route, p[layer].reshape(t, -1), ln_ple[layer][None, :],
                  w_ple_gate[layer].astype(BF16), w_ple_proj[layer].astype(BF16), yexp)
    return xc.reshape(b, s, d)
```

```python
import functools

import numpy as np
import jax
import jax.numpy as jnp
from jax import lax
from jax.experimental import pallas as pl
from jax.experimental.pallas import tpu as pltpu

F32 = jnp.float32
BF16 = jnp.bfloat16

HG_HEADS = 4
HG_KDIM = 128
HG_VDIM = 128
HG_F = HG_HEADS * HG_KDIM
HG_V = HG_HEADS * HG_VDIM
MLA_HEADS = 8
MLA_NOPE = 64
MLA_ROPE = 32
MLA_VDIM = 64
MLA_QK = MLA_NOPE + MLA_ROPE
MLA_Q_RANK = 256
MLA_KV_RANK = 128
ROPE_THETA = 10000.0
N_GROUPS = 8
EXPERTS_PER_GROUP = 8
N_EXPERTS = N_GROUPS * EXPERTS_PER_GROUP
D_EXPERT = 256
EPS = 1e-6

LANES = 128
VMEM_LIMIT_BYTES = 56 * 1024 * 1024

ROW_TILE = 256
HG_CHUNK = 64
HG_HEADS_PER_STEP = 2
HG_CHUNKS_PER_MATMUL = 4
ATTN_Q_TILE = 256
MOE_ROWS = 256
HG_SAFE_LOG_DECAY = -60.0


def _cparams(sem):
    return pltpu.CompilerParams(dimension_semantics=sem, vmem_limit_bytes=VMEM_LIMIT_BYTES)


def _const_spec(shape):
    nd = len(shape)
    return pl.BlockSpec(shape, lambda *_: (0,) * nd)


def _sigmoid(x):
    return 1.0 / (1.0 + jnp.exp(-x))


def _silu(x):
    return x * _sigmoid(x)


def _rms_scale(x, n):
    return lax.rsqrt(jnp.sum(x * x, axis=-1, keepdims=True) * (1.0 / n) + EPS)


def _dot(a, b):
    return jnp.dot(a, b, preferred_element_type=F32)


def _dot_nt(a, b):
    return lax.dot_general(a, b, (((1,), (1,)), ((), ())), preferred_element_type=F32)


def _dot_tn(a, b):
    return lax.dot_general(a, b, (((0,), (0,)), ((), ())), preferred_element_type=F32)


def _inproj_kernel(x_ref, g_ref, whg_ref, wmla_ref, wgate_ref, lb_ref, nq_ref, nkv_ref,
                   qs_ref, lf_ref, kk_ref, v_ref, og_ref, mla_ref, gates_ref):
    x = x_ref[...]
    d = x.shape[-1]
    h = (x * _rms_scale(x, d) * g_ref[...]).astype(BF16)

    qs_ref[...] = _silu(_dot(h, whg_ref[:, 0:HG_F])).astype(BF16)
    for direction in range(2):
        cols = slice(HG_F * (1 + direction), HG_F * (2 + direction))
        out_cols = slice(HG_F * direction, HG_F * (direction + 1))
        lb = lb_ref[direction:direction + 1, :]
        f = lb + (1.0 - lb) * _sigmoid(_dot(h, whg_ref[:, cols]))
        lf_ref[:, out_cols] = jnp.log(f)
        kk_ref[:, out_cols] = (1.0 - f).astype(BF16)
    v_ref[...] = _dot(h, whg_ref[:, 3 * HG_F:3 * HG_F + HG_V]).astype(BF16)
    og_ref[...] = _silu(_dot(h, whg_ref[:, 3 * HG_F + HG_V:3 * HG_F + 2 * HG_V])).astype(BF16)

    zm = _dot(h, wmla_ref[...])
    cq = zm[:, 0:MLA_Q_RANK]
    mla_ref[:, 0:MLA_Q_RANK] = (cq * _rms_scale(cq, MLA_Q_RANK) * nq_ref[...]).astype(BF16)
    ckv = zm[:, MLA_Q_RANK:MLA_Q_RANK + MLA_KV_RANK]
    mla_ref[:, MLA_Q_RANK:MLA_Q_RANK + MLA_KV_RANK] = (
        ckv * _rms_scale(ckv, MLA_KV_RANK) * nkv_ref[...]).astype(BF16)
    mla_ref[:, MLA_Q_RANK + MLA_KV_RANK:] = zm[:, MLA_Q_RANK + MLA_KV_RANK:].astype(BF16)

    for half in range(2):
        cols = slice(d * half, d * (half + 1))
        gates_ref[:, cols] = _sigmoid(_dot(h, wgate_ref[:, cols])).astype(BF16)


def _inproj(x2, ln_mix, whg, wmla, wgate, lb, nq, nkv):
    t, d = x2.shape
    tm = ROW_TILE
    mla_w = wmla.shape[1]
    row = lambda w: pl.BlockSpec((tm, w), lambda i: (i, 0))
    out_shape = (
        jax.ShapeDtypeStruct((t, HG_F), BF16),
        jax.ShapeDtypeStruct((t, 2 * HG_F), F32),
        jax.ShapeDtypeStruct((t, 2 * HG_F), BF16),
        jax.ShapeDtypeStruct((t, HG_V), BF16),
        jax.ShapeDtypeStruct((t, HG_V), BF16),
        jax.ShapeDtypeStruct((t, mla_w), BF16),
        jax.ShapeDtypeStruct((t, 2 * d), BF16),
    )
    return pl.pallas_call(
        _inproj_kernel,
        grid=(t // tm,),
        in_specs=[row(d), _const_spec((1, d)), _const_spec(whg.shape), _const_spec(wmla.shape),
                  _const_spec(wgate.shape), _const_spec(lb.shape), _const_spec(nq.shape),
                  _const_spec(nkv.shape)],
        out_specs=(row(HG_F), row(2 * HG_F), row(2 * HG_F), row(HG_V), row(HG_V), row(mla_w),
                   row(2 * d)),
        out_shape=out_shape,
        compiler_params=_cparams(("parallel",)),
        name="inproj",
    )(x2, ln_mix, whg, wmla, wgate, lb, nq, nkv)


def _hgrn_kernel(qs_ref, lff_ref, lfb_ref, kf_ref, kb_ref, v_ref, og_ref, onorm_ref, out_ref,
                 state_ref, ofw_ref, obw_ref, qt_scr, u_scr, dec_scr, g_scr, k_scr, v_scr):
    c = HG_CHUNK
    s_len = qs_ref.shape[1]
    n_chunks = s_len // c
    hb = HG_HEADS_PER_STEP
    kd = HG_KDIM

    row = lax.broadcasted_iota(jnp.int32, (c, c), 0)
    col = lax.broadcasted_iota(jnp.int32, (c, c), 1)
    masks = (row >= col, row <= col)
    rowc = lax.broadcasted_iota(jnp.int32, (c, kd), 0)
    o_refs = (ofw_ref, obw_ref)

    def cum_log_decay(lf, direction):
        n = lf.shape[0]
        in_chunk = lax.broadcasted_iota(jnp.int32, lf.shape, 0) % c
        g = lf
        k = 1
        while k < c:
            if direction == 0:
                g = g + jnp.where(in_chunk >= k, pltpu.roll(g, k, 0), 0.0)
            else:
                g = g + jnp.where(in_chunk < c - k, pltpu.roll(g, n - k, 0), 0.0)
            k *= 2
        return g

    def load(j):
        chains = []
        for hh in range(hb):
            lanes = slice(hh * kd, (hh + 1) * kd)
            for direction in range(2):
                chunk = j if direction == 0 else n_chunks - 1 - j
                rows = pl.ds(pl.multiple_of(chunk * c, c), c)
                g = cum_log_decay((lff_ref, lfb_ref)[direction][0, rows, lanes], direction)
                total = g[c - 1:c, :] if direction == 0 else g[0:1, :]
                chains.append(dict(
                    idx=hh * 2 + direction, direction=direction, rows=rows, lanes=lanes, g=g,
                    total=total,
                    q=qs_ref[0, rows, lanes].astype(F32),
                    k=(kf_ref, kb_ref)[direction][0, rows, lanes].astype(F32),
                    v=v_ref[0, rows, lanes]))
        return chains

    def finish(ch, o, kdec_t_v_scaled):
        (ofw_ref, obw_ref)[ch["direction"]][ch["rows"], ch["lanes"]] = o
        state_ref[ch["idx"]] = kdec_t_v_scaled

    def fast(ch):
        g = ch["g"]
        qt = (ch["q"] * jnp.exp(g)).astype(BF16)
        kt = (ch["k"] * jnp.exp(-g)).astype(BF16)
        st = state_ref[ch["idx"]]
        sc = jnp.where(masks[ch["direction"]], _dot_nt(qt, kt), 0.0)
        o = _dot_nt(qt, st.astype(BF16)) + _dot(sc.astype(BF16), ch["v"])
        finish(ch, o, (st + _dot_tn(ch["v"], kt)) * jnp.exp(ch["total"]))

    def robust(ch):
        g = ch["g"]
        direction = ch["direction"]
        st = state_ref[ch["idx"]]
        o0 = _dot_nt((ch["q"] * jnp.exp(g)).astype(BF16), st.astype(BF16))
        slot = ch["idx"]
        g_scr[slot] = g
        k_scr[slot] = ch["k"]
        v_scr[slot] = ch["v"].astype(F32)
        q = ch["q"]

        def body(s, acc):
            g_s = g_scr[slot, pl.ds(s, 1), :]
            seen = (rowc >= s) if direction == 0 else (rowc <= s)
            decay = jnp.where(seen, jnp.exp(jnp.minimum(g - g_s, 0.0)), 0.0)
            a = jnp.sum(q * decay * k_scr[slot, pl.ds(s, 1), :], axis=-1, keepdims=True)
            return acc + a * v_scr[slot, pl.ds(s, 1), :]

        o = lax.fori_loop(0, c, body, o0)
        kdec = (ch["k"] * jnp.exp(ch["total"] - g)).astype(BF16)
        finish(ch, o, st * jnp.exp(ch["total"]) + _dot_tn(ch["v"], kdec))

    def step(j, carry):
        chains = load(j)
        lowest = chains[0]["total"]
        for ch in chains[1:]:
            lowest = jnp.minimum(lowest, ch["total"])
        safe = jnp.min(lowest) >= HG_SAFE_LOG_DECAY

        @pl.when(safe)
        def _():
            for ch in chains:
                fast(ch)

        @pl.when(jnp.logical_not(safe))
        def _():
            for ch in chains:
                robust(ch)

        return carry

    grp = HG_CHUNKS_PER_MATMUL if n_chunks % HG_CHUNKS_PER_MATMUL == 0 else 1
    gr = grp * c
    n_groups = n_chunks // grp
    grow = lax.broadcasted_iota(jnp.int32, (gr, gr), 0)
    gcol = lax.broadcasted_iota(jnp.int32, (gr, gr), 1)
    same_chunk = (grow // c) == (gcol // c)
    gmasks = (same_chunk & (grow >= gcol), same_chunk & (grow <= gcol))
    chunk_of_row = lax.broadcasted_iota(jnp.int32, (gr, kd), 0) // c

    def block_diag(a):
        return jnp.concatenate([jnp.where(chunk_of_row == i, a, jnp.zeros_like(a))
                                for i in range(grp)], axis=1)

    def phase_a(gi, lowest):
        rows = pl.ds(pl.multiple_of(gi * gr, gr), gr)
        for hh in range(hb):
            lanes = slice(hh * kd, (hh + 1) * kd)
            v = v_ref[0, rows, lanes]
            v_bd = block_diag(v)
            q = qs_ref[0, rows, lanes].astype(F32)
            for direction in range(2):
                idx = hh * 2 + direction
                g = cum_log_decay((lff_ref, lfb_ref)[direction][0, rows, lanes], direction)
                k = (kf_ref, kb_ref)[direction][0, rows, lanes].astype(F32)
                qt = (q * jnp.exp(g)).astype(BF16)
                kt = (k * jnp.exp(-g)).astype(BF16)
                qt_scr[idx, rows, :] = qt
                sc = jnp.where(gmasks[direction], _dot_nt(qt, kt), 0.0)
                o_refs[direction][rows, lanes] = _dot(sc.astype(BF16), v)
                u_scr[idx, pl.ds(gi * grp, grp)] = _dot_tn(v_bd, kt).reshape(grp, HG_VDIM, kd)
                for i in range(grp):
                    edge = i * c + (c - 1 if direction == 0 else 0)
                    total = g[edge:edge + 1, :]
                    lowest = jnp.minimum(lowest, total)
                    dec_scr[idx, pl.ds(gi * grp + i, 1), :] = jnp.exp(total)
        return lowest

    lowest = lax.fori_loop(0, n_groups, phase_a, jnp.zeros((1, kd), F32))
    all_safe = jnp.min(lowest) >= HG_SAFE_LOG_DECAY
    state_ref[...] = jnp.zeros_like(state_ref)

    @pl.when(all_safe)
    def _():
        def phase_b(j, carry):
            for hh in range(hb):
                lanes = slice(hh * kd, (hh + 1) * kd)
                for direction in range(2):
                    idx = hh * 2 + direction
                    gi = j if direction == 0 else n_groups - 1 - j
                    rows = pl.ds(pl.multiple_of(gi * gr, gr), gr)
                    st = state_ref[idx]
                    seen = [None] * grp
                    for i in (range(grp) if direction == 0 else reversed(range(grp))):
                        ci = gi * grp + i
                        seen[i] = st.astype(BF16)
                        st = (st + u_scr[idx, ci]) * dec_scr[idx, pl.ds(ci, 1), :]
                    state_ref[idx] = st
                    o_refs[direction][rows, lanes] += _dot_nt(
                        block_diag(qt_scr[idx, rows, :]), jnp.concatenate(seen, axis=1))
            return carry

        lax.fori_loop(0, n_groups, phase_b, 0)

    @pl.when(jnp.logical_not(all_safe))
    def _():
        lax.fori_loop(0, n_chunks, step, 0)

    blk = 256 if s_len % 256 == 0 else c

    def epilogue(i, carry):
        rows = pl.ds(pl.multiple_of(i * blk, blk), blk)
        for hh in range(hb):
            lanes = slice(hh * kd, (hh + 1) * kd)
            o = ofw_ref[rows, lanes] + obw_ref[rows, lanes]
            y = o * _rms_scale(o, HG_VDIM) * onorm_ref[...]
            out_ref[0, rows, lanes] = (y * og_ref[0, rows, lanes].astype(F32)).astype(BF16)
        return carry

    lax.fori_loop(0, s_len // blk, epilogue, 0)


def _hgrn(qs, lf, kk, v, og, onorm):
    b, s, _ = qs.shape
    hb = HG_HEADS_PER_STEP
    w = hb * HG_KDIM
    n_hsteps = HG_HEADS // hb
    fwd = pl.BlockSpec((1, s, w), lambda bi, hi: (bi, 0, hi))
    bwd = pl.BlockSpec((1, s, w), lambda bi, hi: (bi, 0, n_hsteps + hi))
    return pl.pallas_call(
        _hgrn_kernel,
        grid=(b, n_hsteps),
        in_specs=[fwd, fwd, bwd, fwd, bwd, fwd, fwd, _const_spec(onorm.shape)],
        out_specs=fwd,
        out_shape=jax.ShapeDtypeStruct((b, s, HG_V), BF16),
        scratch_shapes=[
            pltpu.VMEM((2 * hb, HG_VDIM, HG_KDIM), F32),
            pltpu.VMEM((s, w), F32),
            pltpu.VMEM((s, w), F32),
            pltpu.VMEM((2 * hb, s, HG_KDIM), BF16),
            pltpu.VMEM((2 * hb, s // HG_CHUNK, HG_VDIM, HG_KDIM), F32),
            pltpu.VMEM((2 * hb, max(s // HG_CHUNK, 8), HG_KDIM), F32),
            pltpu.VMEM((2 * hb, HG_CHUNK, HG_KDIM), F32),
            pltpu.VMEM((2 * hb, HG_CHUNK, HG_KDIM), F32),
            pltpu.VMEM((2 * hb, HG_CHUNK, HG_VDIM), F32),
        ],
        compiler_params=_cparams(("parallel", "parallel")),
        name="hgrn",
    )(qs, lf, lf, kk, kk, v, og, onorm)


def _mlaprep_kernel(mla_ref, pos_ref, wq_ref, wk_ref, wv_ref, gq_ref, gk_ref, rope_ref,
                    q_ref, k_ref, v_ref):
    cq = mla_ref[:, 0:MLA_Q_RANK]
    ckr = mla_ref[:, MLA_Q_RANK:]
    ang = pos_ref[...] * rope_ref[0:1, :]
    cos = jnp.cos(ang)
    sin = jnp.sin(ang)
    sin_lo = sin * rope_ref[1:2, :]
    sin_hi = sin * rope_ref[2:3, :]
    half = MLA_ROPE // 2

    def norm_rope(x, gain, scale):
        y = x * (_rms_scale(x, MLA_QK) * scale) * gain
        return y * cos + pltpu.roll(y, half, 1) * sin_lo + pltpu.roll(y, LANES - half, 1) * sin_hi

    qm = _dot(cq, wq_ref[...])
    km = _dot(ckr, wk_ref[...])
    for hd in range(MLA_HEADS):
        lanes = slice(hd * LANES, (hd + 1) * LANES)
        q_ref[:, lanes] = norm_rope(qm[:, lanes], gq_ref[...], MLA_QK ** -0.5).astype(BF16)
        k_ref[:, lanes] = norm_rope(km[:, lanes], gk_ref[...], 1.0).astype(BF16)
    v_ref[...] = _dot(ckr, wv_ref[...]).astype(BF16)


def _mlaprep(mla_in, posf, wq, wk, wv, gq, gk, rope_tab):
    t = mla_in.shape[0]
    tm = ROW_TILE
    row = lambda w: pl.BlockSpec((tm, w), lambda i: (i, 0))
    hw = MLA_HEADS * LANES
    vw = MLA_HEADS * MLA_VDIM
    return pl.pallas_call(
        _mlaprep_kernel,
        grid=(t // tm,),
        in_specs=[row(mla_in.shape[1]), row(1), _const_spec(wq.shape), _const_spec(wk.shape),
                  _const_spec(wv.shape), _const_spec(gq.shape), _const_spec(gk.shape),
                  _const_spec(rope_tab.shape)],
        out_specs=(row(hw), row(hw), row(vw)),
        out_shape=(jax.ShapeDtypeStruct((t, hw), BF16), jax.ShapeDtypeStruct((t, hw), BF16),
                   jax.ShapeDtypeStruct((t, vw), BF16)),
        compiler_params=_cparams(("parallel",)),
        name="mlaprep",
    )(mla_in, posf, wq, wk, wv, gq, gk, rope_tab)


def _attn_kernel(q_ref, k_ref, v_ref, o_ref):
    for j in range(2):
        lanes = slice(j * LANES, (j + 1) * LANES)
        s = _dot_nt(q_ref[0, :, lanes], k_ref[0, :, lanes])
        p = jnp.exp(s - jnp.max(s, axis=-1, keepdims=True))
        l = jnp.sum(p, axis=-1, keepdims=True)
        vl = slice(j * MLA_VDIM, (j + 1) * MLA_VDIM)
        o = _dot(p.astype(BF16), v_ref[0, :, vl])
        o_ref[0, :, vl] = (o / l).astype(BF16)


def _attn(q, k, v):
    b, s, _ = q.shape
    tq = min(ATTN_Q_TILE, s)
    return pl.pallas_call(
        _attn_kernel,
        grid=(b, MLA_HEADS // 2, s // tq),
        in_specs=[pl.BlockSpec((1, tq, 2 * LANES), lambda bi, hp, i: (bi, i, hp)),
                  pl.BlockSpec((1, s, 2 * LANES), lambda bi, hp, i: (bi, 0, hp)),
                  pl.BlockSpec((1, s, 2 * MLA_VDIM), lambda bi, hp, i: (bi, 0, hp))],
        out_specs=pl.BlockSpec((1, tq, 2 * MLA_VDIM), lambda bi, hp, i: (bi, i, hp)),
        out_shape=jax.ShapeDtypeStruct((b, s, MLA_HEADS * MLA_VDIM), BF16),
        compiler_params=_cparams(("parallel", "parallel", "arbitrary")),
        name="attn",
    )(q, k, v)


def _merge_kernel(x_ref, ya_ref, yb_ref, gates_ref, woa_ref, wob_ref, wout_ref, lnm_ref, wr_ref,
                  br_ref, x1_ref, hm_ref, route_ref):
    d = x_ref.shape[-1]
    y_a = _dot(ya_ref[...], woa_ref[...])
    y_b = _dot(yb_ref[...], wob_ref[...])
    merged = gates_ref[:, 0:d].astype(F32) * y_a + gates_ref[:, d:2 * d].astype(F32) * y_b
    x1 = x_ref[...] + _dot(merged.astype(BF16), wout_ref[...])
    x1_ref[...] = x1
    hm = x1 * _rms_scale(x1, d) * lnm_ref[...]
    hm_ref[...] = hm

    logits = jnp.dot(hm, wr_ref[...], preferred_element_type=F32,
                     precision=lax.Precision.HIGHEST) + br_ref[...]
    lane = lax.broadcasted_iota(jnp.int32, logits.shape, 1)
    neg = -jnp.inf
    big = jnp.int32(2 ** 30)
    is_group = lane < N_GROUPS
    gl = jnp.where(is_group, logits, neg)
    gmax = jnp.max(gl, axis=-1, keepdims=True)
    p_group = 1.0 / jnp.sum(jnp.where(is_group, jnp.exp(gl - gmax), 0.0), axis=-1, keepdims=True)
    g_sel = jnp.min(jnp.where(gl == gmax, lane, big), axis=-1, keepdims=True)
    lo = N_GROUPS + g_sel * EXPERTS_PER_GROUP
    in_group = (lane >= lo) & (lane < lo + EXPERTS_PER_GROUP)
    el = jnp.where(in_group, logits, neg)
    v1 = jnp.max(el, axis=-1, keepdims=True)
    i1 = jnp.min(jnp.where(el == v1, lane, big), axis=-1, keepdims=True)
    el2 = jnp.where(lane == i1, neg, el)
    v2 = jnp.max(el2, axis=-1, keepdims=True)
    i2 = jnp.min(jnp.where(el2 == v2, lane, big), axis=-1, keepdims=True)
    e21 = jnp.exp(v2 - v1)
    w1 = p_group / (1.0 + e21)
    w2 = w1 * e21
    route = jnp.where(lane == 0, (i1 - N_GROUPS).astype(F32),
                      jnp.where(lane == 1, (i2 - N_GROUPS).astype(F32),
                                jnp.where(lane == 2, w1, jnp.where(lane == 3, w2, 0.0))))
    route_ref[...] = route


def _merge(x2, ya, yb, gates, woa, wob, wout, lnm, wr, br):
    t, d = x2.shape
    tm = ROW_TILE
    row = lambda w: pl.BlockSpec((tm, w), lambda i: (i, 0))
    return pl.pallas_call(
        _merge_kernel,
        grid=(t // tm,),
        in_specs=[row(d), row(ya.shape[1]), row(yb.shape[1]), row(2 * d), _const_spec(woa.shape),
                  _const_spec(wob.shape), _const_spec(wout.shape), _const_spec(lnm.shape),
                  _const_spec(wr.shape), _const_spec(br.shape)],
        out_specs=(row(d), row(d), row(LANES)),
        out_shape=(jax.ShapeDtypeStruct((t, d), F32), jax.ShapeDtypeStruct((t, d), F32),
                   jax.ShapeDtypeStruct((t, LANES), F32)),
        compiler_params=_cparams(("parallel",)),
        name="merge",
    )(x2, ya, yb, gates, woa, wob, wout, lnm, wr, br)


def _start_row_gather(src_hbm, idx_ref, base, stride, buf, sem):
    for r in range(buf.shape[0]):
        pltpu.make_async_copy(src_hbm.at[pl.ds(idx_ref[base + stride * r], 1)],
                              buf.at[pl.ds(r, 1)], sem).start()


def _wait_row_gather(src_hbm, buf, sem):
    pltpu.make_async_copy(src_hbm.at[pl.ds(0, buf.shape[0])], buf, sem).wait()


def _experts_kernel(be_ref, tok_ref, tokn_ref, hm_ref, w1a_ref, w3a_ref, w2a_ref, w1b_ref,
                    w3b_ref, w2b_ref, yb_ref, xa, xb, sems, w1c, w3c, w2c):
    j = pl.program_id(0)
    last = pl.num_programs(0) - 1
    rows = MOE_ROWS

    @pl.when(j == 0)
    def _():
        _start_row_gather(hm_ref, tok_ref, 0, 1, xa, sems.at[0])

    def compute(buf, out_rows, blk, w_refs):
        new_expert = be_ref[blk] != be_ref[jnp.maximum(blk - 1, 0)]

        @pl.when(jnp.logical_or(blk == 0, new_expert))
        def _():
            w1c[...] = w_refs[0][0].astype(BF16)
            w3c[...] = w_refs[1][0].astype(BF16)
            w2c[...] = w_refs[2][0].astype(BF16)

        x = buf[...].astype(BF16)
        hmid = (_silu(_dot(x, w1c[...])) * _dot(x, w3c[...])).astype(BF16)
        yb_ref[out_rows, :] = _dot(hmid, w2c[...])

    _wait_row_gather(hm_ref, xa, sems.at[0])
    _start_row_gather(hm_ref, tok_ref, rows, 1, xb, sems.at[1])
    compute(xa, slice(0, rows), 2 * j, (w1a_ref, w3a_ref, w2a_ref))

    _wait_row_gather(hm_ref, xb, sems.at[1])
    _start_row_gather(hm_ref, tokn_ref, 0, 1, xa, sems.at[0])
    compute(xb, slice(rows, 2 * rows), 2 * j + 1, (w1b_ref, w3b_ref, w2b_ref))

    @pl.when(j == last)
    def _():
        _wait_row_gather(hm_ref, xa, sems.at[0])


def _experts(block_expert, tok_sorted, hm, w1, w3, w2):
    nblk = block_expert.shape[0]
    nsteps = nblk // 2
    d = hm.shape[1]
    de = w1.shape[2]
    wsel_a = lambda j, be: (be[2 * j], 0, 0)
    wsel_b = lambda j, be: (be[2 * j + 1], 0, 0)
    return pl.pallas_call(
        _experts_kernel,
        grid_spec=pltpu.PrefetchScalarGridSpec(
            num_scalar_prefetch=1,
            grid=(nsteps,),
            in_specs=[pl.BlockSpec((2 * MOE_ROWS,), lambda j, be: (j,), memory_space=pltpu.SMEM),
                      pl.BlockSpec((2 * MOE_ROWS,), lambda j, be: (jnp.minimum(j + 1, nsteps - 1),),
                                   memory_space=pltpu.SMEM),
                      pl.BlockSpec(memory_space=pl.ANY),
                      pl.BlockSpec((1, d, de), wsel_a), pl.BlockSpec((1, d, de), wsel_a),
                      pl.BlockSpec((1, de, d), wsel_a),
                      pl.BlockSpec((1, d, de), wsel_b), pl.BlockSpec((1, d, de), wsel_b),
                      pl.BlockSpec((1, de, d), wsel_b)],
            out_specs=pl.BlockSpec((2 * MOE_ROWS, d), lambda j, be: (j, 0)),
            scratch_shapes=[pltpu.VMEM((MOE_ROWS, d), F32), pltpu.VMEM((MOE_ROWS, d), F32),
                            pltpu.SemaphoreType.DMA((2,)),
                            pltpu.VMEM((d, de), BF16), pltpu.VMEM((d, de), BF16),
                            pltpu.VMEM((de, d), BF16)]),
        out_shape=jax.ShapeDtypeStruct((nblk * MOE_ROWS, d), F32),
        compiler_params=_cparams(("arbitrary",)),
        name="experts",
    )(block_expert, tok_sorted, tok_sorted, hm, w1, w3, w2, w1, w3, w2)


def _ple_kernel(dest_ref, destn_ref, x1_ref, route_ref, p_ref, ln_ref, wg_ref, wp_ref, yb_ref,
                out_ref, ya0, ya1, yb0, yb1, sems):
    j = pl.program_id(0)
    last = pl.num_programs(0) - 1
    tm = ROW_TILE
    d = x1_ref.shape[1]

    def start(idx_ref, base, bufs, sem):
        _start_row_gather(yb_ref, idx_ref, base, 2, bufs[0], sem)
        _start_row_gather(yb_ref, idx_ref, base + 1, 2, bufs[1], sem)

    def wait(bufs, sem):
        _wait_row_gather(yb_ref, bufs[0], sem)
        _wait_row_gather(yb_ref, bufs[1], sem)

    def compute(bufs, rows):
        x2 = (x1_ref[rows, :] + route_ref[rows, 2:3] * bufs[0][...]
              + route_ref[rows, 3:4] * bufs[1][...])
        h = (x2 * _rms_scale(x2, d) * ln_ref[...]).astype(BF16)
        pp = _dot(p_ref[rows, :].astype(BF16), wp_ref[...])
        out_ref[rows, :] = x2 + pp * _sigmoid(_dot(h, wg_ref[...]))

    @pl.when(j == 0)
    def _():
        start(dest_ref, 0, (ya0, ya1), sems.at[0])

    wait((ya0, ya1), sems.at[0])
    start(dest_ref, 2 * tm, (yb0, yb1), sems.at[1])
    compute((ya0, ya1), slice(0, tm))

    wait((yb0, yb1), sems.at[1])
    start(destn_ref, 0, (ya0, ya1), sems.at[0])
    compute((yb0, yb1), slice(tm, 2 * tm))

    @pl.when(j == last)
    def _():
        wait((ya0, ya1), sems.at[0])


def _ple(dest_flat, x1, route, p2, ln, wg, wp, yb):
    t, d = x1.shape
    tm = ROW_TILE
    nsteps = t // (2 * tm)
    row = lambda w: pl.BlockSpec((2 * tm, w), lambda j: (j, 0))
    return pl.pallas_call(
        _ple_kernel,
        grid=(nsteps,),
        in_specs=[pl.BlockSpec((4 * tm,), lambda j: (j,), memory_space=pltpu.SMEM),
                  pl.BlockSpec((4 * tm,), lambda j: (jnp.minimum(j + 1, nsteps - 1),),
                               memory_space=pltpu.SMEM),
                  row(d), row(LANES), row(p2.shape[1]), _const_spec(ln.shape),
                  _const_spec(wg.shape), _const_spec(wp.shape),
                  pl.BlockSpec(memory_space=pl.ANY)],
        out_specs=row(d),
        out_shape=jax.ShapeDtypeStruct((t, d), F32),
        scratch_shapes=[pltpu.VMEM((tm, d), F32)] * 4 + [pltpu.SemaphoreType.DMA((2,))],
        compiler_params=_cparams(("arbitrary",)),
        name="ple",
    )(dest_flat, dest_flat, x1, route, p2, ln, wg, wp, yb)


def _head_pad(w, width):
    r = w.shape[0]
    w = w.reshape(r, MLA_HEADS, width)
    return jnp.pad(w, ((0, 0), (0, 0), (0, LANES - width))).reshape(r, MLA_HEADS * LANES)


def _rope_table():
    half = MLA_ROPE // 2
    inv_freq = ROPE_THETA ** (-np.arange(half, dtype=np.float32) / half)
    tab = np.zeros((8, LANES), np.float32)
    tab[0, MLA_NOPE:MLA_NOPE + half] = inv_freq
    tab[0, MLA_NOPE + half:MLA_QK] = inv_freq
    tab[1, MLA_NOPE + half:MLA_QK] = 1.0
    tab[2, MLA_NOPE:MLA_NOPE + half] = -1.0
    return jnp.asarray(tab)


def _route_tables(ids):
    a = ids.shape[0] * 2
    e = ids.reshape(a)
    onehot = (e[:, None] == jnp.arange(N_EXPERTS, dtype=jnp.int32)[None, :]).astype(jnp.int32)
    csum = jnp.cumsum(onehot, axis=0)
    counts = csum[-1]
    rank = jnp.sum(csum * onehot, axis=1) - 1
    pcounts = (counts + MOE_ROWS - 1) // MOE_ROWS * MOE_ROWS
    pends = jnp.cumsum(pcounts)
    pstarts = pends - pcounts
    dest = (jnp.sum(onehot * pstarts[None, :], axis=1) + rank).astype(jnp.int32)
    nblk = (a + N_EXPERTS * (MOE_ROWS - 1)) // MOE_ROWS
    nblk += nblk % 2
    blk_start = jnp.arange(nblk, dtype=jnp.int32) * MOE_ROWS
    block_expert = jnp.searchsorted(pends, jnp.minimum(blk_start, pends[-1] - 1), side='right')
    block_expert = jnp.minimum(block_expert, N_EXPERTS - 1).astype(jnp.int32)
    tok_sorted = jnp.zeros((nblk * MOE_ROWS,), jnp.int32).at[dest].set(
        jnp.arange(a, dtype=jnp.int32) // 2)
    return dest, tok_sorted, block_expert


def kernel(x, p, positions, ln_mix, w_in, hg_lb, hg_onorm, w_oA, mla_qa_norm, mla_kva_norm, w_uq,
           w_ukv, q_norm, k_norm, w_oB, w_out, ln_moe, w_rg, b_rg, w_re, b_re, w1, w3, w2, ln_ple,
           w_ple_gate, w_ple_proj):
    b, s, d = x.shape
    t = b * s
    depth = w_in.shape[0]
    lb_all = jnp.cumsum(jax.nn.softmax(hg_lb.astype(F32), axis=1), axis=1)
    posf = positions.astype(F32).reshape(t, 1)
    rope_tab = _rope_table()
    xc = x.reshape(t, d)

    for layer in range(depth):
        wi = w_in[layer]
        n_hg = 3 * HG_F + 2 * HG_V
        n_mla = MLA_Q_RANK + MLA_KV_RANK + MLA_ROPE
        mla_w = -(-n_mla // LANES) * LANES
        whg = wi[:, :n_hg].astype(BF16)
        wmla = jnp.pad(wi[:, n_hg:n_hg + n_mla], ((0, 0), (0, mla_w - n_mla))).astype(BF16)
        wgate = wi[:, n_hg + n_mla:].astype(BF16)
        lb = lb_all[:, layer, :]

        qs, lf, kk, v, og, mla_in, gates = _inproj(
            xc, ln_mix[layer][None, :], whg, wmla, wgate, lb, mla_qa_norm[layer][None, :],
            mla_kva_norm[layer][None, :])

        r3 = lambda a: a.reshape(b, s, a.shape[-1])
        ya = _hgrn(r3(qs), r3(lf), r3(kk), r3(v), r3(og), hg_onorm[layer][None, :]).reshape(t, HG_V)

        kv_w = MLA_NOPE + MLA_VDIM
        wkv = w_ukv[layer].reshape(MLA_KV_RANK, MLA_HEADS, kv_w)
        ckr_w = mla_w - MLA_Q_RANK
        wq = _head_pad(w_uq[layer], MLA_QK).astype(BF16)
        wk_nope = _head_pad(wkv[:, :, :MLA_NOPE].reshape(MLA_KV_RANK, MLA_HEADS * MLA_NOPE), MLA_NOPE)
        place = np.zeros((ckr_w - MLA_KV_RANK, MLA_HEADS, LANES), np.float32)
        for j in range(MLA_ROPE):
            place[j, :, MLA_NOPE + j] = 1.0
        wk = jnp.concatenate([wk_nope, jnp.asarray(place.reshape(ckr_w - MLA_KV_RANK, -1))],
                             axis=0).astype(BF16)
        wv = jnp.pad(wkv[:, :, MLA_NOPE:].reshape(MLA_KV_RANK, MLA_HEADS * MLA_VDIM),
                     ((0, ckr_w - MLA_KV_RANK), (0, 0))).astype(BF16)
        gq = jnp.pad(q_norm[layer], (0, LANES - MLA_QK))[None, :]
        gk = jnp.pad(k_norm[layer], (0, LANES - MLA_QK))[None, :]
        qh, kh, vh = _mlaprep(mla_in, posf, wq, wk, wv, gq, gk, rope_tab)
        yb = _attn(r3(qh), r3(kh), r3(vh)).reshape(t, MLA_HEADS * MLA_VDIM)

        wr = jnp.pad(jnp.concatenate([w_rg[layer], w_re[layer]], axis=1),
                     ((0, 0), (0, LANES - N_GROUPS - N_EXPERTS)))
        br = jnp.pad(jnp.concatenate([b_rg[layer], b_re[layer]]),
                     (0, LANES - N_GROUPS - N_EXPERTS))[None, :]
        x1, hm, route = _merge(xc, ya, yb, gates, w_oA[layer].astype(BF16), w_oB[layer].astype(BF16),
                               w_out[layer].astype(BF16), ln_moe[layer][None, :], wr, br)

        ids = route[:, 0:2].astype(jnp.int32)
        dest, tok_sorted, block_expert = _route_tables(ids)
        yexp = _experts(block_expert, tok_sorted, hm, w1[layer], w3[layer], w2[layer])

        xc = _ple(dest, x1, route, p[layer].reshape(t, -1), ln_ple[layer][None, :],
                  w_ple_gate[layer].astype(BF16), w_ple_proj[layer].astype(BF16), yexp)
    return xc.reshape(b, s, d)
```

```python
import functools

import numpy as np
import jax
import jax.numpy as jnp
from jax import lax
from jax.experimental import pallas as pl
from jax.experimental.pallas import tpu as pltpu

F32 = jnp.float32
BF16 = jnp.bfloat16

HG_HEADS = 4
HG_KDIM = 128
HG_VDIM = 128
HG_F = HG_HEADS * HG_KDIM
HG_V = HG_HEADS * HG_VDIM
MLA_HEADS = 8
MLA_NOPE = 64
MLA_ROPE = 32
MLA_VDIM = 64
MLA_QK = MLA_NOPE + MLA_ROPE
MLA_Q_RANK = 256
MLA_KV_RANK = 128
ROPE_THETA = 10000.0
N_GROUPS = 8
EXPERTS_PER_GROUP = 8
N_EXPERTS = N_GROUPS * EXPERTS_PER_GROUP
D_EXPERT = 256
EPS = 1e-6

LANES = 128
SUBLANES = 8
VMEM_LIMIT_BYTES = 56 * 1024 * 1024

ROW_TILE = 256
HG_CHUNK = 64
HG_HEADS_PER_STEP = 2
HG_CHUNKS_PER_MATMUL = 4
ATTN_Q_TILE = 256
MOE_ROWS = 256
DISPATCH_TOKENS = 512
HG_SAFE_LOG_DECAY = -60.0


def _cparams(sem):
    return pltpu.CompilerParams(dimension_semantics=sem, vmem_limit_bytes=VMEM_LIMIT_BYTES)


def _const_spec(shape):
    nd = len(shape)
    return pl.BlockSpec(shape, lambda *_: (0,) * nd)


def _sigmoid(x):
    return 1.0 / (1.0 + jnp.exp(-x))


def _silu(x):
    return x * _sigmoid(x)


def _rms_scale(x, n):
    return lax.rsqrt(jnp.sum(x * x, axis=-1, keepdims=True) * (1.0 / n) + EPS)


def _dot(a, b):
    return jnp.dot(a, b, preferred_element_type=F32)


def _dot_nt(a, b):
    return lax.dot_general(a, b, (((1,), (1,)), ((), ())), preferred_element_type=F32)


def _dot_tn(a, b):
    return lax.dot_general(a, b, (((0,), (0,)), ((), ())), preferred_element_type=F32)


def _inproj_kernel(x_ref, g_ref, whg_ref, wmla_ref, wgate_ref, lb_ref, nq_ref, nkv_ref,
                   qs_ref, lf_ref, kk_ref, v_ref, og_ref, mla_ref, gates_ref):
    x = x_ref[...]
    d = x.shape[-1]
    h = (x * _rms_scale(x, d) * g_ref[...]).astype(BF16)

    qs_ref[...] = _silu(_dot(h, whg_ref[:, 0:HG_F])).astype(BF16)
    for direction in range(2):
        cols = slice(HG_F * (1 + direction), HG_F * (2 + direction))
        out_cols = slice(HG_F * direction, HG_F * (direction + 1))
        lb = lb_ref[direction:direction + 1, :]
        f = lb + (1.0 - lb) * _sigmoid(_dot(h, whg_ref[:, cols]))
        lf_ref[:, out_cols] = jnp.log(f)
        kk_ref[:, out_cols] = (1.0 - f).astype(BF16)
    v_ref[...] = _dot(h, whg_ref[:, 3 * HG_F:3 * HG_F + HG_V]).astype(BF16)
    og_ref[...] = _silu(_dot(h, whg_ref[:, 3 * HG_F + HG_V:3 * HG_F + 2 * HG_V])).astype(BF16)

    zm = _dot(h, wmla_ref[...])
    cq = zm[:, 0:MLA_Q_RANK]
    mla_ref[:, 0:MLA_Q_RANK] = (cq * _rms_scale(cq, MLA_Q_RANK) * nq_ref[...]).astype(BF16)
    ckv = zm[:, MLA_Q_RANK:MLA_Q_RANK + MLA_KV_RANK]
    mla_ref[:, MLA_Q_RANK:MLA_Q_RANK + MLA_KV_RANK] = (
        ckv * _rms_scale(ckv, MLA_KV_RANK) * nkv_ref[...]).astype(BF16)
    mla_ref[:, MLA_Q_RANK + MLA_KV_RANK:] = zm[:, MLA_Q_RANK + MLA_KV_RANK:].astype(BF16)

    for half in range(2):
        cols = slice(d * half, d * (half + 1))
        gates_ref[:, cols] = _sigmoid(_dot(h, wgate_ref[:, cols])).astype(BF16)


def _inproj(x2, ln_mix, whg, wmla, wgate, lb, nq, nkv):
    t, d = x2.shape
    tm = ROW_TILE
    mla_w = wmla.shape[1]
    row = lambda w: pl.BlockSpec((tm, w), lambda i: (i, 0))
    out_shape = (
        jax.ShapeDtypeStruct((t, HG_F), BF16),
        jax.ShapeDtypeStruct((t, 2 * HG_F), F32),
        jax.ShapeDtypeStruct((t, 2 * HG_F), BF16),
        jax.ShapeDtypeStruct((t, HG_V), BF16),
        jax.ShapeDtypeStruct((t, HG_V), BF16),
        jax.ShapeDtypeStruct((t, mla_w), BF16),
        jax.ShapeDtypeStruct((t, 2 * d), BF16),
    )
    return pl.pallas_call(
        _inproj_kernel,
        grid=(t // tm,),
        in_specs=[row(d), _const_spec((1, d)), _const_spec(whg.shape), _const_spec(wmla.shape),
                  _const_spec(wgate.shape), _const_spec(lb.shape), _const_spec(nq.shape),
                  _const_spec(nkv.shape)],
        out_specs=(row(HG_F), row(2 * HG_F), row(2 * HG_F), row(HG_V), row(HG_V), row(mla_w),
                   row(2 * d)),
        out_shape=out_shape,
        compiler_params=_cparams(("parallel",)),
        name="inproj",
    )(x2, ln_mix, whg, wmla, wgate, lb, nq, nkv)


def _hgrn_kernel(qs_ref, lff_ref, lfb_ref, kf_ref, kb_ref, v_ref, og_ref, onorm_ref, out_ref,
                 state_ref, ofw_ref, obw_ref, qt_scr, u_scr, dec_scr, g_scr, k_scr, v_scr):
    c = HG_CHUNK
    s_len = qs_ref.shape[1]
    n_chunks = s_len // c
    hb = HG_HEADS_PER_STEP
    kd = HG_KDIM

    row = lax.broadcasted_iota(jnp.int32, (c, c), 0)
    col = lax.broadcasted_iota(jnp.int32, (c, c), 1)
    masks = (row >= col, row <= col)
    rowc = lax.broadcasted_iota(jnp.int32, (c, kd), 0)
    o_refs = (ofw_ref, obw_ref)

    def cum_log_decay(lf, direction):
        n = lf.shape[0]
        in_chunk = lax.broadcasted_iota(jnp.int32, lf.shape, 0) % c
        g = lf
        k = 1
        while k < c:
            if direction == 0:
                g = g + jnp.where(in_chunk >= k, pltpu.roll(g, k, 0), 0.0)
            else:
                g = g + jnp.where(in_chunk < c - k, pltpu.roll(g, n - k, 0), 0.0)
            k *= 2
        return g

    def load(j):
        chains = []
        for hh in range(hb):
            lanes = slice(hh * kd, (hh + 1) * kd)
            for direction in range(2):
                chunk = j if direction == 0 else n_chunks - 1 - j
                rows = pl.ds(pl.multiple_of(chunk * c, c), c)
                g = cum_log_decay((lff_ref, lfb_ref)[direction][0, rows, lanes], direction)
                total = g[c - 1:c, :] if direction == 0 else g[0:1, :]
                chains.append(dict(
                    idx=hh * 2 + direction, direction=direction, rows=rows, lanes=lanes, g=g,
                    total=total,
                    q=qs_ref[0, rows, lanes].astype(F32),
                    k=(kf_ref, kb_ref)[direction][0, rows, lanes].astype(F32),
                    v=v_ref[0, rows, lanes]))
        return chains

    def finish(ch, o, kdec_t_v_scaled):
        (ofw_ref, obw_ref)[ch["direction"]][ch["rows"], ch["lanes"]] = o
        state_ref[ch["idx"]] = kdec_t_v_scaled

    def fast(ch):
        g = ch["g"]
        qt = (ch["q"] * jnp.exp(g)).astype(BF16)
        kt = (ch["k"] * jnp.exp(-g)).astype(BF16)
        st = state_ref[ch["idx"]]
        sc = jnp.where(masks[ch["direction"]], _dot_nt(qt, kt), 0.0)
        o = _dot_nt(qt, st.astype(BF16)) + _dot(sc.astype(BF16), ch["v"])
        finish(ch, o, (st + _dot_tn(ch["v"], kt)) * jnp.exp(ch["total"]))

    def robust(ch):
        g = ch["g"]
        direction = ch["direction"]
        st = state_ref[ch["idx"]]
        o0 = _dot_nt((ch["q"] * jnp.exp(g)).astype(BF16), st.astype(BF16))
        slot = ch["idx"]
        g_scr[slot] = g
        k_scr[slot] = ch["k"]
        v_scr[slot] = ch["v"].astype(F32)
        q = ch["q"]

        def body(s, acc):
            g_s = g_scr[slot, pl.ds(s, 1), :]
            seen = (rowc >= s) if direction == 0 else (rowc <= s)
            decay = jnp.where(seen, jnp.exp(jnp.minimum(g - g_s, 0.0)), 0.0)
            a = jnp.sum(q * decay * k_scr[slot, pl.ds(s, 1), :], axis=-1, keepdims=True)
            return acc + a * v_scr[slot, pl.ds(s, 1), :]

        o = lax.fori_loop(0, c, body, o0)
        kdec = (ch["k"] * jnp.exp(ch["total"] - g)).astype(BF16)
        finish(ch, o, st * jnp.exp(ch["total"]) + _dot_tn(ch["v"], kdec))

    def step(j, carry):
        chains = load(j)
        lowest = chains[0]["total"]
        for ch in chains[1:]:
            lowest = jnp.minimum(lowest, ch["total"])
        safe = jnp.min(lowest) >= HG_SAFE_LOG_DECAY

        @pl.when(safe)
        def _():
            for ch in chains:
                fast(ch)

        @pl.when(jnp.logical_not(safe))
        def _():
            for ch in chains:
                robust(ch)

        return carry

    grp = HG_CHUNKS_PER_MATMUL if n_chunks % HG_CHUNKS_PER_MATMUL == 0 else 1
    gr = grp * c
    n_groups = n_chunks // grp
    grow = lax.broadcasted_iota(jnp.int32, (gr, gr), 0)
    gcol = lax.broadcasted_iota(jnp.int32, (gr, gr), 1)
    same_chunk = (grow // c) == (gcol // c)
    gmasks = (same_chunk & (grow >= gcol), same_chunk & (grow <= gcol))
    chunk_of_row = lax.broadcasted_iota(jnp.int32, (gr, kd), 0) // c

    def block_diag(a):
        return jnp.concatenate([jnp.where(chunk_of_row == i, a, jnp.zeros_like(a))
                                for i in range(grp)], axis=1)

    def phase_a(gi, lowest):
        rows = pl.ds(pl.multiple_of(gi * gr, gr), gr)
        for hh in range(hb):
            lanes = slice(hh * kd, (hh + 1) * kd)
            v = v_ref[0, rows, lanes]
            v_bd = block_diag(v)
            q = qs_ref[0, rows, lanes].astype(F32)
            for direction in range(2):
                idx = hh * 2 + direction
                g = cum_log_decay((lff_ref, lfb_ref)[direction][0, rows, lanes], direction)
                k = (kf_ref, kb_ref)[direction][0, rows, lanes].astype(F32)
                qt = (q * jnp.exp(g)).astype(BF16)
                kt = (k * jnp.exp(-g)).astype(BF16)
                qt_scr[idx, rows, :] = qt
                sc = jnp.where(gmasks[direction], _dot_nt(qt, kt), 0.0)
                o_refs[direction][rows, lanes] = _dot(sc.astype(BF16), v)
                u_scr[idx, pl.ds(gi * grp, grp)] = _dot_tn(v_bd, kt).reshape(grp, HG_VDIM, kd)
                for i in range(grp):
                    edge = i * c + (c - 1 if direction == 0 else 0)
                    total = g[edge:edge + 1, :]
                    lowest = jnp.minimum(lowest, total)
                    dec_scr[idx, pl.ds(gi * grp + i, 1), :] = jnp.exp(total)
        return lowest

    lowest = lax.fori_loop(0, n_groups, phase_a, jnp.zeros((1, kd), F32))
    all_safe = jnp.min(lowest) >= HG_SAFE_LOG_DECAY
    state_ref[...] = jnp.zeros_like(state_ref)

    @pl.when(all_safe)
    def _():
        def phase_b(j, carry):
            for hh in range(hb):
                lanes = slice(hh * kd, (hh + 1) * kd)
                for direction in range(2):
                    idx = hh * 2 + direction
                    gi = j if direction == 0 else n_groups - 1 - j
                    rows = pl.ds(pl.multiple_of(gi * gr, gr), gr)
                    st = state_ref[idx]
                    seen = [None] * grp
                    for i in (range(grp) if direction == 0 else reversed(range(grp))):
                        ci = gi * grp + i
                        seen[i] = st.astype(BF16)
                        st = (st + u_scr[idx, ci]) * dec_scr[idx, pl.ds(ci, 1), :]
                    state_ref[idx] = st
                    o_refs[direction][rows, lanes] += _dot_nt(
                        block_diag(qt_scr[idx, rows, :]), jnp.concatenate(seen, axis=1))
            return carry

        lax.fori_loop(0, n_groups, phase_b, 0)

    @pl.when(jnp.logical_not(all_safe))
    def _():
        lax.fori_loop(0, n_chunks, step, 0)

    blk = 256 if s_len % 256 == 0 else c

    def epilogue(i, carry):
        rows = pl.ds(pl.multiple_of(i * blk, blk), blk)
        for hh in range(hb):
            lanes = slice(hh * kd, (hh + 1) * kd)
            o = ofw_ref[rows, lanes] + obw_ref[rows, lanes]
            y = o * _rms_scale(o, HG_VDIM) * onorm_ref[...]
            out_ref[0, rows, lanes] = (y * og_ref[0, rows, lanes].astype(F32)).astype(BF16)
        return carry

    lax.fori_loop(0, s_len // blk, epilogue, 0)


def _hgrn(qs, lf, kk, v, og, onorm):
    b, s, _ = qs.shape
    hb = HG_HEADS_PER_STEP
    w = hb * HG_KDIM
    n_hsteps = HG_HEADS // hb
    fwd = pl.BlockSpec((1, s, w), lambda bi, hi: (bi, 0, hi))
    bwd = pl.BlockSpec((1, s, w), lambda bi, hi: (bi, 0, n_hsteps + hi))
    return pl.pallas_call(
        _hgrn_kernel,
        grid=(b, n_hsteps),
        in_specs=[fwd, fwd, bwd, fwd, bwd, fwd, fwd, _const_spec(onorm.shape)],
        out_specs=fwd,
        out_shape=jax.ShapeDtypeStruct((b, s, HG_V), BF16),
        scratch_shapes=[
            pltpu.VMEM((2 * hb, HG_VDIM, HG_KDIM), F32),
            pltpu.VMEM((s, w), F32),
            pltpu.VMEM((s, w), F32),
            pltpu.VMEM((2 * hb, s, HG_KDIM), BF16),
            pltpu.VMEM((2 * hb, s // HG_CHUNK, HG_VDIM, HG_KDIM), F32),
            pltpu.VMEM((2 * hb, max(s // HG_CHUNK, 8), HG_KDIM), F32),
            pltpu.VMEM((2 * hb, HG_CHUNK, HG_KDIM), F32),
            pltpu.VMEM((2 * hb, HG_CHUNK, HG_KDIM), F32),
            pltpu.VMEM((2 * hb, HG_CHUNK, HG_VDIM), F32),
        ],
        compiler_params=_cparams(("parallel", "parallel")),
        name="hgrn",
    )(qs, lf, lf, kk, kk, v, og, onorm)


def _mlaprep_kernel(mla_ref, pos_ref, wq_ref, wk_ref, wv_ref, gq_ref, gk_ref, rope_ref,
                    q_ref, k_ref, v_ref):
    cq = mla_ref[:, 0:MLA_Q_RANK]
    ckr = mla_ref[:, MLA_Q_RANK:]
    ang = pos_ref[...] * rope_ref[0:1, :]
    cos = jnp.cos(ang)
    sin = jnp.sin(ang)
    sin_lo = sin * rope_ref[1:2, :]
    sin_hi = sin * rope_ref[2:3, :]
    half = MLA_ROPE // 2

    def norm_rope(x, gain, scale):
        y = x * (_rms_scale(x, MLA_QK) * scale) * gain
        return y * cos + pltpu.roll(y, half, 1) * sin_lo + pltpu.roll(y, LANES - half, 1) * sin_hi

    qm = _dot(cq, wq_ref[...])
    km = _dot(ckr, wk_ref[...])
    for hd in range(MLA_HEADS):
        lanes = slice(hd * LANES, (hd + 1) * LANES)
        q_ref[:, lanes] = norm_rope(qm[:, lanes], gq_ref[...], MLA_QK ** -0.5).astype(BF16)
        k_ref[:, lanes] = norm_rope(km[:, lanes], gk_ref[...], 1.0).astype(BF16)
    v_ref[...] = _dot(ckr, wv_ref[...]).astype(BF16)


def _mlaprep(mla_in, posf, wq, wk, wv, gq, gk, rope_tab):
    t = mla_in.shape[0]
    tm = ROW_TILE
    row = lambda w: pl.BlockSpec((tm, w), lambda i: (i, 0))
    hw = MLA_HEADS * LANES
    vw = MLA_HEADS * MLA_VDIM
    return pl.pallas_call(
        _mlaprep_kernel,
        grid=(t // tm,),
        in_specs=[row(mla_in.shape[1]), row(1), _const_spec(wq.shape), _const_spec(wk.shape),
                  _const_spec(wv.shape), _const_spec(gq.shape), _const_spec(gk.shape),
                  _const_spec(rope_tab.shape)],
        out_specs=(row(hw), row(hw), row(vw)),
        out_shape=(jax.ShapeDtypeStruct((t, hw), BF16), jax.ShapeDtypeStruct((t, hw), BF16),
                   jax.ShapeDtypeStruct((t, vw), BF16)),
        compiler_params=_cparams(("parallel",)),
        name="mlaprep",
    )(mla_in, posf, wq, wk, wv, gq, gk, rope_tab)


def _attn_kernel(q_ref, k_ref, v_ref, o_ref):
    for j in range(2):
        lanes = slice(j * LANES, (j + 1) * LANES)
        s = _dot_nt(q_ref[0, :, lanes], k_ref[0, :, lanes])
        p = jnp.exp(s - jnp.max(s, axis=-1, keepdims=True))
        l = jnp.sum(p, axis=-1, keepdims=True)
        vl = slice(j * MLA_VDIM, (j + 1) * MLA_VDIM)
        o = _dot(p.astype(BF16), v_ref[0, :, vl])
        o_ref[0, :, vl] = (o / l).astype(BF16)


def _attn(q, k, v):
    b, s, _ = q.shape
    tq = min(ATTN_Q_TILE, s)
    return pl.pallas_call(
        _attn_kernel,
        grid=(b, MLA_HEADS // 2, s // tq),
        in_specs=[pl.BlockSpec((1, tq, 2 * LANES), lambda bi, hp, i: (bi, i, hp)),
                  pl.BlockSpec((1, s, 2 * LANES), lambda bi, hp, i: (bi, 0, hp)),
                  pl.BlockSpec((1, s, 2 * MLA_VDIM), lambda bi, hp, i: (bi, 0, hp))],
        out_specs=pl.BlockSpec((1, tq, 2 * MLA_VDIM), lambda bi, hp, i: (bi, i, hp)),
        out_shape=jax.ShapeDtypeStruct((b, s, MLA_HEADS * MLA_VDIM), BF16),
        compiler_params=_cparams(("parallel", "parallel", "arbitrary")),
        name="attn",
    )(q, k, v)


def _merge_kernel(x_ref, ya_ref, yb_ref, gates_ref, woa_ref, wob_ref, wout_ref, lnm_ref, wr_ref,
                  br_ref, x1_ref, hm8_ref, route_ref):
    d = x_ref.shape[-1]
    y_a = _dot(ya_ref[...], woa_ref[...])
    y_b = _dot(yb_ref[...], wob_ref[...])
    merged = gates_ref[:, 0:d].astype(F32) * y_a + gates_ref[:, d:2 * d].astype(F32) * y_b
    x1 = x_ref[...] + _dot(merged.astype(BF16), wout_ref[...])
    x1_ref[...] = x1
    hm = x1 * _rms_scale(x1, d) * lnm_ref[...]
    _store_token_tiles(hm8_ref, 0, hm)

    logits = jnp.dot(hm, wr_ref[...], preferred_element_type=F32,
                     precision=lax.Precision.HIGHEST) + br_ref[...]
    lane = lax.broadcasted_iota(jnp.int32, logits.shape, 1)
    neg = -jnp.inf
    big = jnp.int32(2 ** 30)
    is_group = lane < N_GROUPS
    gl = jnp.where(is_group, logits, neg)
    gmax = jnp.max(gl, axis=-1, keepdims=True)
    p_group = 1.0 / jnp.sum(jnp.where(is_group, jnp.exp(gl - gmax), 0.0), axis=-1, keepdims=True)
    g_sel = jnp.min(jnp.where(gl == gmax, lane, big), axis=-1, keepdims=True)
    lo = N_GROUPS + g_sel * EXPERTS_PER_GROUP
    in_group = (lane >= lo) & (lane < lo + EXPERTS_PER_GROUP)
    el = jnp.where(in_group, logits, neg)
    v1 = jnp.max(el, axis=-1, keepdims=True)
    i1 = jnp.min(jnp.where(el == v1, lane, big), axis=-1, keepdims=True)
    el2 = jnp.where(lane == i1, neg, el)
    v2 = jnp.max(el2, axis=-1, keepdims=True)
    i2 = jnp.min(jnp.where(el2 == v2, lane, big), axis=-1, keepdims=True)
    e21 = jnp.exp(v2 - v1)
    w1 = p_group / (1.0 + e21)
    w2 = w1 * e21
    route = jnp.where(lane == 0, (i1 - N_GROUPS).astype(F32),
                      jnp.where(lane == 1, (i2 - N_GROUPS).astype(F32),
                                jnp.where(lane == 2, w1, jnp.where(lane == 3, w2, 0.0))))
    route_ref[...] = route


def _merge(x2, ya, yb, gates, woa, wob, wout, lnm, wr, br):
    t, d = x2.shape
    tm = ROW_TILE
    row = lambda w: pl.BlockSpec((tm, w), lambda i: (i, 0))
    return pl.pallas_call(
        _merge_kernel,
        grid=(t // tm,),
        in_specs=[row(d), row(ya.shape[1]), row(yb.shape[1]), row(2 * d), _const_spec(woa.shape),
                  _const_spec(wob.shape), _const_spec(wout.shape), _const_spec(lnm.shape),
                  _const_spec(wr.shape), _const_spec(br.shape)],
        out_specs=(row(d), pl.BlockSpec((tm * SUBLANES, LANES), lambda i: (i, 0)), row(LANES)),
        out_shape=(jax.ShapeDtypeStruct((t, d), F32),
                   jax.ShapeDtypeStruct((t * SUBLANES, LANES), F32),
                   jax.ShapeDtypeStruct((t, LANES), F32)),
        compiler_params=_cparams(("parallel",)),
        name="merge",
    )(x2, ya, yb, gates, woa, wob, wout, lnm, wr, br)


def _store_token_tiles(ref8, tok0, val):
    n = val.shape[0]
    for j in range(SUBLANES):
        ref8[pl.ds(tok0 * SUBLANES + j, n, stride=SUBLANES), :] = val[:, j * LANES:(j + 1) * LANES]


def _load_token_tiles(ref8, tok0, n, dtype):
    return jnp.concatenate(
        [ref8[pl.ds(tok0 * SUBLANES + j, n, stride=SUBLANES), :].astype(dtype)
         for j in range(SUBLANES)], axis=1)


def _token_tile(ref8, tok):
    return ref8.at[pl.ds(pl.multiple_of(tok * SUBLANES, SUBLANES), SUBLANES)]


def _dispatch_kernel(dest_ref, hm8_ref, xb_in_ref, xb8_ref, sem):
    del xb_in_ref
    i = pl.program_id(0)
    n_tok = DISPATCH_TOKENS

    def drain():
        rows = 2 * n_tok * SUBLANES
        pltpu.make_async_copy(hm8_ref.at[pl.ds(0, rows)], xb8_ref.at[pl.ds(0, rows)], sem).wait()

    @pl.when(i > 0)
    def _():
        drain()

    for r in range(n_tok):
        src = _token_tile(hm8_ref, i * n_tok + r)
        for slot in range(2):
            pltpu.make_async_copy(src, _token_tile(xb8_ref, dest_ref[2 * r + slot]), sem).start()

    @pl.when(i == pl.num_programs(0) - 1)
    def _():
        drain()


def _dispatch(dest_flat, hm8, xb_init):
    t = hm8.shape[0] // SUBLANES
    return pl.pallas_call(
        _dispatch_kernel,
        grid=(t // DISPATCH_TOKENS,),
        in_specs=[pl.BlockSpec((2 * DISPATCH_TOKENS,), lambda i: (i,), memory_space=pltpu.SMEM),
                  pl.BlockSpec(memory_space=pl.ANY), pl.BlockSpec(memory_space=pl.ANY)],
        out_specs=pl.BlockSpec(memory_space=pl.ANY),
        out_shape=jax.ShapeDtypeStruct(xb_init.shape, xb_init.dtype),
        scratch_shapes=[pltpu.SemaphoreType.DMA(())],
        input_output_aliases={2: 0},
        compiler_params=_cparams(("arbitrary",)),
        name="dispatch",
    )(dest_flat, hm8, xb_init)


def _experts_kernel(be_ref, nu_ref, xb8_ref, w1_ref, w3_ref, w2_ref, yb8_ref, w1b, w3b, w2b):
    i = pl.program_id(0)
    used = i < nu_ref[0]
    new_expert = jnp.logical_or(i == 0, be_ref[i] != be_ref[jnp.maximum(i - 1, 0)])

    @pl.when(jnp.logical_and(used, new_expert))
    def _():
        w1b[...] = w1_ref[0].astype(BF16)
        w3b[...] = w3_ref[0].astype(BF16)
        w2b[...] = w2_ref[0].astype(BF16)

    @pl.when(used)
    def _():
        x = _load_token_tiles(xb8_ref, 0, MOE_ROWS, BF16)
        hmid = (_silu(_dot(x, w1b[...])) * _dot(x, w3b[...])).astype(BF16)
        _store_token_tiles(yb8_ref, 0, _dot(hmid, w2b[...]))

    @pl.when(jnp.logical_not(used))
    def _():
        yb8_ref[...] = jnp.zeros_like(yb8_ref)


def _experts(block_expert, n_used, xb8, w1, w3, w2):
    nblk = block_expert.shape[0]
    d = w1.shape[1]
    de = w1.shape[2]
    blk_rows = MOE_ROWS * SUBLANES
    rows_in = lambda i, be, nu: (jnp.minimum(i, nu[0] - 1), 0)
    wsel = lambda i, be, nu: (be[i], 0, 0)
    return pl.pallas_call(
        _experts_kernel,
        grid_spec=pltpu.PrefetchScalarGridSpec(
            num_scalar_prefetch=2,
            grid=(nblk,),
            in_specs=[pl.BlockSpec((blk_rows, LANES), rows_in),
                      pl.BlockSpec((1, d, de), wsel), pl.BlockSpec((1, d, de), wsel),
                      pl.BlockSpec((1, de, d), wsel)],
            out_specs=pl.BlockSpec((blk_rows, LANES), lambda i, be, nu: (i, 0)),
            scratch_shapes=[pltpu.VMEM((d, de), BF16), pltpu.VMEM((d, de), BF16),
                            pltpu.VMEM((de, d), BF16)]),
        out_shape=jax.ShapeDtypeStruct((nblk * blk_rows, LANES), F32),
        compiler_params=_cparams(("arbitrary",)),
        name="experts",
    )(block_expert, n_used, xb8, w1, w3, w2)


def _ple_kernel(dest_ref, destn_ref, x1_ref, route_ref, p_ref, ln_ref, wg_ref, wp_ref, yb8_ref,
                out_ref, ya0, ya1, yb0, yb1, sems):
    j = pl.program_id(0)
    last = pl.num_programs(0) - 1
    tm = ROW_TILE
    d = x1_ref.shape[1]

    def start(idx_ref, base, bufs, sem):
        for r in range(tm):
            for slot in range(2):
                pltpu.make_async_copy(_token_tile(yb8_ref, idx_ref[base + 2 * r + slot]),
                                      bufs[slot].at[pl.ds(r * SUBLANES, SUBLANES)], sem).start()

    def wait(bufs, sem):
        for buf in bufs:
            pltpu.make_async_copy(yb8_ref.at[pl.ds(0, tm * SUBLANES)], buf, sem).wait()

    def compute(bufs, rows):
        x2 = (x1_ref[rows, :] + route_ref[rows, 2:3] * _load_token_tiles(bufs[0], 0, tm, F32)
              + route_ref[rows, 3:4] * _load_token_tiles(bufs[1], 0, tm, F32))
        h = (x2 * _rms_scale(x2, d) * ln_ref[...]).astype(BF16)
        pp = _dot(p_ref[rows, :].astype(BF16), wp_ref[...])
        out_ref[rows, :] = x2 + pp * _sigmoid(_dot(h, wg_ref[...]))

    @pl.when(j == 0)
    def _():
        start(dest_ref, 0, (ya0, ya1), sems.at[0])

    wait((ya0, ya1), sems.at[0])
    start(dest_ref, 2 * tm, (yb0, yb1), sems.at[1])
    compute((ya0, ya1), slice(0, tm))

    wait((yb0, yb1), sems.at[1])
    start(destn_ref, 0, (ya0, ya1), sems.at[0])
    compute((yb0, yb1), slice(tm, 2 * tm))

    @pl.when(j == last)
    def _():
        wait((ya0, ya1), sems.at[0])


def _ple(dest_flat, x1, route, p2, ln, wg, wp, yb):
    t, d = x1.shape
    tm = ROW_TILE
    nsteps = t // (2 * tm)
    row = lambda w: pl.BlockSpec((2 * tm, w), lambda j: (j, 0))
    return pl.pallas_call(
        _ple_kernel,
        grid=(nsteps,),
        in_specs=[pl.BlockSpec((4 * tm,), lambda j: (j,), memory_space=pltpu.SMEM),
                  pl.BlockSpec((4 * tm,), lambda j: (jnp.minimum(j + 1, nsteps - 1),),
                               memory_space=pltpu.SMEM),
                  row(d), row(LANES), row(p2.shape[1]), _const_spec(ln.shape),
                  _const_spec(wg.shape), _const_spec(wp.shape),
                  pl.BlockSpec(memory_space=pl.ANY)],
        out_specs=row(d),
        out_shape=jax.ShapeDtypeStruct((t, d), F32),
        scratch_shapes=([pltpu.VMEM((tm * SUBLANES, LANES), F32)] * 4
                        + [pltpu.SemaphoreType.DMA((2,))]),
        compiler_params=_cparams(("arbitrary",)),
        name="ple",
    )(dest_flat, dest_flat, x1, route, p2, ln, wg, wp, yb)


def _head_pad(w, width):
    r = w.shape[0]
    w = w.reshape(r, MLA_HEADS, width)
    return jnp.pad(w, ((0, 0), (0, 0), (0, LANES - width))).reshape(r, MLA_HEADS * LANES)


def _rope_table():
    half = MLA_ROPE // 2
    inv_freq = ROPE_THETA ** (-np.arange(half, dtype=np.float32) / half)
    tab = np.zeros((8, LANES), np.float32)
    tab[0, MLA_NOPE:MLA_NOPE + half] = inv_freq
    tab[0, MLA_NOPE + half:MLA_QK] = inv_freq
    tab[1, MLA_NOPE + half:MLA_QK] = 1.0
    tab[2, MLA_NOPE:MLA_NOPE + half] = -1.0
    return jnp.asarray(tab)


def _route_tables(ids):
    a = ids.shape[0] * 2
    e = ids.reshape(a)
    onehot = (e[:, None] == jnp.arange(N_EXPERTS, dtype=jnp.int32)[None, :]).astype(jnp.int32)
    csum = jnp.cumsum(onehot, axis=0)
    counts = csum[-1]
    rank = jnp.sum(csum * onehot, axis=1) - 1
    pcounts = (counts + MOE_ROWS - 1) // MOE_ROWS * MOE_ROWS
    pends = jnp.cumsum(pcounts)
    pstarts = pends - pcounts
    dest = (jnp.sum(onehot * pstarts[None, :], axis=1) + rank).astype(jnp.int32)
    nblk = (a + N_EXPERTS * (MOE_ROWS - 1)) // MOE_ROWS
    n_used = (pends[-1] // MOE_ROWS).astype(jnp.int32)
    blk_start = jnp.arange(nblk, dtype=jnp.int32) * MOE_ROWS
    block_expert = jnp.searchsorted(pends, jnp.minimum(blk_start, pends[-1] - 1), side='right')
    block_expert = jnp.minimum(block_expert, N_EXPERTS - 1).astype(jnp.int32)
    return dest, block_expert, n_used.reshape(1), nblk


def kernel(x, p, positions, ln_mix, w_in, hg_lb, hg_onorm, w_oA, mla_qa_norm, mla_kva_norm, w_uq,
           w_ukv, q_norm, k_norm, w_oB, w_out, ln_moe, w_rg, b_rg, w_re, b_re, w1, w3, w2, ln_ple,
           w_ple_gate, w_ple_proj):
    b, s, d = x.shape
    t = b * s
    depth = w_in.shape[0]
    lb_all = jnp.cumsum(jax.nn.softmax(hg_lb.astype(F32), axis=1), axis=1)
    posf = positions.astype(F32).reshape(t, 1)
    rope_tab = _rope_table()
    xc = x.reshape(t, d)

    for layer in range(depth):
        wi = w_in[layer]
        n_hg = 3 * HG_F + 2 * HG_V
        n_mla = MLA_Q_RANK + MLA_KV_RANK + MLA_ROPE
        mla_w = -(-n_mla // LANES) * LANES
        whg = wi[:, :n_hg].astype(BF16)
        wmla = jnp.pad(wi[:, n_hg:n_hg + n_mla], ((0, 0), (0, mla_w - n_mla))).astype(BF16)
        wgate = wi[:, n_hg + n_mla:].astype(BF16)
        lb = lb_all[:, layer, :]

        qs, lf, kk, v, og, mla_in, gates = _inproj(
            xc, ln_mix[layer][None, :], whg, wmla, wgate, lb, mla_qa_norm[layer][None, :],
            mla_kva_norm[layer][None, :])

        r3 = lambda a: a.reshape(b, s, a.shape[-1])
        ya = _hgrn(r3(qs), r3(lf), r3(kk), r3(v), r3(og), hg_onorm[layer][None, :]).reshape(t, HG_V)

        kv_w = MLA_NOPE + MLA_VDIM
        wkv = w_ukv[layer].reshape(MLA_KV_RANK, MLA_HEADS, kv_w)
        ckr_w = mla_w - MLA_Q_RANK
        wq = _head_pad(w_uq[layer], MLA_QK).astype(BF16)
        wk_nope = _head_pad(wkv[:, :, :MLA_NOPE].reshape(MLA_KV_RANK, MLA_HEADS * MLA_NOPE), MLA_NOPE)
        place = np.zeros((ckr_w - MLA_KV_RANK, MLA_HEADS, LANES), np.float32)
        for j in range(MLA_ROPE):
            place[j, :, MLA_NOPE + j] = 1.0
        wk = jnp.concatenate([wk_nope, jnp.asarray(place.reshape(ckr_w - MLA_KV_RANK, -1))],
                             axis=0).astype(BF16)
        wv = jnp.pad(wkv[:, :, MLA_NOPE:].reshape(MLA_KV_RANK, MLA_HEADS * MLA_VDIM),
                     ((0, ckr_w - MLA_KV_RANK), (0, 0))).astype(BF16)
        gq = jnp.pad(q_norm[layer], (0, LANES - MLA_QK))[None, :]
        gk = jnp.pad(k_norm[layer], (0, LANES - MLA_QK))[None, :]
        qh, kh, vh = _mlaprep(mla_in, posf, wq, wk, wv, gq, gk, rope_tab)
        yb = _attn(r3(qh), r3(kh), r3(vh)).reshape(t, MLA_HEADS * MLA_VDIM)

        wr = jnp.pad(jnp.concatenate([w_rg[layer], w_re[layer]], axis=1),
                     ((0, 0), (0, LANES - N_GROUPS - N_EXPERTS)))
        br = jnp.pad(jnp.concatenate([b_rg[layer], b_re[layer]]),
                     (0, LANES - N_GROUPS - N_EXPERTS))[None, :]
        x1, hm8, route = _merge(xc, ya, yb, gates, w_oA[layer].astype(BF16), w_oB[layer].astype(BF16),
                               w_out[layer].astype(BF16), ln_moe[layer][None, :], wr, br)

        ids = route[:, 0:2].astype(jnp.int32)
        dest, block_expert, n_used, nblk = _route_tables(ids)
        xb8 = _dispatch(dest, hm8, jnp.zeros((nblk * MOE_ROWS * SUBLANES, LANES), F32))
        yexp = _experts(block_expert, n_used, xb8, w1[layer], w3[layer], w2[layer])

        xc = _ple(dest, x1, route, p[layer].reshape(t, -1), ln_ple[layer][None, :],
                  w_ple_gate[layer].astype(BF16), w_ple_proj[layer].astype(BF16), yexp)
    return xc.reshape(b, s, d)
```

```python
import functools

import numpy as np
import jax
import jax.numpy as jnp
from jax import lax
from jax.experimental import pallas as pl
from jax.experimental.pallas import tpu as pltpu

F32 = jnp.float32
BF16 = jnp.bfloat16

HG_HEADS = 4
HG_KDIM = 128
HG_VDIM = 128
HG_F = HG_HEADS * HG_KDIM
HG_V = HG_HEADS * HG_VDIM
MLA_HEADS = 8
MLA_NOPE = 64
MLA_ROPE = 32
MLA_VDIM = 64
MLA_QK = MLA_NOPE + MLA_ROPE
MLA_Q_RANK = 256
MLA_KV_RANK = 128
ROPE_THETA = 10000.0
N_GROUPS = 8
EXPERTS_PER_GROUP = 8
N_EXPERTS = N_GROUPS * EXPERTS_PER_GROUP
D_EXPERT = 256
EPS = 1e-6

LANES = 128
SUBLANES = 8
VMEM_LIMIT_BYTES = 56 * 1024 * 1024

ROW_TILE = 256
MERGE_ROW_TILE = 512
INPROJ_ROW_TILE = 256
HG_CHUNK = 64
HG_HEADS_PER_STEP = 2
HG_CHUNKS_PER_MATMUL = 4
ATTN_Q_TILE = 256
MOE_ROWS = 256
DISPATCH_TOKENS = 512
HG_SAFE_LOG_DECAY = -60.0


def _cparams(sem):
    return pltpu.CompilerParams(dimension_semantics=sem, vmem_limit_bytes=VMEM_LIMIT_BYTES)


def _const_spec(shape):
    nd = len(shape)
    return pl.BlockSpec(shape, lambda *_: (0,) * nd)


def _sigmoid(x):
    return 1.0 / (1.0 + jnp.exp(-x))


def _silu(x):
    return x * _sigmoid(x)


def _rms_scale(x, n):
    return lax.rsqrt(jnp.sum(x * x, axis=-1, keepdims=True) * (1.0 / n) + EPS)


def _dot(a, b):
    return jnp.dot(a, b, preferred_element_type=F32)


def _dot_nt(a, b):
    return lax.dot_general(a, b, (((1,), (1,)), ((), ())), preferred_element_type=F32)


def _dot_tn(a, b):
    return lax.dot_general(a, b, (((0,), (0,)), ((), ())), preferred_element_type=F32)


def _inproj_kernel(x_ref, g_ref, whg_ref, wmla_ref, wgate_ref, lb_ref, nq_ref, nkv_ref,
                   qs_ref, lf_ref, kk_ref, v_ref, og_ref, mla_ref, gates_ref):
    x = x_ref[...]
    d = x.shape[-1]
    h = (x * _rms_scale(x, d) * g_ref[...]).astype(BF16)

    qs_ref[...] = _silu(_dot(h, whg_ref[:, 0:HG_F])).astype(BF16)
    for direction in range(2):
        cols = slice(HG_F * (1 + direction), HG_F * (2 + direction))
        out_cols = slice(HG_F * direction, HG_F * (direction + 1))
        lb = lb_ref[direction:direction + 1, :]
        f = lb + (1.0 - lb) * _sigmoid(_dot(h, whg_ref[:, cols]))
        lf_ref[:, out_cols] = jnp.log(f)
        kk_ref[:, out_cols] = (1.0 - f).astype(BF16)
    v_ref[...] = _dot(h, whg_ref[:, 3 * HG_F:3 * HG_F + HG_V]).astype(BF16)
    og_ref[...] = _silu(_dot(h, whg_ref[:, 3 * HG_F + HG_V:3 * HG_F + 2 * HG_V])).astype(BF16)

    zm = _dot(h, wmla_ref[...])
    cq = zm[:, 0:MLA_Q_RANK]
    mla_ref[:, 0:MLA_Q_RANK] = (cq * _rms_scale(cq, MLA_Q_RANK) * nq_ref[...]).astype(BF16)
    ckv = zm[:, MLA_Q_RANK:MLA_Q_RANK + MLA_KV_RANK]
    mla_ref[:, MLA_Q_RANK:MLA_Q_RANK + MLA_KV_RANK] = (
        ckv * _rms_scale(ckv, MLA_KV_RANK) * nkv_ref[...]).astype(BF16)
    mla_ref[:, MLA_Q_RANK + MLA_KV_RANK:] = zm[:, MLA_Q_RANK + MLA_KV_RANK:].astype(BF16)

    for half in range(2):
        cols = slice(d * half, d * (half + 1))
        gates_ref[:, cols] = _sigmoid(_dot(h, wgate_ref[:, cols])).astype(BF16)


def _inproj(x2, ln_mix, whg, wmla, wgate, lb, nq, nkv):
    t, d = x2.shape
    tm = INPROJ_ROW_TILE
    mla_w = wmla.shape[1]
    row = lambda w: pl.BlockSpec((tm, w), lambda i: (i, 0))
    out_shape = (
        jax.ShapeDtypeStruct((t, HG_F), BF16),
        jax.ShapeDtypeStruct((t, 2 * HG_F), F32),
        jax.ShapeDtypeStruct((t, 2 * HG_F), BF16),
        jax.ShapeDtypeStruct((t, HG_V), BF16),
        jax.ShapeDtypeStruct((t, HG_V), BF16),
        jax.ShapeDtypeStruct((t, mla_w), BF16),
        jax.ShapeDtypeStruct((t, 2 * d), BF16),
    )
    return pl.pallas_call(
        _inproj_kernel,
        grid=(t // tm,),
        in_specs=[row(d), _const_spec((1, d)), _const_spec(whg.shape), _const_spec(wmla.shape),
                  _const_spec(wgate.shape), _const_spec(lb.shape), _const_spec(nq.shape),
                  _const_spec(nkv.shape)],
        out_specs=(row(HG_F), row(2 * HG_F), row(2 * HG_F), row(HG_V), row(HG_V), row(mla_w),
                   row(2 * d)),
        out_shape=out_shape,
        compiler_params=_cparams(("parallel",)),
        name="inproj",
    )(x2, ln_mix, whg, wmla, wgate, lb, nq, nkv)


def _hgrn_kernel(qs_ref, lff_ref, lfb_ref, kf_ref, kb_ref, v_ref, og_ref, onorm_ref, out_ref,
                 state_ref, ofw_ref, obw_ref, qt_scr, u_scr, dec_scr, g_scr, k_scr, v_scr):
    c = HG_CHUNK
    s_len = qs_ref.shape[1]
    n_chunks = s_len // c
    hb = HG_HEADS_PER_STEP
    kd = HG_KDIM

    row = lax.broadcasted_iota(jnp.int32, (c, c), 0)
    col = lax.broadcasted_iota(jnp.int32, (c, c), 1)
    masks = (row >= col, row <= col)
    rowc = lax.broadcasted_iota(jnp.int32, (c, kd), 0)
    o_refs = (ofw_ref, obw_ref)

    def cum_log_decay(lf, direction):
        n = lf.shape[0]
        in_chunk = lax.broadcasted_iota(jnp.int32, lf.shape, 0) % c
        g = lf
        k = 1
        while k < c:
            if direction == 0:
                g = g + jnp.where(in_chunk >= k, pltpu.roll(g, k, 0), 0.0)
            else:
                g = g + jnp.where(in_chunk < c - k, pltpu.roll(g, n - k, 0), 0.0)
            k *= 2
        return g

    def load(j):
        chains = []
        for hh in range(hb):
            lanes = slice(hh * kd, (hh + 1) * kd)
            for direction in range(2):
                chunk = j if direction == 0 else n_chunks - 1 - j
                rows = pl.ds(pl.multiple_of(chunk * c, c), c)
                g = cum_log_decay((lff_ref, lfb_ref)[direction][0, rows, lanes], direction)
                total = g[c - 1:c, :] if direction == 0 else g[0:1, :]
                chains.append(dict(
                    idx=hh * 2 + direction, direction=direction, rows=rows, lanes=lanes, g=g,
                    total=total,
                    q=qs_ref[0, rows, lanes].astype(F32),
                    k=(kf_ref, kb_ref)[direction][0, rows, lanes].astype(F32),
                    v=v_ref[0, rows, lanes]))
        return chains

    def finish(ch, o, kdec_t_v_scaled):
        (ofw_ref, obw_ref)[ch["direction"]][ch["rows"], ch["lanes"]] = o
        state_ref[ch["idx"]] = kdec_t_v_scaled

    def fast(ch):
        g = ch["g"]
        qt = (ch["q"] * jnp.exp(g)).astype(BF16)
        kt = (ch["k"] * jnp.exp(-g)).astype(BF16)
        st = state_ref[ch["idx"]]
        sc = jnp.where(masks[ch["direction"]], _dot_nt(qt, kt), 0.0)
        o = _dot_nt(qt, st.astype(BF16)) + _dot(sc.astype(BF16), ch["v"])
        finish(ch, o, (st + _dot_tn(ch["v"], kt)) * jnp.exp(ch["total"]))

    def robust(ch):
        g = ch["g"]
        direction = ch["direction"]
        st = state_ref[ch["idx"]]
        o0 = _dot_nt((ch["q"] * jnp.exp(g)).astype(BF16), st.astype(BF16))
        slot = ch["idx"]
        g_scr[slot] = g
        k_scr[slot] = ch["k"]
        v_scr[slot] = ch["v"].astype(F32)
        q = ch["q"]

        def body(s, acc):
            g_s = g_scr[slot, pl.ds(s, 1), :]
            seen = (rowc >= s) if direction == 0 else (rowc <= s)
            decay = jnp.where(seen, jnp.exp(jnp.minimum(g - g_s, 0.0)), 0.0)
            a = jnp.sum(q * decay * k_scr[slot, pl.ds(s, 1), :], axis=-1, keepdims=True)
            return acc + a * v_scr[slot, pl.ds(s, 1), :]

        o = lax.fori_loop(0, c, body, o0)
        kdec = (ch["k"] * jnp.exp(ch["total"] - g)).astype(BF16)
        finish(ch, o, st * jnp.exp(ch["total"]) + _dot_tn(ch["v"], kdec))

    def step(j, carry):
        chains = load(j)
        lowest = chains[0]["total"]
        for ch in chains[1:]:
            lowest = jnp.minimum(lowest, ch["total"])
        safe = jnp.min(lowest) >= HG_SAFE_LOG_DECAY

        @pl.when(safe)
        def _():
            for ch in chains:
                fast(ch)

        @pl.when(jnp.logical_not(safe))
        def _():
            for ch in chains:
                robust(ch)

        return carry

    grp = HG_CHUNKS_PER_MATMUL if n_chunks % HG_CHUNKS_PER_MATMUL == 0 else 1
    gr = grp * c
    n_groups = n_chunks // grp
    grow = lax.broadcasted_iota(jnp.int32, (gr, gr), 0)
    gcol = lax.broadcasted_iota(jnp.int32, (gr, gr), 1)
    same_chunk = (grow // c) == (gcol // c)
    gmasks = (same_chunk & (grow >= gcol), same_chunk & (grow <= gcol))
    chunk_of_row = lax.broadcasted_iota(jnp.int32, (gr, kd), 0) // c

    def block_diag(a):
        return jnp.concatenate([jnp.where(chunk_of_row == i, a, jnp.zeros_like(a))
                                for i in range(grp)], axis=1)

    def phase_a(gi, lowest):
        rows = pl.ds(pl.multiple_of(gi * gr, gr), gr)
        for hh in range(hb):
            lanes = slice(hh * kd, (hh + 1) * kd)
            v = v_ref[0, rows, lanes]
            v_bd = block_diag(v)
            q = qs_ref[0, rows, lanes].astype(F32)
            for direction in range(2):
                idx = hh * 2 + direction
                g = cum_log_decay((lff_ref, lfb_ref)[direction][0, rows, lanes], direction)
                k = (kf_ref, kb_ref)[direction][0, rows, lanes].astype(F32)
                qt = (q * jnp.exp(g)).astype(BF16)
                kt = (k * jnp.exp(-g)).astype(BF16)
                qt_scr[idx, rows, :] = qt
                sc = jnp.where(gmasks[direction], _dot_nt(qt, kt), 0.0)
                o_refs[direction][rows, lanes] = _dot(sc.astype(BF16), v)
                u_scr[idx, pl.ds(gi * grp, grp)] = _dot_tn(v_bd, kt).reshape(grp, HG_VDIM, kd)
                for i in range(grp):
                    edge = i * c + (c - 1 if direction == 0 else 0)
                    total = g[edge:edge + 1, :]
                    lowest = jnp.minimum(lowest, total)
                    dec_scr[idx, pl.ds(gi * grp + i, 1), :] = jnp.exp(total)
        return lowest

    lowest = lax.fori_loop(0, n_groups, phase_a, jnp.zeros((1, kd), F32))
    all_safe = jnp.min(lowest) >= HG_SAFE_LOG_DECAY
    state_ref[...] = jnp.zeros_like(state_ref)

    @pl.when(all_safe)
    def _():
        def phase_b(j, carry):
            for hh in range(hb):
                lanes = slice(hh * kd, (hh + 1) * kd)
                for direction in range(2):
                    idx = hh * 2 + direction
                    gi = j if direction == 0 else n_groups - 1 - j
                    rows = pl.ds(pl.multiple_of(gi * gr, gr), gr)
                    st = state_ref[idx]
                    seen = [None] * grp
                    for i in (range(grp) if direction == 0 else reversed(range(grp))):
                        ci = gi * grp + i
                        seen[i] = st.astype(BF16)
                        st = (st + u_scr[idx, ci]) * dec_scr[idx, pl.ds(ci, 1), :]
                    state_ref[idx] = st
                    o_refs[direction][rows, lanes] += _dot_nt(
                        block_diag(qt_scr[idx, rows, :]), jnp.concatenate(seen, axis=1))
            return carry

        lax.fori_loop(0, n_groups, phase_b, 0)

    @pl.when(jnp.logical_not(all_safe))
    def _():
        lax.fori_loop(0, n_chunks, step, 0)

    blk = 256 if s_len % 256 == 0 else c

    def epilogue(i, carry):
        rows = pl.ds(pl.multiple_of(i * blk, blk), blk)
        for hh in range(hb):
            lanes = slice(hh * kd, (hh + 1) * kd)
            o = ofw_ref[rows, lanes] + obw_ref[rows, lanes]
            y = o * _rms_scale(o, HG_VDIM) * onorm_ref[...]
            out_ref[0, rows, lanes] = (y * og_ref[0, rows, lanes].astype(F32)).astype(BF16)
        return carry

    lax.fori_loop(0, s_len // blk, epilogue, 0)


def _hgrn(qs, lf, kk, v, og, onorm):
    b, s, _ = qs.shape
    hb = HG_HEADS_PER_STEP
    w = hb * HG_KDIM
    n_hsteps = HG_HEADS // hb
    fwd = pl.BlockSpec((1, s, w), lambda bi, hi: (bi, 0, hi))
    bwd = pl.BlockSpec((1, s, w), lambda bi, hi: (bi, 0, n_hsteps + hi))
    return pl.pallas_call(
        _hgrn_kernel,
        grid=(b, n_hsteps),
        in_specs=[fwd, fwd, bwd, fwd, bwd, fwd, fwd, _const_spec(onorm.shape)],
        out_specs=fwd,
        out_shape=jax.ShapeDtypeStruct((b, s, HG_V), BF16),
        scratch_shapes=[
            pltpu.VMEM((2 * hb, HG_VDIM, HG_KDIM), F32),
            pltpu.VMEM((s, w), F32),
            pltpu.VMEM((s, w), F32),
            pltpu.VMEM((2 * hb, s, HG_KDIM), BF16),
            pltpu.VMEM((2 * hb, s // HG_CHUNK, HG_VDIM, HG_KDIM), F32),
            pltpu.VMEM((2 * hb, max(s // HG_CHUNK, 8), HG_KDIM), F32),
            pltpu.VMEM((2 * hb, HG_CHUNK, HG_KDIM), F32),
            pltpu.VMEM((2 * hb, HG_CHUNK, HG_KDIM), F32),
            pltpu.VMEM((2 * hb, HG_CHUNK, HG_VDIM), F32),
        ],
        compiler_params=_cparams(("parallel", "parallel")),
        name="hgrn",
    )(qs, lf, lf, kk, kk, v, og, onorm)


def _mlaprep_kernel(mla_ref, pos_ref, wq_ref, wk_ref, wv_ref, gq_ref, gk_ref, rope_ref,
                    q_ref, k_ref, v_ref):
    cq = mla_ref[:, 0:MLA_Q_RANK]
    ckr = mla_ref[:, MLA_Q_RANK:]
    ang = pos_ref[...] * rope_ref[0:1, :]
    cos = jnp.cos(ang)
    sin = jnp.sin(ang)
    sin_lo = sin * rope_ref[1:2, :]
    sin_hi = sin * rope_ref[2:3, :]
    half = MLA_ROPE // 2

    def norm_rope(x, gain, scale):
        y = x * (_rms_scale(x, MLA_QK) * scale) * gain
        return y * cos + pltpu.roll(y, half, 1) * sin_lo + pltpu.roll(y, LANES - half, 1) * sin_hi

    qm = _dot(cq, wq_ref[...])
    km = _dot(ckr, wk_ref[...])
    for hd in range(MLA_HEADS):
        lanes = slice(hd * LANES, (hd + 1) * LANES)
        q_ref[:, lanes] = norm_rope(qm[:, lanes], gq_ref[...], MLA_QK ** -0.5).astype(BF16)
        k_ref[:, lanes] = norm_rope(km[:, lanes], gk_ref[...], 1.0).astype(BF16)
    v_ref[...] = _dot(ckr, wv_ref[...]).astype(BF16)


def _mlaprep(mla_in, posf, wq, wk, wv, gq, gk, rope_tab):
    t = mla_in.shape[0]
    tm = ROW_TILE
    row = lambda w: pl.BlockSpec((tm, w), lambda i: (i, 0))
    hw = MLA_HEADS * LANES
    vw = MLA_HEADS * MLA_VDIM
    return pl.pallas_call(
        _mlaprep_kernel,
        grid=(t // tm,),
        in_specs=[row(mla_in.shape[1]), row(1), _const_spec(wq.shape), _const_spec(wk.shape),
                  _const_spec(wv.shape), _const_spec(gq.shape), _const_spec(gk.shape),
                  _const_spec(rope_tab.shape)],
        out_specs=(row(hw), row(hw), row(vw)),
        out_shape=(jax.ShapeDtypeStruct((t, hw), BF16), jax.ShapeDtypeStruct((t, hw), BF16),
                   jax.ShapeDtypeStruct((t, vw), BF16)),
        compiler_params=_cparams(("parallel",)),
        name="mlaprep",
    )(mla_in, posf, wq, wk, wv, gq, gk, rope_tab)


def _attn_kernel(q_ref, k_ref, v_ref, o_ref):
    for j in range(2):
        lanes = slice(j * LANES, (j + 1) * LANES)
        s = _dot_nt(q_ref[0, :, lanes], k_ref[0, :, lanes])
        p = jnp.exp(s - jnp.max(s, axis=-1, keepdims=True))
        l = jnp.sum(p, axis=-1, keepdims=True)
        vl = slice(j * MLA_VDIM, (j + 1) * MLA_VDIM)
        o = _dot(p.astype(BF16), v_ref[0, :, vl])
        o_ref[0, :, vl] = (o / l).astype(BF16)


def _attn(q, k, v):
    b, s, _ = q.shape
    tq = min(ATTN_Q_TILE, s)
    return pl.pallas_call(
        _attn_kernel,
        grid=(b, MLA_HEADS // 2, s // tq),
        in_specs=[pl.BlockSpec((1, tq, 2 * LANES), lambda bi, hp, i: (bi, i, hp)),
                  pl.BlockSpec((1, s, 2 * LANES), lambda bi, hp, i: (bi, 0, hp)),
                  pl.BlockSpec((1, s, 2 * MLA_VDIM), lambda bi, hp, i: (bi, 0, hp))],
        out_specs=pl.BlockSpec((1, tq, 2 * MLA_VDIM), lambda bi, hp, i: (bi, i, hp)),
        out_shape=jax.ShapeDtypeStruct((b, s, MLA_HEADS * MLA_VDIM), BF16),
        compiler_params=_cparams(("parallel", "parallel", "arbitrary")),
        name="attn",
    )(q, k, v)


def _merge_kernel(x_ref, ya_ref, yb_ref, gates_ref, woa_ref, wob_ref, wout_ref, lnm_ref, wr_ref,
                  br_ref, x1_ref, hm8_ref, route_ref):
    d = x_ref.shape[-1]
    y_a = _dot(ya_ref[...], woa_ref[...])
    y_b = _dot(yb_ref[...], wob_ref[...])
    merged = gates_ref[:, 0:d].astype(F32) * y_a + gates_ref[:, d:2 * d].astype(F32) * y_b
    x1 = x_ref[...] + _dot(merged.astype(BF16), wout_ref[...])
    x1_ref[...] = x1
    hm = x1 * _rms_scale(x1, d) * lnm_ref[...]
    _store_token_tiles(hm8_ref, 0, hm)

    hm_hi = hm.astype(BF16)
    hm_lo = (hm - hm_hi.astype(F32)).astype(BF16)
    hh = _dot(hm_hi, wr_ref[...])
    logits = (hh[:, 0:LANES] + hh[:, LANES:2 * LANES] + _dot(hm_lo, wr_ref[:, 0:LANES])
              + br_ref[...])
    lane = lax.broadcasted_iota(jnp.int32, logits.shape, 1)
    neg = -jnp.inf
    big = jnp.int32(2 ** 30)
    is_group = lane < N_GROUPS
    gl = jnp.where(is_group, logits, neg)
    gmax = jnp.max(gl, axis=-1, keepdims=True)
    p_group = 1.0 / jnp.sum(jnp.where(is_group, jnp.exp(gl - gmax), 0.0), axis=-1, keepdims=True)
    g_sel = jnp.min(jnp.where(gl == gmax, lane, big), axis=-1, keepdims=True)
    lo = N_GROUPS + g_sel * EXPERTS_PER_GROUP
    in_group = (lane >= lo) & (lane < lo + EXPERTS_PER_GROUP)
    el = jnp.where(in_group, logits, neg)
    v1 = jnp.max(el, axis=-1, keepdims=True)
    i1 = jnp.min(jnp.where(el == v1, lane, big), axis=-1, keepdims=True)
    el2 = jnp.where(lane == i1, neg, el)
    v2 = jnp.max(el2, axis=-1, keepdims=True)
    i2 = jnp.min(jnp.where(el2 == v2, lane, big), axis=-1, keepdims=True)
    e21 = jnp.exp(v2 - v1)
    w1 = p_group / (1.0 + e21)
    w2 = w1 * e21
    route = jnp.where(lane == 0, (i1 - N_GROUPS).astype(F32),
                      jnp.where(lane == 1, (i2 - N_GROUPS).astype(F32),
                                jnp.where(lane == 2, w1, jnp.where(lane == 3, w2, 0.0))))
    route_ref[...] = route


def _merge(x2, ya, yb, gates, woa, wob, wout, lnm, wr, br):
    t, d = x2.shape
    tm = MERGE_ROW_TILE
    row = lambda w: pl.BlockSpec((tm, w), lambda i: (i, 0))
    return pl.pallas_call(
        _merge_kernel,
        grid=(t // tm,),
        in_specs=[row(d), row(ya.shape[1]), row(yb.shape[1]), row(2 * d), _const_spec(woa.shape),
                  _const_spec(wob.shape), _const_spec(wout.shape), _const_spec(lnm.shape),
                  _const_spec(wr.shape), _const_spec(br.shape)],
        out_specs=(row(d), pl.BlockSpec((tm * SUBLANES, LANES), lambda i: (i, 0)), row(LANES)),
        out_shape=(jax.ShapeDtypeStruct((t, d), F32),
                   jax.ShapeDtypeStruct((t * SUBLANES, LANES), F32),
                   jax.ShapeDtypeStruct((t, LANES), F32)),
        compiler_params=_cparams(("parallel",)),
        name="merge",
    )(x2, ya, yb, gates, woa, wob, wout, lnm, wr, br)


def _store_token_tiles(ref8, tok0, val):
    n = val.shape[0]
    for j in range(SUBLANES):
        ref8[pl.ds(tok0 * SUBLANES + j, n, stride=SUBLANES), :] = val[:, j * LANES:(j + 1) * LANES]


def _load_token_tiles(ref8, tok0, n, dtype):
    return jnp.concatenate(
        [ref8[pl.ds(tok0 * SUBLANES + j, n, stride=SUBLANES), :].astype(dtype)
         for j in range(SUBLANES)], axis=1)


def _token_tile(ref8, tok):
    return ref8.at[pl.ds(pl.multiple_of(tok * SUBLANES, SUBLANES), SUBLANES)]


def _dispatch_kernel(dest_ref, hm8_ref, xb_in_ref, xb8_ref, sem):
    del xb_in_ref
    n_tok = DISPATCH_TOKENS
    for r in range(n_tok):
        src = hm8_ref.at[pl.ds(r * SUBLANES, SUBLANES)]
        for slot in range(2):
            pltpu.make_async_copy(src, _token_tile(xb8_ref, dest_ref[2 * r + slot]), sem).start()
    for _ in range(2):
        pltpu.make_async_copy(hm8_ref, xb8_ref.at[pl.ds(0, n_tok * SUBLANES)], sem).wait()


def _dispatch(dest_flat, hm8, xb_init):
    t = hm8.shape[0] // SUBLANES
    return pl.pallas_call(
        _dispatch_kernel,
        grid=(t // DISPATCH_TOKENS,),
        in_specs=[pl.BlockSpec((2 * DISPATCH_TOKENS,), lambda i: (i,), memory_space=pltpu.SMEM),
                  pl.BlockSpec((DISPATCH_TOKENS * SUBLANES, LANES), lambda i: (i, 0)),
                  pl.BlockSpec(memory_space=pl.ANY)],
        out_specs=pl.BlockSpec(memory_space=pl.ANY),
        out_shape=jax.ShapeDtypeStruct(xb_init.shape, xb_init.dtype),
        scratch_shapes=[pltpu.SemaphoreType.DMA(())],
        input_output_aliases={2: 0},
        compiler_params=_cparams(("arbitrary",)),
        name="dispatch",
    )(dest_flat, hm8, xb_init)


def _experts_kernel(be_ref, nu_ref, xb8_ref, w1_ref, w3_ref, w2_ref, yb8_ref, w1b, w3b, w2b):
    i = pl.program_id(0)
    used = i < nu_ref[0]
    new_expert = jnp.logical_or(i == 0, be_ref[i] != be_ref[jnp.maximum(i - 1, 0)])

    @pl.when(jnp.logical_and(used, new_expert))
    def _():
        w1b[...] = w1_ref[0].astype(BF16)
        w3b[...] = w3_ref[0].astype(BF16)
        w2b[...] = w2_ref[0].astype(BF16)

    @pl.when(used)
    def _():
        x = _load_token_tiles(xb8_ref, 0, MOE_ROWS, BF16)
        hmid = (_silu(_dot(x, w1b[...])) * _dot(x, w3b[...])).astype(BF16)
        _store_token_tiles(yb8_ref, 0, _dot(hmid, w2b[...]))

    @pl.when(jnp.logical_not(used))
    def _():
        yb8_ref[...] = jnp.zeros_like(yb8_ref)


def _experts(block_expert, n_used, xb8, w1, w3, w2):
    nblk = block_expert.shape[0]
    d = w1.shape[1]
    de = w1.shape[2]
    blk_rows = MOE_ROWS * SUBLANES
    rows_in = lambda i, be, nu: (jnp.minimum(i, nu[0] - 1), 0)
    wsel = lambda i, be, nu: (be[i], 0, 0)
    return pl.pallas_call(
        _experts_kernel,
        grid_spec=pltpu.PrefetchScalarGridSpec(
            num_scalar_prefetch=2,
            grid=(nblk,),
            in_specs=[pl.BlockSpec((blk_rows, LANES), rows_in),
                      pl.BlockSpec((1, d, de), wsel), pl.BlockSpec((1, d, de), wsel),
                      pl.BlockSpec((1, de, d), wsel)],
            out_specs=pl.BlockSpec((blk_rows, LANES), lambda i, be, nu: (i, 0)),
            scratch_shapes=[pltpu.VMEM((d, de), BF16), pltpu.VMEM((d, de), BF16),
                            pltpu.VMEM((de, d), BF16)]),
        out_shape=jax.ShapeDtypeStruct((nblk * blk_rows, LANES), F32),
        compiler_params=_cparams(("arbitrary",)),
        name="experts",
    )(block_expert, n_used, xb8, w1, w3, w2)


def _ple_kernel(dest_ref, destn_ref, x1_ref, route_ref, p_ref, ln_ref, wg_ref, wp_ref, yb8_ref,
                out_ref, ya0, ya1, yb0, yb1, sems):
    j = pl.program_id(0)
    last = pl.num_programs(0) - 1
    tm = ROW_TILE
    d = x1_ref.shape[1]

    def start(idx_ref, base, bufs, sem):
        for r in range(tm):
            for slot in range(2):
                pltpu.make_async_copy(_token_tile(yb8_ref, idx_ref[base + 2 * r + slot]),
                                      bufs[slot].at[pl.ds(r * SUBLANES, SUBLANES)], sem).start()

    def wait(bufs, sem):
        for buf in bufs:
            pltpu.make_async_copy(yb8_ref.at[pl.ds(0, tm * SUBLANES)], buf, sem).wait()

    def compute(bufs, rows):
        x2 = (x1_ref[rows, :] + route_ref[rows, 2:3] * _load_token_tiles(bufs[0], 0, tm, F32)
              + route_ref[rows, 3:4] * _load_token_tiles(bufs[1], 0, tm, F32))
        h = (x2 * _rms_scale(x2, d) * ln_ref[...]).astype(BF16)
        pp = _dot(p_ref[rows, :].astype(BF16), wp_ref[...])
        out_ref[rows, :] = x2 + pp * _sigmoid(_dot(h, wg_ref[...]))

    @pl.when(j == 0)
    def _():
        start(dest_ref, 0, (ya0, ya1), sems.at[0])

    wait((ya0, ya1), sems.at[0])
    start(dest_ref, 2 * tm, (yb0, yb1), sems.at[1])
    compute((ya0, ya1), slice(0, tm))

    wait((yb0, yb1), sems.at[1])
    start(destn_ref, 0, (ya0, ya1), sems.at[0])
    compute((yb0, yb1), slice(tm, 2 * tm))

    @pl.when(j == last)
    def _():
        wait((ya0, ya1), sems.at[0])


def _ple(dest_flat, x1, route, p2, ln, wg, wp, yb):
    t, d = x1.shape
    tm = ROW_TILE
    nsteps = t // (2 * tm)
    row = lambda w: pl.BlockSpec((2 * tm, w), lambda j: (j, 0))
    return pl.pallas_call(
        _ple_kernel,
        grid=(nsteps,),
        in_specs=[pl.BlockSpec((4 * tm,), lambda j: (j,), memory_space=pltpu.SMEM),
                  pl.BlockSpec((4 * tm,), lambda j: (jnp.minimum(j + 1, nsteps - 1),),
                               memory_space=pltpu.SMEM),
                  row(d), row(LANES), row(p2.shape[1]), _const_spec(ln.shape),
                  _const_spec(wg.shape), _const_spec(wp.shape),
                  pl.BlockSpec(memory_space=pl.ANY)],
        out_specs=row(d),
        out_shape=jax.ShapeDtypeStruct((t, d), F32),
        scratch_shapes=([pltpu.VMEM((tm * SUBLANES, LANES), F32)] * 4
                        + [pltpu.SemaphoreType.DMA((2,))]),
        compiler_params=_cparams(("arbitrary",)),
        name="ple",
    )(dest_flat, dest_flat, x1, route, p2, ln, wg, wp, yb)


def _head_pad(w, width):
    r = w.shape[0]
    w = w.reshape(r, MLA_HEADS, width)
    return jnp.pad(w, ((0, 0), (0, 0), (0, LANES - width))).reshape(r, MLA_HEADS * LANES)


def _rope_table():
    half = MLA_ROPE // 2
    inv_freq = ROPE_THETA ** (-np.arange(half, dtype=np.float32) / half)
    tab = np.zeros((8, LANES), np.float32)
    tab[0, MLA_NOPE:MLA_NOPE + half] = inv_freq
    tab[0, MLA_NOPE + half:MLA_QK] = inv_freq
    tab[1, MLA_NOPE + half:MLA_QK] = 1.0
    tab[2, MLA_NOPE:MLA_NOPE + half] = -1.0
    return jnp.asarray(tab)


def _route_tables(ids):
    a = ids.shape[0] * 2
    e = ids.reshape(a)
    onehot = (e[:, None] == jnp.arange(N_EXPERTS, dtype=jnp.int32)[None, :]).astype(jnp.int32)
    csum = jnp.cumsum(onehot, axis=0)
    counts = csum[-1]
    rank = jnp.sum(csum * onehot, axis=1) - 1
    pcounts = (counts + MOE_ROWS - 1) // MOE_ROWS * MOE_ROWS
    pends = jnp.cumsum(pcounts)
    pstarts = pends - pcounts
    dest = (jnp.sum(onehot * pstarts[None, :], axis=1) + rank).astype(jnp.int32)
    nblk = (a + N_EXPERTS * (MOE_ROWS - 1)) // MOE_ROWS
    n_used = (pends[-1] // MOE_ROWS).astype(jnp.int32)
    blk_start = jnp.arange(nblk, dtype=jnp.int32) * MOE_ROWS
    block_expert = jnp.searchsorted(pends, jnp.minimum(blk_start, pends[-1] - 1), side='right')
    block_expert = jnp.minimum(block_expert, N_EXPERTS - 1).astype(jnp.int32)
    return dest, block_expert, n_used.reshape(1), nblk


def kernel(x, p, positions, ln_mix, w_in, hg_lb, hg_onorm, w_oA, mla_qa_norm, mla_kva_norm, w_uq,
           w_ukv, q_norm, k_norm, w_oB, w_out, ln_moe, w_rg, b_rg, w_re, b_re, w1, w3, w2, ln_ple,
           w_ple_gate, w_ple_proj):
    b, s, d = x.shape
    t = b * s
    depth = w_in.shape[0]
    lb_all = jnp.cumsum(jax.nn.softmax(hg_lb.astype(F32), axis=1), axis=1)
    posf = positions.astype(F32).reshape(t, 1)
    rope_tab = _rope_table()
    xc = x.reshape(t, d)

    for layer in range(depth):
        wi = w_in[layer]
        n_hg = 3 * HG_F + 2 * HG_V
        n_mla = MLA_Q_RANK + MLA_KV_RANK + MLA_ROPE
        mla_w = -(-n_mla // LANES) * LANES
        whg = wi[:, :n_hg].astype(BF16)
        wmla = jnp.pad(wi[:, n_hg:n_hg + n_mla], ((0, 0), (0, mla_w - n_mla))).astype(BF16)
        wgate = wi[:, n_hg + n_mla:].astype(BF16)
        lb = lb_all[:, layer, :]

        qs, lf, kk, v, og, mla_in, gates = _inproj(
            xc, ln_mix[layer][None, :], whg, wmla, wgate, lb, mla_qa_norm[layer][None, :],
            mla_kva_norm[layer][None, :])

        r3 = lambda a: a.reshape(b, s, a.shape[-1])
        ya = _hgrn(r3(qs), r3(lf), r3(kk), r3(v), r3(og), hg_onorm[layer][None, :]).reshape(t, HG_V)

        kv_w = MLA_NOPE + MLA_VDIM
        wkv = w_ukv[layer].reshape(MLA_KV_RANK, MLA_HEADS, kv_w)
        ckr_w = mla_w - MLA_Q_RANK
        wq = _head_pad(w_uq[layer], MLA_QK).astype(BF16)
        wk_nope = _head_pad(wkv[:, :, :MLA_NOPE].reshape(MLA_KV_RANK, MLA_HEADS * MLA_NOPE), MLA_NOPE)
        place = np.zeros((ckr_w - MLA_KV_RANK, MLA_HEADS, LANES), np.float32)
        for j in range(MLA_ROPE):
            place[j, :, MLA_NOPE + j] = 1.0
        wk = jnp.concatenate([wk_nope, jnp.asarray(place.reshape(ckr_w - MLA_KV_RANK, -1))],
                             axis=0).astype(BF16)
        wv = jnp.pad(wkv[:, :, MLA_NOPE:].reshape(MLA_KV_RANK, MLA_HEADS * MLA_VDIM),
                     ((0, ckr_w - MLA_KV_RANK), (0, 0))).astype(BF16)
        gq = jnp.pad(q_norm[layer], (0, LANES - MLA_QK))[None, :]
        gk = jnp.pad(k_norm[layer], (0, LANES - MLA_QK))[None, :]
        qh, kh, vh = _mlaprep(mla_in, posf, wq, wk, wv, gq, gk, rope_tab)
        yb = _attn(r3(qh), r3(kh), r3(vh)).reshape(t, MLA_HEADS * MLA_VDIM)

        wr = jnp.pad(jnp.concatenate([w_rg[layer], w_re[layer]], axis=1),
                     ((0, 0), (0, LANES - N_GROUPS - N_EXPERTS)))
        wr_hi = wr.astype(BF16)
        wr = jnp.concatenate([wr_hi, (wr - wr_hi.astype(F32)).astype(BF16)], axis=1)
        br = jnp.pad(jnp.concatenate([b_rg[layer], b_re[layer]]),
                     (0, LANES - N_GROUPS - N_EXPERTS))[None, :]
        x1, hm8, route = _merge(xc, ya, yb, gates, w_oA[layer].astype(BF16), w_oB[layer].astype(BF16),
                               w_out[layer].astype(BF16), ln_moe[layer][None, :], wr, br)

        ids = route[:, 0:2].astype(jnp.int32)
        dest, block_expert, n_used, nblk = _route_tables(ids)
        xb8 = _dispatch(dest, hm8, jnp.zeros((nblk * MOE_ROWS * SUBLANES, LANES), F32))
        yexp = _experts(block_expert, n_used, xb8, w1[layer], w3[layer], w2[layer])

        xc = _ple(dest, x1, route, p[layer].reshape(t, -1), ln_ple[layer][None, :],
                  w_ple_gate[layer].astype(BF16), w_ple_proj[layer].astype(BF16), yexp)
    return xc.reshape(b, s, d)
```

```python
import functools

import numpy as np
import jax
import jax.numpy as jnp
from jax import lax
from jax.experimental import pallas as pl
from jax.experimental.pallas import tpu as pltpu

F32 = jnp.float32
BF16 = jnp.bfloat16

HG_HEADS = 4
HG_KDIM = 128
HG_VDIM = 128
HG_F = HG_HEADS * HG_KDIM
HG_V = HG_HEADS * HG_VDIM
MLA_HEADS = 8
MLA_NOPE = 64
MLA_ROPE = 32
MLA_VDIM = 64
MLA_QK = MLA_NOPE + MLA_ROPE
MLA_Q_RANK = 256
MLA_KV_RANK = 128
ROPE_THETA = 10000.0
N_GROUPS = 8
EXPERTS_PER_GROUP = 8
N_EXPERTS = N_GROUPS * EXPERTS_PER_GROUP
D_EXPERT = 256
EPS = 1e-6

LANES = 128
SUBLANES = 8
VMEM_LIMIT_BYTES = 56 * 1024 * 1024

ROW_TILE = 256
MERGE_ROW_TILE = 512
INPROJ_ROW_TILE = 256
HG_CHUNK = 64
HG_HEADS_PER_STEP = 2
HG_CHUNKS_PER_MATMUL = 4
ATTN_Q_TILE = 512
ATTN_HEADS_PER_STEP = 8
MOE_ROWS = 256
DISPATCH_TOKENS = 512
HG_SAFE_LOG_DECAY = -60.0


def _cparams(sem):
    return pltpu.CompilerParams(dimension_semantics=sem, vmem_limit_bytes=VMEM_LIMIT_BYTES)


def _const_spec(shape):
    nd = len(shape)
    return pl.BlockSpec(shape, lambda *_: (0,) * nd)


def _sigmoid(x):
    return 1.0 / (1.0 + jnp.exp(-x))


def _silu(x):
    return x * _sigmoid(x)


def _rms_scale(x, n):
    return lax.rsqrt(jnp.sum(x * x, axis=-1, keepdims=True) * (1.0 / n) + EPS)


def _dot(a, b):
    return jnp.dot(a, b, preferred_element_type=F32)


def _dot_nt(a, b):
    return lax.dot_general(a, b, (((1,), (1,)), ((), ())), preferred_element_type=F32)


def _dot_tn(a, b):
    return lax.dot_general(a, b, (((0,), (0,)), ((), ())), preferred_element_type=F32)


def _inproj_kernel(x_ref, g_ref, whg_ref, wmla_ref, wgate_ref, lb_ref, nq_ref, nkv_ref,
                   qs_ref, lf_ref, kk_ref, v_ref, og_ref, mla_ref, gates_ref):
    x = x_ref[...]
    d = x.shape[-1]
    h = (x * _rms_scale(x, d) * g_ref[...]).astype(BF16)

    qs_ref[...] = _silu(_dot(h, whg_ref[:, 0:HG_F])).astype(BF16)
    for direction in range(2):
        cols = slice(HG_F * (1 + direction), HG_F * (2 + direction))
        out_cols = slice(HG_F * direction, HG_F * (direction + 1))
        lb = lb_ref[direction:direction + 1, :]
        f = lb + (1.0 - lb) * _sigmoid(_dot(h, whg_ref[:, cols]))
        lf_ref[:, out_cols] = jnp.log(f)
        kk_ref[:, out_cols] = (1.0 - f).astype(BF16)
    v_ref[...] = _dot(h, whg_ref[:, 3 * HG_F:3 * HG_F + HG_V]).astype(BF16)
    og_ref[...] = _silu(_dot(h, whg_ref[:, 3 * HG_F + HG_V:3 * HG_F + 2 * HG_V])).astype(BF16)

    zm = _dot(h, wmla_ref[...])
    cq = zm[:, 0:MLA_Q_RANK]
    mla_ref[:, 0:MLA_Q_RANK] = (cq * _rms_scale(cq, MLA_Q_RANK) * nq_ref[...]).astype(BF16)
    ckv = zm[:, MLA_Q_RANK:MLA_Q_RANK + MLA_KV_RANK]
    mla_ref[:, MLA_Q_RANK:MLA_Q_RANK + MLA_KV_RANK] = (
        ckv * _rms_scale(ckv, MLA_KV_RANK) * nkv_ref[...]).astype(BF16)
    mla_ref[:, MLA_Q_RANK + MLA_KV_RANK:] = zm[:, MLA_Q_RANK + MLA_KV_RANK:].astype(BF16)

    for half in range(2):
        cols = slice(d * half, d * (half + 1))
        gates_ref[:, cols] = _sigmoid(_dot(h, wgate_ref[:, cols])).astype(BF16)


def _inproj(x2, ln_mix, whg, wmla, wgate, lb, nq, nkv):
    t, d = x2.shape
    tm = INPROJ_ROW_TILE
    mla_w = wmla.shape[1]
    row = lambda w: pl.BlockSpec((tm, w), lambda i: (i, 0))
    out_shape = (
        jax.ShapeDtypeStruct((t, HG_F), BF16),
        jax.ShapeDtypeStruct((t, 2 * HG_F), F32),
        jax.ShapeDtypeStruct((t, 2 * HG_F), BF16),
        jax.ShapeDtypeStruct((t, HG_V), BF16),
        jax.ShapeDtypeStruct((t, HG_V), BF16),
        jax.ShapeDtypeStruct((t, mla_w), BF16),
        jax.ShapeDtypeStruct((t, 2 * d), BF16),
    )
    return pl.pallas_call(
        _inproj_kernel,
        grid=(t // tm,),
        in_specs=[row(d), _const_spec((1, d)), _const_spec(whg.shape), _const_spec(wmla.shape),
                  _const_spec(wgate.shape), _const_spec(lb.shape), _const_spec(nq.shape),
                  _const_spec(nkv.shape)],
        out_specs=(row(HG_F), row(2 * HG_F), row(2 * HG_F), row(HG_V), row(HG_V), row(mla_w),
                   row(2 * d)),
        out_shape=out_shape,
        compiler_params=_cparams(("parallel",)),
        name="inproj",
    )(x2, ln_mix, whg, wmla, wgate, lb, nq, nkv)


def _hgrn_kernel(qs_ref, lff_ref, lfb_ref, kf_ref, kb_ref, v_ref, og_ref, onorm_ref, out_ref,
                 state_ref, ofw_ref, obw_ref, qt_scr, u_scr, dec_scr, g_scr, k_scr, v_scr):
    c = HG_CHUNK
    s_len = qs_ref.shape[1]
    n_chunks = s_len // c
    hb = HG_HEADS_PER_STEP
    kd = HG_KDIM

    row = lax.broadcasted_iota(jnp.int32, (c, c), 0)
    col = lax.broadcasted_iota(jnp.int32, (c, c), 1)
    masks = (row >= col, row <= col)
    rowc = lax.broadcasted_iota(jnp.int32, (c, kd), 0)
    o_refs = (ofw_ref, obw_ref)

    def cum_log_decay(lf, direction):
        n = lf.shape[0]
        in_chunk = lax.broadcasted_iota(jnp.int32, lf.shape, 0) % c
        g = lf
        k = 1
        while k < c:
            if direction == 0:
                g = g + jnp.where(in_chunk >= k, pltpu.roll(g, k, 0), 0.0)
            else:
                g = g + jnp.where(in_chunk < c - k, pltpu.roll(g, n - k, 0), 0.0)
            k *= 2
        return g

    def load(j):
        chains = []
        for hh in range(hb):
            lanes = slice(hh * kd, (hh + 1) * kd)
            for direction in range(2):
                chunk = j if direction == 0 else n_chunks - 1 - j
                rows = pl.ds(pl.multiple_of(chunk * c, c), c)
                g = cum_log_decay((lff_ref, lfb_ref)[direction][0, rows, lanes], direction)
                total = g[c - 1:c, :] if direction == 0 else g[0:1, :]
                chains.append(dict(
                    idx=hh * 2 + direction, direction=direction, rows=rows, lanes=lanes, g=g,
                    total=total,
                    q=qs_ref[0, rows, lanes].astype(F32),
                    k=(kf_ref, kb_ref)[direction][0, rows, lanes].astype(F32),
                    v=v_ref[0, rows, lanes]))
        return chains

    def finish(ch, o, kdec_t_v_scaled):
        (ofw_ref, obw_ref)[ch["direction"]][ch["rows"], ch["lanes"]] = o
        state_ref[ch["idx"]] = kdec_t_v_scaled

    def fast(ch):
        g = ch["g"]
        qt = (ch["q"] * jnp.exp(g)).astype(BF16)
        kt = (ch["k"] * jnp.exp(-g)).astype(BF16)
        st = state_ref[ch["idx"]]
        sc = jnp.where(masks[ch["direction"]], _dot_nt(qt, kt), 0.0)
        o = _dot_nt(qt, st.astype(BF16)) + _dot(sc.astype(BF16), ch["v"])
        finish(ch, o, (st + _dot_tn(ch["v"], kt)) * jnp.exp(ch["total"]))

    def robust(ch):
        g = ch["g"]
        direction = ch["direction"]
        st = state_ref[ch["idx"]]
        o0 = _dot_nt((ch["q"] * jnp.exp(g)).astype(BF16), st.astype(BF16))
        slot = ch["idx"]
        g_scr[slot] = g
        k_scr[slot] = ch["k"]
        v_scr[slot] = ch["v"].astype(F32)
        q = ch["q"]

        def body(s, acc):
            g_s = g_scr[slot, pl.ds(s, 1), :]
            seen = (rowc >= s) if direction == 0 else (rowc <= s)
            decay = jnp.where(seen, jnp.exp(jnp.minimum(g - g_s, 0.0)), 0.0)
            a = jnp.sum(q * decay * k_scr[slot, pl.ds(s, 1), :], axis=-1, keepdims=True)
            return acc + a * v_scr[slot, pl.ds(s, 1), :]

        o = lax.fori_loop(0, c, body, o0)
        kdec = (ch["k"] * jnp.exp(ch["total"] - g)).astype(BF16)
        finish(ch, o, st * jnp.exp(ch["total"]) + _dot_tn(ch["v"], kdec))

    def step(j, carry):
        chains = load(j)
        lowest = chains[0]["total"]
        for ch in chains[1:]:
            lowest = jnp.minimum(lowest, ch["total"])
        safe = jnp.min(lowest) >= HG_SAFE_LOG_DECAY

        @pl.when(safe)
        def _():
            for ch in chains:
                fast(ch)

        @pl.when(jnp.logical_not(safe))
        def _():
            for ch in chains:
                robust(ch)

        return carry

    grp = HG_CHUNKS_PER_MATMUL if n_chunks % HG_CHUNKS_PER_MATMUL == 0 else 1
    gr = grp * c
    n_groups = n_chunks // grp
    grow = lax.broadcasted_iota(jnp.int32, (gr, gr), 0)
    gcol = lax.broadcasted_iota(jnp.int32, (gr, gr), 1)
    same_chunk = (grow // c) == (gcol // c)
    gmasks = (same_chunk & (grow >= gcol), same_chunk & (grow <= gcol))
    chunk_of_row = lax.broadcasted_iota(jnp.int32, (gr, kd), 0) // c

    def block_diag(a):
        return jnp.concatenate([jnp.where(chunk_of_row == i, a, jnp.zeros_like(a))
                                for i in range(grp)], axis=1)

    def phase_a(gi, lowest):
        rows = pl.ds(pl.multiple_of(gi * gr, gr), gr)
        for hh in range(hb):
            lanes = slice(hh * kd, (hh + 1) * kd)
            v = v_ref[0, rows, lanes]
            v_bd = block_diag(v)
            q = qs_ref[0, rows, lanes].astype(F32)
            for direction in range(2):
                idx = hh * 2 + direction
                g = cum_log_decay((lff_ref, lfb_ref)[direction][0, rows, lanes], direction)
                k = (kf_ref, kb_ref)[direction][0, rows, lanes].astype(F32)
                qt = (q * jnp.exp(g)).astype(BF16)
                kt = (k * jnp.exp(-g)).astype(BF16)
                qt_scr[idx, rows, :] = qt
                sc = jnp.where(gmasks[direction], _dot_nt(qt, kt), 0.0)
                o_refs[direction][rows, lanes] = _dot(sc.astype(BF16), v)
                u_scr[idx, pl.ds(gi * grp, grp)] = _dot_tn(v_bd, kt).reshape(grp, HG_VDIM, kd)
                for i in range(grp):
                    edge = i * c + (c - 1 if direction == 0 else 0)
                    total = g[edge:edge + 1, :]
                    lowest = jnp.minimum(lowest, total)
                    dec_scr[idx, pl.ds(gi * grp + i, 1), :] = jnp.exp(total)
        return lowest

    lowest = lax.fori_loop(0, n_groups, phase_a, jnp.zeros((1, kd), F32))
    all_safe = jnp.min(lowest) >= HG_SAFE_LOG_DECAY
    state_ref[...] = jnp.zeros_like(state_ref)

    @pl.when(all_safe)
    def _():
        def phase_b(j, carry):
            for hh in range(hb):
                lanes = slice(hh * kd, (hh + 1) * kd)
                for direction in range(2):
                    idx = hh * 2 + direction
                    gi = j if direction == 0 else n_groups - 1 - j
                    rows = pl.ds(pl.multiple_of(gi * gr, gr), gr)
                    st = state_ref[idx]
                    seen = [None] * grp
                    for i in (range(grp) if direction == 0 else reversed(range(grp))):
                        ci = gi * grp + i
                        seen[i] = st.astype(BF16)
                        st = (st + u_scr[idx, ci]) * dec_scr[idx, pl.ds(ci, 1), :]
                    state_ref[idx] = st
                    o_refs[direction][rows, lanes] += _dot_nt(
                        block_diag(qt_scr[idx, rows, :]), jnp.concatenate(seen, axis=1))
            return carry

        lax.fori_loop(0, n_groups, phase_b, 0)

    @pl.when(jnp.logical_not(all_safe))
    def _():
        lax.fori_loop(0, n_chunks, step, 0)

    blk = 256 if s_len % 256 == 0 else c

    def epilogue(i, carry):
        rows = pl.ds(pl.multiple_of(i * blk, blk), blk)
        for hh in range(hb):
            lanes = slice(hh * kd, (hh + 1) * kd)
            o = ofw_ref[rows, lanes] + obw_ref[rows, lanes]
            y = o * _rms_scale(o, HG_VDIM) * onorm_ref[...]
            out_ref[0, rows, lanes] = (y * og_ref[0, rows, lanes].astype(F32)).astype(BF16)
        return carry

    lax.fori_loop(0, s_len // blk, epilogue, 0)


def _hgrn(qs, lf, kk, v, og, onorm):
    b, s, _ = qs.shape
    hb = HG_HEADS_PER_STEP
    w = hb * HG_KDIM
    n_hsteps = HG_HEADS // hb
    fwd = pl.BlockSpec((1, s, w), lambda bi, hi: (bi, 0, hi))
    bwd = pl.BlockSpec((1, s, w), lambda bi, hi: (bi, 0, n_hsteps + hi))
    return pl.pallas_call(
        _hgrn_kernel,
        grid=(b, n_hsteps),
        in_specs=[fwd, fwd, bwd, fwd, bwd, fwd, fwd, _const_spec(onorm.shape)],
        out_specs=fwd,
        out_shape=jax.ShapeDtypeStruct((b, s, HG_V), BF16),
        scratch_shapes=[
            pltpu.VMEM((2 * hb, HG_VDIM, HG_KDIM), F32),
            pltpu.VMEM((s, w), F32),
            pltpu.VMEM((s, w), F32),
            pltpu.VMEM((2 * hb, s, HG_KDIM), BF16),
            pltpu.VMEM((2 * hb, s // HG_CHUNK, HG_VDIM, HG_KDIM), F32),
            pltpu.VMEM((2 * hb, max(s // HG_CHUNK, 8), HG_KDIM), F32),
            pltpu.VMEM((2 * hb, HG_CHUNK, HG_KDIM), F32),
            pltpu.VMEM((2 * hb, HG_CHUNK, HG_KDIM), F32),
            pltpu.VMEM((2 * hb, HG_CHUNK, HG_VDIM), F32),
        ],
        compiler_params=_cparams(("parallel", "parallel")),
        name="hgrn",
    )(qs, lf, lf, kk, kk, v, og, onorm)


def _mlaprep_kernel(mla_ref, pos_ref, pos_t_ref, wq_ref, wk_t_ref, wv_ref, gq_ref, gk_t_ref,
                    rope_ref, freq_t_ref, q_ref, k_t_ref, v_ref):
    cq = mla_ref[:, 0:MLA_Q_RANK]
    ckr = mla_ref[:, MLA_Q_RANK:]
    half = MLA_ROPE // 2

    ang = pos_ref[...] * rope_ref[0:1, :]
    cos = jnp.cos(ang)
    sin = jnp.sin(ang)
    sin_lo = sin * rope_ref[1:2, :]
    sin_hi = sin * rope_ref[2:3, :]
    qm = _dot(cq, wq_ref[...])
    for hd in range(MLA_HEADS):
        lanes = slice(hd * LANES, (hd + 1) * LANES)
        x = qm[:, lanes]
        y = x * (_rms_scale(x, MLA_QK) * MLA_QK ** -0.5) * gq_ref[...]
        y = y * cos + pltpu.roll(y, half, 1) * sin_lo + pltpu.roll(y, LANES - half, 1) * sin_hi
        q_ref[:, lanes] = y.astype(BF16)

    km_t = _dot_nt(wk_t_ref[...], ckr)
    reps = km_t.shape[1] // LANES
    lane_tile = lambda a: jnp.concatenate([a] * reps, axis=1)
    ang_t = lane_tile(freq_t_ref[...]) * pos_t_ref[...]
    cos_t = jnp.cos(ang_t)
    sin_t = jnp.sin(ang_t)
    gain_t = lane_tile(gk_t_ref[...])
    for hd in range(MLA_HEADS):
        r0 = hd * LANES
        x = km_t[r0:r0 + LANES, :]
        y = x * lax.rsqrt(jnp.sum(x * x, axis=0, keepdims=True) * (1.0 / MLA_QK) + EPS) * gain_t
        x1 = y[MLA_NOPE:MLA_NOPE + half, :]
        x2 = y[MLA_NOPE + half:MLA_QK, :]
        k_t_ref[r0:r0 + MLA_NOPE, :] = y[0:MLA_NOPE, :].astype(BF16)
        k_t_ref[r0 + MLA_NOPE:r0 + MLA_NOPE + half, :] = (x1 * cos_t - x2 * sin_t).astype(BF16)
        k_t_ref[r0 + MLA_NOPE + half:r0 + MLA_QK, :] = (x2 * cos_t + x1 * sin_t).astype(BF16)
        k_t_ref[r0 + MLA_QK:r0 + LANES, :] = jnp.zeros((LANES - MLA_QK, km_t.shape[1]), BF16)

    v_ref[...] = _dot(ckr, wv_ref[...]).astype(BF16)


def _mlaprep(mla_in, posf, wq, wk_t, wv, gq, gk_t, rope_tab, freq_t):
    t = mla_in.shape[0]
    tm = ROW_TILE
    row = lambda w: pl.BlockSpec((tm, w), lambda i: (i, 0))
    hw = MLA_HEADS * LANES
    vw = MLA_HEADS * MLA_VDIM
    return pl.pallas_call(
        _mlaprep_kernel,
        grid=(t // tm,),
        in_specs=[row(mla_in.shape[1]), row(1), pl.BlockSpec((1, tm), lambda i: (0, i)),
                  _const_spec(wq.shape), _const_spec(wk_t.shape), _const_spec(wv.shape),
                  _const_spec(gq.shape), _const_spec(gk_t.shape), _const_spec(rope_tab.shape),
                  _const_spec(freq_t.shape)],
        out_specs=(row(hw), pl.BlockSpec((hw, tm), lambda i: (0, i)), row(vw)),
        out_shape=(jax.ShapeDtypeStruct((t, hw), BF16), jax.ShapeDtypeStruct((hw, t), BF16),
                   jax.ShapeDtypeStruct((t, vw), BF16)),
        compiler_params=_cparams(("parallel",)),
        name="mlaprep",
    )(mla_in, posf, posf.reshape(1, t), wq, wk_t, wv, gq, gk_t, rope_tab, freq_t)


def _attn_kernel(q_ref, k_t_ref, v_ref, o_ref):
    for j in range(ATTN_HEADS_PER_STEP):
        lanes = slice(j * LANES, (j + 1) * LANES)
        s = _dot(q_ref[0, :, lanes], k_t_ref[lanes, :])
        p = jnp.exp(s - jnp.max(s, axis=-1, keepdims=True))
        l = jnp.sum(p, axis=-1, keepdims=True)
        vl = slice(j * MLA_VDIM, (j + 1) * MLA_VDIM)
        o = _dot(p.astype(BF16), v_ref[0, :, vl])
        o_ref[0, :, vl] = (o / l).astype(BF16)


def _attn(q, k_t, v):
    b, s, _ = q.shape
    tq = min(ATTN_Q_TILE, s)
    nh = ATTN_HEADS_PER_STEP
    return pl.pallas_call(
        _attn_kernel,
        grid=(b, MLA_HEADS // nh, s // tq),
        in_specs=[pl.BlockSpec((1, tq, nh * LANES), lambda bi, hp, i: (bi, i, hp)),
                  pl.BlockSpec((nh * LANES, s), lambda bi, hp, i: (hp, bi)),
                  pl.BlockSpec((1, s, nh * MLA_VDIM), lambda bi, hp, i: (bi, 0, hp))],
        out_specs=pl.BlockSpec((1, tq, nh * MLA_VDIM), lambda bi, hp, i: (bi, i, hp)),
        out_shape=jax.ShapeDtypeStruct((b, s, MLA_HEADS * MLA_VDIM), BF16),
        compiler_params=_cparams(("parallel", "parallel", "arbitrary")),
        name="attn",
    )(q, k_t, v)


def _merge_kernel(x_ref, ya_ref, yb_ref, gates_ref, woa_ref, wob_ref, wout_ref, lnm_ref, wr_ref,
                  br_ref, x1_ref, hm8_ref, route_ref):
    d = x_ref.shape[-1]
    y_a = _dot(ya_ref[...], woa_ref[...])
    y_b = _dot(yb_ref[...], wob_ref[...])
    merged = gates_ref[:, 0:d].astype(F32) * y_a + gates_ref[:, d:2 * d].astype(F32) * y_b
    x1 = x_ref[...] + _dot(merged.astype(BF16), wout_ref[...])
    x1_ref[...] = x1
    hm = x1 * _rms_scale(x1, d) * lnm_ref[...]
    _store_token_tiles(hm8_ref, 0, hm)

    hm_hi = hm.astype(BF16)
    hm_lo = (hm - hm_hi.astype(F32)).astype(BF16)
    hh = _dot(hm_hi, wr_ref[...])
    logits = (hh[:, 0:LANES] + hh[:, LANES:2 * LANES] + _dot(hm_lo, wr_ref[:, 0:LANES])
              + br_ref[...])
    lane = lax.broadcasted_iota(jnp.int32, logits.shape, 1)
    neg = -jnp.inf
    big = jnp.int32(2 ** 30)
    is_group = lane < N_GROUPS
    gl = jnp.where(is_group, logits, neg)
    gmax = jnp.max(gl, axis=-1, keepdims=True)
    p_group = 1.0 / jnp.sum(jnp.where(is_group, jnp.exp(gl - gmax), 0.0), axis=-1, keepdims=True)
    g_sel = jnp.min(jnp.where(gl == gmax, lane, big), axis=-1, keepdims=True)
    lo = N_GROUPS + g_sel * EXPERTS_PER_GROUP
    in_group = (lane >= lo) & (lane < lo + EXPERTS_PER_GROUP)
    el = jnp.where(in_group, logits, neg)
    v1 = jnp.max(el, axis=-1, keepdims=True)
    i1 = jnp.min(jnp.where(el == v1, lane, big), axis=-1, keepdims=True)
    el2 = jnp.where(lane == i1, neg, el)
    v2 = jnp.max(el2, axis=-1, keepdims=True)
    i2 = jnp.min(jnp.where(el2 == v2, lane, big), axis=-1, keepdims=True)
    e21 = jnp.exp(v2 - v1)
    w1 = p_group / (1.0 + e21)
    w2 = w1 * e21
    route = jnp.where(lane == 0, (i1 - N_GROUPS).astype(F32),
                      jnp.where(lane == 1, (i2 - N_GROUPS).astype(F32),
                                jnp.where(lane == 2, w1, jnp.where(lane == 3, w2, 0.0))))
    route_ref[...] = route


def _merge(x2, ya, yb, gates, woa, wob, wout, lnm, wr, br):
    t, d = x2.shape
    tm = MERGE_ROW_TILE
    row = lambda w: pl.BlockSpec((tm, w), lambda i: (i, 0))
    return pl.pallas_call(
        _merge_kernel,
        grid=(t // tm,),
        in_specs=[row(d), row(ya.shape[1]), row(yb.shape[1]), row(2 * d), _const_spec(woa.shape),
                  _const_spec(wob.shape), _const_spec(wout.shape), _const_spec(lnm.shape),
                  _const_spec(wr.shape), _const_spec(br.shape)],
        out_specs=(row(d), pl.BlockSpec((tm * SUBLANES, LANES), lambda i: (i, 0)), row(LANES)),
        out_shape=(jax.ShapeDtypeStruct((t, d), F32),
                   jax.ShapeDtypeStruct((t * SUBLANES, LANES), F32),
                   jax.ShapeDtypeStruct((t, LANES), F32)),
        compiler_params=_cparams(("parallel",)),
        name="merge",
    )(x2, ya, yb, gates, woa, wob, wout, lnm, wr, br)


def _store_token_tiles(ref8, tok0, val):
    n = val.shape[0]
    for j in range(SUBLANES):
        ref8[pl.ds(tok0 * SUBLANES + j, n, stride=SUBLANES), :] = val[:, j * LANES:(j + 1) * LANES]


def _load_token_tiles(ref8, tok0, n, dtype):
    return jnp.concatenate(
        [ref8[pl.ds(tok0 * SUBLANES + j, n, stride=SUBLANES), :].astype(dtype)
         for j in range(SUBLANES)], axis=1)


def _token_tile(ref8, tok):
    return ref8.at[pl.ds(pl.multiple_of(tok * SUBLANES, SUBLANES), SUBLANES)]


def _dispatch_kernel(dest_ref, hm8_ref, xb_in_ref, xb8_ref, sem):
    del xb_in_ref
    n_tok = DISPATCH_TOKENS
    for r in range(n_tok):
        src = hm8_ref.at[pl.ds(r * SUBLANES, SUBLANES)]
        for slot in range(2):
            pltpu.make_async_copy(src, _token_tile(xb8_ref, dest_ref[2 * r + slot]), sem).start()
    for _ in range(2):
        pltpu.make_async_copy(hm8_ref, xb8_ref.at[pl.ds(0, n_tok * SUBLANES)], sem).wait()


def _dispatch(dest_flat, hm8, xb_init):
    t = hm8.shape[0] // SUBLANES
    return pl.pallas_call(
        _dispatch_kernel,
        grid=(t // DISPATCH_TOKENS,),
        in_specs=[pl.BlockSpec((2 * DISPATCH_TOKENS,), lambda i: (i,), memory_space=pltpu.SMEM),
                  pl.BlockSpec((DISPATCH_TOKENS * SUBLANES, LANES), lambda i: (i, 0)),
                  pl.BlockSpec(memory_space=pl.ANY)],
        out_specs=pl.BlockSpec(memory_space=pl.ANY),
        out_shape=jax.ShapeDtypeStruct(xb_init.shape, xb_init.dtype),
        scratch_shapes=[pltpu.SemaphoreType.DMA(())],
        input_output_aliases={2: 0},
        compiler_params=_cparams(("arbitrary",)),
        name="dispatch",
    )(dest_flat, hm8, xb_init)


def _experts_kernel(be_ref, nu_ref, xb8_ref, w1_ref, w3_ref, w2_ref, yb8_ref, w1b, w3b, w2b):
    i = pl.program_id(0)
    used = i < nu_ref[0]
    new_expert = jnp.logical_or(i == 0, be_ref[i] != be_ref[jnp.maximum(i - 1, 0)])

    @pl.when(jnp.logical_and(used, new_expert))
    def _():
        w1b[...] = w1_ref[0].astype(BF16)
        w3b[...] = w3_ref[0].astype(BF16)
        w2b[...] = w2_ref[0].astype(BF16)

    @pl.when(used)
    def _():
        x = _load_token_tiles(xb8_ref, 0, MOE_ROWS, BF16)
        hmid = (_silu(_dot(x, w1b[...])) * _dot(x, w3b[...])).astype(BF16)
        _store_token_tiles(yb8_ref, 0, _dot(hmid, w2b[...]))

    @pl.when(jnp.logical_not(used))
    def _():
        yb8_ref[...] = jnp.zeros_like(yb8_ref)


def _experts(block_expert, n_used, xb8, w1, w3, w2):
    nblk = block_expert.shape[0]
    d = w1.shape[1]
    de = w1.shape[2]
    blk_rows = MOE_ROWS * SUBLANES
    rows_in = lambda i, be, nu: (jnp.minimum(i, nu[0] - 1), 0)
    wsel = lambda i, be, nu: (be[i], 0, 0)
    return pl.pallas_call(
        _experts_kernel,
        grid_spec=pltpu.PrefetchScalarGridSpec(
            num_scalar_prefetch=2,
            grid=(nblk,),
            in_specs=[pl.BlockSpec((blk_rows, LANES), rows_in),
                      pl.BlockSpec((1, d, de), wsel), pl.BlockSpec((1, d, de), wsel),
                      pl.BlockSpec((1, de, d), wsel)],
            out_specs=pl.BlockSpec((blk_rows, LANES), lambda i, be, nu: (i, 0)),
            scratch_shapes=[pltpu.VMEM((d, de), BF16), pltpu.VMEM((d, de), BF16),
                            pltpu.VMEM((de, d), BF16)]),
        out_shape=jax.ShapeDtypeStruct((nblk * blk_rows, LANES), F32),
        compiler_params=_cparams(("arbitrary",)),
        name="experts",
    )(block_expert, n_used, xb8, w1, w3, w2)


def _ple_kernel(dest_ref, destn_ref, x1_ref, route_ref, p_ref, ln_ref, wg_ref, wp_ref, yb8_ref,
                out_ref, ya0, ya1, yb0, yb1, sems):
    j = pl.program_id(0)
    last = pl.num_programs(0) - 1
    tm = ROW_TILE
    d = x1_ref.shape[1]

    def start(idx_ref, base, bufs, sem):
        for r in range(tm):
            for slot in range(2):
                pltpu.make_async_copy(_token_tile(yb8_ref, idx_ref[base + 2 * r + slot]),
                                      bufs[slot].at[pl.ds(r * SUBLANES, SUBLANES)], sem).start()

    def wait(bufs, sem):
        for buf in bufs:
            pltpu.make_async_copy(yb8_ref.at[pl.ds(0, tm * SUBLANES)], buf, sem).wait()

    def compute(bufs, rows):
        x2 = (x1_ref[rows, :] + route_ref[rows, 2:3] * _load_token_tiles(bufs[0], 0, tm, F32)
              + route_ref[rows, 3:4] * _load_token_tiles(bufs[1], 0, tm, F32))
        h = (x2 * _rms_scale(x2, d) * ln_ref[...]).astype(BF16)
        pp = _dot(p_ref[rows, :].astype(BF16), wp_ref[...])
        out_ref[rows, :] = x2 + pp * _sigmoid(_dot(h, wg_ref[...]))

    @pl.when(j == 0)
    def _():
        start(dest_ref, 0, (ya0, ya1), sems.at[0])

    wait((ya0, ya1), sems.at[0])
    start(dest_ref, 2 * tm, (yb0, yb1), sems.at[1])
    compute((ya0, ya1), slice(0, tm))

    wait((yb0, yb1), sems.at[1])
    start(destn_ref, 0, (ya0, ya1), sems.at[0])
    compute((yb0, yb1), slice(tm, 2 * tm))

    @pl.when(j == last)
    def _():
        wait((ya0, ya1), sems.at[0])


def _ple(dest_flat, x1, route, p2, ln, wg, wp, yb):
    t, d = x1.shape
    tm = ROW_TILE
    nsteps = t // (2 * tm)
    row = lambda w: pl.BlockSpec((2 * tm, w), lambda j: (j, 0))
    return pl.pallas_call(
        _ple_kernel,
        grid=(nsteps,),
        in_specs=[pl.BlockSpec((4 * tm,), lambda j: (j,), memory_space=pltpu.SMEM),
                  pl.BlockSpec((4 * tm,), lambda j: (jnp.minimum(j + 1, nsteps - 1),),
                               memory_space=pltpu.SMEM),
                  row(d), row(LANES), row(p2.shape[1]), _const_spec(ln.shape),
                  _const_spec(wg.shape), _const_spec(wp.shape),
                  pl.BlockSpec(memory_space=pl.ANY)],
        out_specs=row(d),
        out_shape=jax.ShapeDtypeStruct((t, d), F32),
        scratch_shapes=([pltpu.VMEM((tm * SUBLANES, LANES), F32)] * 4
                        + [pltpu.SemaphoreType.DMA((2,))]),
        compiler_params=_cparams(("arbitrary",)),
        name="ple",
    )(dest_flat, dest_flat, x1, route, p2, ln, wg, wp, yb)


def _head_pad(w, width):
    r = w.shape[0]
    w = w.reshape(r, MLA_HEADS, width)
    return jnp.pad(w, ((0, 0), (0, 0), (0, LANES - width))).reshape(r, MLA_HEADS * LANES)


def _rope_table():
    half = MLA_ROPE // 2
    inv_freq = ROPE_THETA ** (-np.arange(half, dtype=np.float32) / half)
    tab = np.zeros((8, LANES), np.float32)
    tab[0, MLA_NOPE:MLA_NOPE + half] = inv_freq
    tab[0, MLA_NOPE + half:MLA_QK] = inv_freq
    tab[1, MLA_NOPE + half:MLA_QK] = 1.0
    tab[2, MLA_NOPE:MLA_NOPE + half] = -1.0
    freq_t = np.broadcast_to(inv_freq[:, None], (half, LANES))
    return jnp.asarray(tab), jnp.asarray(freq_t)


def _route_tables(ids):
    a = ids.shape[0] * 2
    e = ids.reshape(a)
    onehot = (e[:, None] == jnp.arange(N_EXPERTS, dtype=jnp.int32)[None, :]).astype(jnp.int32)
    csum = jnp.cumsum(onehot, axis=0)
    counts = csum[-1]
    rank = jnp.sum(csum * onehot, axis=1) - 1
    pcounts = (counts + MOE_ROWS - 1) // MOE_ROWS * MOE_ROWS
    pends = jnp.cumsum(pcounts)
    pstarts = pends - pcounts
    dest = (jnp.sum(onehot * pstarts[None, :], axis=1) + rank).astype(jnp.int32)
    nblk = (a + N_EXPERTS * (MOE_ROWS - 1)) // MOE_ROWS
    n_used = (pends[-1] // MOE_ROWS).astype(jnp.int32)
    blk_start = jnp.arange(nblk, dtype=jnp.int32) * MOE_ROWS
    block_expert = jnp.searchsorted(pends, jnp.minimum(blk_start, pends[-1] - 1), side='right')
    block_expert = jnp.minimum(block_expert, N_EXPERTS - 1).astype(jnp.int32)
    return dest, block_expert, n_used.reshape(1), nblk


def kernel(x, p, positions, ln_mix, w_in, hg_lb, hg_onorm, w_oA, mla_qa_norm, mla_kva_norm, w_uq,
           w_ukv, q_norm, k_norm, w_oB, w_out, ln_moe, w_rg, b_rg, w_re, b_re, w1, w3, w2, ln_ple,
           w_ple_gate, w_ple_proj):
    b, s, d = x.shape
    t = b * s
    depth = w_in.shape[0]
    lb_all = jnp.cumsum(jax.nn.softmax(hg_lb.astype(F32), axis=1), axis=1)
    posf = positions.astype(F32).reshape(t, 1)
    rope_tab, freq_t = _rope_table()
    xc = x.reshape(t, d)

    for layer in range(depth):
        wi = w_in[layer]
        n_hg = 3 * HG_F + 2 * HG_V
        n_mla = MLA_Q_RANK + MLA_KV_RANK + MLA_ROPE
        mla_w = -(-n_mla // LANES) * LANES
        whg = wi[:, :n_hg].astype(BF16)
        wmla = jnp.pad(wi[:, n_hg:n_hg + n_mla], ((0, 0), (0, mla_w - n_mla))).astype(BF16)
        wgate = wi[:, n_hg + n_mla:].astype(BF16)
        lb = lb_all[:, layer, :]

        qs, lf, kk, v, og, mla_in, gates = _inproj(
            xc, ln_mix[layer][None, :], whg, wmla, wgate, lb, mla_qa_norm[layer][None, :],
            mla_kva_norm[layer][None, :])

        r3 = lambda a: a.reshape(b, s, a.shape[-1])
        ya = _hgrn(r3(qs), r3(lf), r3(kk), r3(v), r3(og), hg_onorm[layer][None, :]).reshape(t, HG_V)

        kv_w = MLA_NOPE + MLA_VDIM
        wkv = w_ukv[layer].reshape(MLA_KV_RANK, MLA_HEADS, kv_w)
        ckr_w = mla_w - MLA_Q_RANK
        wq = _head_pad(w_uq[layer], MLA_QK).astype(BF16)
        wk_nope = _head_pad(wkv[:, :, :MLA_NOPE].reshape(MLA_KV_RANK, MLA_HEADS * MLA_NOPE), MLA_NOPE)
        place = np.zeros((ckr_w - MLA_KV_RANK, MLA_HEADS, LANES), np.float32)
        for j in range(MLA_ROPE):
            place[j, :, MLA_NOPE + j] = 1.0
        wk = jnp.concatenate([wk_nope, jnp.asarray(place.reshape(ckr_w - MLA_KV_RANK, -1))],
                             axis=0).astype(BF16)
        wv = jnp.pad(wkv[:, :, MLA_NOPE:].reshape(MLA_KV_RANK, MLA_HEADS * MLA_VDIM),
                     ((0, ckr_w - MLA_KV_RANK), (0, 0))).astype(BF16)
        gq = jnp.pad(q_norm[layer], (0, LANES - MLA_QK))[None, :]
        gk_t = jnp.broadcast_to(jnp.pad(k_norm[layer], (0, LANES - MLA_QK))[:, None], (LANES, LANES))
        qh, kh_t, vh = _mlaprep(mla_in, posf, wq, wk.T, wv, gq, gk_t, rope_tab, freq_t)
        yb = _attn(r3(qh), kh_t, r3(vh)).reshape(t, MLA_HEADS * MLA_VDIM)

        wr = jnp.pad(jnp.concatenate([w_rg[layer], w_re[layer]], axis=1),
                     ((0, 0), (0, LANES - N_GROUPS - N_EXPERTS)))
        wr_hi = wr.astype(BF16)
        wr = jnp.concatenate([wr_hi, (wr - wr_hi.astype(F32)).astype(BF16)], axis=1)
        br = jnp.pad(jnp.concatenate([b_rg[layer], b_re[layer]]),
                     (0, LANES - N_GROUPS - N_EXPERTS))[None, :]
        x1, hm8, route = _merge(xc, ya, yb, gates, w_oA[layer].astype(BF16), w_oB[layer].astype(BF16),
                               w_out[layer].astype(BF16), ln_moe[layer][None, :], wr, br)

        ids = route[:, 0:2].astype(jnp.int32)
        dest, block_expert, n_used, nblk = _route_tables(ids)
        xb8 = _dispatch(dest, hm8, jnp.zeros((nblk * MOE_ROWS * SUBLANES, LANES), F32))
        yexp = _experts(block_expert, n_used, xb8, w1[layer], w3[layer], w2[layer])

        xc = _ple(dest, x1, route, p[layer].reshape(t, -1), ln_ple[layer][None, :],
                  w_ple_gate[layer].astype(BF16), w_ple_proj[layer].astype(BF16), yexp)
    return xc.reshape(b, s, d)
```

```python
import functools

import numpy as np
import jax
import jax.numpy as jnp
from jax import lax
from jax.experimental import pallas as pl
from jax.experimental.pallas import tpu as pltpu

F32 = jnp.float32
BF16 = jnp.bfloat16

HG_HEADS = 4
HG_KDIM = 128
HG_VDIM = 128
HG_F = HG_HEADS * HG_KDIM
HG_V = HG_HEADS * HG_VDIM
MLA_HEADS = 8
MLA_NOPE = 64
MLA_ROPE = 32
MLA_VDIM = 64
MLA_QK = MLA_NOPE + MLA_ROPE
MLA_Q_RANK = 256
MLA_KV_RANK = 128
ROPE_THETA = 10000.0
N_GROUPS = 8
EXPERTS_PER_GROUP = 8
N_EXPERTS = N_GROUPS * EXPERTS_PER_GROUP
D_EXPERT = 256
EPS = 1e-6

LANES = 128
TOKEN_ROWS = 4
VMEM_LIMIT_BYTES = 56 * 1024 * 1024

ROW_TILE = 256
MERGE_ROW_TILE = 512
INPROJ_ROW_TILE = 256
HG_CHUNK = 64
HG_HEADS_PER_STEP = 2
HG_CHUNKS_PER_MATMUL = 4
ATTN_Q_TILE = 512
ATTN_HEADS_PER_STEP = 8
MOE_ROWS = 256
DISPATCH_TOKENS = 512
HG_SAFE_LOG_DECAY = -60.0


def _cparams(sem):
    return pltpu.CompilerParams(dimension_semantics=sem, vmem_limit_bytes=VMEM_LIMIT_BYTES)


def _const_spec(shape):
    nd = len(shape)
    return pl.BlockSpec(shape, lambda *_: (0,) * nd)


def _sigmoid(x):
    return 1.0 / (1.0 + jnp.exp(-x))


def _silu(x):
    return x * _sigmoid(x)


def _rms_scale(x, n):
    return lax.rsqrt(jnp.sum(x * x, axis=-1, keepdims=True) * (1.0 / n) + EPS)


def _dot(a, b):
    return jnp.dot(a, b, preferred_element_type=F32)


def _dot_nt(a, b):
    return lax.dot_general(a, b, (((1,), (1,)), ((), ())), preferred_element_type=F32)


def _dot_tn(a, b):
    return lax.dot_general(a, b, (((0,), (0,)), ((), ())), preferred_element_type=F32)


def _inproj_kernel(x_ref, g_ref, whg_ref, wmla_ref, wgate_ref, lb_ref, nq_ref, nkv_ref,
                   qs_ref, lf_ref, kk_ref, v_ref, og_ref, mla_ref, gates_ref):
    x = x_ref[...]
    d = x.shape[-1]
    h = (x * _rms_scale(x, d) * g_ref[...]).astype(BF16)

    qs_ref[...] = _silu(_dot(h, whg_ref[:, 0:HG_F])).astype(BF16)
    for direction in range(2):
        cols = slice(HG_F * (1 + direction), HG_F * (2 + direction))
        out_cols = slice(HG_F * direction, HG_F * (direction + 1))
        lb = lb_ref[direction:direction + 1, :]
        f = lb + (1.0 - lb) * _sigmoid(_dot(h, whg_ref[:, cols]))
        lf_ref[:, out_cols] = jnp.log(f)
        kk_ref[:, out_cols] = (1.0 - f).astype(BF16)
    v_ref[...] = _dot(h, whg_ref[:, 3 * HG_F:3 * HG_F + HG_V]).astype(BF16)
    og_ref[...] = _silu(_dot(h, whg_ref[:, 3 * HG_F + HG_V:3 * HG_F + 2 * HG_V])).astype(BF16)

    zm = _dot(h, wmla_ref[...])
    cq = zm[:, 0:MLA_Q_RANK]
    mla_ref[:, 0:MLA_Q_RANK] = (cq * _rms_scale(cq, MLA_Q_RANK) * nq_ref[...]).astype(BF16)
    ckv = zm[:, MLA_Q_RANK:MLA_Q_RANK + MLA_KV_RANK]
    mla_ref[:, MLA_Q_RANK:MLA_Q_RANK + MLA_KV_RANK] = (
        ckv * _rms_scale(ckv, MLA_KV_RANK) * nkv_ref[...]).astype(BF16)
    mla_ref[:, MLA_Q_RANK + MLA_KV_RANK:] = zm[:, MLA_Q_RANK + MLA_KV_RANK:].astype(BF16)

    for half in range(2):
        cols = slice(d * half, d * (half + 1))
        gates_ref[:, cols] = _sigmoid(_dot(h, wgate_ref[:, cols])).astype(BF16)


def _inproj(x2, ln_mix, whg, wmla, wgate, lb, nq, nkv):
    t, d = x2.shape
    tm = INPROJ_ROW_TILE
    mla_w = wmla.shape[1]
    row = lambda w: pl.BlockSpec((tm, w), lambda i: (i, 0))
    out_shape = (
        jax.ShapeDtypeStruct((t, HG_F), BF16),
        jax.ShapeDtypeStruct((t, 2 * HG_F), F32),
        jax.ShapeDtypeStruct((t, 2 * HG_F), BF16),
        jax.ShapeDtypeStruct((t, HG_V), BF16),
        jax.ShapeDtypeStruct((t, HG_V), BF16),
        jax.ShapeDtypeStruct((t, mla_w), BF16),
        jax.ShapeDtypeStruct((t, 2 * d), BF16),
    )
    return pl.pallas_call(
        _inproj_kernel,
        grid=(t // tm,),
        in_specs=[row(d), _const_spec((1, d)), _const_spec(whg.shape), _const_spec(wmla.shape),
                  _const_spec(wgate.shape), _const_spec(lb.shape), _const_spec(nq.shape),
                  _const_spec(nkv.shape)],
        out_specs=(row(HG_F), row(2 * HG_F), row(2 * HG_F), row(HG_V), row(HG_V), row(mla_w),
                   row(2 * d)),
        out_shape=out_shape,
        compiler_params=_cparams(("parallel",)),
        name="inproj",
    )(x2, ln_mix, whg, wmla, wgate, lb, nq, nkv)


def _hgrn_kernel(qs_ref, lff_ref, lfb_ref, kf_ref, kb_ref, v_ref, og_ref, onorm_ref, out_ref,
                 state_ref, ofw_ref, obw_ref, qt_scr, u_scr, dec_scr, g_scr, k_scr, v_scr):
    c = HG_CHUNK
    s_len = qs_ref.shape[1]
    n_chunks = s_len // c
    hb = HG_HEADS_PER_STEP
    kd = HG_KDIM

    row = lax.broadcasted_iota(jnp.int32, (c, c), 0)
    col = lax.broadcasted_iota(jnp.int32, (c, c), 1)
    masks = (row >= col, row <= col)
    rowc = lax.broadcasted_iota(jnp.int32, (c, kd), 0)
    o_refs = (ofw_ref, obw_ref)

    def cum_log_decay(lf, direction):
        n = lf.shape[0]
        in_chunk = lax.broadcasted_iota(jnp.int32, lf.shape, 0) % c
        g = lf
        k = 1
        while k < c:
            if direction == 0:
                g = g + jnp.where(in_chunk >= k, pltpu.roll(g, k, 0), 0.0)
            else:
                g = g + jnp.where(in_chunk < c - k, pltpu.roll(g, n - k, 0), 0.0)
            k *= 2
        return g

    def load(j):
        chains = []
        for hh in range(hb):
            lanes = slice(hh * kd, (hh + 1) * kd)
            for direction in range(2):
                chunk = j if direction == 0 else n_chunks - 1 - j
                rows = pl.ds(pl.multiple_of(chunk * c, c), c)
                g = cum_log_decay((lff_ref, lfb_ref)[direction][0, rows, lanes], direction)
                total = g[c - 1:c, :] if direction == 0 else g[0:1, :]
                chains.append(dict(
                    idx=hh * 2 + direction, direction=direction, rows=rows, lanes=lanes, g=g,
                    total=total,
                    q=qs_ref[0, rows, lanes].astype(F32),
                    k=(kf_ref, kb_ref)[direction][0, rows, lanes].astype(F32),
                    v=v_ref[0, rows, lanes]))
        return chains

    def finish(ch, o, kdec_t_v_scaled):
        (ofw_ref, obw_ref)[ch["direction"]][ch["rows"], ch["lanes"]] = o
        state_ref[ch["idx"]] = kdec_t_v_scaled

    def fast(ch):
        g = ch["g"]
        qt = (ch["q"] * jnp.exp(g)).astype(BF16)
        kt = (ch["k"] * jnp.exp(-g)).astype(BF16)
        st = state_ref[ch["idx"]]
        sc = jnp.where(masks[ch["direction"]], _dot_nt(qt, kt), 0.0)
        o = _dot_nt(qt, st.astype(BF16)) + _dot(sc.astype(BF16), ch["v"])
        finish(ch, o, (st + _dot_tn(ch["v"], kt)) * jnp.exp(ch["total"]))

    def robust(ch):
        g = ch["g"]
        direction = ch["direction"]
        st = state_ref[ch["idx"]]
        o0 = _dot_nt((ch["q"] * jnp.exp(g)).astype(BF16), st.astype(BF16))
        slot = ch["idx"]
        g_scr[slot] = g
        k_scr[slot] = ch["k"]
        v_scr[slot] = ch["v"].astype(F32)
        q = ch["q"]

        def body(s, acc):
            g_s = g_scr[slot, pl.ds(s, 1), :]
            seen = (rowc >= s) if direction == 0 else (rowc <= s)
            decay = jnp.where(seen, jnp.exp(jnp.minimum(g - g_s, 0.0)), 0.0)
            a = jnp.sum(q * decay * k_scr[slot, pl.ds(s, 1), :], axis=-1, keepdims=True)
            return acc + a * v_scr[slot, pl.ds(s, 1), :]

        o = lax.fori_loop(0, c, body, o0)
        kdec = (ch["k"] * jnp.exp(ch["total"] - g)).astype(BF16)
        finish(ch, o, st * jnp.exp(ch["total"]) + _dot_tn(ch["v"], kdec))

    def step(j, carry):
        chains = load(j)
        lowest = chains[0]["total"]
        for ch in chains[1:]:
            lowest = jnp.minimum(lowest, ch["total"])
        safe = jnp.min(lowest) >= HG_SAFE_LOG_DECAY

        @pl.when(safe)
        def _():
            for ch in chains:
                fast(ch)

        @pl.when(jnp.logical_not(safe))
        def _():
            for ch in chains:
                robust(ch)

        return carry

    grp = HG_CHUNKS_PER_MATMUL if n_chunks % HG_CHUNKS_PER_MATMUL == 0 else 1
    gr = grp * c
    n_groups = n_chunks // grp
    grow = lax.broadcasted_iota(jnp.int32, (gr, gr), 0)
    gcol = lax.broadcasted_iota(jnp.int32, (gr, gr), 1)
    same_chunk = (grow // c) == (gcol // c)
    gmasks = (same_chunk & (grow >= gcol), same_chunk & (grow <= gcol))
    chunk_of_row = lax.broadcasted_iota(jnp.int32, (gr, kd), 0) // c

    def block_diag(a):
        return jnp.concatenate([jnp.where(chunk_of_row == i, a, jnp.zeros_like(a))
                                for i in range(grp)], axis=1)

    def phase_a(gi, lowest):
        rows = pl.ds(pl.multiple_of(gi * gr, gr), gr)
        for hh in range(hb):
            lanes = slice(hh * kd, (hh + 1) * kd)
            v = v_ref[0, rows, lanes]
            v_bd = block_diag(v)
            q = qs_ref[0, rows, lanes].astype(F32)
            for direction in range(2):
                idx = hh * 2 + direction
                g = cum_log_decay((lff_ref, lfb_ref)[direction][0, rows, lanes], direction)
                k = (kf_ref, kb_ref)[direction][0, rows, lanes].astype(F32)
                qt = (q * jnp.exp(g)).astype(BF16)
                kt = (k * jnp.exp(-g)).astype(BF16)
                qt_scr[idx, rows, :] = qt
                sc = jnp.where(gmasks[direction], _dot_nt(qt, kt), 0.0)
                o_refs[direction][rows, lanes] = _dot(sc.astype(BF16), v)
                u_scr[idx, pl.ds(gi * grp, grp)] = _dot_tn(v_bd, kt).reshape(grp, HG_VDIM, kd)
                for i in range(grp):
                    edge = i * c + (c - 1 if direction == 0 else 0)
                    total = g[edge:edge + 1, :]
                    lowest = jnp.minimum(lowest, total)
                    dec_scr[idx, pl.ds(gi * grp + i, 1), :] = jnp.exp(total)
        return lowest

    lowest = lax.fori_loop(0, n_groups, phase_a, jnp.zeros((1, kd), F32))
    all_safe = jnp.min(lowest) >= HG_SAFE_LOG_DECAY
    state_ref[...] = jnp.zeros_like(state_ref)

    @pl.when(all_safe)
    def _():
        def phase_b(j, carry):
            for hh in range(hb):
                lanes = slice(hh * kd, (hh + 1) * kd)
                for direction in range(2):
                    idx = hh * 2 + direction
                    gi = j if direction == 0 else n_groups - 1 - j
                    rows = pl.ds(pl.multiple_of(gi * gr, gr), gr)
                    st = state_ref[idx]
                    seen = [None] * grp
                    for i in (range(grp) if direction == 0 else reversed(range(grp))):
                        ci = gi * grp + i
                        seen[i] = st.astype(BF16)
                        st = (st + u_scr[idx, ci]) * dec_scr[idx, pl.ds(ci, 1), :]
                    state_ref[idx] = st
                    o_refs[direction][rows, lanes] += _dot_nt(
                        block_diag(qt_scr[idx, rows, :]), jnp.concatenate(seen, axis=1))
            return carry

        lax.fori_loop(0, n_groups, phase_b, 0)

    @pl.when(jnp.logical_not(all_safe))
    def _():
        lax.fori_loop(0, n_chunks, step, 0)

    blk = 256 if s_len % 256 == 0 else c

    def epilogue(i, carry):
        rows = pl.ds(pl.multiple_of(i * blk, blk), blk)
        for hh in range(hb):
            lanes = slice(hh * kd, (hh + 1) * kd)
            o = ofw_ref[rows, lanes] + obw_ref[rows, lanes]
            y = o * _rms_scale(o, HG_VDIM) * onorm_ref[...]
            out_ref[0, rows, lanes] = (y * og_ref[0, rows, lanes].astype(F32)).astype(BF16)
        return carry

    lax.fori_loop(0, s_len // blk, epilogue, 0)


def _hgrn(qs, lf, kk, v, og, onorm):
    b, s, _ = qs.shape
    hb = HG_HEADS_PER_STEP
    w = hb * HG_KDIM
    n_hsteps = HG_HEADS // hb
    fwd = pl.BlockSpec((1, s, w), lambda bi, hi: (bi, 0, hi))
    bwd = pl.BlockSpec((1, s, w), lambda bi, hi: (bi, 0, n_hsteps + hi))
    return pl.pallas_call(
        _hgrn_kernel,
        grid=(b, n_hsteps),
        in_specs=[fwd, fwd, bwd, fwd, bwd, fwd, fwd, _const_spec(onorm.shape)],
        out_specs=fwd,
        out_shape=jax.ShapeDtypeStruct((b, s, HG_V), BF16),
        scratch_shapes=[
            pltpu.VMEM((2 * hb, HG_VDIM, HG_KDIM), F32),
            pltpu.VMEM((s, w), F32),
            pltpu.VMEM((s, w), F32),
            pltpu.VMEM((2 * hb, s, HG_KDIM), BF16),
            pltpu.VMEM((2 * hb, s // HG_CHUNK, HG_VDIM, HG_KDIM), F32),
            pltpu.VMEM((2 * hb, max(s // HG_CHUNK, 8), HG_KDIM), F32),
            pltpu.VMEM((2 * hb, HG_CHUNK, HG_KDIM), F32),
            pltpu.VMEM((2 * hb, HG_CHUNK, HG_KDIM), F32),
            pltpu.VMEM((2 * hb, HG_CHUNK, HG_VDIM), F32),
        ],
        compiler_params=_cparams(("parallel", "parallel")),
        name="hgrn",
    )(qs, lf, lf, kk, kk, v, og, onorm)


def _mlaprep_kernel(mla_ref, pos_ref, pos_t_ref, wq_ref, wk_t_ref, wv_ref, gq_ref, gk_t_ref,
                    rope_ref, freq_t_ref, q_ref, k_t_ref, v_ref):
    cq = mla_ref[:, 0:MLA_Q_RANK]
    ckr = mla_ref[:, MLA_Q_RANK:]
    half = MLA_ROPE // 2

    ang = pos_ref[...] * rope_ref[0:1, :]
    cos = jnp.cos(ang)
    sin = jnp.sin(ang)
    sin_lo = sin * rope_ref[1:2, :]
    sin_hi = sin * rope_ref[2:3, :]
    qm = _dot(cq, wq_ref[...])
    for hd in range(MLA_HEADS):
        lanes = slice(hd * LANES, (hd + 1) * LANES)
        x = qm[:, lanes]
        y = x * (_rms_scale(x, MLA_QK) * MLA_QK ** -0.5) * gq_ref[...]
        y = y * cos + pltpu.roll(y, half, 1) * sin_lo + pltpu.roll(y, LANES - half, 1) * sin_hi
        q_ref[:, lanes] = y.astype(BF16)

    km_t = _dot_nt(wk_t_ref[...], ckr)
    reps = km_t.shape[1] // LANES
    lane_tile = lambda a: jnp.concatenate([a] * reps, axis=1)
    ang_t = lane_tile(freq_t_ref[...]) * pos_t_ref[...]
    cos_t = jnp.cos(ang_t)
    sin_t = jnp.sin(ang_t)
    gain_t = lane_tile(gk_t_ref[...])
    for hd in range(MLA_HEADS):
        r0 = hd * LANES
        x = km_t[r0:r0 + LANES, :]
        y = x * lax.rsqrt(jnp.sum(x * x, axis=0, keepdims=True) * (1.0 / MLA_QK) + EPS) * gain_t
        x1 = y[MLA_NOPE:MLA_NOPE + half, :]
        x2 = y[MLA_NOPE + half:MLA_QK, :]
        k_t_ref[r0:r0 + MLA_NOPE, :] = y[0:MLA_NOPE, :].astype(BF16)
        k_t_ref[r0 + MLA_NOPE:r0 + MLA_NOPE + half, :] = (x1 * cos_t - x2 * sin_t).astype(BF16)
        k_t_ref[r0 + MLA_NOPE + half:r0 + MLA_QK, :] = (x2 * cos_t + x1 * sin_t).astype(BF16)
        k_t_ref[r0 + MLA_QK:r0 + LANES, :] = jnp.zeros((LANES - MLA_QK, km_t.shape[1]), BF16)

    v_ref[...] = _dot(ckr, wv_ref[...]).astype(BF16)


def _mlaprep(mla_in, posf, wq, wk_t, wv, gq, gk_t, rope_tab, freq_t):
    t = mla_in.shape[0]
    tm = ROW_TILE
    row = lambda w: pl.BlockSpec((tm, w), lambda i: (i, 0))
    hw = MLA_HEADS * LANES
    vw = MLA_HEADS * MLA_VDIM
    return pl.pallas_call(
        _mlaprep_kernel,
        grid=(t // tm,),
        in_specs=[row(mla_in.shape[1]), row(1), pl.BlockSpec((1, tm), lambda i: (0, i)),
                  _const_spec(wq.shape), _const_spec(wk_t.shape), _const_spec(wv.shape),
                  _const_spec(gq.shape), _const_spec(gk_t.shape), _const_spec(rope_tab.shape),
                  _const_spec(freq_t.shape)],
        out_specs=(row(hw), pl.BlockSpec((hw, tm), lambda i: (0, i)), row(vw)),
        out_shape=(jax.ShapeDtypeStruct((t, hw), BF16), jax.ShapeDtypeStruct((hw, t), BF16),
                   jax.ShapeDtypeStruct((t, vw), BF16)),
        compiler_params=_cparams(("parallel",)),
        name="mlaprep",
    )(mla_in, posf, posf.reshape(1, t), wq, wk_t, wv, gq, gk_t, rope_tab, freq_t)


def _attn_kernel(q_ref, k_t_ref, v_ref, o_ref):
    for j in range(ATTN_HEADS_PER_STEP):
        lanes = slice(j * LANES, (j + 1) * LANES)
        s = _dot(q_ref[0, :, lanes], k_t_ref[lanes, :])
        p = jnp.exp(s - jnp.max(s, axis=-1, keepdims=True))
        l = jnp.sum(p, axis=-1, keepdims=True)
        vl = slice(j * MLA_VDIM, (j + 1) * MLA_VDIM)
        o = _dot(p.astype(BF16), v_ref[0, :, vl])
        o_ref[0, :, vl] = (o / l).astype(BF16)


def _attn(q, k_t, v):
    b, s, _ = q.shape
    tq = min(ATTN_Q_TILE, s)
    nh = ATTN_HEADS_PER_STEP
    return pl.pallas_call(
        _attn_kernel,
        grid=(b, MLA_HEADS // nh, s // tq),
        in_specs=[pl.BlockSpec((1, tq, nh * LANES), lambda bi, hp, i: (bi, i, hp)),
                  pl.BlockSpec((nh * LANES, s), lambda bi, hp, i: (hp, bi)),
                  pl.BlockSpec((1, s, nh * MLA_VDIM), lambda bi, hp, i: (bi, 0, hp))],
        out_specs=pl.BlockSpec((1, tq, nh * MLA_VDIM), lambda bi, hp, i: (bi, i, hp)),
        out_shape=jax.ShapeDtypeStruct((b, s, MLA_HEADS * MLA_VDIM), BF16),
        compiler_params=_cparams(("parallel", "parallel", "arbitrary")),
        name="attn",
    )(q, k_t, v)


def _merge_kernel(x_ref, ya_ref, yb_ref, gates_ref, woa_ref, wob_ref, wout_ref, lnm_ref, wr_ref,
                  br_ref, x1_ref, hm8_ref, route_ref):
    d = x_ref.shape[-1]
    y_a = _dot(ya_ref[...], woa_ref[...])
    y_b = _dot(yb_ref[...], wob_ref[...])
    merged = gates_ref[:, 0:d].astype(F32) * y_a + gates_ref[:, d:2 * d].astype(F32) * y_b
    x1 = x_ref[...] + _dot(merged.astype(BF16), wout_ref[...])
    x1_ref[...] = x1
    hm = x1 * _rms_scale(x1, d) * lnm_ref[...]
    _store_token_tiles(hm8_ref, 0, hm)

    hm_hi = hm.astype(BF16)
    hm_lo = (hm - hm_hi.astype(F32)).astype(BF16)
    hh = _dot(hm_hi, wr_ref[...])
    logits = (hh[:, 0:LANES] + hh[:, LANES:2 * LANES] + _dot(hm_lo, wr_ref[:, 0:LANES])
              + br_ref[...])
    lane = lax.broadcasted_iota(jnp.int32, logits.shape, 1)
    neg = -jnp.inf
    big = jnp.int32(2 ** 30)
    is_group = lane < N_GROUPS
    gl = jnp.where(is_group, logits, neg)
    gmax = jnp.max(gl, axis=-1, keepdims=True)
    p_group = 1.0 / jnp.sum(jnp.where(is_group, jnp.exp(gl - gmax), 0.0), axis=-1, keepdims=True)
    g_sel = jnp.min(jnp.where(gl == gmax, lane, big), axis=-1, keepdims=True)
    lo = N_GROUPS + g_sel * EXPERTS_PER_GROUP
    in_group = (lane >= lo) & (lane < lo + EXPERTS_PER_GROUP)
    el = jnp.where(in_group, logits, neg)
    v1 = jnp.max(el, axis=-1, keepdims=True)
    i1 = jnp.min(jnp.where(el == v1, lane, big), axis=-1, keepdims=True)
    el2 = jnp.where(lane == i1, neg, el)
    v2 = jnp.max(el2, axis=-1, keepdims=True)
    i2 = jnp.min(jnp.where(el2 == v2, lane, big), axis=-1, keepdims=True)
    e21 = jnp.exp(v2 - v1)
    w1 = p_group / (1.0 + e21)
    w2 = w1 * e21
    route = jnp.where(lane == 0, (i1 - N_GROUPS).astype(F32),
                      jnp.where(lane == 1, (i2 - N_GROUPS).astype(F32),
                                jnp.where(lane == 2, w1, jnp.where(lane == 3, w2, 0.0))))
    route_ref[...] = route


def _merge(x2, ya, yb, gates, woa, wob, wout, lnm, wr, br):
    t, d = x2.shape
    tm = MERGE_ROW_TILE
    row = lambda w: pl.BlockSpec((tm, w), lambda i: (i, 0))
    return pl.pallas_call(
        _merge_kernel,
        grid=(t // tm,),
        in_specs=[row(d), row(ya.shape[1]), row(yb.shape[1]), row(2 * d), _const_spec(woa.shape),
                  _const_spec(wob.shape), _const_spec(wout.shape), _const_spec(lnm.shape),
                  _const_spec(wr.shape), _const_spec(br.shape)],
        out_specs=(row(d), pl.BlockSpec((tm * TOKEN_ROWS, LANES), lambda i: (i, 0)), row(LANES)),
        out_shape=(jax.ShapeDtypeStruct((t, d), F32),
                   jax.ShapeDtypeStruct((t * TOKEN_ROWS, LANES), jnp.uint32),
                   jax.ShapeDtypeStruct((t, LANES), F32)),
        compiler_params=_cparams(("parallel",)),
        name="merge",
    )(x2, ya, yb, gates, woa, wob, wout, lnm, wr, br)


_HIGH_HALF = 0xFFFF0000


def _store_token_tiles(ref4, tok0, val):
    n, width = val.shape
    as_bits = lambda a: lax.bitcast_convert_type(a.astype(BF16).astype(F32), jnp.uint32)
    words = (as_bits(val[:, :width // 2]) >> 16) | (as_bits(val[:, width // 2:]) & jnp.uint32(_HIGH_HALF))
    for j in range(TOKEN_ROWS):
        ref4[pl.ds(tok0 * TOKEN_ROWS + j, n, stride=TOKEN_ROWS), :] = words[:, j * LANES:(j + 1) * LANES]


def _load_token_tiles(ref4, tok0, n):
    words = jnp.concatenate(
        [ref4[pl.ds(tok0 * TOKEN_ROWS + j, n, stride=TOKEN_ROWS), :] for j in range(TOKEN_ROWS)],
        axis=1)
    low = lax.bitcast_convert_type(words << 16, F32).astype(BF16)
    high = lax.bitcast_convert_type(words & jnp.uint32(_HIGH_HALF), F32).astype(BF16)
    return jnp.concatenate([low, high], axis=1)


def _token_tile(ref8, tok):
    return ref8.at[pl.ds(pl.multiple_of(tok * TOKEN_ROWS, TOKEN_ROWS), TOKEN_ROWS)]


def _dispatch_kernel(dest_ref, hm8_ref, xb_in_ref, xb8_ref, sem):
    del xb_in_ref
    n_tok = DISPATCH_TOKENS
    for r in range(n_tok):
        src = hm8_ref.at[pl.ds(r * TOKEN_ROWS, TOKEN_ROWS)]
        for slot in range(2):
            pltpu.make_async_copy(src, _token_tile(xb8_ref, dest_ref[2 * r + slot]), sem).start()
    for _ in range(2):
        pltpu.make_async_copy(hm8_ref, xb8_ref.at[pl.ds(0, n_tok * TOKEN_ROWS)], sem).wait()


def _dispatch(dest_flat, hm8, xb_init):
    t = hm8.shape[0] // TOKEN_ROWS
    return pl.pallas_call(
        _dispatch_kernel,
        grid=(t // DISPATCH_TOKENS,),
        in_specs=[pl.BlockSpec((2 * DISPATCH_TOKENS,), lambda i: (i,), memory_space=pltpu.SMEM),
                  pl.BlockSpec((DISPATCH_TOKENS * TOKEN_ROWS, LANES), lambda i: (i, 0)),
                  pl.BlockSpec(memory_space=pl.ANY)],
        out_specs=pl.BlockSpec(memory_space=pl.ANY),
        out_shape=jax.ShapeDtypeStruct(xb_init.shape, xb_init.dtype),
        scratch_shapes=[pltpu.SemaphoreType.DMA(())],
        input_output_aliases={2: 0},
        compiler_params=_cparams(("arbitrary",)),
        name="dispatch",
    )(dest_flat, hm8, xb_init)


def _experts_kernel(be_ref, nu_ref, xb8_ref, w1_ref, w3_ref, w2_ref, yb8_ref, w1b, w3b, w2b):
    i = pl.program_id(0)
    used = i < nu_ref[0]
    new_expert = jnp.logical_or(i == 0, be_ref[i] != be_ref[jnp.maximum(i - 1, 0)])

    @pl.when(jnp.logical_and(used, new_expert))
    def _():
        w1b[...] = w1_ref[0].astype(BF16)
        w3b[...] = w3_ref[0].astype(BF16)
        w2b[...] = w2_ref[0].astype(BF16)

    @pl.when(used)
    def _():
        x = _load_token_tiles(xb8_ref, 0, MOE_ROWS)
        hmid = (_silu(_dot(x, w1b[...])) * _dot(x, w3b[...])).astype(BF16)
        _store_token_tiles(yb8_ref, 0, _dot(hmid, w2b[...]))

    @pl.when(jnp.logical_not(used))
    def _():
        yb8_ref[...] = jnp.zeros_like(yb8_ref)


def _experts(block_expert, n_used, xb8, w1, w3, w2):
    nblk = block_expert.shape[0]
    d = w1.shape[1]
    de = w1.shape[2]
    blk_rows = MOE_ROWS * TOKEN_ROWS
    rows_in = lambda i, be, nu: (jnp.minimum(i, nu[0] - 1), 0)
    wsel = lambda i, be, nu: (be[i], 0, 0)
    return pl.pallas_call(
        _experts_kernel,
        grid_spec=pltpu.PrefetchScalarGridSpec(
            num_scalar_prefetch=2,
            grid=(nblk,),
            in_specs=[pl.BlockSpec((blk_rows, LANES), rows_in),
                      pl.BlockSpec((1, d, de), wsel), pl.BlockSpec((1, d, de), wsel),
                      pl.BlockSpec((1, de, d), wsel)],
            out_specs=pl.BlockSpec((blk_rows, LANES), lambda i, be, nu: (i, 0)),
            scratch_shapes=[pltpu.VMEM((d, de), BF16), pltpu.VMEM((d, de), BF16),
                            pltpu.VMEM((de, d), BF16)]),
        out_shape=jax.ShapeDtypeStruct((nblk * blk_rows, LANES), jnp.uint32),
        compiler_params=_cparams(("arbitrary",)),
        name="experts",
    )(block_expert, n_used, xb8, w1, w3, w2)


def _ple_kernel(dest_ref, destn_ref, x1_ref, route_ref, p_ref, ln_ref, wg_ref, wp_ref, yb8_ref,
                out_ref, ya0, ya1, yb0, yb1, sems):
    j = pl.program_id(0)
    last = pl.num_programs(0) - 1
    tm = ROW_TILE
    d = x1_ref.shape[1]

    def start(idx_ref, base, bufs, sem):
        for r in range(tm):
            for slot in range(2):
                pltpu.make_async_copy(_token_tile(yb8_ref, idx_ref[base + 2 * r + slot]),
                                      bufs[slot].at[pl.ds(r * TOKEN_ROWS, TOKEN_ROWS)], sem).start()

    def wait(bufs, sem):
        for buf in bufs:
            pltpu.make_async_copy(yb8_ref.at[pl.ds(0, tm * TOKEN_ROWS)], buf, sem).wait()

    def compute(bufs, rows):
        x2 = (x1_ref[rows, :]
              + route_ref[rows, 2:3] * _load_token_tiles(bufs[0], 0, tm).astype(F32)
              + route_ref[rows, 3:4] * _load_token_tiles(bufs[1], 0, tm).astype(F32))
        h = (x2 * _rms_scale(x2, d) * ln_ref[...]).astype(BF16)
        pp = _dot(p_ref[rows, :].astype(BF16), wp_ref[...])
        out_ref[rows, :] = x2 + pp * _sigmoid(_dot(h, wg_ref[...]))

    @pl.when(j == 0)
    def _():
        start(dest_ref, 0, (ya0, ya1), sems.at[0])

    wait((ya0, ya1), sems.at[0])
    start(dest_ref, 2 * tm, (yb0, yb1), sems.at[1])
    compute((ya0, ya1), slice(0, tm))

    wait((yb0, yb1), sems.at[1])
    start(destn_ref, 0, (ya0, ya1), sems.at[0])
    compute((yb0, yb1), slice(tm, 2 * tm))

    @pl.when(j == last)
    def _():
        wait((ya0, ya1), sems.at[0])


def _ple(dest_flat, x1, route, p2, ln, wg, wp, yb):
    t, d = x1.shape
    tm = ROW_TILE
    nsteps = t // (2 * tm)
    row = lambda w: pl.BlockSpec((2 * tm, w), lambda j: (j, 0))
    return pl.pallas_call(
        _ple_kernel,
        grid=(nsteps,),
        in_specs=[pl.BlockSpec((4 * tm,), lambda j: (j,), memory_space=pltpu.SMEM),
                  pl.BlockSpec((4 * tm,), lambda j: (jnp.minimum(j + 1, nsteps - 1),),
                               memory_space=pltpu.SMEM),
                  row(d), row(LANES), row(p2.shape[1]), _const_spec(ln.shape),
                  _const_spec(wg.shape), _const_spec(wp.shape),
                  pl.BlockSpec(memory_space=pl.ANY)],
        out_specs=row(d),
        out_shape=jax.ShapeDtypeStruct((t, d), F32),
        scratch_shapes=([pltpu.VMEM((tm * TOKEN_ROWS, LANES), jnp.uint32)] * 4
                        + [pltpu.SemaphoreType.DMA((2,))]),
        compiler_params=_cparams(("arbitrary",)),
        name="ple",
    )(dest_flat, dest_flat, x1, route, p2, ln, wg, wp, yb)


def _head_pad(w, width):
    r = w.shape[0]
    w = w.reshape(r, MLA_HEADS, width)
    return jnp.pad(w, ((0, 0), (0, 0), (0, LANES - width))).reshape(r, MLA_HEADS * LANES)


def _rope_table():
    half = MLA_ROPE // 2
    inv_freq = ROPE_THETA ** (-np.arange(half, dtype=np.float32) / half)
    tab = np.zeros((8, LANES), np.float32)
    tab[0, MLA_NOPE:MLA_NOPE + half] = inv_freq
    tab[0, MLA_NOPE + half:MLA_QK] = inv_freq
    tab[1, MLA_NOPE + half:MLA_QK] = 1.0
    tab[2, MLA_NOPE:MLA_NOPE + half] = -1.0
    freq_t = np.broadcast_to(inv_freq[:, None], (half, LANES))
    return jnp.asarray(tab), jnp.asarray(freq_t)


def _route_tables(ids):
    a = ids.shape[0] * 2
    e = ids.reshape(a)
    onehot = (e[:, None] == jnp.arange(N_EXPERTS, dtype=jnp.int32)[None, :]).astype(jnp.int32)
    csum = jnp.cumsum(onehot, axis=0)
    counts = csum[-1]
    rank = jnp.sum(csum * onehot, axis=1) - 1
    pcounts = (counts + MOE_ROWS - 1) // MOE_ROWS * MOE_ROWS
    pends = jnp.cumsum(pcounts)
    pstarts = pends - pcounts
    dest = (jnp.sum(onehot * pstarts[None, :], axis=1) + rank).astype(jnp.int32)
    nblk = (a + N_EXPERTS * (MOE_ROWS - 1)) // MOE_ROWS
    n_used = (pends[-1] // MOE_ROWS).astype(jnp.int32)
    blk_start = jnp.arange(nblk, dtype=jnp.int32) * MOE_ROWS
    block_expert = jnp.searchsorted(pends, jnp.minimum(blk_start, pends[-1] - 1), side='right')
    block_expert = jnp.minimum(block_expert, N_EXPERTS - 1).astype(jnp.int32)
    return dest, block_expert, n_used.reshape(1), nblk


def kernel(x, p, positions, ln_mix, w_in, hg_lb, hg_onorm, w_oA, mla_qa_norm, mla_kva_norm, w_uq,
           w_ukv, q_norm, k_norm, w_oB, w_out, ln_moe, w_rg, b_rg, w_re, b_re, w1, w3, w2, ln_ple,
           w_ple_gate, w_ple_proj):
    b, s, d = x.shape
    t = b * s
    depth = w_in.shape[0]
    lb_all = jnp.cumsum(jax.nn.softmax(hg_lb.astype(F32), axis=1), axis=1)
    posf = positions.astype(F32).reshape(t, 1)
    rope_tab, freq_t = _rope_table()
    xc = x.reshape(t, d)

    for layer in range(depth):
        wi = w_in[layer]
        n_hg = 3 * HG_F + 2 * HG_V
        n_mla = MLA_Q_RANK + MLA_KV_RANK + MLA_ROPE
        mla_w = -(-n_mla // LANES) * LANES
        whg = wi[:, :n_hg].astype(BF16)
        wmla = jnp.pad(wi[:, n_hg:n_hg + n_mla], ((0, 0), (0, mla_w - n_mla))).astype(BF16)
        wgate = wi[:, n_hg + n_mla:].astype(BF16)
        lb = lb_all[:, layer, :]

        qs, lf, kk, v, og, mla_in, gates = _inproj(
            xc, ln_mix[layer][None, :], whg, wmla, wgate, lb, mla_qa_norm[layer][None, :],
            mla_kva_norm[layer][None, :])

        r3 = lambda a: a.reshape(b, s, a.shape[-1])
        ya = _hgrn(r3(qs), r3(lf), r3(kk), r3(v), r3(og), hg_onorm[layer][None, :]).reshape(t, HG_V)

        kv_w = MLA_NOPE + MLA_VDIM
        wkv = w_ukv[layer].reshape(MLA_KV_RANK, MLA_HEADS, kv_w)
        ckr_w = mla_w - MLA_Q_RANK
        wq = _head_pad(w_uq[layer], MLA_QK).astype(BF16)
        wk_nope = _head_pad(wkv[:, :, :MLA_NOPE].reshape(MLA_KV_RANK, MLA_HEADS * MLA_NOPE), MLA_NOPE)
        place = np.zeros((ckr_w - MLA_KV_RANK, MLA_HEADS, LANES), np.float32)
        for j in range(MLA_ROPE):
            place[j, :, MLA_NOPE + j] = 1.0
        wk = jnp.concatenate([wk_nope, jnp.asarray(place.reshape(ckr_w - MLA_KV_RANK, -1))],
                             axis=0).astype(BF16)
        wv = jnp.pad(wkv[:, :, MLA_NOPE:].reshape(MLA_KV_RANK, MLA_HEADS * MLA_VDIM),
                     ((0, ckr_w - MLA_KV_RANK), (0, 0))).astype(BF16)
        gq = jnp.pad(q_norm[layer], (0, LANES - MLA_QK))[None, :]
        gk_t = jnp.broadcast_to(jnp.pad(k_norm[layer], (0, LANES - MLA_QK))[:, None], (LANES, LANES))
        qh, kh_t, vh = _mlaprep(mla_in, posf, wq, wk.T, wv, gq, gk_t, rope_tab, freq_t)
        yb = _attn(r3(qh), kh_t, r3(vh)).reshape(t, MLA_HEADS * MLA_VDIM)

        wr = jnp.pad(jnp.concatenate([w_rg[layer], w_re[layer]], axis=1),
                     ((0, 0), (0, LANES - N_GROUPS - N_EXPERTS)))
        wr_hi = wr.astype(BF16)
        wr = jnp.concatenate([wr_hi, (wr - wr_hi.astype(F32)).astype(BF16)], axis=1)
        br = jnp.pad(jnp.concatenate([b_rg[layer], b_re[layer]]),
                     (0, LANES - N_GROUPS - N_EXPERTS))[None, :]
        x1, hm8, route = _merge(xc, ya, yb, gates, w_oA[layer].astype(BF16), w_oB[layer].astype(BF16),
                               w_out[layer].astype(BF16), ln_moe[layer][None, :], wr, br)

        ids = route[:, 0:2].astype(jnp.int32)
        dest, block_expert, n_used, nblk = _route_tables(ids)
        xb8 = _dispatch(dest, hm8, jnp.zeros((nblk * MOE_ROWS * TOKEN_ROWS, LANES), jnp.uint32))
        yexp = _experts(block_expert, n_used, xb8, w1[layer], w3[layer], w2[layer])

        xc = _ple(dest, x1, route, p[layer].reshape(t, -1), ln_ple[layer][None, :],
                  w_ple_gate[layer].astype(BF16), w_ple_proj[layer].astype(BF16), yexp)
    return xc.reshape(b, s, d)
```

```python
import functools

import numpy as np
import jax
import jax.numpy as jnp
from jax import lax
from jax.experimental import pallas as pl
from jax.experimental.pallas import tpu as pltpu

F32 = jnp.float32
BF16 = jnp.bfloat16

HG_HEADS = 4
HG_KDIM = 128
HG_VDIM = 128
HG_F = HG_HEADS * HG_KDIM
HG_V = HG_HEADS * HG_VDIM
MLA_HEADS = 8
MLA_NOPE = 64
MLA_ROPE = 32
MLA_VDIM = 64
MLA_QK = MLA_NOPE + MLA_ROPE
MLA_Q_RANK = 256
MLA_KV_RANK = 128
ROPE_THETA = 10000.0
N_GROUPS = 8
EXPERTS_PER_GROUP = 8
N_EXPERTS = N_GROUPS * EXPERTS_PER_GROUP
D_EXPERT = 256
EPS = 1e-6

LANES = 128
TOKEN_ROWS = 4
VMEM_LIMIT_BYTES = 56 * 1024 * 1024

ROW_TILE = 256
MERGE_ROW_TILE = 512
INPROJ_ROW_TILE = 256
HG_CHUNK = 64
HG_HEADS_PER_STEP = 2
HG_CHUNKS_PER_MATMUL = 4
ATTN_Q_TILE = 512
ATTN_HEADS_PER_STEP = 8
MOE_ROWS = 256
DISPATCH_TOKENS = 512
HG_SAFE_LOG_DECAY = -60.0


def _cparams(sem):
    return pltpu.CompilerParams(dimension_semantics=sem, vmem_limit_bytes=VMEM_LIMIT_BYTES)


def _const_spec(shape):
    nd = len(shape)
    return pl.BlockSpec(shape, lambda *_: (0,) * nd)


def _sigmoid(x):
    return 1.0 / (1.0 + jnp.exp(-x))


def _silu(x):
    return x * _sigmoid(x)


def _rms_scale(x, n):
    return lax.rsqrt(jnp.sum(x * x, axis=-1, keepdims=True) * (1.0 / n) + EPS)


def _dot(a, b):
    return jnp.dot(a, b, preferred_element_type=F32)


def _dot_nt(a, b):
    return lax.dot_general(a, b, (((1,), (1,)), ((), ())), preferred_element_type=F32)


def _dot_tn(a, b):
    return lax.dot_general(a, b, (((0,), (0,)), ((), ())), preferred_element_type=F32)


def _inproj_kernel(x_ref, g_ref, whg_ref, wmla_ref, wgate_ref, lb_ref, nq_ref, nkv_ref,
                   qs_ref, lf_ref, kk_ref, v_ref, og_ref, mla_ref, gates_ref):
    x = x_ref[...]
    d = x.shape[-1]
    h = (x * _rms_scale(x, d) * g_ref[...]).astype(BF16)

    qs_ref[...] = _silu(_dot(h, whg_ref[:, 0:HG_F])).astype(BF16)
    for direction in range(2):
        cols = slice(HG_F * (1 + direction), HG_F * (2 + direction))
        out_cols = slice(HG_F * direction, HG_F * (direction + 1))
        lb = lb_ref[direction:direction + 1, :]
        f = lb + (1.0 - lb) * _sigmoid(_dot(h, whg_ref[:, cols]))
        lf_ref[:, out_cols] = jnp.log(f)
        kk_ref[:, out_cols] = (1.0 - f).astype(BF16)
    v_ref[...] = _dot(h, whg_ref[:, 3 * HG_F:3 * HG_F + HG_V]).astype(BF16)
    og_ref[...] = _silu(_dot(h, whg_ref[:, 3 * HG_F + HG_V:3 * HG_F + 2 * HG_V])).astype(BF16)

    zm = _dot(h, wmla_ref[...])
    cq = zm[:, 0:MLA_Q_RANK]
    mla_ref[:, 0:MLA_Q_RANK] = (cq * _rms_scale(cq, MLA_Q_RANK) * nq_ref[...]).astype(BF16)
    ckv = zm[:, MLA_Q_RANK:MLA_Q_RANK + MLA_KV_RANK]
    mla_ref[:, MLA_Q_RANK:MLA_Q_RANK + MLA_KV_RANK] = (
        ckv * _rms_scale(ckv, MLA_KV_RANK) * nkv_ref[...]).astype(BF16)
    mla_ref[:, MLA_Q_RANK + MLA_KV_RANK:] = zm[:, MLA_Q_RANK + MLA_KV_RANK:].astype(BF16)

    for half in range(2):
        cols = slice(d * half, d * (half + 1))
        gates_ref[:, cols] = _sigmoid(_dot(h, wgate_ref[:, cols])).astype(BF16)


def _inproj(x2, ln_mix, whg, wmla, wgate, lb, nq, nkv):
    t, d = x2.shape
    tm = INPROJ_ROW_TILE
    mla_w = wmla.shape[1]
    row = lambda w: pl.BlockSpec((tm, w), lambda i: (i, 0))
    out_shape = (
        jax.ShapeDtypeStruct((t, HG_F), BF16),
        jax.ShapeDtypeStruct((t, 2 * HG_F), F32),
        jax.ShapeDtypeStruct((t, 2 * HG_F), BF16),
        jax.ShapeDtypeStruct((t, HG_V), BF16),
        jax.ShapeDtypeStruct((t, HG_V), BF16),
        jax.ShapeDtypeStruct((t, mla_w), BF16),
        jax.ShapeDtypeStruct((t, 2 * d), BF16),
    )
    return pl.pallas_call(
        _inproj_kernel,
        grid=(t // tm,),
        in_specs=[row(d), _const_spec((1, d)), _const_spec(whg.shape), _const_spec(wmla.shape),
                  _const_spec(wgate.shape), _const_spec(lb.shape), _const_spec(nq.shape),
                  _const_spec(nkv.shape)],
        out_specs=(row(HG_F), row(2 * HG_F), row(2 * HG_F), row(HG_V), row(HG_V), row(mla_w),
                   row(2 * d)),
        out_shape=out_shape,
        compiler_params=_cparams(("parallel",)),
        name="inproj",
    )(x2, ln_mix, whg, wmla, wgate, lb, nq, nkv)


def _hgrn_kernel(qs_ref, lff_ref, lfb_ref, kf_ref, kb_ref, v_ref, og_ref, onorm_ref, out_ref,
                 state_ref, ofw_ref, obw_ref, qt_scr, u_scr, dec_scr, g_scr, k_scr, v_scr):
    c = HG_CHUNK
    s_len = qs_ref.shape[1]
    n_chunks = s_len // c
    hb = HG_HEADS_PER_STEP
    kd = HG_KDIM

    row = lax.broadcasted_iota(jnp.int32, (c, c), 0)
    col = lax.broadcasted_iota(jnp.int32, (c, c), 1)
    masks = (row >= col, row <= col)
    rowc = lax.broadcasted_iota(jnp.int32, (c, kd), 0)
    o_refs = (ofw_ref, obw_ref)

    def cum_log_decay(lf, direction):
        n = lf.shape[0]
        in_chunk = lax.broadcasted_iota(jnp.int32, lf.shape, 0) % c
        g = lf
        k = 1
        while k < c:
            if direction == 0:
                g = g + jnp.where(in_chunk >= k, pltpu.roll(g, k, 0), 0.0)
            else:
                g = g + jnp.where(in_chunk < c - k, pltpu.roll(g, n - k, 0), 0.0)
            k *= 2
        return g

    def load(j):
        chains = []
        for hh in range(hb):
            lanes = slice(hh * kd, (hh + 1) * kd)
            for direction in range(2):
                chunk = j if direction == 0 else n_chunks - 1 - j
                rows = pl.ds(pl.multiple_of(chunk * c, c), c)
                g = cum_log_decay((lff_ref, lfb_ref)[direction][0, rows, lanes], direction)
                total = g[c - 1:c, :] if direction == 0 else g[0:1, :]
                chains.append(dict(
                    idx=hh * 2 + direction, direction=direction, rows=rows, lanes=lanes, g=g,
                    total=total,
                    q=qs_ref[0, rows, lanes].astype(F32),
                    k=(kf_ref, kb_ref)[direction][0, rows, lanes].astype(F32),
                    v=v_ref[0, rows, lanes]))
        return chains

    def finish(ch, o, kdec_t_v_scaled):
        (ofw_ref, obw_ref)[ch["direction"]][ch["rows"], ch["lanes"]] = o
        state_ref[ch["idx"]] = kdec_t_v_scaled

    def fast(ch):
        g = ch["g"]
        qt = (ch["q"] * jnp.exp(g)).astype(BF16)
        kt = (ch["k"] * jnp.exp(-g)).astype(BF16)
        st = state_ref[ch["idx"]]
        sc = jnp.where(masks[ch["direction"]], _dot_nt(qt, kt), 0.0)
        o = _dot_nt(qt, st.astype(BF16)) + _dot(sc.astype(BF16), ch["v"])
        finish(ch, o, (st + _dot_tn(ch["v"], kt)) * jnp.exp(ch["total"]))

    def robust(ch):
        g = ch["g"]
        direction = ch["direction"]
        st = state_ref[ch["idx"]]
        o0 = _dot_nt((ch["q"] * jnp.exp(g)).astype(BF16), st.astype(BF16))
        slot = ch["idx"]
        g_scr[slot] = g
        k_scr[slot] = ch["k"]
        v_scr[slot] = ch["v"].astype(F32)
        q = ch["q"]

        def body(s, acc):
            g_s = g_scr[slot, pl.ds(s, 1), :]
            seen = (rowc >= s) if direction == 0 else (rowc <= s)
            decay = jnp.where(seen, jnp.exp(jnp.minimum(g - g_s, 0.0)), 0.0)
            a = jnp.sum(q * decay * k_scr[slot, pl.ds(s, 1), :], axis=-1, keepdims=True)
            return acc + a * v_scr[slot, pl.ds(s, 1), :]

        o = lax.fori_loop(0, c, body, o0)
        kdec = (ch["k"] * jnp.exp(ch["total"] - g)).astype(BF16)
        finish(ch, o, st * jnp.exp(ch["total"]) + _dot_tn(ch["v"], kdec))

    def step(j, carry):
        chains = load(j)
        lowest = chains[0]["total"]
        for ch in chains[1:]:
            lowest = jnp.minimum(lowest, ch["total"])
        safe = jnp.min(lowest) >= HG_SAFE_LOG_DECAY

        @pl.when(safe)
        def _():
            for ch in chains:
                fast(ch)

        @pl.when(jnp.logical_not(safe))
        def _():
            for ch in chains:
                robust(ch)

        return carry

    grp = HG_CHUNKS_PER_MATMUL if n_chunks % HG_CHUNKS_PER_MATMUL == 0 else 1
    gr = grp * c
    n_groups = n_chunks // grp
    grow = lax.broadcasted_iota(jnp.int32, (gr, gr), 0)
    gcol = lax.broadcasted_iota(jnp.int32, (gr, gr), 1)
    same_chunk = (grow // c) == (gcol // c)
    gmasks = (same_chunk & (grow >= gcol), same_chunk & (grow <= gcol))
    chunk_of_row = lax.broadcasted_iota(jnp.int32, (gr, kd), 0) // c

    def block_diag(a):
        return jnp.concatenate([jnp.where(chunk_of_row == i, a, jnp.zeros_like(a))
                                for i in range(grp)], axis=1)

    def phase_a(gi, lowest):
        rows = pl.ds(pl.multiple_of(gi * gr, gr), gr)
        for hh in range(hb):
            lanes = slice(hh * kd, (hh + 1) * kd)
            v = v_ref[0, rows, lanes]
            v_t = v.T
            q = qs_ref[0, rows, lanes].astype(F32)
            for direction in range(2):
                idx = hh * 2 + direction
                g = cum_log_decay((lff_ref, lfb_ref)[direction][0, rows, lanes], direction)
                k = (kf_ref, kb_ref)[direction][0, rows, lanes].astype(F32)
                qt = (q * jnp.exp(g)).astype(BF16)
                kt = (k * jnp.exp(-g)).astype(BF16)
                qt_scr[idx, rows, :] = qt
                sc = jnp.where(gmasks[direction], _dot_nt(qt, kt), 0.0)
                o_refs[direction][rows, lanes] = _dot(sc.astype(BF16), v)
                u = _dot(v_t, block_diag(kt))
                for i in range(grp):
                    u_scr[idx, gi * grp + i] = u[:, i * kd:(i + 1) * kd]
                    edge = i * c + (c - 1 if direction == 0 else 0)
                    total = g[edge:edge + 1, :]
                    lowest = jnp.minimum(lowest, total)
                    dec_scr[idx, pl.ds(gi * grp + i, 1), :] = jnp.exp(total)
        return lowest

    lowest = lax.fori_loop(0, n_groups, phase_a, jnp.zeros((1, kd), F32))
    all_safe = jnp.min(lowest) >= HG_SAFE_LOG_DECAY
    state_ref[...] = jnp.zeros_like(state_ref)

    @pl.when(all_safe)
    def _():
        def phase_b(j, carry):
            for hh in range(hb):
                lanes = slice(hh * kd, (hh + 1) * kd)
                for direction in range(2):
                    idx = hh * 2 + direction
                    gi = j if direction == 0 else n_groups - 1 - j
                    rows = pl.ds(pl.multiple_of(gi * gr, gr), gr)
                    st = state_ref[idx]
                    seen = [None] * grp
                    for i in (range(grp) if direction == 0 else reversed(range(grp))):
                        ci = gi * grp + i
                        seen[i] = st.astype(BF16)
                        st = (st + u_scr[idx, ci]) * dec_scr[idx, pl.ds(ci, 1), :]
                    state_ref[idx] = st
                    o_refs[direction][rows, lanes] += _dot_nt(
                        block_diag(qt_scr[idx, rows, :]), jnp.concatenate(seen, axis=1))
            return carry

        lax.fori_loop(0, n_groups, phase_b, 0)

    @pl.when(jnp.logical_not(all_safe))
    def _():
        lax.fori_loop(0, n_chunks, step, 0)

    blk = 256 if s_len % 256 == 0 else c

    def epilogue(i, carry):
        rows = pl.ds(pl.multiple_of(i * blk, blk), blk)
        for hh in range(hb):
            lanes = slice(hh * kd, (hh + 1) * kd)
            o = ofw_ref[rows, lanes] + obw_ref[rows, lanes]
            y = o * _rms_scale(o, HG_VDIM) * onorm_ref[...]
            out_ref[0, rows, lanes] = (y * og_ref[0, rows, lanes].astype(F32)).astype(BF16)
        return carry

    lax.fori_loop(0, s_len // blk, epilogue, 0)


def _hgrn(qs, lf, kk, v, og, onorm):
    b, s, _ = qs.shape
    hb = HG_HEADS_PER_STEP
    w = hb * HG_KDIM
    n_hsteps = HG_HEADS // hb
    fwd = pl.BlockSpec((1, s, w), lambda bi, hi: (bi, 0, hi))
    bwd = pl.BlockSpec((1, s, w), lambda bi, hi: (bi, 0, n_hsteps + hi))
    return pl.pallas_call(
        _hgrn_kernel,
        grid=(b, n_hsteps),
        in_specs=[fwd, fwd, bwd, fwd, bwd, fwd, fwd, _const_spec(onorm.shape)],
        out_specs=fwd,
        out_shape=jax.ShapeDtypeStruct((b, s, HG_V), BF16),
        scratch_shapes=[
            pltpu.VMEM((2 * hb, HG_VDIM, HG_KDIM), F32),
            pltpu.VMEM((s, w), F32),
            pltpu.VMEM((s, w), F32),
            pltpu.VMEM((2 * hb, s, HG_KDIM), BF16),
            pltpu.VMEM((2 * hb, s // HG_CHUNK, HG_VDIM, HG_KDIM), F32),
            pltpu.VMEM((2 * hb, max(s // HG_CHUNK, 8), HG_KDIM), F32),
            pltpu.VMEM((2 * hb, HG_CHUNK, HG_KDIM), F32),
            pltpu.VMEM((2 * hb, HG_CHUNK, HG_KDIM), F32),
            pltpu.VMEM((2 * hb, HG_CHUNK, HG_VDIM), F32),
        ],
        compiler_params=_cparams(("parallel", "parallel")),
        name="hgrn",
    )(qs, lf, lf, kk, kk, v, og, onorm)


def _mlaprep_kernel(mla_ref, pos_ref, pos_t_ref, wq_ref, wk_t_ref, wv_ref, gq_ref, gk_t_ref,
                    rope_ref, freq_t_ref, q_ref, k_t_ref, v_ref):
    cq = mla_ref[:, 0:MLA_Q_RANK]
    ckr = mla_ref[:, MLA_Q_RANK:]
    half = MLA_ROPE // 2

    ang = pos_ref[...] * rope_ref[0:1, :]
    cos = jnp.cos(ang)
    sin = jnp.sin(ang)
    sin_lo = sin * rope_ref[1:2, :]
    sin_hi = sin * rope_ref[2:3, :]
    qm = _dot(cq, wq_ref[...])
    for hd in range(MLA_HEADS):
        lanes = slice(hd * LANES, (hd + 1) * LANES)
        x = qm[:, lanes]
        y = x * (_rms_scale(x, MLA_QK) * MLA_QK ** -0.5) * gq_ref[...]
        y = y * cos + pltpu.roll(y, half, 1) * sin_lo + pltpu.roll(y, LANES - half, 1) * sin_hi
        q_ref[:, lanes] = y.astype(BF16)

    km_t = _dot_nt(wk_t_ref[...], ckr)
    reps = km_t.shape[1] // LANES
    lane_tile = lambda a: jnp.concatenate([a] * reps, axis=1)
    ang_t = lane_tile(freq_t_ref[...]) * pos_t_ref[...]
    cos_t = jnp.cos(ang_t)
    sin_t = jnp.sin(ang_t)
    gain_t = lane_tile(gk_t_ref[...])
    for hd in range(MLA_HEADS):
        r0 = hd * LANES
        x = km_t[r0:r0 + LANES, :]
        y = x * lax.rsqrt(jnp.sum(x * x, axis=0, keepdims=True) * (1.0 / MLA_QK) + EPS) * gain_t
        x1 = y[MLA_NOPE:MLA_NOPE + half, :]
        x2 = y[MLA_NOPE + half:MLA_QK, :]
        k_t_ref[r0:r0 + MLA_NOPE, :] = y[0:MLA_NOPE, :].astype(BF16)
        k_t_ref[r0 + MLA_NOPE:r0 + MLA_NOPE + half, :] = (x1 * cos_t - x2 * sin_t).astype(BF16)
        k_t_ref[r0 + MLA_NOPE + half:r0 + MLA_QK, :] = (x2 * cos_t + x1 * sin_t).astype(BF16)
        k_t_ref[r0 + MLA_QK:r0 + LANES, :] = jnp.zeros((LANES - MLA_QK, km_t.shape[1]), BF16)

    v_ref[...] = _dot(ckr, wv_ref[...]).astype(BF16)


def _mlaprep(mla_in, posf, wq, wk_t, wv, gq, gk_t, rope_tab, freq_t):
    t = mla_in.shape[0]
    tm = ROW_TILE
    row = lambda w: pl.BlockSpec((tm, w), lambda i: (i, 0))
    hw = MLA_HEADS * LANES
    vw = MLA_HEADS * MLA_VDIM
    return pl.pallas_call(
        _mlaprep_kernel,
        grid=(t // tm,),
        in_specs=[row(mla_in.shape[1]), row(1), pl.BlockSpec((1, tm), lambda i: (0, i)),
                  _const_spec(wq.shape), _const_spec(wk_t.shape), _const_spec(wv.shape),
                  _const_spec(gq.shape), _const_spec(gk_t.shape), _const_spec(rope_tab.shape),
                  _const_spec(freq_t.shape)],
        out_specs=(row(hw), pl.BlockSpec((hw, tm), lambda i: (0, i)), row(vw)),
        out_shape=(jax.ShapeDtypeStruct((t, hw), BF16), jax.ShapeDtypeStruct((hw, t), BF16),
                   jax.ShapeDtypeStruct((t, vw), BF16)),
        compiler_params=_cparams(("parallel",)),
        name="mlaprep",
    )(mla_in, posf, posf.reshape(1, t), wq, wk_t, wv, gq, gk_t, rope_tab, freq_t)


def _attn_kernel(q_ref, k_t_ref, v_ref, o_ref):
    for j in range(ATTN_HEADS_PER_STEP):
        lanes = slice(j * LANES, (j + 1) * LANES)
        s = _dot(q_ref[0, :, lanes], k_t_ref[lanes, :])
        p = jnp.exp(s - jnp.max(s, axis=-1, keepdims=True))
        l = jnp.sum(p, axis=-1, keepdims=True)
        vl = slice(j * MLA_VDIM, (j + 1) * MLA_VDIM)
        o = _dot(p.astype(BF16), v_ref[0, :, vl])
        o_ref[0, :, vl] = (o / l).astype(BF16)


def _attn(q, k_t, v):
    b, s, _ = q.shape
    tq = min(ATTN_Q_TILE, s)
    nh = ATTN_HEADS_PER_STEP
    return pl.pallas_call(
        _attn_kernel,
        grid=(b, MLA_HEADS // nh, s // tq),
        in_specs=[pl.BlockSpec((1, tq, nh * LANES), lambda bi, hp, i: (bi, i, hp)),
                  pl.BlockSpec((nh * LANES, s), lambda bi, hp, i: (hp, bi)),
                  pl.BlockSpec((1, s, nh * MLA_VDIM), lambda bi, hp, i: (bi, 0, hp))],
        out_specs=pl.BlockSpec((1, tq, nh * MLA_VDIM), lambda bi, hp, i: (bi, i, hp)),
        out_shape=jax.ShapeDtypeStruct((b, s, MLA_HEADS * MLA_VDIM), BF16),
        compiler_params=_cparams(("parallel", "parallel", "arbitrary")),
        name="attn",
    )(q, k_t, v)


def _merge_kernel(x_ref, ya_ref, yb_ref, gates_ref, woa_ref, wob_ref, wout_ref, lnm_ref, wr_ref,
                  br_ref, x1_ref, hm8_ref, route_ref):
    d = x_ref.shape[-1]
    y_a = _dot(ya_ref[...], woa_ref[...])
    y_b = _dot(yb_ref[...], wob_ref[...])
    merged = gates_ref[:, 0:d].astype(F32) * y_a + gates_ref[:, d:2 * d].astype(F32) * y_b
    x1 = x_ref[...] + _dot(merged.astype(BF16), wout_ref[...])
    x1_ref[...] = x1
    hm = x1 * _rms_scale(x1, d) * lnm_ref[...]
    _store_token_tiles(hm8_ref, 0, hm)

    hm_hi = hm.astype(BF16)
    hm_lo = (hm - hm_hi.astype(F32)).astype(BF16)
    hh = _dot(hm_hi, wr_ref[...])
    logits = (hh[:, 0:LANES] + hh[:, LANES:2 * LANES] + _dot(hm_lo, wr_ref[:, 0:LANES])
              + br_ref[...])
    lane = lax.broadcasted_iota(jnp.int32, logits.shape, 1)
    neg = -jnp.inf
    big = jnp.int32(2 ** 30)
    is_group = lane < N_GROUPS
    gl = jnp.where(is_group, logits, neg)
    gmax = jnp.max(gl, axis=-1, keepdims=True)
    p_group = 1.0 / jnp.sum(jnp.where(is_group, jnp.exp(gl - gmax), 0.0), axis=-1, keepdims=True)
    g_sel = jnp.min(jnp.where(gl == gmax, lane, big), axis=-1, keepdims=True)
    lo = N_GROUPS + g_sel * EXPERTS_PER_GROUP
    in_group = (lane >= lo) & (lane < lo + EXPERTS_PER_GROUP)
    el = jnp.where(in_group, logits, neg)
    v1 = jnp.max(el, axis=-1, keepdims=True)
    i1 = jnp.min(jnp.where(el == v1, lane, big), axis=-1, keepdims=True)
    el2 = jnp.where(lane == i1, neg, el)
    v2 = jnp.max(el2, axis=-1, keepdims=True)
    i2 = jnp.min(jnp.where(el2 == v2, lane, big), axis=-1, keepdims=True)
    e21 = jnp.exp(v2 - v1)
    w1 = p_group / (1.0 + e21)
    w2 = w1 * e21
    route = jnp.where(lane == 0, (i1 - N_GROUPS).astype(F32),
                      jnp.where(lane == 1, (i2 - N_GROUPS).astype(F32),
                                jnp.where(lane == 2, w1, jnp.where(lane == 3, w2, 0.0))))
    route_ref[...] = route


def _merge(x2, ya, yb, gates, woa, wob, wout, lnm, wr, br):
    t, d = x2.shape
    tm = MERGE_ROW_TILE
    row = lambda w: pl.BlockSpec((tm, w), lambda i: (i, 0))
    return pl.pallas_call(
        _merge_kernel,
        grid=(t // tm,),
        in_specs=[row(d), row(ya.shape[1]), row(yb.shape[1]), row(2 * d), _const_spec(woa.shape),
                  _const_spec(wob.shape), _const_spec(wout.shape), _const_spec(lnm.shape),
                  _const_spec(wr.shape), _const_spec(br.shape)],
        out_specs=(row(d), pl.BlockSpec((tm * TOKEN_ROWS, LANES), lambda i: (i, 0)), row(LANES)),
        out_shape=(jax.ShapeDtypeStruct((t, d), F32),
                   jax.ShapeDtypeStruct((t * TOKEN_ROWS, LANES), jnp.uint32),
                   jax.ShapeDtypeStruct((t, LANES), F32)),
        compiler_params=_cparams(("parallel",)),
        name="merge",
    )(x2, ya, yb, gates, woa, wob, wout, lnm, wr, br)


_HIGH_HALF = 0xFFFF0000


def _store_token_tiles(ref4, tok0, val):
    n, width = val.shape
    as_bits = lambda a: lax.bitcast_convert_type(a.astype(BF16).astype(F32), jnp.uint32)
    words = (as_bits(val[:, :width // 2]) >> 16) | (as_bits(val[:, width // 2:]) & jnp.uint32(_HIGH_HALF))
    for j in range(TOKEN_ROWS):
        ref4[pl.ds(tok0 * TOKEN_ROWS + j, n, stride=TOKEN_ROWS), :] = words[:, j * LANES:(j + 1) * LANES]


def _load_token_tiles(ref4, tok0, n):
    words = jnp.concatenate(
        [ref4[pl.ds(tok0 * TOKEN_ROWS + j, n, stride=TOKEN_ROWS), :] for j in range(TOKEN_ROWS)],
        axis=1)
    low = lax.bitcast_convert_type(words << 16, F32).astype(BF16)
    high = lax.bitcast_convert_type(words & jnp.uint32(_HIGH_HALF), F32).astype(BF16)
    return jnp.concatenate([low, high], axis=1)


def _token_tile(ref8, tok):
    return ref8.at[pl.ds(pl.multiple_of(tok * TOKEN_ROWS, TOKEN_ROWS), TOKEN_ROWS)]


def _dispatch_kernel(dest_ref, hm8_ref, xb_in_ref, xb8_ref, sem):
    del xb_in_ref
    n_tok = DISPATCH_TOKENS
    for r in range(n_tok):
        src = hm8_ref.at[pl.ds(r * TOKEN_ROWS, TOKEN_ROWS)]
        for slot in range(2):
            pltpu.make_async_copy(src, _token_tile(xb8_ref, dest_ref[2 * r + slot]),
                                  sem).start(priority=slot)
    for _ in range(2):
        pltpu.make_async_copy(hm8_ref, xb8_ref.at[pl.ds(0, n_tok * TOKEN_ROWS)], sem).wait()


def _dispatch(dest_flat, hm8, xb_init):
    t = hm8.shape[0] // TOKEN_ROWS
    return pl.pallas_call(
        _dispatch_kernel,
        grid=(t // DISPATCH_TOKENS,),
        in_specs=[pl.BlockSpec((2 * DISPATCH_TOKENS,), lambda i: (i,), memory_space=pltpu.SMEM),
                  pl.BlockSpec((DISPATCH_TOKENS * TOKEN_ROWS, LANES), lambda i: (i, 0)),
                  pl.BlockSpec(memory_space=pl.ANY)],
        out_specs=pl.BlockSpec(memory_space=pl.ANY),
        out_shape=jax.ShapeDtypeStruct(xb_init.shape, xb_init.dtype),
        scratch_shapes=[pltpu.SemaphoreType.DMA(())],
        input_output_aliases={2: 0},
        compiler_params=_cparams(("arbitrary",)),
        name="dispatch",
    )(dest_flat, hm8, xb_init)


def _experts_kernel(be_ref, nu_ref, xb8_ref, w1a_ref, w3a_ref, w2a_ref, w1b_ref, w3b_ref, w2b_ref,
                    yb8_ref, w1c, w3c, w2c):
    j = pl.program_id(0)
    for half, w_refs in enumerate(((w1a_ref, w3a_ref, w2a_ref), (w1b_ref, w3b_ref, w2b_ref))):
        blk = 2 * j + half
        used = blk < nu_ref[0]
        new_expert = jnp.logical_or(blk == 0, be_ref[blk] != be_ref[jnp.maximum(blk - 1, 0)])
        tok0 = half * MOE_ROWS

        @pl.when(jnp.logical_and(used, new_expert))
        def _():
            w1c[...] = w_refs[0][0].astype(BF16)
            w3c[...] = w_refs[1][0].astype(BF16)
            w2c[...] = w_refs[2][0].astype(BF16)

        @pl.when(used)
        def _():
            x = _load_token_tiles(xb8_ref, tok0, MOE_ROWS)
            hmid = (_silu(_dot(x, w1c[...])) * _dot(x, w3c[...])).astype(BF16)
            _store_token_tiles(yb8_ref, tok0, _dot(hmid, w2c[...]))

        @pl.when(jnp.logical_not(used))
        def _():
            rows = pl.ds(tok0 * TOKEN_ROWS, MOE_ROWS * TOKEN_ROWS)
            yb8_ref[rows, :] = jnp.zeros((MOE_ROWS * TOKEN_ROWS, LANES), jnp.uint32)


def _experts(block_expert, n_used, xb8, w1, w3, w2):
    nblk = block_expert.shape[0]
    nsteps = nblk // 2
    d = w1.shape[1]
    de = w1.shape[2]
    step_rows = 2 * MOE_ROWS * TOKEN_ROWS
    rows_in = lambda j, be, nu: (jnp.minimum(j, (nu[0] - 1) // 2), 0)
    wsel_a = lambda j, be, nu: (be[2 * j], 0, 0)
    wsel_b = lambda j, be, nu: (be[2 * j + 1], 0, 0)
    return pl.pallas_call(
        _experts_kernel,
        grid_spec=pltpu.PrefetchScalarGridSpec(
            num_scalar_prefetch=2,
            grid=(nsteps,),
            in_specs=[pl.BlockSpec((step_rows, LANES), rows_in),
                      pl.BlockSpec((1, d, de), wsel_a), pl.BlockSpec((1, d, de), wsel_a),
                      pl.BlockSpec((1, de, d), wsel_a),
                      pl.BlockSpec((1, d, de), wsel_b), pl.BlockSpec((1, d, de), wsel_b),
                      pl.BlockSpec((1, de, d), wsel_b)],
            out_specs=pl.BlockSpec((step_rows, LANES), lambda j, be, nu: (j, 0)),
            scratch_shapes=[pltpu.VMEM((d, de), BF16), pltpu.VMEM((d, de), BF16),
                            pltpu.VMEM((de, d), BF16)]),
        out_shape=jax.ShapeDtypeStruct((nblk * MOE_ROWS * TOKEN_ROWS, LANES), jnp.uint32),
        compiler_params=_cparams(("arbitrary",)),
        name="experts",
    )(block_expert, n_used, xb8, w1, w3, w2, w1, w3, w2)


def _ple_kernel(dest_ref, destn_ref, x1_ref, route_ref, p_ref, ln_ref, wg_ref, wp_ref, yb8_ref,
                out_ref, ya0, ya1, yb0, yb1, sems):
    j = pl.program_id(0)
    last = pl.num_programs(0) - 1
    tm = ROW_TILE
    d = x1_ref.shape[1]

    def start(idx_ref, base, bufs, sem):
        for r in range(tm):
            for slot in range(2):
                pltpu.make_async_copy(_token_tile(yb8_ref, idx_ref[base + 2 * r + slot]),
                                      bufs[slot].at[pl.ds(r * TOKEN_ROWS, TOKEN_ROWS)],
                                      sem).start(priority=slot)

    def wait(bufs, sem):
        for buf in bufs:
            pltpu.make_async_copy(yb8_ref.at[pl.ds(0, tm * TOKEN_ROWS)], buf, sem).wait()

    def compute(bufs, rows):
        x2 = (x1_ref[rows, :]
              + route_ref[rows, 2:3] * _load_token_tiles(bufs[0], 0, tm).astype(F32)
              + route_ref[rows, 3:4] * _load_token_tiles(bufs[1], 0, tm).astype(F32))
        h = (x2 * _rms_scale(x2, d) * ln_ref[...]).astype(BF16)
        pp = _dot(p_ref[rows, :].astype(BF16), wp_ref[...])
        out_ref[rows, :] = x2 + pp * _sigmoid(_dot(h, wg_ref[...]))

    @pl.when(j == 0)
    def _():
        start(dest_ref, 0, (ya0, ya1), sems.at[0])

    wait((ya0, ya1), sems.at[0])
    start(dest_ref, 2 * tm, (yb0, yb1), sems.at[1])
    compute((ya0, ya1), slice(0, tm))

    wait((yb0, yb1), sems.at[1])
    start(destn_ref, 0, (ya0, ya1), sems.at[0])
    compute((yb0, yb1), slice(tm, 2 * tm))

    @pl.when(j == last)
    def _():
        wait((ya0, ya1), sems.at[0])


def _ple(dest_flat, x1, route, p2, ln, wg, wp, yb):
    t, d = x1.shape
    tm = ROW_TILE
    nsteps = t // (2 * tm)
    row = lambda w: pl.BlockSpec((2 * tm, w), lambda j: (j, 0))
    return pl.pallas_call(
        _ple_kernel,
        grid=(nsteps,),
        in_specs=[pl.BlockSpec((4 * tm,), lambda j: (j,), memory_space=pltpu.SMEM),
                  pl.BlockSpec((4 * tm,), lambda j: (jnp.minimum(j + 1, nsteps - 1),),
                               memory_space=pltpu.SMEM),
                  row(d), row(LANES), row(p2.shape[1]), _const_spec(ln.shape),
                  _const_spec(wg.shape), _const_spec(wp.shape),
                  pl.BlockSpec(memory_space=pl.ANY)],
        out_specs=row(d),
        out_shape=jax.ShapeDtypeStruct((t, d), F32),
        scratch_shapes=([pltpu.VMEM((tm * TOKEN_ROWS, LANES), jnp.uint32)] * 4
                        + [pltpu.SemaphoreType.DMA((2,))]),
        compiler_params=_cparams(("arbitrary",)),
        name="ple",
    )(dest_flat, dest_flat, x1, route, p2, ln, wg, wp, yb)


def _head_pad(w, width):
    r = w.shape[0]
    w = w.reshape(r, MLA_HEADS, width)
    return jnp.pad(w, ((0, 0), (0, 0), (0, LANES - width))).reshape(r, MLA_HEADS * LANES)


def _rope_table():
    half = MLA_ROPE // 2
    inv_freq = ROPE_THETA ** (-np.arange(half, dtype=np.float32) / half)
    tab = np.zeros((8, LANES), np.float32)
    tab[0, MLA_NOPE:MLA_NOPE + half] = inv_freq
    tab[0, MLA_NOPE + half:MLA_QK] = inv_freq
    tab[1, MLA_NOPE + half:MLA_QK] = 1.0
    tab[2, MLA_NOPE:MLA_NOPE + half] = -1.0
    freq_t = np.broadcast_to(inv_freq[:, None], (half, LANES))
    return jnp.asarray(tab), jnp.asarray(freq_t)


def _route_tables(ids):
    a = ids.shape[0] * 2
    e = ids.reshape(a)
    onehot = (e[:, None] == jnp.arange(N_EXPERTS, dtype=jnp.int32)[None, :]).astype(jnp.int32)
    csum = jnp.cumsum(onehot, axis=0)
    counts = csum[-1]
    rank = jnp.sum(csum * onehot, axis=1) - 1
    pcounts = (counts + MOE_ROWS - 1) // MOE_ROWS * MOE_ROWS
    pends = jnp.cumsum(pcounts)
    pstarts = pends - pcounts
    dest = (jnp.sum(onehot * pstarts[None, :], axis=1) + rank).astype(jnp.int32)
    nblk = (a + N_EXPERTS * (MOE_ROWS - 1)) // MOE_ROWS
    nblk += nblk % 2
    n_used = (pends[-1] // MOE_ROWS).astype(jnp.int32)
    blk_start = jnp.arange(nblk, dtype=jnp.int32) * MOE_ROWS
    block_expert = jnp.searchsorted(pends, jnp.minimum(blk_start, pends[-1] - 1), side='right')
    block_expert = jnp.minimum(block_expert, N_EXPERTS - 1).astype(jnp.int32)
    return dest, block_expert, n_used.reshape(1), nblk


def kernel(x, p, positions, ln_mix, w_in, hg_lb, hg_onorm, w_oA, mla_qa_norm, mla_kva_norm, w_uq,
           w_ukv, q_norm, k_norm, w_oB, w_out, ln_moe, w_rg, b_rg, w_re, b_re, w1, w3, w2, ln_ple,
           w_ple_gate, w_ple_proj):
    b, s, d = x.shape
    t = b * s
    depth = w_in.shape[0]
    lb_all = jnp.cumsum(jax.nn.softmax(hg_lb.astype(F32), axis=1), axis=1)
    posf = positions.astype(F32).reshape(t, 1)
    rope_tab, freq_t = _rope_table()
    xc = x.reshape(t, d)

    for layer in range(depth):
        wi = w_in[layer]
        n_hg = 3 * HG_F + 2 * HG_V
        n_mla = MLA_Q_RANK + MLA_KV_RANK + MLA_ROPE
        mla_w = -(-n_mla // LANES) * LANES
        whg = wi[:, :n_hg].astype(BF16)
        wmla = jnp.pad(wi[:, n_hg:n_hg + n_mla], ((0, 0), (0, mla_w - n_mla))).astype(BF16)
        wgate = wi[:, n_hg + n_mla:].astype(BF16)
        lb = lb_all[:, layer, :]

        qs, lf, kk, v, og, mla_in, gates = _inproj(
            xc, ln_mix[layer][None, :], whg, wmla, wgate, lb, mla_qa_norm[layer][None, :],
            mla_kva_norm[layer][None, :])

        r3 = lambda a: a.reshape(b, s, a.shape[-1])
        ya = _hgrn(r3(qs), r3(lf), r3(kk), r3(v), r3(og), hg_onorm[layer][None, :]).reshape(t, HG_V)

        kv_w = MLA_NOPE + MLA_VDIM
        wkv = w_ukv[layer].reshape(MLA_KV_RANK, MLA_HEADS, kv_w)
        ckr_w = mla_w - MLA_Q_RANK
        wq = _head_pad(w_uq[layer], MLA_QK).astype(BF16)
        wk_nope = _head_pad(wkv[:, :, :MLA_NOPE].reshape(MLA_KV_RANK, MLA_HEADS * MLA_NOPE), MLA_NOPE)
        place = np.zeros((ckr_w - MLA_KV_RANK, MLA_HEADS, LANES), np.float32)
        for j in range(MLA_ROPE):
            place[j, :, MLA_NOPE + j] = 1.0
        wk = jnp.concatenate([wk_nope, jnp.asarray(place.reshape(ckr_w - MLA_KV_RANK, -1))],
                             axis=0).astype(BF16)
        wv = jnp.pad(wkv[:, :, MLA_NOPE:].reshape(MLA_KV_RANK, MLA_HEADS * MLA_VDIM),
                     ((0, ckr_w - MLA_KV_RANK), (0, 0))).astype(BF16)
        gq = jnp.pad(q_norm[layer], (0, LANES - MLA_QK))[None, :]
        gk_t = jnp.broadcast_to(jnp.pad(k_norm[layer], (0, LANES - MLA_QK))[:, None], (LANES, LANES))
        qh, kh_t, vh = _mlaprep(mla_in, posf, wq, wk.T, wv, gq, gk_t, rope_tab, freq_t)
        yb = _attn(r3(qh), kh_t, r3(vh)).reshape(t, MLA_HEADS * MLA_VDIM)

        wr = jnp.pad(jnp.concatenate([w_rg[layer], w_re[layer]], axis=1),
                     ((0, 0), (0, LANES - N_GROUPS - N_EXPERTS)))
        wr_hi = wr.astype(BF16)
        wr = jnp.concatenate([wr_hi, (wr - wr_hi.astype(F32)).astype(BF16)], axis=1)
        br = jnp.pad(jnp.concatenate([b_rg[layer], b_re[layer]]),
                     (0, LANES - N_GROUPS - N_EXPERTS))[None, :]
        x1, hm8, route = _merge(xc, ya, yb, gates, w_oA[layer].astype(BF16), w_oB[layer].astype(BF16),
                               w_out[layer].astype(BF16), ln_moe[layer][None, :], wr, br)

        ids = route[:, 0:2].astype(jnp.int32)
        dest, block_expert, n_used, nblk = _route_tables(ids)
        xb8 = _dispatch(dest, hm8, jnp.zeros((nblk * MOE_ROWS * TOKEN_ROWS, LANES), jnp.uint32))
        yexp = _experts(block_expert, n_used, xb8, w1[layer], w3[layer], w2[layer])

        xc = _ple(dest, x1, route, p[layer].reshape(t, -1), ln_ple[layer][None, :],
                  w_ple_gate[layer].astype(BF16), w_ple_proj[layer].astype(BF16), yexp)
    return xc.reshape(b, s, d)
```

```python
import functools

import numpy as np
import jax
import jax.numpy as jnp
from jax import lax
from jax.experimental import pallas as pl
from jax.experimental.pallas import tpu as pltpu

F32 = jnp.float32
BF16 = jnp.bfloat16

HG_HEADS = 4
HG_KDIM = 128
HG_VDIM = 128
HG_F = HG_HEADS * HG_KDIM
HG_V = HG_HEADS * HG_VDIM
MLA_HEADS = 8
MLA_NOPE = 64
MLA_ROPE = 32
MLA_VDIM = 64
MLA_QK = MLA_NOPE + MLA_ROPE
MLA_Q_RANK = 256
MLA_KV_RANK = 128
ROPE_THETA = 10000.0
N_GROUPS = 8
EXPERTS_PER_GROUP = 8
N_EXPERTS = N_GROUPS * EXPERTS_PER_GROUP
D_EXPERT = 256
EPS = 1e-6

LANES = 128
TOKEN_ROWS = 4
VMEM_LIMIT_BYTES = 56 * 1024 * 1024

ROW_TILE = 256
MERGE_ROW_TILE = 512
INPROJ_ROW_TILE = 256
MLAPREP_ROW_TILE = 512
HG_CHUNK = 64
HG_HEADS_PER_STEP = 2
HG_CHUNKS_PER_MATMUL = 4
ATTN_Q_TILE = 512
ATTN_HEADS_PER_STEP = 8
MOE_ROWS = 256
DISPATCH_TOKENS = 512
HG_SAFE_LOG_DECAY = -60.0


def _cparams(sem):
    return pltpu.CompilerParams(dimension_semantics=sem, vmem_limit_bytes=VMEM_LIMIT_BYTES)


def _const_spec(shape):
    nd = len(shape)
    return pl.BlockSpec(shape, lambda *_: (0,) * nd)


def _sigmoid(x):
    return 1.0 / (1.0 + jnp.exp(-x))


def _silu(x):
    return x * _sigmoid(x)


def _rms_scale(x, n):
    return lax.rsqrt(jnp.sum(x * x, axis=-1, keepdims=True) * (1.0 / n) + EPS)


def _dot(a, b):
    return jnp.dot(a, b, preferred_element_type=F32)


def _dot_nt(a, b):
    return lax.dot_general(a, b, (((1,), (1,)), ((), ())), preferred_element_type=F32)


def _dot_tn(a, b):
    return lax.dot_general(a, b, (((0,), (0,)), ((), ())), preferred_element_type=F32)


def _chunk_cumsum(a, direction):
    n = a.shape[0]
    in_chunk = lax.broadcasted_iota(jnp.int32, a.shape, 0) % HG_CHUNK
    k = 1
    while k < HG_CHUNK:
        if direction == 0:
            a = a + jnp.where(in_chunk >= k, pltpu.roll(a, k, 0), 0.0)
        else:
            a = a + jnp.where(in_chunk < HG_CHUNK - k, pltpu.roll(a, n - k, 0), 0.0)
        k *= 2
    return a


def _inproj_kernel(x_ref, g_ref, whg_ref, wmla_ref, wgate_ref, lb_ref, nq_ref, nkv_ref,
                   qs_ref, lf_ref, kk_ref, v_ref, og_ref, mla_ref, gates_ref):
    x = x_ref[...]
    d = x.shape[-1]
    h = (x * _rms_scale(x, d) * g_ref[...]).astype(BF16)

    qs_ref[...] = _silu(_dot(h, whg_ref[:, 0:HG_F])).astype(BF16)
    for direction in range(2):
        cols = slice(HG_F * (1 + direction), HG_F * (2 + direction))
        out_cols = slice(HG_F * direction, HG_F * (direction + 1))
        lb = lb_ref[direction:direction + 1, :]
        f = lb + (1.0 - lb) * _sigmoid(_dot(h, whg_ref[:, cols]))
        lf_ref[:, out_cols] = jnp.log(f)
        kk_ref[:, out_cols] = (1.0 - f).astype(BF16)
    v_ref[...] = _dot(h, whg_ref[:, 3 * HG_F:3 * HG_F + HG_V]).astype(BF16)
    og_ref[...] = _silu(_dot(h, whg_ref[:, 3 * HG_F + HG_V:3 * HG_F + 2 * HG_V])).astype(BF16)

    zm = _dot(h, wmla_ref[...])
    cq = zm[:, 0:MLA_Q_RANK]
    mla_ref[:, 0:MLA_Q_RANK] = (cq * _rms_scale(cq, MLA_Q_RANK) * nq_ref[...]).astype(BF16)
    ckv = zm[:, MLA_Q_RANK:MLA_Q_RANK + MLA_KV_RANK]
    mla_ref[:, MLA_Q_RANK:MLA_Q_RANK + MLA_KV_RANK] = (
        ckv * _rms_scale(ckv, MLA_KV_RANK) * nkv_ref[...]).astype(BF16)
    mla_ref[:, MLA_Q_RANK + MLA_KV_RANK:] = zm[:, MLA_Q_RANK + MLA_KV_RANK:].astype(BF16)

    for half in range(2):
        cols = slice(d * half, d * (half + 1))
        gates_ref[:, cols] = _sigmoid(_dot(h, wgate_ref[:, cols])).astype(BF16)


def _inproj(x2, ln_mix, whg, wmla, wgate, lb, nq, nkv):
    t, d = x2.shape
    tm = INPROJ_ROW_TILE
    mla_w = wmla.shape[1]
    row = lambda w: pl.BlockSpec((tm, w), lambda i: (i, 0))
    out_shape = (
        jax.ShapeDtypeStruct((t, HG_F), BF16),
        jax.ShapeDtypeStruct((t, 2 * HG_F), F32),
        jax.ShapeDtypeStruct((t, 2 * HG_F), BF16),
        jax.ShapeDtypeStruct((t, HG_V), BF16),
        jax.ShapeDtypeStruct((t, HG_V), BF16),
        jax.ShapeDtypeStruct((t, mla_w), BF16),
        jax.ShapeDtypeStruct((t, 2 * d), BF16),
    )
    return pl.pallas_call(
        _inproj_kernel,
        grid=(t // tm,),
        in_specs=[row(d), _const_spec((1, d)), _const_spec(whg.shape), _const_spec(wmla.shape),
                  _const_spec(wgate.shape), _const_spec(lb.shape), _const_spec(nq.shape),
                  _const_spec(nkv.shape)],
        out_specs=(row(HG_F), row(2 * HG_F), row(2 * HG_F), row(HG_V), row(HG_V), row(mla_w),
                   row(2 * d)),
        out_shape=out_shape,
        compiler_params=_cparams(("parallel",)),
        name="inproj",
    )(x2, ln_mix, whg, wmla, wgate, lb, nq, nkv)


def _hgrn_kernel(qs_ref, lff_ref, lfb_ref, kf_ref, kb_ref, v_ref, og_ref, onorm_ref, out_ref,
                 state_ref, ofw_ref, obw_ref, qt_scr, u_scr, dec_scr, g_scr, k_scr, v_scr):
    c = HG_CHUNK
    s_len = qs_ref.shape[1]
    n_chunks = s_len // c
    hb = HG_HEADS_PER_STEP
    kd = HG_KDIM

    row = lax.broadcasted_iota(jnp.int32, (c, c), 0)
    col = lax.broadcasted_iota(jnp.int32, (c, c), 1)
    masks = (row >= col, row <= col)
    rowc = lax.broadcasted_iota(jnp.int32, (c, kd), 0)
    o_refs = (ofw_ref, obw_ref)

    def load(j):
        chains = []
        for hh in range(hb):
            lanes = slice(hh * kd, (hh + 1) * kd)
            for direction in range(2):
                chunk = j if direction == 0 else n_chunks - 1 - j
                rows = pl.ds(pl.multiple_of(chunk * c, c), c)
                g = _chunk_cumsum((lff_ref, lfb_ref)[direction][0, rows, lanes], direction)
                total = g[c - 1:c, :] if direction == 0 else g[0:1, :]
                chains.append(dict(
                    idx=hh * 2 + direction, direction=direction, rows=rows, lanes=lanes, g=g,
                    total=total,
                    q=qs_ref[0, rows, lanes].astype(F32),
                    k=(kf_ref, kb_ref)[direction][0, rows, lanes].astype(F32),
                    v=v_ref[0, rows, lanes]))
        return chains

    def finish(ch, o, kdec_t_v_scaled):
        (ofw_ref, obw_ref)[ch["direction"]][ch["rows"], ch["lanes"]] = o
        state_ref[ch["idx"]] = kdec_t_v_scaled

    def fast(ch):
        g = ch["g"]
        qt = (ch["q"] * jnp.exp(g)).astype(BF16)
        kt = (ch["k"] * jnp.exp(-g)).astype(BF16)
        st = state_ref[ch["idx"]]
        sc = jnp.where(masks[ch["direction"]], _dot_nt(qt, kt), 0.0)
        o = _dot_nt(qt, st.astype(BF16)) + _dot(sc.astype(BF16), ch["v"])
        finish(ch, o, (st + _dot_tn(ch["v"], kt)) * jnp.exp(ch["total"]))

    def robust(ch):
        g = ch["g"]
        direction = ch["direction"]
        st = state_ref[ch["idx"]]
        o0 = _dot_nt((ch["q"] * jnp.exp(g)).astype(BF16), st.astype(BF16))
        slot = ch["idx"]
        g_scr[slot] = g
        k_scr[slot] = ch["k"]
        v_scr[slot] = ch["v"].astype(F32)
        q = ch["q"]

        def body(s, acc):
            g_s = g_scr[slot, pl.ds(s, 1), :]
            seen = (rowc >= s) if direction == 0 else (rowc <= s)
            decay = jnp.where(seen, jnp.exp(jnp.minimum(g - g_s, 0.0)), 0.0)
            a = jnp.sum(q * decay * k_scr[slot, pl.ds(s, 1), :], axis=-1, keepdims=True)
            return acc + a * v_scr[slot, pl.ds(s, 1), :]

        o = lax.fori_loop(0, c, body, o0)
        kdec = (ch["k"] * jnp.exp(ch["total"] - g)).astype(BF16)
        finish(ch, o, st * jnp.exp(ch["total"]) + _dot_tn(ch["v"], kdec))

    def step(j, carry):
        chains = load(j)
        lowest = chains[0]["total"]
        for ch in chains[1:]:
            lowest = jnp.minimum(lowest, ch["total"])
        safe = jnp.min(lowest) >= HG_SAFE_LOG_DECAY

        @pl.when(safe)
        def _():
            for ch in chains:
                fast(ch)

        @pl.when(jnp.logical_not(safe))
        def _():
            for ch in chains:
                robust(ch)

        return carry

    grp = HG_CHUNKS_PER_MATMUL if n_chunks % HG_CHUNKS_PER_MATMUL == 0 else 1
    gr = grp * c
    n_groups = n_chunks // grp
    grow = lax.broadcasted_iota(jnp.int32, (gr, gr), 0)
    gcol = lax.broadcasted_iota(jnp.int32, (gr, gr), 1)
    same_chunk = (grow // c) == (gcol // c)
    gmasks = (same_chunk & (grow >= gcol), same_chunk & (grow <= gcol))
    chunk_of_row = lax.broadcasted_iota(jnp.int32, (gr, kd), 0) // c

    def block_diag(a):
        return jnp.concatenate([jnp.where(chunk_of_row == i, a, jnp.zeros_like(a))
                                for i in range(grp)], axis=1)

    def phase_a(gi, lowest):
        rows = pl.ds(pl.multiple_of(gi * gr, gr), gr)
        for hh in range(hb):
            lanes = slice(hh * kd, (hh + 1) * kd)
            v = v_ref[0, rows, lanes]
            v_t = v.T
            q = qs_ref[0, rows, lanes].astype(F32)
            for direction in range(2):
                idx = hh * 2 + direction
                g = _chunk_cumsum((lff_ref, lfb_ref)[direction][0, rows, lanes], direction)
                k = (kf_ref, kb_ref)[direction][0, rows, lanes].astype(F32)
                qt = (q * jnp.exp(g)).astype(BF16)
                kt = (k * jnp.exp(-g)).astype(BF16)
                qt_scr[idx, rows, :] = qt
                sc = jnp.where(gmasks[direction], _dot_nt(qt, kt), 0.0)
                o_refs[direction][rows, lanes] = _dot(sc.astype(BF16), v)
                u = _dot(v_t, block_diag(kt))
                for i in range(grp):
                    u_scr[idx, gi * grp + i] = u[:, i * kd:(i + 1) * kd]
                    edge = i * c + (c - 1 if direction == 0 else 0)
                    total = g[edge:edge + 1, :]
                    lowest = jnp.minimum(lowest, total)
                    dec_scr[idx, pl.ds(gi * grp + i, 1), :] = jnp.exp(total)
        return lowest

    lowest = lax.fori_loop(0, n_groups, phase_a, jnp.zeros((1, kd), F32))
    all_safe = jnp.min(lowest) >= HG_SAFE_LOG_DECAY
    state_ref[...] = jnp.zeros_like(state_ref)

    @pl.when(all_safe)
    def _():
        def phase_b(j, carry):
            for hh in range(hb):
                lanes = slice(hh * kd, (hh + 1) * kd)
                for direction in range(2):
                    idx = hh * 2 + direction
                    gi = j if direction == 0 else n_groups - 1 - j
                    rows = pl.ds(pl.multiple_of(gi * gr, gr), gr)
                    st = state_ref[idx]
                    seen = [None] * grp
                    for i in (range(grp) if direction == 0 else reversed(range(grp))):
                        ci = gi * grp + i
                        seen[i] = st.astype(BF16)
                        st = (st + u_scr[idx, ci]) * dec_scr[idx, pl.ds(ci, 1), :]
                    state_ref[idx] = st
                    o_refs[direction][rows, lanes] += _dot_nt(
                        block_diag(qt_scr[idx, rows, :]), jnp.concatenate(seen, axis=1))
            return carry

        lax.fori_loop(0, n_groups, phase_b, 0)

    @pl.when(jnp.logical_not(all_safe))
    def _():
        lax.fori_loop(0, n_chunks, step, 0)

    blk = 256 if s_len % 256 == 0 else c

    def epilogue(i, carry):
        rows = pl.ds(pl.multiple_of(i * blk, blk), blk)
        for hh in range(hb):
            lanes = slice(hh * kd, (hh + 1) * kd)
            o = ofw_ref[rows, lanes] + obw_ref[rows, lanes]
            y = o * _rms_scale(o, HG_VDIM) * onorm_ref[...]
            out_ref[0, rows, lanes] = (y * og_ref[0, rows, lanes].astype(F32)).astype(BF16)
        return carry

    lax.fori_loop(0, s_len // blk, epilogue, 0)


def _hgrn(qs, lf, kk, v, og, onorm):
    b, s, _ = qs.shape
    hb = HG_HEADS_PER_STEP
    w = hb * HG_KDIM
    n_hsteps = HG_HEADS // hb
    fwd = pl.BlockSpec((1, s, w), lambda bi, hi: (bi, 0, hi))
    bwd = pl.BlockSpec((1, s, w), lambda bi, hi: (bi, 0, n_hsteps + hi))
    return pl.pallas_call(
        _hgrn_kernel,
        grid=(b, n_hsteps),
        in_specs=[fwd, fwd, bwd, fwd, bwd, fwd, fwd, _const_spec(onorm.shape)],
        out_specs=fwd,
        out_shape=jax.ShapeDtypeStruct((b, s, HG_V), BF16),
        scratch_shapes=[
            pltpu.VMEM((2 * hb, HG_VDIM, HG_KDIM), F32),
            pltpu.VMEM((s, w), F32),
            pltpu.VMEM((s, w), F32),
            pltpu.VMEM((2 * hb, s, HG_KDIM), BF16),
            pltpu.VMEM((2 * hb, s // HG_CHUNK, HG_VDIM, HG_KDIM), F32),
            pltpu.VMEM((2 * hb, max(s // HG_CHUNK, 8), HG_KDIM), F32),
            pltpu.VMEM((2 * hb, HG_CHUNK, HG_KDIM), F32),
            pltpu.VMEM((2 * hb, HG_CHUNK, HG_KDIM), F32),
            pltpu.VMEM((2 * hb, HG_CHUNK, HG_VDIM), F32),
        ],
        compiler_params=_cparams(("parallel", "parallel")),
        name="hgrn",
    )(qs, lf, lf, kk, kk, v, og, onorm)


def _mlaprep_kernel(mla_ref, pos_ref, pos_t_ref, wq_ref, wk_t_ref, wv_ref, gq_ref, gk_t_ref,
                    rope_ref, freq_t_ref, q_ref, k_t_ref, v_ref):
    cq = mla_ref[:, 0:MLA_Q_RANK]
    ckr = mla_ref[:, MLA_Q_RANK:]
    half = MLA_ROPE // 2

    ang = pos_ref[...] * rope_ref[0:1, :]
    cos = jnp.cos(ang)
    sin = jnp.sin(ang)
    sin_lo = sin * rope_ref[1:2, :]
    sin_hi = sin * rope_ref[2:3, :]
    qm = _dot(cq, wq_ref[...])
    for hd in range(MLA_HEADS):
        lanes = slice(hd * LANES, (hd + 1) * LANES)
        x = qm[:, lanes]
        y = x * (_rms_scale(x, MLA_QK) * MLA_QK ** -0.5) * gq_ref[...]
        y = y * cos + pltpu.roll(y, half, 1) * sin_lo + pltpu.roll(y, LANES - half, 1) * sin_hi
        q_ref[:, lanes] = y.astype(BF16)

    km_t = _dot_nt(wk_t_ref[...], ckr)
    reps = km_t.shape[1] // LANES
    lane_tile = lambda a: jnp.concatenate([a] * reps, axis=1)
    ang_t = lane_tile(freq_t_ref[...]) * pos_t_ref[...]
    cos_t = jnp.cos(ang_t)
    sin_t = jnp.sin(ang_t)
    gain_t = lane_tile(gk_t_ref[...])
    for hd in range(MLA_HEADS):
        r0 = hd * LANES
        x = km_t[r0:r0 + LANES, :]
        y = x * lax.rsqrt(jnp.sum(x * x, axis=0, keepdims=True) * (1.0 / MLA_QK) + EPS) * gain_t
        x1 = y[MLA_NOPE:MLA_NOPE + half, :]
        x2 = y[MLA_NOPE + half:MLA_QK, :]
        k_t_ref[r0:r0 + MLA_NOPE, :] = y[0:MLA_NOPE, :].astype(BF16)
        k_t_ref[r0 + MLA_NOPE:r0 + MLA_NOPE + half, :] = (x1 * cos_t - x2 * sin_t).astype(BF16)
        k_t_ref[r0 + MLA_NOPE + half:r0 + MLA_QK, :] = (x2 * cos_t + x1 * sin_t).astype(BF16)
        k_t_ref[r0 + MLA_QK:r0 + LANES, :] = jnp.zeros((LANES - MLA_QK, km_t.shape[1]), BF16)

    v_ref[...] = _dot(ckr, wv_ref[...]).astype(BF16)


def _mlaprep(mla_in, posf, wq, wk_t, wv, gq, gk_t, rope_tab, freq_t):
    t = mla_in.shape[0]
    tm = MLAPREP_ROW_TILE
    row = lambda w: pl.BlockSpec((tm, w), lambda i: (i, 0))
    hw = MLA_HEADS * LANES
    vw = MLA_HEADS * MLA_VDIM
    return pl.pallas_call(
        _mlaprep_kernel,
        grid=(t // tm,),
        in_specs=[row(mla_in.shape[1]), row(1), pl.BlockSpec((1, tm), lambda i: (0, i)),
                  _const_spec(wq.shape), _const_spec(wk_t.shape), _const_spec(wv.shape),
                  _const_spec(gq.shape), _const_spec(gk_t.shape), _const_spec(rope_tab.shape),
                  _const_spec(freq_t.shape)],
        out_specs=(row(hw), pl.BlockSpec((hw, tm), lambda i: (0, i)), row(vw)),
        out_shape=(jax.ShapeDtypeStruct((t, hw), BF16), jax.ShapeDtypeStruct((hw, t), BF16),
                   jax.ShapeDtypeStruct((t, vw), BF16)),
        compiler_params=_cparams(("parallel",)),
        name="mlaprep",
    )(mla_in, posf, posf.reshape(1, t), wq, wk_t, wv, gq, gk_t, rope_tab, freq_t)


def _attn_kernel(q_ref, k_t_ref, v_ref, o_ref):
    for j in range(ATTN_HEADS_PER_STEP):
        lanes = slice(j * LANES, (j + 1) * LANES)
        s = _dot(q_ref[0, :, lanes], k_t_ref[lanes, :])
        p = jnp.exp(s - jnp.max(s, axis=-1, keepdims=True))
        l = jnp.sum(p, axis=-1, keepdims=True)
        vl = slice(j * MLA_VDIM, (j + 1) * MLA_VDIM)
        o = _dot(p.astype(BF16), v_ref[0, :, vl])
        o_ref[0, :, vl] = (o / l).astype(BF16)


def _attn(q, k_t, v):
    b, s, _ = q.shape
    tq = min(ATTN_Q_TILE, s)
    nh = ATTN_HEADS_PER_STEP
    return pl.pallas_call(
        _attn_kernel,
        grid=(b, MLA_HEADS // nh, s // tq),
        in_specs=[pl.BlockSpec((1, tq, nh * LANES), lambda bi, hp, i: (bi, i, hp)),
                  pl.BlockSpec((nh * LANES, s), lambda bi, hp, i: (hp, bi)),
                  pl.BlockSpec((1, s, nh * MLA_VDIM), lambda bi, hp, i: (bi, 0, hp))],
        out_specs=pl.BlockSpec((1, tq, nh * MLA_VDIM), lambda bi, hp, i: (bi, i, hp)),
        out_shape=jax.ShapeDtypeStruct((b, s, MLA_HEADS * MLA_VDIM), BF16),
        compiler_params=_cparams(("parallel", "parallel", "arbitrary")),
        name="attn",
    )(q, k_t, v)


def _merge_kernel(x_ref, ya_ref, yb_ref, gates_ref, woa_ref, wob_ref, wout_ref, lnm_ref, wr_ref,
                  br_ref, x1_ref, hm8_ref, route_ref):
    d = x_ref.shape[-1]
    y_a = _dot(ya_ref[...], woa_ref[...])
    y_b = _dot(yb_ref[...], wob_ref[...])
    merged = gates_ref[:, 0:d].astype(F32) * y_a + gates_ref[:, d:2 * d].astype(F32) * y_b
    x1 = x_ref[...] + _dot(merged.astype(BF16), wout_ref[...])
    x1_ref[...] = x1
    hm = x1 * _rms_scale(x1, d) * lnm_ref[...]
    _store_token_tiles(hm8_ref, 0, hm)

    hm_hi = hm.astype(BF16)
    hm_lo = (hm - hm_hi.astype(F32)).astype(BF16)
    hh = _dot(hm_hi, wr_ref[...])
    logits = (hh[:, 0:LANES] + hh[:, LANES:2 * LANES] + _dot(hm_lo, wr_ref[:, 0:LANES])
              + br_ref[...])
    lane = lax.broadcasted_iota(jnp.int32, logits.shape, 1)
    neg = -jnp.inf
    big = jnp.int32(2 ** 30)
    is_group = lane < N_GROUPS
    gl = jnp.where(is_group, logits, neg)
    gmax = jnp.max(gl, axis=-1, keepdims=True)
    p_group = 1.0 / jnp.sum(jnp.where(is_group, jnp.exp(gl - gmax), 0.0), axis=-1, keepdims=True)
    g_sel = jnp.min(jnp.where(gl == gmax, lane, big), axis=-1, keepdims=True)
    lo = N_GROUPS + g_sel * EXPERTS_PER_GROUP
    in_group = (lane >= lo) & (lane < lo + EXPERTS_PER_GROUP)
    el = jnp.where(in_group, logits, neg)
    v1 = jnp.max(el, axis=-1, keepdims=True)
    i1 = jnp.min(jnp.where(el == v1, lane, big), axis=-1, keepdims=True)
    el2 = jnp.where(lane == i1, neg, el)
    v2 = jnp.max(el2, axis=-1, keepdims=True)
    i2 = jnp.min(jnp.where(el2 == v2, lane, big), axis=-1, keepdims=True)
    e21 = jnp.exp(v2 - v1)
    w1 = p_group / (1.0 + e21)
    w2 = w1 * e21
    route = jnp.where(lane == 0, (i1 - N_GROUPS).astype(F32),
                      jnp.where(lane == 1, (i2 - N_GROUPS).astype(F32),
                                jnp.where(lane == 2, w1, jnp.where(lane == 3, w2, 0.0))))
    route_ref[...] = route


def _merge(x2, ya, yb, gates, woa, wob, wout, lnm, wr, br):
    t, d = x2.shape
    tm = MERGE_ROW_TILE
    row = lambda w: pl.BlockSpec((tm, w), lambda i: (i, 0))
    return pl.pallas_call(
        _merge_kernel,
        grid=(t // tm,),
        in_specs=[row(d), row(ya.shape[1]), row(yb.shape[1]), row(2 * d), _const_spec(woa.shape),
                  _const_spec(wob.shape), _const_spec(wout.shape), _const_spec(lnm.shape),
                  _const_spec(wr.shape), _const_spec(br.shape)],
        out_specs=(row(d), pl.BlockSpec((tm * TOKEN_ROWS, LANES), lambda i: (i, 0)), row(LANES)),
        out_shape=(jax.ShapeDtypeStruct((t, d), F32),
                   jax.ShapeDtypeStruct((t * TOKEN_ROWS, LANES), jnp.uint32),
                   jax.ShapeDtypeStruct((t, LANES), F32)),
        compiler_params=_cparams(("parallel",)),
        name="merge",
    )(x2, ya, yb, gates, woa, wob, wout, lnm, wr, br)


_HIGH_HALF = 0xFFFF0000


def _store_token_tiles(ref4, tok0, val):
    n, width = val.shape
    as_bits = lambda a: lax.bitcast_convert_type(a.astype(BF16).astype(F32), jnp.uint32)
    words = (as_bits(val[:, :width // 2]) >> 16) | (as_bits(val[:, width // 2:]) & jnp.uint32(_HIGH_HALF))
    for j in range(TOKEN_ROWS):
        ref4[pl.ds(tok0 * TOKEN_ROWS + j, n, stride=TOKEN_ROWS), :] = words[:, j * LANES:(j + 1) * LANES]


def _load_token_tiles(ref4, tok0, n):
    words = jnp.concatenate(
        [ref4[pl.ds(tok0 * TOKEN_ROWS + j, n, stride=TOKEN_ROWS), :] for j in range(TOKEN_ROWS)],
        axis=1)
    low = lax.bitcast_convert_type(words << 16, F32).astype(BF16)
    high = lax.bitcast_convert_type(words & jnp.uint32(_HIGH_HALF), F32).astype(BF16)
    return jnp.concatenate([low, high], axis=1)


def _token_tile(ref8, tok):
    return ref8.at[pl.ds(pl.multiple_of(tok * TOKEN_ROWS, TOKEN_ROWS), TOKEN_ROWS)]


def _dispatch_kernel(dest_ref, hm8_ref, xb_in_ref, xb8_ref, sem):
    del xb_in_ref
    n_tok = DISPATCH_TOKENS
    for r in range(n_tok):
        src = hm8_ref.at[pl.ds(r * TOKEN_ROWS, TOKEN_ROWS)]
        for slot in range(2):
            pltpu.make_async_copy(src, _token_tile(xb8_ref, dest_ref[2 * r + slot]),
                                  sem).start(priority=slot)
    for _ in range(2):
        pltpu.make_async_copy(hm8_ref, xb8_ref.at[pl.ds(0, n_tok * TOKEN_ROWS)], sem).wait()


def _dispatch(dest_flat, hm8, xb_init):
    t = hm8.shape[0] // TOKEN_ROWS
    return pl.pallas_call(
        _dispatch_kernel,
        grid=(t // DISPATCH_TOKENS,),
        in_specs=[pl.BlockSpec((2 * DISPATCH_TOKENS,), lambda i: (i,), memory_space=pltpu.SMEM),
                  pl.BlockSpec((DISPATCH_TOKENS * TOKEN_ROWS, LANES), lambda i: (i, 0)),
                  pl.BlockSpec(memory_space=pl.ANY)],
        out_specs=pl.BlockSpec(memory_space=pl.ANY),
        out_shape=jax.ShapeDtypeStruct(xb_init.shape, xb_init.dtype),
        scratch_shapes=[pltpu.SemaphoreType.DMA(())],
        input_output_aliases={2: 0},
        compiler_params=_cparams(("arbitrary",)),
        name="dispatch",
    )(dest_flat, hm8, xb_init)


def _experts_kernel(be_ref, nu_ref, xb8_ref, w1a_ref, w3a_ref, w2a_ref, w1b_ref, w3b_ref, w2b_ref,
                    yb8_ref, w1c, w3c, w2c):
    j = pl.program_id(0)
    for half, w_refs in enumerate(((w1a_ref, w3a_ref, w2a_ref), (w1b_ref, w3b_ref, w2b_ref))):
        blk = 2 * j + half
        used = blk < nu_ref[0]
        new_expert = jnp.logical_or(blk == 0, be_ref[blk] != be_ref[jnp.maximum(blk - 1, 0)])
        tok0 = half * MOE_ROWS

        @pl.when(jnp.logical_and(used, new_expert))
        def _():
            w1c[...] = w_refs[0][0].astype(BF16)
            w3c[...] = w_refs[1][0].astype(BF16)
            w2c[...] = w_refs[2][0].astype(BF16)

        @pl.when(used)
        def _():
            x = _load_token_tiles(xb8_ref, tok0, MOE_ROWS)
            hmid = (_silu(_dot(x, w1c[...])) * _dot(x, w3c[...])).astype(BF16)
            _store_token_tiles(yb8_ref, tok0, _dot(hmid, w2c[...]))

        @pl.when(jnp.logical_not(used))
        def _():
            rows = pl.ds(tok0 * TOKEN_ROWS, MOE_ROWS * TOKEN_ROWS)
            yb8_ref[rows, :] = jnp.zeros((MOE_ROWS * TOKEN_ROWS, LANES), jnp.uint32)


def _experts(block_expert, n_used, xb8, w1, w3, w2):
    nblk = block_expert.shape[0]
    nsteps = nblk // 2
    d = w1.shape[1]
    de = w1.shape[2]
    step_rows = 2 * MOE_ROWS * TOKEN_ROWS
    rows_in = lambda j, be, nu: (jnp.minimum(j, (nu[0] - 1) // 2), 0)
    wsel_a = lambda j, be, nu: (be[2 * j], 0, 0)
    wsel_b = lambda j, be, nu: (be[2 * j + 1], 0, 0)
    return pl.pallas_call(
        _experts_kernel,
        grid_spec=pltpu.PrefetchScalarGridSpec(
            num_scalar_prefetch=2,
            grid=(nsteps,),
            in_specs=[pl.BlockSpec((step_rows, LANES), rows_in),
                      pl.BlockSpec((1, d, de), wsel_a), pl.BlockSpec((1, d, de), wsel_a),
                      pl.BlockSpec((1, de, d), wsel_a),
                      pl.BlockSpec((1, d, de), wsel_b), pl.BlockSpec((1, d, de), wsel_b),
                      pl.BlockSpec((1, de, d), wsel_b)],
            out_specs=pl.BlockSpec((step_rows, LANES), lambda j, be, nu: (j, 0)),
            scratch_shapes=[pltpu.VMEM((d, de), BF16), pltpu.VMEM((d, de), BF16),
                            pltpu.VMEM((de, d), BF16)]),
        out_shape=jax.ShapeDtypeStruct((nblk * MOE_ROWS * TOKEN_ROWS, LANES), jnp.uint32),
        compiler_params=_cparams(("arbitrary",)),
        name="experts",
    )(block_expert, n_used, xb8, w1, w3, w2, w1, w3, w2)


def _ple_kernel(dest_ref, destn_ref, x1_ref, route_ref, p_ref, ln_ref, wg_ref, wp_ref, yb8_ref,
                out_ref, ya0, ya1, yb0, yb1, sems):
    j = pl.program_id(0)
    last = pl.num_programs(0) - 1
    tm = ROW_TILE
    d = x1_ref.shape[1]

    def start(idx_ref, base, bufs, sem):
        for r in range(tm):
            for slot in range(2):
                pltpu.make_async_copy(_token_tile(yb8_ref, idx_ref[base + 2 * r + slot]),
                                      bufs[slot].at[pl.ds(r * TOKEN_ROWS, TOKEN_ROWS)],
                                      sem).start(priority=slot)

    def wait(bufs, sem):
        for buf in bufs:
            pltpu.make_async_copy(yb8_ref.at[pl.ds(0, tm * TOKEN_ROWS)], buf, sem).wait()

    def compute(bufs, rows):
        x2 = (x1_ref[rows, :]
              + route_ref[rows, 2:3] * _load_token_tiles(bufs[0], 0, tm).astype(F32)
              + route_ref[rows, 3:4] * _load_token_tiles(bufs[1], 0, tm).astype(F32))
        h = (x2 * _rms_scale(x2, d) * ln_ref[...]).astype(BF16)
        pp = _dot(p_ref[rows, :].astype(BF16), wp_ref[...])
        out_ref[rows, :] = x2 + pp * _sigmoid(_dot(h, wg_ref[...]))

    @pl.when(j == 0)
    def _():
        start(dest_ref, 0, (ya0, ya1), sems.at[0])

    wait((ya0, ya1), sems.at[0])
    start(dest_ref, 2 * tm, (yb0, yb1), sems.at[1])
    compute((ya0, ya1), slice(0, tm))

    wait((yb0, yb1), sems.at[1])
    start(destn_ref, 0, (ya0, ya1), sems.at[0])
    compute((yb0, yb1), slice(tm, 2 * tm))

    @pl.when(j == last)
    def _():
        wait((ya0, ya1), sems.at[0])


def _ple(dest_flat, x1, route, p2, ln, wg, wp, yb):
    t, d = x1.shape
    tm = ROW_TILE
    nsteps = t // (2 * tm)
    row = lambda w: pl.BlockSpec((2 * tm, w), lambda j: (j, 0))
    return pl.pallas_call(
        _ple_kernel,
        grid=(nsteps,),
        in_specs=[pl.BlockSpec((4 * tm,), lambda j: (j,), memory_space=pltpu.SMEM),
                  pl.BlockSpec((4 * tm,), lambda j: (jnp.minimum(j + 1, nsteps - 1),),
                               memory_space=pltpu.SMEM),
                  row(d), row(LANES), row(p2.shape[1]), _const_spec(ln.shape),
                  _const_spec(wg.shape), _const_spec(wp.shape),
                  pl.BlockSpec(memory_space=pl.ANY)],
        out_specs=row(d),
        out_shape=jax.ShapeDtypeStruct((t, d), F32),
        scratch_shapes=([pltpu.VMEM((tm * TOKEN_ROWS, LANES), jnp.uint32)] * 4
                        + [pltpu.SemaphoreType.DMA((2,))]),
        compiler_params=_cparams(("arbitrary",)),
        name="ple",
    )(dest_flat, dest_flat, x1, route, p2, ln, wg, wp, yb)


def _head_pad(w, width):
    r = w.shape[0]
    w = w.reshape(r, MLA_HEADS, width)
    return jnp.pad(w, ((0, 0), (0, 0), (0, LANES - width))).reshape(r, MLA_HEADS * LANES)


def _rope_table():
    half = MLA_ROPE // 2
    inv_freq = ROPE_THETA ** (-np.arange(half, dtype=np.float32) / half)
    tab = np.zeros((8, LANES), np.float32)
    tab[0, MLA_NOPE:MLA_NOPE + half] = inv_freq
    tab[0, MLA_NOPE + half:MLA_QK] = inv_freq
    tab[1, MLA_NOPE + half:MLA_QK] = 1.0
    tab[2, MLA_NOPE:MLA_NOPE + half] = -1.0
    freq_t = np.broadcast_to(inv_freq[:, None], (half, LANES))
    return jnp.asarray(tab), jnp.asarray(freq_t)


def _route_tables(ids):
    a = ids.shape[0] * 2
    e = ids.reshape(a)
    onehot = (e[:, None] == jnp.arange(N_EXPERTS, dtype=jnp.int32)[None, :]).astype(jnp.int32)
    csum = jnp.cumsum(onehot, axis=0)
    counts = csum[-1]
    rank = jnp.sum(csum * onehot, axis=1) - 1
    pcounts = (counts + MOE_ROWS - 1) // MOE_ROWS * MOE_ROWS
    pends = jnp.cumsum(pcounts)
    pstarts = pends - pcounts
    dest = (jnp.sum(onehot * pstarts[None, :], axis=1) + rank).astype(jnp.int32)
    nblk = (a + N_EXPERTS * (MOE_ROWS - 1)) // MOE_ROWS
    nblk += nblk % 2
    n_used = (pends[-1] // MOE_ROWS).astype(jnp.int32)
    blk_start = jnp.arange(nblk, dtype=jnp.int32) * MOE_ROWS
    blk_start = jnp.minimum(blk_start, pends[-1] - 1)
    block_expert = jnp.sum((pends[None, :] <= blk_start[:, None]).astype(jnp.int32), axis=1)
    block_expert = jnp.minimum(block_expert, N_EXPERTS - 1).astype(jnp.int32)
    return dest, block_expert, n_used.reshape(1), nblk


def kernel(x, p, positions, ln_mix, w_in, hg_lb, hg_onorm, w_oA, mla_qa_norm, mla_kva_norm, w_uq,
           w_ukv, q_norm, k_norm, w_oB, w_out, ln_moe, w_rg, b_rg, w_re, b_re, w1, w3, w2, ln_ple,
           w_ple_gate, w_ple_proj):
    b, s, d = x.shape
    t = b * s
    depth = w_in.shape[0]
    lb_all = jnp.cumsum(jax.nn.softmax(hg_lb.astype(F32), axis=1), axis=1)
    posf = positions.astype(F32).reshape(t, 1)
    rope_tab, freq_t = _rope_table()
    xc = x.reshape(t, d)

    for layer in range(depth):
        wi = w_in[layer]
        n_hg = 3 * HG_F + 2 * HG_V
        n_mla = MLA_Q_RANK + MLA_KV_RANK + MLA_ROPE
        mla_w = -(-n_mla // LANES) * LANES
        whg = wi[:, :n_hg].astype(BF16)
        wmla = jnp.pad(wi[:, n_hg:n_hg + n_mla], ((0, 0), (0, mla_w - n_mla))).astype(BF16)
        wgate = wi[:, n_hg + n_mla:].astype(BF16)
        lb = lb_all[:, layer, :]

        qs, lf, kk, v, og, mla_in, gates = _inproj(
            xc, ln_mix[layer][None, :], whg, wmla, wgate, lb, mla_qa_norm[layer][None, :],
            mla_kva_norm[layer][None, :])

        r3 = lambda a: a.reshape(b, s, a.shape[-1])
        ya = _hgrn(r3(qs), r3(lf), r3(kk), r3(v), r3(og), hg_onorm[layer][None, :]).reshape(t, HG_V)

        kv_w = MLA_NOPE + MLA_VDIM
        wkv = w_ukv[layer].reshape(MLA_KV_RANK, MLA_HEADS, kv_w)
        ckr_w = mla_w - MLA_Q_RANK
        wq = _head_pad(w_uq[layer], MLA_QK).astype(BF16)
        wk_nope = _head_pad(wkv[:, :, :MLA_NOPE].reshape(MLA_KV_RANK, MLA_HEADS * MLA_NOPE), MLA_NOPE)
        place = np.zeros((ckr_w - MLA_KV_RANK, MLA_HEADS, LANES), np.float32)
        for j in range(MLA_ROPE):
            place[j, :, MLA_NOPE + j] = 1.0
        wk = jnp.concatenate([wk_nope, jnp.asarray(place.reshape(ckr_w - MLA_KV_RANK, -1))],
                             axis=0).astype(BF16)
        wv = jnp.pad(wkv[:, :, MLA_NOPE:].reshape(MLA_KV_RANK, MLA_HEADS * MLA_VDIM),
                     ((0, ckr_w - MLA_KV_RANK), (0, 0))).astype(BF16)
        gq = jnp.pad(q_norm[layer], (0, LANES - MLA_QK))[None, :]
        gk_t = jnp.broadcast_to(jnp.pad(k_norm[layer], (0, LANES - MLA_QK))[:, None], (LANES, LANES))
        qh, kh_t, vh = _mlaprep(mla_in, posf, wq, wk.T, wv, gq, gk_t, rope_tab, freq_t)
        yb = _attn(r3(qh), kh_t, r3(vh)).reshape(t, MLA_HEADS * MLA_VDIM)

        wr = jnp.pad(jnp.concatenate([w_rg[layer], w_re[layer]], axis=1),
                     ((0, 0), (0, LANES - N_GROUPS - N_EXPERTS)))
        wr_hi = wr.astype(BF16)
        wr = jnp.concatenate([wr_hi, (wr - wr_hi.astype(F32)).astype(BF16)], axis=1)
        br = jnp.pad(jnp.concatenate([b_rg[layer], b_re[layer]]),
                     (0, LANES - N_GROUPS - N_EXPERTS))[None, :]
        x1, hm8, route = _merge(xc, ya, yb, gates, w_oA[layer].astype(BF16), w_oB[layer].astype(BF16),
                               w_out[layer].astype(BF16), ln_moe[layer][None, :], wr, br)

        ids = route[:, 0:2].astype(jnp.int32)
        dest, block_expert, n_used, nblk = _route_tables(ids)
        xb8 = _dispatch(dest, hm8, jnp.zeros((nblk * MOE_ROWS * TOKEN_ROWS, LANES), jnp.uint32))
        yexp = _experts(block_expert, n_used, xb8, w1[layer], w3[layer], w2[layer])

        xc = _ple(dest, x1, route, p[layer].reshape(t, -1), ln_ple[layer][None, :],
                  w_ple_gate[layer].astype(BF16), w_ple_proj[layer].astype(BF16), yexp)
    return xc.reshape(b, s, d)
```

```python
import functools

import numpy as np
import jax
import jax.numpy as jnp
from jax import lax
from jax.experimental import pallas as pl
from jax.experimental.pallas import tpu as pltpu

F32 = jnp.float32
BF16 = jnp.bfloat16

HG_HEADS = 4
HG_KDIM = 128
HG_VDIM = 128
HG_F = HG_HEADS * HG_KDIM
HG_V = HG_HEADS * HG_VDIM
MLA_HEADS = 8
MLA_NOPE = 64
MLA_ROPE = 32
MLA_VDIM = 64
MLA_QK = MLA_NOPE + MLA_ROPE
MLA_Q_RANK = 256
MLA_KV_RANK = 128
ROPE_THETA = 10000.0
N_GROUPS = 8
EXPERTS_PER_GROUP = 8
N_EXPERTS = N_GROUPS * EXPERTS_PER_GROUP
D_EXPERT = 256
EPS = 1e-6

LANES = 128
TOKEN_ROWS = 4
VMEM_LIMIT_BYTES = 56 * 1024 * 1024

ROW_TILE = 256
MERGE_ROW_TILE = 512
INPROJ_ROW_TILE = 256
MLAPREP_ROW_TILE = 512
HG_CHUNK = 64
HG_HEADS_PER_STEP = 2
HG_CHUNKS_PER_MATMUL = 4
ATTN_Q_TILE = 512
ATTN_HEADS_PER_STEP = 8
MOE_ROWS = 256
DISPATCH_TOKENS = 512
HG_SAFE_LOG_DECAY = -60.0


def _cparams(sem):
    return pltpu.CompilerParams(dimension_semantics=sem, vmem_limit_bytes=VMEM_LIMIT_BYTES)


def _const_spec(shape):
    nd = len(shape)
    return pl.BlockSpec(shape, lambda *_: (0,) * nd)


def _sigmoid(x):
    return 1.0 / (1.0 + jnp.exp(-x))


def _silu(x):
    return x * _sigmoid(x)


def _rms_scale(x, n):
    return lax.rsqrt(jnp.sum(x * x, axis=-1, keepdims=True) * (1.0 / n) + EPS)


def _dot(a, b):
    return jnp.dot(a, b, preferred_element_type=F32)


def _dot_nt(a, b):
    return lax.dot_general(a, b, (((1,), (1,)), ((), ())), preferred_element_type=F32)


def _dot_tn(a, b):
    return lax.dot_general(a, b, (((0,), (0,)), ((), ())), preferred_element_type=F32)


def _chunk_cumsum(a, direction):
    n = a.shape[0]
    in_chunk = lax.broadcasted_iota(jnp.int32, a.shape, 0) % HG_CHUNK
    k = 1
    while k < HG_CHUNK:
        if direction == 0:
            a = a + jnp.where(in_chunk >= k, pltpu.roll(a, k, 0), 0.0)
        else:
            a = a + jnp.where(in_chunk < HG_CHUNK - k, pltpu.roll(a, n - k, 0), 0.0)
        k *= 2
    return a


def _inproj_kernel(x_ref, g_ref, whg_ref, wmla_ref, wgate_ref, lb_ref, nq_ref, nkv_ref,
                   qs_ref, lf_ref, kk_ref, v_ref, og_ref, mla_ref, gates_ref):
    x = x_ref[...]
    d = x.shape[-1]
    h = (x * _rms_scale(x, d) * g_ref[...]).astype(BF16)

    qs_ref[...] = _silu(_dot(h, whg_ref[:, 0:HG_F])).astype(BF16)
    for direction in range(2):
        cols = slice(HG_F * (1 + direction), HG_F * (2 + direction))
        out_cols = slice(HG_F * direction, HG_F * (direction + 1))
        lb = lb_ref[direction:direction + 1, :]
        f = lb + (1.0 - lb) * _sigmoid(_dot(h, whg_ref[:, cols]))
        lf_ref[:, out_cols] = jnp.log(f)
        kk_ref[:, out_cols] = (1.0 - f).astype(BF16)
    v_ref[...] = _dot(h, whg_ref[:, 3 * HG_F:3 * HG_F + HG_V]).astype(BF16)
    og_ref[...] = _silu(_dot(h, whg_ref[:, 3 * HG_F + HG_V:3 * HG_F + 2 * HG_V])).astype(BF16)

    zm = _dot(h, wmla_ref[...])
    cq = zm[:, 0:MLA_Q_RANK]
    mla_ref[:, 0:MLA_Q_RANK] = (cq * _rms_scale(cq, MLA_Q_RANK) * nq_ref[...]).astype(BF16)
    ckv = zm[:, MLA_Q_RANK:MLA_Q_RANK + MLA_KV_RANK]
    mla_ref[:, MLA_Q_RANK:MLA_Q_RANK + MLA_KV_RANK] = (
        ckv * _rms_scale(ckv, MLA_KV_RANK) * nkv_ref[...]).astype(BF16)
    mla_ref[:, MLA_Q_RANK + MLA_KV_RANK:] = zm[:, MLA_Q_RANK + MLA_KV_RANK:].astype(BF16)

    for half in range(2):
        cols = slice(d * half, d * (half + 1))
        gates_ref[:, cols] = _sigmoid(_dot(h, wgate_ref[:, cols])).astype(BF16)


def _inproj(x2, ln_mix, whg, wmla, wgate, lb, nq, nkv):
    t, d = x2.shape
    tm = INPROJ_ROW_TILE
    mla_w = wmla.shape[1]
    row = lambda w: pl.BlockSpec((tm, w), lambda i: (i, 0))
    out_shape = (
        jax.ShapeDtypeStruct((t, HG_F), BF16),
        jax.ShapeDtypeStruct((t, 2 * HG_F), F32),
        jax.ShapeDtypeStruct((t, 2 * HG_F), BF16),
        jax.ShapeDtypeStruct((t, HG_V), BF16),
        jax.ShapeDtypeStruct((t, HG_V), BF16),
        jax.ShapeDtypeStruct((t, mla_w), BF16),
        jax.ShapeDtypeStruct((t, 2 * d), BF16),
    )
    return pl.pallas_call(
        _inproj_kernel,
        grid=(t // tm,),
        in_specs=[row(d), _const_spec((1, d)), _const_spec(whg.shape), _const_spec(wmla.shape),
                  _const_spec(wgate.shape), _const_spec(lb.shape), _const_spec(nq.shape),
                  _const_spec(nkv.shape)],
        out_specs=(row(HG_F), row(2 * HG_F), row(2 * HG_F), row(HG_V), row(HG_V), row(mla_w),
                   row(2 * d)),
        out_shape=out_shape,
        compiler_params=_cparams(("parallel",)),
        name="inproj",
    )(x2, ln_mix, whg, wmla, wgate, lb, nq, nkv)


def _hgrn_kernel(qs_ref, lff_ref, lfb_ref, kf_ref, kb_ref, v_ref, og_ref, onorm_ref, out_ref,
                 state_ref, ofw_ref, obw_ref, qt_scr, u_scr, dec_scr, g_scr, k_scr, v_scr):
    c = HG_CHUNK
    s_len = qs_ref.shape[1]
    n_chunks = s_len // c
    hb = HG_HEADS_PER_STEP
    kd = HG_KDIM

    row = lax.broadcasted_iota(jnp.int32, (c, c), 0)
    col = lax.broadcasted_iota(jnp.int32, (c, c), 1)
    masks = (row >= col, row <= col)
    rowc = lax.broadcasted_iota(jnp.int32, (c, kd), 0)
    o_refs = (ofw_ref, obw_ref)

    def load(j):
        chains = []
        for hh in range(hb):
            lanes = slice(hh * kd, (hh + 1) * kd)
            for direction in range(2):
                chunk = j if direction == 0 else n_chunks - 1 - j
                rows = pl.ds(pl.multiple_of(chunk * c, c), c)
                g = _chunk_cumsum((lff_ref, lfb_ref)[direction][0, rows, lanes], direction)
                total = g[c - 1:c, :] if direction == 0 else g[0:1, :]
                chains.append(dict(
                    idx=hh * 2 + direction, direction=direction, rows=rows, lanes=lanes, g=g,
                    total=total,
                    q=qs_ref[0, rows, lanes].astype(F32),
                    k=(kf_ref, kb_ref)[direction][0, rows, lanes].astype(F32),
                    v=v_ref[0, rows, lanes]))
        return chains

    def finish(ch, o, kdec_t_v_scaled):
        (ofw_ref, obw_ref)[ch["direction"]][ch["rows"], ch["lanes"]] = o
        state_ref[ch["idx"]] = kdec_t_v_scaled

    def fast(ch):
        g = ch["g"]
        qt = (ch["q"] * jnp.exp(g)).astype(BF16)
        kt = (ch["k"] * jnp.exp(-g)).astype(BF16)
        st = state_ref[ch["idx"]]
        sc = jnp.where(masks[ch["direction"]], _dot_nt(qt, kt), 0.0)
        o = _dot_nt(qt, st.astype(BF16)) + _dot(sc.astype(BF16), ch["v"])
        finish(ch, o, (st + _dot_tn(ch["v"], kt)) * jnp.exp(ch["total"]))

    def robust(ch):
        g = ch["g"]
        direction = ch["direction"]
        st = state_ref[ch["idx"]]
        o0 = _dot_nt((ch["q"] * jnp.exp(g)).astype(BF16), st.astype(BF16))
        slot = ch["idx"]
        g_scr[slot] = g
        k_scr[slot] = ch["k"]
        v_scr[slot] = ch["v"].astype(F32)
        q = ch["q"]

        def body(s, acc):
            g_s = g_scr[slot, pl.ds(s, 1), :]
            seen = (rowc >= s) if direction == 0 else (rowc <= s)
            decay = jnp.where(seen, jnp.exp(jnp.minimum(g - g_s, 0.0)), 0.0)
            a = jnp.sum(q * decay * k_scr[slot, pl.ds(s, 1), :], axis=-1, keepdims=True)
            return acc + a * v_scr[slot, pl.ds(s, 1), :]

        o = lax.fori_loop(0, c, body, o0)
        kdec = (ch["k"] * jnp.exp(ch["total"] - g)).astype(BF16)
        finish(ch, o, st * jnp.exp(ch["total"]) + _dot_tn(ch["v"], kdec))

    def step(j, carry):
        chains = load(j)
        lowest = chains[0]["total"]
        for ch in chains[1:]:
            lowest = jnp.minimum(lowest, ch["total"])
        safe = jnp.min(lowest) >= HG_SAFE_LOG_DECAY

        @pl.when(safe)
        def _():
            for ch in chains:
                fast(ch)

        @pl.when(jnp.logical_not(safe))
        def _():
            for ch in chains:
                robust(ch)

        return carry

    grp = HG_CHUNKS_PER_MATMUL if n_chunks % HG_CHUNKS_PER_MATMUL == 0 else 1
    gr = grp * c
    n_groups = n_chunks // grp
    grow = lax.broadcasted_iota(jnp.int32, (gr, gr), 0)
    gcol = lax.broadcasted_iota(jnp.int32, (gr, gr), 1)
    same_chunk = (grow // c) == (gcol // c)
    gmasks = (same_chunk & (grow >= gcol), same_chunk & (grow <= gcol))
    chunk_of_row = lax.broadcasted_iota(jnp.int32, (gr, kd), 0) // c

    def block_diag(a):
        return jnp.concatenate([jnp.where(chunk_of_row == i, a, jnp.zeros_like(a))
                                for i in range(grp)], axis=1)

    def phase_a(gi, lowest):
        rows = pl.ds(pl.multiple_of(gi * gr, gr), gr)
        for hh in range(hb):
            lanes = slice(hh * kd, (hh + 1) * kd)
            v = v_ref[0, rows, lanes]
            v_t = v.T
            q = qs_ref[0, rows, lanes].astype(F32)
            for direction in range(2):
                idx = hh * 2 + direction
                g = _chunk_cumsum((lff_ref, lfb_ref)[direction][0, rows, lanes], direction)
                k = (kf_ref, kb_ref)[direction][0, rows, lanes].astype(F32)
                qt = (q * jnp.exp(g)).astype(BF16)
                kt = (k * jnp.exp(-g)).astype(BF16)
                qt_scr[idx, rows, :] = qt
                sc = jnp.where(gmasks[direction], _dot_nt(qt, kt), 0.0)
                o_refs[direction][rows, lanes] = _dot(sc.astype(BF16), v)
                u = _dot(v_t, block_diag(kt))
                for i in range(grp):
                    u_scr[idx, gi * grp + i] = u[:, i * kd:(i + 1) * kd]
                    edge = i * c + (c - 1 if direction == 0 else 0)
                    total = g[edge:edge + 1, :]
                    lowest = jnp.minimum(lowest, total)
                    dec_scr[idx, pl.ds(gi * grp + i, 1), :] = jnp.exp(total)
        return lowest

    lowest = lax.fori_loop(0, n_groups, phase_a, jnp.zeros((1, kd), F32))
    all_safe = jnp.min(lowest) >= HG_SAFE_LOG_DECAY
    state_ref[...] = jnp.zeros_like(state_ref)

    @pl.when(all_safe)
    def _():
        def phase_b(j, carry):
            for hh in range(hb):
                lanes = slice(hh * kd, (hh + 1) * kd)
                for direction in range(2):
                    idx = hh * 2 + direction
                    gi = j if direction == 0 else n_groups - 1 - j
                    rows = pl.ds(pl.multiple_of(gi * gr, gr), gr)
                    st = state_ref[idx]
                    seen = [None] * grp
                    for i in (range(grp) if direction == 0 else reversed(range(grp))):
                        ci = gi * grp + i
                        seen[i] = st.astype(BF16)
                        st = (st + u_scr[idx, ci]) * dec_scr[idx, pl.ds(ci, 1), :]
                    state_ref[idx] = st
                    o_refs[direction][rows, lanes] += _dot_nt(
                        block_diag(qt_scr[idx, rows, :]), jnp.concatenate(seen, axis=1))
            return carry

        lax.fori_loop(0, n_groups, phase_b, 0)

    @pl.when(jnp.logical_not(all_safe))
    def _():
        lax.fori_loop(0, n_chunks, step, 0)

    blk = 256 if s_len % 256 == 0 else c

    def epilogue(i, carry):
        rows = pl.ds(pl.multiple_of(i * blk, blk), blk)
        for hh in range(hb):
            lanes = slice(hh * kd, (hh + 1) * kd)
            o = ofw_ref[rows, lanes] + obw_ref[rows, lanes]
            y = o * _rms_scale(o, HG_VDIM) * onorm_ref[...]
            out_ref[0, rows, lanes] = (y * og_ref[0, rows, lanes].astype(F32)).astype(BF16)
        return carry

    lax.fori_loop(0, s_len // blk, epilogue, 0)


def _hgrn(qs, lf, kk, v, og, onorm):
    b, s, _ = qs.shape
    hb = HG_HEADS_PER_STEP
    w = hb * HG_KDIM
    n_hsteps = HG_HEADS // hb
    fwd = pl.BlockSpec((1, s, w), lambda bi, hi: (bi, 0, hi))
    bwd = pl.BlockSpec((1, s, w), lambda bi, hi: (bi, 0, n_hsteps + hi))
    return pl.pallas_call(
        _hgrn_kernel,
        grid=(b, n_hsteps),
        in_specs=[fwd, fwd, bwd, fwd, bwd, fwd, fwd, _const_spec(onorm.shape)],
        out_specs=fwd,
        out_shape=jax.ShapeDtypeStruct((b, s, HG_V), BF16),
        scratch_shapes=[
            pltpu.VMEM((2 * hb, HG_VDIM, HG_KDIM), F32),
            pltpu.VMEM((s, w), F32),
            pltpu.VMEM((s, w), F32),
            pltpu.VMEM((2 * hb, s, HG_KDIM), BF16),
            pltpu.VMEM((2 * hb, s // HG_CHUNK, HG_VDIM, HG_KDIM), F32),
            pltpu.VMEM((2 * hb, max(s // HG_CHUNK, 8), HG_KDIM), F32),
            pltpu.VMEM((2 * hb, HG_CHUNK, HG_KDIM), F32),
            pltpu.VMEM((2 * hb, HG_CHUNK, HG_KDIM), F32),
            pltpu.VMEM((2 * hb, HG_CHUNK, HG_VDIM), F32),
        ],
        compiler_params=_cparams(("parallel", "parallel")),
        name="hgrn",
    )(qs, lf, lf, kk, kk, v, og, onorm)


def _mlaprep_kernel(mla_ref, pos_ref, pos_t_ref, wq_ref, wk_t_ref, wv_ref, gq_ref, gk_t_ref,
                    rope_ref, freq_t_ref, q_ref, k_t_ref, v_ref):
    cq = mla_ref[:, 0:MLA_Q_RANK]
    ckr = mla_ref[:, MLA_Q_RANK:]
    half = MLA_ROPE // 2

    ang = pos_ref[...] * rope_ref[0:1, :]
    cos = jnp.cos(ang)
    sin = jnp.sin(ang)
    sin_lo = sin * rope_ref[1:2, :]
    sin_hi = sin * rope_ref[2:3, :]
    qm = _dot(cq, wq_ref[...])
    for hd in range(MLA_HEADS):
        lanes = slice(hd * LANES, (hd + 1) * LANES)
        x = qm[:, lanes]
        y = x * (_rms_scale(x, MLA_QK) * MLA_QK ** -0.5) * gq_ref[...]
        y = y * cos + pltpu.roll(y, half, 1) * sin_lo + pltpu.roll(y, LANES - half, 1) * sin_hi
        q_ref[:, lanes] = y.astype(BF16)

    km_t = _dot_nt(wk_t_ref[...], ckr)
    reps = km_t.shape[1] // LANES
    lane_tile = lambda a: jnp.concatenate([a] * reps, axis=1)
    ang_t = lane_tile(freq_t_ref[...]) * pos_t_ref[...]
    cos_t = jnp.cos(ang_t)
    sin_t = jnp.sin(ang_t)
    gain_t = lane_tile(gk_t_ref[...])
    for hd in range(MLA_HEADS):
        r0 = hd * LANES
        x = km_t[r0:r0 + LANES, :]
        y = x * lax.rsqrt(jnp.sum(x * x, axis=0, keepdims=True) * (1.0 / MLA_QK) + EPS) * gain_t
        x1 = y[MLA_NOPE:MLA_NOPE + half, :]
        x2 = y[MLA_NOPE + half:MLA_QK, :]
        k_t_ref[r0:r0 + MLA_NOPE, :] = y[0:MLA_NOPE, :].astype(BF16)
        k_t_ref[r0 + MLA_NOPE:r0 + MLA_NOPE + half, :] = (x1 * cos_t - x2 * sin_t).astype(BF16)
        k_t_ref[r0 + MLA_NOPE + half:r0 + MLA_QK, :] = (x2 * cos_t + x1 * sin_t).astype(BF16)
        k_t_ref[r0 + MLA_QK:r0 + LANES, :] = jnp.zeros((LANES - MLA_QK, km_t.shape[1]), BF16)

    v_ref[...] = _dot(ckr, wv_ref[...]).astype(BF16)


def _mlaprep(mla_in, posf, wq, wk_t, wv, gq, gk_t, rope_tab, freq_t):
    t = mla_in.shape[0]
    tm = MLAPREP_ROW_TILE
    row = lambda w: pl.BlockSpec((tm, w), lambda i: (i, 0))
    hw = MLA_HEADS * LANES
    vw = MLA_HEADS * MLA_VDIM
    return pl.pallas_call(
        _mlaprep_kernel,
        grid=(t // tm,),
        in_specs=[row(mla_in.shape[1]), row(1), pl.BlockSpec((1, tm), lambda i: (0, i)),
                  _const_spec(wq.shape), _const_spec(wk_t.shape), _const_spec(wv.shape),
                  _const_spec(gq.shape), _const_spec(gk_t.shape), _const_spec(rope_tab.shape),
                  _const_spec(freq_t.shape)],
        out_specs=(row(hw), pl.BlockSpec((hw, tm), lambda i: (0, i)), row(vw)),
        out_shape=(jax.ShapeDtypeStruct((t, hw), BF16), jax.ShapeDtypeStruct((hw, t), BF16),
                   jax.ShapeDtypeStruct((t, vw), BF16)),
        compiler_params=_cparams(("parallel",)),
        name="mlaprep",
    )(mla_in, posf, posf.reshape(1, t), wq, wk_t, wv, gq, gk_t, rope_tab, freq_t)


def _attn_kernel(q_ref, k_t_ref, v_ref, o_ref):
    for j in range(ATTN_HEADS_PER_STEP):
        lanes = slice(j * LANES, (j + 1) * LANES)
        s = _dot(q_ref[0, :, lanes], k_t_ref[lanes, :])
        p = jnp.exp(s - jnp.max(s, axis=-1, keepdims=True))
        l = jnp.sum(p, axis=-1, keepdims=True)
        vl = slice(j * MLA_VDIM, (j + 1) * MLA_VDIM)
        o = _dot(p.astype(BF16), v_ref[0, :, vl])
        o_ref[0, :, vl] = (o / l).astype(BF16)


def _attn(q, k_t, v):
    b, s, _ = q.shape
    tq = min(ATTN_Q_TILE, s)
    nh = ATTN_HEADS_PER_STEP
    return pl.pallas_call(
        _attn_kernel,
        grid=(b, MLA_HEADS // nh, s // tq),
        in_specs=[pl.BlockSpec((1, tq, nh * LANES), lambda bi, hp, i: (bi, i, hp)),
                  pl.BlockSpec((nh * LANES, s), lambda bi, hp, i: (hp, bi)),
                  pl.BlockSpec((1, s, nh * MLA_VDIM), lambda bi, hp, i: (bi, 0, hp))],
        out_specs=pl.BlockSpec((1, tq, nh * MLA_VDIM), lambda bi, hp, i: (bi, i, hp)),
        out_shape=jax.ShapeDtypeStruct((b, s, MLA_HEADS * MLA_VDIM), BF16),
        compiler_params=_cparams(("parallel", "parallel", "arbitrary")),
        name="attn",
    )(q, k_t, v)


def _merge_kernel(x_ref, ya_ref, yb_ref, gates_ref, woa_ref, wob_ref, wout_ref, lnm_ref, wr_ref,
                  br_ref, x1_ref, hm8_ref, route_ref):
    d = x_ref.shape[-1]
    y_a = _dot(ya_ref[...], woa_ref[...])
    y_b = _dot(yb_ref[...], wob_ref[...])
    merged = gates_ref[:, 0:d].astype(F32) * y_a + gates_ref[:, d:2 * d].astype(F32) * y_b
    x1 = x_ref[...] + _dot(merged.astype(BF16), wout_ref[...])
    x1_ref[...] = x1
    hm = x1 * _rms_scale(x1, d) * lnm_ref[...]
    _store_token_tiles(hm8_ref, 0, hm)

    hm_hi = hm.astype(BF16)
    hm_lo = (hm - hm_hi.astype(F32)).astype(BF16)
    hh = _dot(hm_hi, wr_ref[...])
    logits = (hh[:, 0:LANES] + hh[:, LANES:2 * LANES] + _dot(hm_lo, wr_ref[:, 0:LANES])
              + br_ref[...])
    lane = lax.broadcasted_iota(jnp.int32, logits.shape, 1)
    neg = -jnp.inf
    big = jnp.int32(2 ** 30)
    is_group = lane < N_GROUPS
    gl = jnp.where(is_group, logits, neg)
    gmax = jnp.max(gl, axis=-1, keepdims=True)
    p_group = 1.0 / jnp.sum(jnp.where(is_group, jnp.exp(gl - gmax), 0.0), axis=-1, keepdims=True)
    g_sel = jnp.min(jnp.where(gl == gmax, lane, big), axis=-1, keepdims=True)
    lo = N_GROUPS + g_sel * EXPERTS_PER_GROUP
    in_group = (lane >= lo) & (lane < lo + EXPERTS_PER_GROUP)
    el = jnp.where(in_group, logits, neg)
    v1 = jnp.max(el, axis=-1, keepdims=True)
    i1 = jnp.min(jnp.where(el == v1, lane, big), axis=-1, keepdims=True)
    el2 = jnp.where(lane == i1, neg, el)
    v2 = jnp.max(el2, axis=-1, keepdims=True)
    i2 = jnp.min(jnp.where(el2 == v2, lane, big), axis=-1, keepdims=True)
    e21 = jnp.exp(v2 - v1)
    w1 = p_group / (1.0 + e21)
    w2 = w1 * e21
    route = jnp.where(lane == 0, (i1 - N_GROUPS).astype(F32),
                      jnp.where(lane == 1, (i2 - N_GROUPS).astype(F32),
                                jnp.where(lane == 2, w1, jnp.where(lane == 3, w2, 0.0))))
    route_ref[...] = route


def _merge(x2, ya, yb, gates, woa, wob, wout, lnm, wr, br):
    t, d = x2.shape
    tm = MERGE_ROW_TILE
    row = lambda w: pl.BlockSpec((tm, w), lambda i: (i, 0))
    return pl.pallas_call(
        _merge_kernel,
        grid=(t // tm,),
        in_specs=[row(d), row(ya.shape[1]), row(yb.shape[1]), row(2 * d), _const_spec(woa.shape),
                  _const_spec(wob.shape), _const_spec(wout.shape), _const_spec(lnm.shape),
                  _const_spec(wr.shape), _const_spec(br.shape)],
        out_specs=(row(d), pl.BlockSpec((tm * TOKEN_ROWS, LANES), lambda i: (i, 0)), row(LANES)),
        out_shape=(jax.ShapeDtypeStruct((t, d), F32),
                   jax.ShapeDtypeStruct((t * TOKEN_ROWS, LANES), jnp.uint32),
                   jax.ShapeDtypeStruct((t, LANES), F32)),
        compiler_params=_cparams(("parallel",)),
        name="merge",
    )(x2, ya, yb, gates, woa, wob, wout, lnm, wr, br)


_HIGH_HALF = 0xFFFF0000


def _store_token_tiles(ref4, tok0, val):
    n, width = val.shape
    as_bits = lambda a: lax.bitcast_convert_type(a.astype(BF16).astype(F32), jnp.uint32)
    words = (as_bits(val[:, :width // 2]) >> 16) | (as_bits(val[:, width // 2:]) & jnp.uint32(_HIGH_HALF))
    for j in range(TOKEN_ROWS):
        ref4[pl.ds(tok0 * TOKEN_ROWS + j, n, stride=TOKEN_ROWS), :] = words[:, j * LANES:(j + 1) * LANES]


def _load_token_tiles(ref4, tok0, n):
    words = jnp.concatenate(
        [ref4[pl.ds(tok0 * TOKEN_ROWS + j, n, stride=TOKEN_ROWS), :] for j in range(TOKEN_ROWS)],
        axis=1)
    low = lax.bitcast_convert_type(words << 16, F32).astype(BF16)
    high = lax.bitcast_convert_type(words & jnp.uint32(_HIGH_HALF), F32).astype(BF16)
    return jnp.concatenate([low, high], axis=1)


def _token_tile(ref8, tok):
    return ref8.at[pl.ds(pl.multiple_of(tok * TOKEN_ROWS, TOKEN_ROWS), TOKEN_ROWS)]


def _dispatch_kernel(dest_ref, hm8_ref, xb_in_ref, xb8_ref, sem):
    del xb_in_ref
    n_tok = DISPATCH_TOKENS
    for r in range(n_tok):
        src = hm8_ref.at[pl.ds(r * TOKEN_ROWS, TOKEN_ROWS)]
        for slot in range(2):
            pltpu.make_async_copy(src, _token_tile(xb8_ref, dest_ref[2 * r + slot]),
                                  sem).start(priority=slot)
    for _ in range(2):
        pltpu.make_async_copy(hm8_ref, xb8_ref.at[pl.ds(0, n_tok * TOKEN_ROWS)], sem).wait()


def _dispatch(dest_flat, hm8, xb_init):
    t = hm8.shape[0] // TOKEN_ROWS
    return pl.pallas_call(
        _dispatch_kernel,
        grid=(t // DISPATCH_TOKENS,),
        in_specs=[pl.BlockSpec((2 * DISPATCH_TOKENS,), lambda i: (i,), memory_space=pltpu.SMEM),
                  pl.BlockSpec((DISPATCH_TOKENS * TOKEN_ROWS, LANES), lambda i: (i, 0)),
                  pl.BlockSpec(memory_space=pl.ANY)],
        out_specs=pl.BlockSpec(memory_space=pl.ANY),
        out_shape=jax.ShapeDtypeStruct(xb_init.shape, xb_init.dtype),
        scratch_shapes=[pltpu.SemaphoreType.DMA(())],
        input_output_aliases={2: 0},
        compiler_params=_cparams(("arbitrary",)),
        name="dispatch",
    )(dest_flat, hm8, xb_init)


def _experts_kernel(be_ref, nu_ref, nxt_ref, par_ref, xb8_ref, w1_hbm, w3_hbm, w2_hbm, yb8_ref,
                    w1f, w3f, w2f, sems, w1c, w3c, w2c):
    j = pl.program_id(0)

    def fetch(e, slot):
        return [pltpu.make_async_copy(src.at[e], dst.at[slot], sems.at[slot])
                for src, dst in ((w1_hbm, w1f), (w3_hbm, w3f), (w2_hbm, w2f))]

    for half in range(2):
        blk = 2 * j + half
        used = blk < nu_ref[0]
        new_expert = jnp.logical_or(blk == 0, be_ref[blk] != be_ref[jnp.maximum(blk - 1, 0)])
        tok0 = half * MOE_ROWS

        @pl.when(jnp.logical_and(used, new_expert))
        def _():
            slot = par_ref[blk]

            @pl.when(blk == 0)
            def _():
                for cp in fetch(be_ref[0], 0):
                    cp.start()

            for cp in fetch(be_ref[blk], slot):
                cp.wait()

            @pl.when(nxt_ref[blk] >= 0)
            def _():
                for cp in fetch(nxt_ref[blk], 1 - slot):
                    cp.start()

            w1c[...] = w1f[slot].astype(BF16)
            w3c[...] = w3f[slot].astype(BF16)
            w2c[...] = w2f[slot].astype(BF16)

        @pl.when(used)
        def _():
            x = _load_token_tiles(xb8_ref, tok0, MOE_ROWS)
            hmid = (_silu(_dot(x, w1c[...])) * _dot(x, w3c[...])).astype(BF16)
            _store_token_tiles(yb8_ref, tok0, _dot(hmid, w2c[...]))

        @pl.when(jnp.logical_not(used))
        def _():
            rows = pl.ds(tok0 * TOKEN_ROWS, MOE_ROWS * TOKEN_ROWS)
            yb8_ref[rows, :] = jnp.zeros((MOE_ROWS * TOKEN_ROWS, LANES), jnp.uint32)


def _experts(block_expert, n_used, next_expert, slot_parity, xb8, w1, w3, w2):
    nblk = block_expert.shape[0]
    nsteps = nblk // 2
    d = w1.shape[1]
    de = w1.shape[2]
    step_rows = 2 * MOE_ROWS * TOKEN_ROWS
    rows_in = lambda j, be, nu, nx, pa: (jnp.minimum(j, (nu[0] - 1) // 2), 0)
    hbm = pl.BlockSpec(memory_space=pl.ANY)
    return pl.pallas_call(
        _experts_kernel,
        grid_spec=pltpu.PrefetchScalarGridSpec(
            num_scalar_prefetch=4,
            grid=(nsteps,),
            in_specs=[pl.BlockSpec((step_rows, LANES), rows_in), hbm, hbm, hbm],
            out_specs=pl.BlockSpec((step_rows, LANES), lambda j, be, nu, nx, pa: (j, 0)),
            scratch_shapes=[pltpu.VMEM((2, d, de), F32), pltpu.VMEM((2, d, de), F32),
                            pltpu.VMEM((2, de, d), F32), pltpu.SemaphoreType.DMA((2,)),
                            pltpu.VMEM((d, de), BF16), pltpu.VMEM((d, de), BF16),
                            pltpu.VMEM((de, d), BF16)]),
        out_shape=jax.ShapeDtypeStruct((nblk * MOE_ROWS * TOKEN_ROWS, LANES), jnp.uint32),
        compiler_params=_cparams(("arbitrary",)),
        name="experts",
    )(block_expert, n_used, next_expert, slot_parity, xb8, w1, w3, w2)


def _ple_kernel(dest_ref, destn_ref, x1_ref, route_ref, p_ref, ln_ref, wg_ref, wp_ref, yb8_ref,
                out_ref, ya0, ya1, yb0, yb1, sems):
    j = pl.program_id(0)
    last = pl.num_programs(0) - 1
    tm = ROW_TILE
    d = x1_ref.shape[1]

    def start(idx_ref, base, bufs, sem):
        for r in range(tm):
            for slot in range(2):
                pltpu.make_async_copy(_token_tile(yb8_ref, idx_ref[base + 2 * r + slot]),
                                      bufs[slot].at[pl.ds(r * TOKEN_ROWS, TOKEN_ROWS)],
                                      sem).start(priority=slot)

    def wait(bufs, sem):
        for buf in bufs:
            pltpu.make_async_copy(yb8_ref.at[pl.ds(0, tm * TOKEN_ROWS)], buf, sem).wait()

    def compute(bufs, rows):
        x2 = (x1_ref[rows, :]
              + route_ref[rows, 2:3] * _load_token_tiles(bufs[0], 0, tm).astype(F32)
              + route_ref[rows, 3:4] * _load_token_tiles(bufs[1], 0, tm).astype(F32))
        h = (x2 * _rms_scale(x2, d) * ln_ref[...]).astype(BF16)
        pp = _dot(p_ref[rows, :].astype(BF16), wp_ref[...])
        out_ref[rows, :] = x2 + pp * _sigmoid(_dot(h, wg_ref[...]))

    @pl.when(j == 0)
    def _():
        start(dest_ref, 0, (ya0, ya1), sems.at[0])

    wait((ya0, ya1), sems.at[0])
    start(dest_ref, 2 * tm, (yb0, yb1), sems.at[1])
    compute((ya0, ya1), slice(0, tm))

    wait((yb0, yb1), sems.at[1])
    start(destn_ref, 0, (ya0, ya1), sems.at[0])
    compute((yb0, yb1), slice(tm, 2 * tm))

    @pl.when(j == last)
    def _():
        wait((ya0, ya1), sems.at[0])


def _ple(dest_flat, x1, route, p2, ln, wg, wp, yb):
    t, d = x1.shape
    tm = ROW_TILE
    nsteps = t // (2 * tm)
    row = lambda w: pl.BlockSpec((2 * tm, w), lambda j: (j, 0))
    return pl.pallas_call(
        _ple_kernel,
        grid=(nsteps,),
        in_specs=[pl.BlockSpec((4 * tm,), lambda j: (j,), memory_space=pltpu.SMEM),
                  pl.BlockSpec((4 * tm,), lambda j: (jnp.minimum(j + 1, nsteps - 1),),
                               memory_space=pltpu.SMEM),
                  row(d), row(LANES), row(p2.shape[1]), _const_spec(ln.shape),
                  _const_spec(wg.shape), _const_spec(wp.shape),
                  pl.BlockSpec(memory_space=pl.ANY)],
        out_specs=row(d),
        out_shape=jax.ShapeDtypeStruct((t, d), F32),
        scratch_shapes=([pltpu.VMEM((tm * TOKEN_ROWS, LANES), jnp.uint32)] * 4
                        + [pltpu.SemaphoreType.DMA((2,))]),
        compiler_params=_cparams(("arbitrary",)),
        name="ple",
    )(dest_flat, dest_flat, x1, route, p2, ln, wg, wp, yb)


def _head_pad(w, width):
    r = w.shape[0]
    w = w.reshape(r, MLA_HEADS, width)
    return jnp.pad(w, ((0, 0), (0, 0), (0, LANES - width))).reshape(r, MLA_HEADS * LANES)


def _rope_table():
    half = MLA_ROPE // 2
    inv_freq = ROPE_THETA ** (-np.arange(half, dtype=np.float32) / half)
    tab = np.zeros((8, LANES), np.float32)
    tab[0, MLA_NOPE:MLA_NOPE + half] = inv_freq
    tab[0, MLA_NOPE + half:MLA_QK] = inv_freq
    tab[1, MLA_NOPE + half:MLA_QK] = 1.0
    tab[2, MLA_NOPE:MLA_NOPE + half] = -1.0
    freq_t = np.broadcast_to(inv_freq[:, None], (half, LANES))
    return jnp.asarray(tab), jnp.asarray(freq_t)


def _route_tables(ids):
    a = ids.shape[0] * 2
    e = ids.reshape(a)
    onehot = (e[:, None] == jnp.arange(N_EXPERTS, dtype=jnp.int32)[None, :]).astype(jnp.int32)
    csum = jnp.cumsum(onehot, axis=0)
    counts = csum[-1]
    rank = jnp.sum(csum * onehot, axis=1) - 1
    pcounts = (counts + MOE_ROWS - 1) // MOE_ROWS * MOE_ROWS
    pends = jnp.cumsum(pcounts)
    pstarts = pends - pcounts
    dest = (jnp.sum(onehot * pstarts[None, :], axis=1) + rank).astype(jnp.int32)
    nblk = (a + N_EXPERTS * (MOE_ROWS - 1)) // MOE_ROWS
    nblk += nblk % 2
    n_used = (pends[-1] // MOE_ROWS).astype(jnp.int32)
    blk_start = jnp.arange(nblk, dtype=jnp.int32) * MOE_ROWS
    blk_start = jnp.minimum(blk_start, pends[-1] - 1)
    block_expert = jnp.sum((pends[None, :] <= blk_start[:, None]).astype(jnp.int32), axis=1)
    block_expert = jnp.minimum(block_expert, N_EXPERTS - 1).astype(jnp.int32)
    present = counts > 0
    ids_or_end = jnp.where(present, jnp.arange(N_EXPERTS, dtype=jnp.int32), N_EXPERTS)
    first_at_or_after = jnp.flip(lax.cummin(jnp.flip(ids_or_end)))
    next_present = jnp.concatenate([first_at_or_after[1:], jnp.full((1,), N_EXPERTS, jnp.int32)])
    next_present = jnp.where(next_present == N_EXPERTS, -1, next_present)
    parity = (jnp.cumsum(present.astype(jnp.int32)) - 1) % 2
    return (dest, block_expert, n_used.reshape(1), next_present[block_expert].astype(jnp.int32),
            parity[block_expert].astype(jnp.int32), nblk)


def kernel(x, p, positions, ln_mix, w_in, hg_lb, hg_onorm, w_oA, mla_qa_norm, mla_kva_norm, w_uq,
           w_ukv, q_norm, k_norm, w_oB, w_out, ln_moe, w_rg, b_rg, w_re, b_re, w1, w3, w2, ln_ple,
           w_ple_gate, w_ple_proj):
    b, s, d = x.shape
    t = b * s
    depth = w_in.shape[0]
    lb_all = jnp.cumsum(jax.nn.softmax(hg_lb.astype(F32), axis=1), axis=1)
    posf = positions.astype(F32).reshape(t, 1)
    rope_tab, freq_t = _rope_table()
    xc = x.reshape(t, d)

    for layer in range(depth):
        wi = w_in[layer]
        n_hg = 3 * HG_F + 2 * HG_V
        n_mla = MLA_Q_RANK + MLA_KV_RANK + MLA_ROPE
        mla_w = -(-n_mla // LANES) * LANES
        whg = wi[:, :n_hg].astype(BF16)
        wmla = jnp.pad(wi[:, n_hg:n_hg + n_mla], ((0, 0), (0, mla_w - n_mla))).astype(BF16)
        wgate = wi[:, n_hg + n_mla:].astype(BF16)
        lb = lb_all[:, layer, :]

        qs, lf, kk, v, og, mla_in, gates = _inproj(
            xc, ln_mix[layer][None, :], whg, wmla, wgate, lb, mla_qa_norm[layer][None, :],
            mla_kva_norm[layer][None, :])

        r3 = lambda a: a.reshape(b, s, a.shape[-1])
        ya = _hgrn(r3(qs), r3(lf), r3(kk), r3(v), r3(og), hg_onorm[layer][None, :]).reshape(t, HG_V)

        kv_w = MLA_NOPE + MLA_VDIM
        wkv = w_ukv[layer].reshape(MLA_KV_RANK, MLA_HEADS, kv_w)
        ckr_w = mla_w - MLA_Q_RANK
        wq = _head_pad(w_uq[layer], MLA_QK).astype(BF16)
        wk_nope = _head_pad(wkv[:, :, :MLA_NOPE].reshape(MLA_KV_RANK, MLA_HEADS * MLA_NOPE), MLA_NOPE)
        place = np.zeros((ckr_w - MLA_KV_RANK, MLA_HEADS, LANES), np.float32)
        for j in range(MLA_ROPE):
            place[j, :, MLA_NOPE + j] = 1.0
        wk = jnp.concatenate([wk_nope, jnp.asarray(place.reshape(ckr_w - MLA_KV_RANK, -1))],
                             axis=0).astype(BF16)
        wv = jnp.pad(wkv[:, :, MLA_NOPE:].reshape(MLA_KV_RANK, MLA_HEADS * MLA_VDIM),
                     ((0, ckr_w - MLA_KV_RANK), (0, 0))).astype(BF16)
        gq = jnp.pad(q_norm[layer], (0, LANES - MLA_QK))[None, :]
        gk_t = jnp.broadcast_to(jnp.pad(k_norm[layer], (0, LANES - MLA_QK))[:, None], (LANES, LANES))
        qh, kh_t, vh = _mlaprep(mla_in, posf, wq, wk.T, wv, gq, gk_t, rope_tab, freq_t)
        yb = _attn(r3(qh), kh_t, r3(vh)).reshape(t, MLA_HEADS * MLA_VDIM)

        wr = jnp.pad(jnp.concatenate([w_rg[layer], w_re[layer]], axis=1),
                     ((0, 0), (0, LANES - N_GROUPS - N_EXPERTS)))
        wr_hi = wr.astype(BF16)
        wr = jnp.concatenate([wr_hi, (wr - wr_hi.astype(F32)).astype(BF16)], axis=1)
        br = jnp.pad(jnp.concatenate([b_rg[layer], b_re[layer]]),
                     (0, LANES - N_GROUPS - N_EXPERTS))[None, :]
        x1, hm8, route = _merge(xc, ya, yb, gates, w_oA[layer].astype(BF16), w_oB[layer].astype(BF16),
                               w_out[layer].astype(BF16), ln_moe[layer][None, :], wr, br)

        ids = route[:, 0:2].astype(jnp.int32)
        dest, block_expert, n_used, next_expert, slot_parity, nblk = _route_tables(ids)
        xb8 = _dispatch(dest, hm8, jnp.zeros((nblk * MOE_ROWS * TOKEN_ROWS, LANES), jnp.uint32))
        yexp = _experts(block_expert, n_used, next_expert, slot_parity, xb8,
                        w1[layer], w3[layer], w2[layer])

        xc = _ple(dest, x1, route, p[layer].reshape(t, -1), ln_ple[layer][None, :],
                  w_ple_gate[layer].astype(BF16), w_ple_proj[layer].astype(BF16), yexp)
    return xc.reshape(b, s, d)
```

```python
import functools

import numpy as np
import jax
import jax.numpy as jnp
from jax import lax
from jax.experimental import pallas as pl
from jax.experimental.pallas import tpu as pltpu

F32 = jnp.float32
BF16 = jnp.bfloat16

HG_HEADS = 4
HG_KDIM = 128
HG_VDIM = 128
HG_F = HG_HEADS * HG_KDIM
HG_V = HG_HEADS * HG_VDIM
MLA_HEADS = 8
MLA_NOPE = 64
MLA_ROPE = 32
MLA_VDIM = 64
MLA_QK = MLA_NOPE + MLA_ROPE
MLA_Q_RANK = 256
MLA_KV_RANK = 128
ROPE_THETA = 10000.0
N_GROUPS = 8
EXPERTS_PER_GROUP = 8
N_EXPERTS = N_GROUPS * EXPERTS_PER_GROUP
D_EXPERT = 256
EPS = 1e-6

LANES = 128
TOKEN_ROWS = 4
VMEM_LIMIT_BYTES = 56 * 1024 * 1024

ROW_TILE = 256
MERGE_ROW_TILE = 512
INPROJ_ROW_TILE = 256
MLAPREP_ROW_TILE = 512
HG_CHUNK = 64
HG_HEADS_PER_STEP = 2
HG_CHUNKS_PER_MATMUL = 4
ATTN_Q_TILE = 512
ATTN_HEADS_PER_STEP = 8
MOE_ROWS = 256
DISPATCH_TOKENS = 1024
HG_SAFE_LOG_DECAY = -75.0


def _cparams(sem):
    return pltpu.CompilerParams(dimension_semantics=sem, vmem_limit_bytes=VMEM_LIMIT_BYTES)


def _const_spec(shape):
    nd = len(shape)
    return pl.BlockSpec(shape, lambda *_: (0,) * nd)


def _sigmoid(x):
    return 1.0 / (1.0 + jnp.exp(-x))


def _silu(x):
    return x * _sigmoid(x)


def _rms_scale(x, n):
    return lax.rsqrt(jnp.sum(x * x, axis=-1, keepdims=True) * (1.0 / n) + EPS)


def _dot(a, b):
    return jnp.dot(a, b, preferred_element_type=F32)


def _dot_nt(a, b):
    return lax.dot_general(a, b, (((1,), (1,)), ((), ())), preferred_element_type=F32)


def _dot_tn(a, b):
    return lax.dot_general(a, b, (((0,), (0,)), ((), ())), preferred_element_type=F32)


def _chunk_cumsum(a, direction):
    n = a.shape[0]
    in_chunk = lax.broadcasted_iota(jnp.int32, a.shape, 0) % HG_CHUNK
    k = 1
    while k < HG_CHUNK:
        if direction == 0:
            a = a + jnp.where(in_chunk >= k, pltpu.roll(a, k, 0), 0.0)
        else:
            a = a + jnp.where(in_chunk < HG_CHUNK - k, pltpu.roll(a, n - k, 0), 0.0)
        k *= 2
    return a


def _inproj_kernel(x_ref, g_ref, whg_ref, wmla_ref, wgate_ref, lb_ref, nq_ref, nkv_ref,
                   qs_ref, lf_ref, kk_ref, v_ref, og_ref, mla_ref, gates_ref):
    x = x_ref[...]
    d = x.shape[-1]
    h = (x * _rms_scale(x, d) * g_ref[...]).astype(BF16)

    qs_ref[...] = _silu(_dot(h, whg_ref[:, 0:HG_F])).astype(BF16)
    for direction in range(2):
        cols = slice(HG_F * (1 + direction), HG_F * (2 + direction))
        out_cols = slice(HG_F * direction, HG_F * (direction + 1))
        lb = lb_ref[direction:direction + 1, :]
        f = lb + (1.0 - lb) * _sigmoid(_dot(h, whg_ref[:, cols]))
        lf_ref[:, out_cols] = jnp.log(f)
        kk_ref[:, out_cols] = (1.0 - f).astype(BF16)
    v_ref[...] = _dot(h, whg_ref[:, 3 * HG_F:3 * HG_F + HG_V]).astype(BF16)
    og_ref[...] = _silu(_dot(h, whg_ref[:, 3 * HG_F + HG_V:3 * HG_F + 2 * HG_V])).astype(BF16)

    zm = _dot(h, wmla_ref[...])
    cq = zm[:, 0:MLA_Q_RANK]
    mla_ref[:, 0:MLA_Q_RANK] = (cq * _rms_scale(cq, MLA_Q_RANK) * nq_ref[...]).astype(BF16)
    ckv = zm[:, MLA_Q_RANK:MLA_Q_RANK + MLA_KV_RANK]
    mla_ref[:, MLA_Q_RANK:MLA_Q_RANK + MLA_KV_RANK] = (
        ckv * _rms_scale(ckv, MLA_KV_RANK) * nkv_ref[...]).astype(BF16)
    mla_ref[:, MLA_Q_RANK + MLA_KV_RANK:] = zm[:, MLA_Q_RANK + MLA_KV_RANK:].astype(BF16)

    for half in range(2):
        cols = slice(d * half, d * (half + 1))
        gates_ref[:, cols] = _sigmoid(_dot(h, wgate_ref[:, cols])).astype(BF16)


def _inproj(x2, ln_mix, whg, wmla, wgate, lb, nq, nkv):
    t, d = x2.shape
    tm = INPROJ_ROW_TILE
    mla_w = wmla.shape[1]
    row = lambda w: pl.BlockSpec((tm, w), lambda i: (i, 0))
    out_shape = (
        jax.ShapeDtypeStruct((t, HG_F), BF16),
        jax.ShapeDtypeStruct((t, 2 * HG_F), F32),
        jax.ShapeDtypeStruct((t, 2 * HG_F), BF16),
        jax.ShapeDtypeStruct((t, HG_V), BF16),
        jax.ShapeDtypeStruct((t, HG_V), BF16),
        jax.ShapeDtypeStruct((t, mla_w), BF16),
        jax.ShapeDtypeStruct((t, 2 * d), BF16),
    )
    return pl.pallas_call(
        _inproj_kernel,
        grid=(t // tm,),
        in_specs=[row(d), _const_spec((1, d)), _const_spec(whg.shape), _const_spec(wmla.shape),
                  _const_spec(wgate.shape), _const_spec(lb.shape), _const_spec(nq.shape),
                  _const_spec(nkv.shape)],
        out_specs=(row(HG_F), row(2 * HG_F), row(2 * HG_F), row(HG_V), row(HG_V), row(mla_w),
                   row(2 * d)),
        out_shape=out_shape,
        compiler_params=_cparams(("parallel",)),
        name="inproj",
    )(x2, ln_mix, whg, wmla, wgate, lb, nq, nkv)


def _hgrn_kernel(qs_ref, lff_ref, lfb_ref, kf_ref, kb_ref, v_ref, og_ref, onorm_ref, out_ref,
                 state_ref, ofw_ref, obw_ref, qt_scr, u_scr, dec_scr, g_scr, k_scr, v_scr):
    c = HG_CHUNK
    s_len = qs_ref.shape[1]
    n_chunks = s_len // c
    hb = HG_HEADS_PER_STEP
    kd = HG_KDIM

    row = lax.broadcasted_iota(jnp.int32, (c, c), 0)
    col = lax.broadcasted_iota(jnp.int32, (c, c), 1)
    masks = (row >= col, row <= col)
    rowc = lax.broadcasted_iota(jnp.int32, (c, kd), 0)
    o_refs = (ofw_ref, obw_ref)

    def load(j):
        chains = []
        for hh in range(hb):
            lanes = slice(hh * kd, (hh + 1) * kd)
            for direction in range(2):
                chunk = j if direction == 0 else n_chunks - 1 - j
                rows = pl.ds(pl.multiple_of(chunk * c, c), c)
                g = _chunk_cumsum((lff_ref, lfb_ref)[direction][0, rows, lanes], direction)
                total = g[c - 1:c, :] if direction == 0 else g[0:1, :]
                chains.append(dict(
                    idx=hh * 2 + direction, direction=direction, rows=rows, lanes=lanes, g=g,
                    total=total,
                    q=qs_ref[0, rows, lanes].astype(F32),
                    k=(kf_ref, kb_ref)[direction][0, rows, lanes].astype(F32),
                    v=v_ref[0, rows, lanes]))
        return chains

    def finish(ch, o, kdec_t_v_scaled):
        (ofw_ref, obw_ref)[ch["direction"]][ch["rows"], ch["lanes"]] = o
        state_ref[ch["idx"]] = kdec_t_v_scaled

    def fast(ch):
        g = ch["g"]
        qt = (ch["q"] * jnp.exp(g)).astype(BF16)
        kt = (ch["k"] * jnp.exp(-g)).astype(BF16)
        st = state_ref[ch["idx"]]
        sc = jnp.where(masks[ch["direction"]], _dot_nt(qt, kt), 0.0)
        o = _dot_nt(qt, st.astype(BF16)) + _dot(sc.astype(BF16), ch["v"])
        finish(ch, o, (st + _dot_tn(ch["v"], kt)) * jnp.exp(ch["total"]))

    def robust(ch):
        g = ch["g"]
        direction = ch["direction"]
        st = state_ref[ch["idx"]]
        o0 = _dot_nt((ch["q"] * jnp.exp(g)).astype(BF16), st.astype(BF16))
        slot = ch["idx"]
        g_scr[slot] = g
        k_scr[slot] = ch["k"]
        v_scr[slot] = ch["v"].astype(F32)
        q = ch["q"]

        def body(s, acc):
            g_s = g_scr[slot, pl.ds(s, 1), :]
            seen = (rowc >= s) if direction == 0 else (rowc <= s)
            decay = jnp.where(seen, jnp.exp(jnp.minimum(g - g_s, 0.0)), 0.0)
            a = jnp.sum(q * decay * k_scr[slot, pl.ds(s, 1), :], axis=-1, keepdims=True)
            return acc + a * v_scr[slot, pl.ds(s, 1), :]

        o = lax.fori_loop(0, c, body, o0)
        kdec = (ch["k"] * jnp.exp(ch["total"] - g)).astype(BF16)
        finish(ch, o, st * jnp.exp(ch["total"]) + _dot_tn(ch["v"], kdec))

    def step(j, carry):
        chains = load(j)
        lowest = chains[0]["total"]
        for ch in chains[1:]:
            lowest = jnp.minimum(lowest, ch["total"])
        safe = jnp.min(lowest) >= HG_SAFE_LOG_DECAY

        @pl.when(safe)
        def _():
            for ch in chains:
                fast(ch)

        @pl.when(jnp.logical_not(safe))
        def _():
            for ch in chains:
                robust(ch)

        return carry

    grp = HG_CHUNKS_PER_MATMUL if n_chunks % HG_CHUNKS_PER_MATMUL == 0 else 1
    gr = grp * c
    n_groups = n_chunks // grp
    grow = lax.broadcasted_iota(jnp.int32, (gr, gr), 0)
    gcol = lax.broadcasted_iota(jnp.int32, (gr, gr), 1)
    same_chunk = (grow // c) == (gcol // c)
    gmasks = (same_chunk & (grow >= gcol), same_chunk & (grow <= gcol))
    chunk_of_row = lax.broadcasted_iota(jnp.int32, (gr, kd), 0) // c

    def block_diag(a):
        return jnp.concatenate([jnp.where(chunk_of_row == i, a, jnp.zeros_like(a))
                                for i in range(grp)], axis=1)

    def phase_a(gi, lowest):
        rows = pl.ds(pl.multiple_of(gi * gr, gr), gr)
        for hh in range(hb):
            lanes = slice(hh * kd, (hh + 1) * kd)
            v = v_ref[0, rows, lanes]
            v_t = v.T
            q = qs_ref[0, rows, lanes].astype(F32)
            for direction in range(2):
                idx = hh * 2 + direction
                g = _chunk_cumsum((lff_ref, lfb_ref)[direction][0, rows, lanes], direction)
                k = (kf_ref, kb_ref)[direction][0, rows, lanes].astype(F32)
                qt = (q * jnp.exp(g)).astype(BF16)
                kt = (k * jnp.exp(-g)).astype(BF16)
                qt_scr[idx, rows, :] = qt
                sc = jnp.where(gmasks[direction], _dot_nt(qt, kt), 0.0)
                o_refs[direction][rows, lanes] = _dot(sc.astype(BF16), v)
                u = _dot(v_t, block_diag(kt))
                for i in range(grp):
                    u_scr[idx, gi * grp + i] = u[:, i * kd:(i + 1) * kd]
                    edge = i * c + (c - 1 if direction == 0 else 0)
                    total = g[edge:edge + 1, :]
                    lowest = jnp.minimum(lowest, total)
                    dec_scr[idx, pl.ds(gi * grp + i, 1), :] = jnp.exp(total)
        return lowest

    lowest = lax.fori_loop(0, n_groups, phase_a, jnp.zeros((1, kd), F32))
    all_safe = jnp.min(lowest) >= HG_SAFE_LOG_DECAY
    state_ref[...] = jnp.zeros_like(state_ref)

    @pl.when(all_safe)
    def _():
        def phase_b(j, carry):
            for hh in range(hb):
                lanes = slice(hh * kd, (hh + 1) * kd)
                for direction in range(2):
                    idx = hh * 2 + direction
                    gi = j if direction == 0 else n_groups - 1 - j
                    rows = pl.ds(pl.multiple_of(gi * gr, gr), gr)
                    st = state_ref[idx]
                    seen = [None] * grp
                    for i in (range(grp) if direction == 0 else reversed(range(grp))):
                        ci = gi * grp + i
                        seen[i] = st.astype(BF16)
                        st = (st + u_scr[idx, ci]) * dec_scr[idx, pl.ds(ci, 1), :]
                    state_ref[idx] = st
                    o_refs[direction][rows, lanes] += _dot_nt(
                        block_diag(qt_scr[idx, rows, :]), jnp.concatenate(seen, axis=1))
            return carry

        lax.fori_loop(0, n_groups, phase_b, 0)

    @pl.when(jnp.logical_not(all_safe))
    def _():
        lax.fori_loop(0, n_chunks, step, 0)

    blk = 256 if s_len % 256 == 0 else c

    def epilogue(i, carry):
        rows = pl.ds(pl.multiple_of(i * blk, blk), blk)
        for hh in range(hb):
            lanes = slice(hh * kd, (hh + 1) * kd)
            o = ofw_ref[rows, lanes] + obw_ref[rows, lanes]
            y = o * _rms_scale(o, HG_VDIM) * onorm_ref[...]
            out_ref[0, rows, lanes] = (y * og_ref[0, rows, lanes].astype(F32)).astype(BF16)
        return carry

    lax.fori_loop(0, s_len // blk, epilogue, 0)


def _hgrn(qs, lf, kk, v, og, onorm):
    b, s, _ = qs.shape
    hb = HG_HEADS_PER_STEP
    w = hb * HG_KDIM
    n_hsteps = HG_HEADS // hb
    fwd = pl.BlockSpec((1, s, w), lambda bi, hi: (bi, 0, hi))
    bwd = pl.BlockSpec((1, s, w), lambda bi, hi: (bi, 0, n_hsteps + hi))
    return pl.pallas_call(
        _hgrn_kernel,
        grid=(b, n_hsteps),
        in_specs=[fwd, fwd, bwd, fwd, bwd, fwd, fwd, _const_spec(onorm.shape)],
        out_specs=fwd,
        out_shape=jax.ShapeDtypeStruct((b, s, HG_V), BF16),
        scratch_shapes=[
            pltpu.VMEM((2 * hb, HG_VDIM, HG_KDIM), F32),
            pltpu.VMEM((s, w), F32),
            pltpu.VMEM((s, w), F32),
            pltpu.VMEM((2 * hb, s, HG_KDIM), BF16),
            pltpu.VMEM((2 * hb, s // HG_CHUNK, HG_VDIM, HG_KDIM), F32),
            pltpu.VMEM((2 * hb, max(s // HG_CHUNK, 8), HG_KDIM), F32),
            pltpu.VMEM((2 * hb, HG_CHUNK, HG_KDIM), F32),
            pltpu.VMEM((2 * hb, HG_CHUNK, HG_KDIM), F32),
            pltpu.VMEM((2 * hb, HG_CHUNK, HG_VDIM), F32),
        ],
        compiler_params=_cparams(("parallel", "parallel")),
        name="hgrn",
    )(qs, lf, lf, kk, kk, v, og, onorm)


def _mlaprep_kernel(mla_ref, pos_ref, pos_t_ref, wq_ref, wk_t_ref, wv_ref, gq_ref, gk_t_ref,
                    rope_ref, freq_t_ref, q_ref, k_t_ref, v_ref):
    cq = mla_ref[:, 0:MLA_Q_RANK]
    ckr = mla_ref[:, MLA_Q_RANK:]
    half = MLA_ROPE // 2

    ang = pos_ref[...] * rope_ref[0:1, :]
    cos = jnp.cos(ang)
    sin = jnp.sin(ang)
    sin_lo = sin * rope_ref[1:2, :]
    sin_hi = sin * rope_ref[2:3, :]
    qm = _dot(cq, wq_ref[...])
    for hd in range(MLA_HEADS):
        lanes = slice(hd * LANES, (hd + 1) * LANES)
        x = qm[:, lanes]
        y = x * (_rms_scale(x, MLA_QK) * MLA_QK ** -0.5) * gq_ref[...]
        y = y * cos + pltpu.roll(y, half, 1) * sin_lo + pltpu.roll(y, LANES - half, 1) * sin_hi
        q_ref[:, lanes] = y.astype(BF16)

    km_t = _dot_nt(wk_t_ref[...], ckr)
    reps = km_t.shape[1] // LANES
    lane_tile = lambda a: jnp.concatenate([a] * reps, axis=1)
    ang_t = lane_tile(freq_t_ref[...]) * pos_t_ref[...]
    cos_t = jnp.cos(ang_t)
    sin_t = jnp.sin(ang_t)
    gain_t = lane_tile(gk_t_ref[...])
    for hd in range(MLA_HEADS):
        r0 = hd * LANES
        x = km_t[r0:r0 + LANES, :]
        y = x * lax.rsqrt(jnp.sum(x * x, axis=0, keepdims=True) * (1.0 / MLA_QK) + EPS) * gain_t
        x1 = y[MLA_NOPE:MLA_NOPE + half, :]
        x2 = y[MLA_NOPE + half:MLA_QK, :]
        k_t_ref[r0:r0 + MLA_NOPE, :] = y[0:MLA_NOPE, :].astype(BF16)
        k_t_ref[r0 + MLA_NOPE:r0 + MLA_NOPE + half, :] = (x1 * cos_t - x2 * sin_t).astype(BF16)
        k_t_ref[r0 + MLA_NOPE + half:r0 + MLA_QK, :] = (x2 * cos_t + x1 * sin_t).astype(BF16)
        k_t_ref[r0 + MLA_QK:r0 + LANES, :] = jnp.zeros((LANES - MLA_QK, km_t.shape[1]), BF16)

    v_ref[...] = _dot(ckr, wv_ref[...]).astype(BF16)


def _mlaprep(mla_in, posf, wq, wk_t, wv, gq, gk_t, rope_tab, freq_t):
    t = mla_in.shape[0]
    tm = MLAPREP_ROW_TILE
    row = lambda w: pl.BlockSpec((tm, w), lambda i: (i, 0))
    hw = MLA_HEADS * LANES
    vw = MLA_HEADS * MLA_VDIM
    return pl.pallas_call(
        _mlaprep_kernel,
        grid=(t // tm,),
        in_specs=[row(mla_in.shape[1]), row(1), pl.BlockSpec((1, tm), lambda i: (0, i)),
                  _const_spec(wq.shape), _const_spec(wk_t.shape), _const_spec(wv.shape),
                  _const_spec(gq.shape), _const_spec(gk_t.shape), _const_spec(rope_tab.shape),
                  _const_spec(freq_t.shape)],
        out_specs=(row(hw), pl.BlockSpec((hw, tm), lambda i: (0, i)), row(vw)),
        out_shape=(jax.ShapeDtypeStruct((t, hw), BF16), jax.ShapeDtypeStruct((hw, t), BF16),
                   jax.ShapeDtypeStruct((t, vw), BF16)),
        compiler_params=_cparams(("parallel",)),
        name="mlaprep",
    )(mla_in, posf, posf.reshape(1, t), wq, wk_t, wv, gq, gk_t, rope_tab, freq_t)


def _attn_kernel(q_ref, k_t_ref, v_ref, o_ref):
    for j in range(ATTN_HEADS_PER_STEP):
        lanes = slice(j * LANES, (j + 1) * LANES)
        s = _dot(q_ref[0, :, lanes], k_t_ref[lanes, :])
        p = jnp.exp(s - jnp.max(s, axis=-1, keepdims=True))
        l = jnp.sum(p, axis=-1, keepdims=True)
        vl = slice(j * MLA_VDIM, (j + 1) * MLA_VDIM)
        o = _dot(p.astype(BF16), v_ref[0, :, vl])
        o_ref[0, :, vl] = (o / l).astype(BF16)


def _attn(q, k_t, v):
    b, s, _ = q.shape
    tq = min(ATTN_Q_TILE, s)
    nh = ATTN_HEADS_PER_STEP
    return pl.pallas_call(
        _attn_kernel,
        grid=(b, MLA_HEADS // nh, s // tq),
        in_specs=[pl.BlockSpec((1, tq, nh * LANES), lambda bi, hp, i: (bi, i, hp)),
                  pl.BlockSpec((nh * LANES, s), lambda bi, hp, i: (hp, bi)),
                  pl.BlockSpec((1, s, nh * MLA_VDIM), lambda bi, hp, i: (bi, 0, hp))],
        out_specs=pl.BlockSpec((1, tq, nh * MLA_VDIM), lambda bi, hp, i: (bi, i, hp)),
        out_shape=jax.ShapeDtypeStruct((b, s, MLA_HEADS * MLA_VDIM), BF16),
        compiler_params=_cparams(("parallel", "parallel", "arbitrary")),
        name="attn",
    )(q, k_t, v)


def _merge_kernel(x_ref, ya_ref, yb_ref, gates_ref, woa_ref, wob_ref, wout_ref, lnm_ref, wr_ref,
                  br_ref, x1_ref, hm8_ref, route_ref):
    d = x_ref.shape[-1]
    y_a = _dot(ya_ref[...], woa_ref[...])
    y_b = _dot(yb_ref[...], wob_ref[...])
    merged = gates_ref[:, 0:d].astype(F32) * y_a + gates_ref[:, d:2 * d].astype(F32) * y_b
    x1 = x_ref[...] + _dot(merged.astype(BF16), wout_ref[...])
    x1_ref[...] = x1
    hm = x1 * _rms_scale(x1, d) * lnm_ref[...]
    _store_token_tiles(hm8_ref, 0, hm)

    hm_hi = hm.astype(BF16)
    hm_lo = (hm - hm_hi.astype(F32)).astype(BF16)
    hh = _dot(hm_hi, wr_ref[...])
    logits = (hh[:, 0:LANES] + hh[:, LANES:2 * LANES] + _dot(hm_lo, wr_ref[:, 0:LANES])
              + br_ref[...])
    lane = lax.broadcasted_iota(jnp.int32, logits.shape, 1)
    neg = -jnp.inf
    big = jnp.int32(2 ** 30)
    is_group = lane < N_GROUPS
    gl = jnp.where(is_group, logits, neg)
    gmax = jnp.max(gl, axis=-1, keepdims=True)
    p_group = 1.0 / jnp.sum(jnp.where(is_group, jnp.exp(gl - gmax), 0.0), axis=-1, keepdims=True)
    g_sel = jnp.min(jnp.where(gl == gmax, lane, big), axis=-1, keepdims=True)
    lo = N_GROUPS + g_sel * EXPERTS_PER_GROUP
    in_group = (lane >= lo) & (lane < lo + EXPERTS_PER_GROUP)
    el = jnp.where(in_group, logits, neg)
    v1 = jnp.max(el, axis=-1, keepdims=True)
    i1 = jnp.min(jnp.where(el == v1, lane, big), axis=-1, keepdims=True)
    el2 = jnp.where(lane == i1, neg, el)
    v2 = jnp.max(el2, axis=-1, keepdims=True)
    i2 = jnp.min(jnp.where(el2 == v2, lane, big), axis=-1, keepdims=True)
    e21 = jnp.exp(v2 - v1)
    w1 = p_group / (1.0 + e21)
    w2 = w1 * e21
    route = jnp.where(lane == 0, (i1 - N_GROUPS).astype(F32),
                      jnp.where(lane == 1, (i2 - N_GROUPS).astype(F32),
                                jnp.where(lane == 2, w1, jnp.where(lane == 3, w2, 0.0))))
    route_ref[...] = route


def _merge(x2, ya, yb, gates, woa, wob, wout, lnm, wr, br):
    t, d = x2.shape
    tm = MERGE_ROW_TILE
    row = lambda w: pl.BlockSpec((tm, w), lambda i: (i, 0))
    return pl.pallas_call(
        _merge_kernel,
        grid=(t // tm,),
        in_specs=[row(d), row(ya.shape[1]), row(yb.shape[1]), row(2 * d), _const_spec(woa.shape),
                  _const_spec(wob.shape), _const_spec(wout.shape), _const_spec(lnm.shape),
                  _const_spec(wr.shape), _const_spec(br.shape)],
        out_specs=(row(d), pl.BlockSpec((tm * TOKEN_ROWS, LANES), lambda i: (i, 0)), row(LANES)),
        out_shape=(jax.ShapeDtypeStruct((t, d), F32),
                   jax.ShapeDtypeStruct((t * TOKEN_ROWS, LANES), jnp.uint32),
                   jax.ShapeDtypeStruct((t, LANES), F32)),
        compiler_params=_cparams(("parallel",)),
        name="merge",
    )(x2, ya, yb, gates, woa, wob, wout, lnm, wr, br)


_HIGH_HALF = 0xFFFF0000


def _store_token_tiles(ref4, tok0, val):
    n, width = val.shape
    as_bits = lambda a: lax.bitcast_convert_type(a.astype(BF16).astype(F32), jnp.uint32)
    words = (as_bits(val[:, :width // 2]) >> 16) | (as_bits(val[:, width // 2:]) & jnp.uint32(_HIGH_HALF))
    for j in range(TOKEN_ROWS):
        ref4[pl.ds(tok0 * TOKEN_ROWS + j, n, stride=TOKEN_ROWS), :] = words[:, j * LANES:(j + 1) * LANES]


def _load_token_tiles(ref4, tok0, n):
    words = jnp.concatenate(
        [ref4[pl.ds(tok0 * TOKEN_ROWS + j, n, stride=TOKEN_ROWS), :] for j in range(TOKEN_ROWS)],
        axis=1)
    low = lax.bitcast_convert_type(words << 16, F32).astype(BF16)
    high = lax.bitcast_convert_type(words & jnp.uint32(_HIGH_HALF), F32).astype(BF16)
    return jnp.concatenate([low, high], axis=1)


def _token_tile(ref8, tok):
    return ref8.at[pl.ds(pl.multiple_of(tok * TOKEN_ROWS, TOKEN_ROWS), TOKEN_ROWS)]


def _dispatch_kernel(dest_ref, hm8_ref, xb_in_ref, xb8_ref, sem):
    del xb_in_ref
    n_tok = DISPATCH_TOKENS
    for r in range(n_tok):
        src = hm8_ref.at[pl.ds(r * TOKEN_ROWS, TOKEN_ROWS)]
        for slot in range(2):
            pltpu.make_async_copy(src, _token_tile(xb8_ref, dest_ref[2 * r + slot]),
                                  sem).start(priority=slot)
    for _ in range(2):
        pltpu.make_async_copy(hm8_ref, xb8_ref.at[pl.ds(0, n_tok * TOKEN_ROWS)], sem).wait()


def _dispatch(dest_flat, hm8, xb_init):
    t = hm8.shape[0] // TOKEN_ROWS
    return pl.pallas_call(
        _dispatch_kernel,
        grid=(t // DISPATCH_TOKENS,),
        in_specs=[pl.BlockSpec((2 * DISPATCH_TOKENS,), lambda i: (i,), memory_space=pltpu.SMEM),
                  pl.BlockSpec((DISPATCH_TOKENS * TOKEN_ROWS, LANES), lambda i: (i, 0)),
                  pl.BlockSpec(memory_space=pl.ANY)],
        out_specs=pl.BlockSpec(memory_space=pl.ANY),
        out_shape=jax.ShapeDtypeStruct(xb_init.shape, xb_init.dtype),
        scratch_shapes=[pltpu.SemaphoreType.DMA(())],
        input_output_aliases={2: 0},
        compiler_params=_cparams(("arbitrary",)),
        name="dispatch",
    )(dest_flat, hm8, xb_init)


def _experts_kernel(be_ref, nu_ref, nxt_ref, par_ref, xb8_ref, w1_hbm, w3_hbm, w2_hbm, yb8_ref,
                    w1f, w3f, w2f, sems, w1c, w3c, w2c):
    j = pl.program_id(0)

    def fetch(e, slot):
        return [pltpu.make_async_copy(src.at[e], dst.at[slot], sems.at[slot])
                for src, dst in ((w1_hbm, w1f), (w3_hbm, w3f), (w2_hbm, w2f))]

    for half in range(2):
        blk = 2 * j + half
        used = blk < nu_ref[0]
        new_expert = jnp.logical_or(blk == 0, be_ref[blk] != be_ref[jnp.maximum(blk - 1, 0)])
        tok0 = half * MOE_ROWS

        @pl.when(jnp.logical_and(used, new_expert))
        def _():
            slot = par_ref[blk]

            @pl.when(blk == 0)
            def _():
                for cp in fetch(be_ref[0], 0):
                    cp.start()

            for cp in fetch(be_ref[blk], slot):
                cp.wait()

            @pl.when(nxt_ref[blk] >= 0)
            def _():
                for cp in fetch(nxt_ref[blk], 1 - slot):
                    cp.start()

            w1c[...] = w1f[slot].astype(BF16)
            w3c[...] = w3f[slot].astype(BF16)
            w2c[...] = w2f[slot].astype(BF16)

        @pl.when(used)
        def _():
            x = _load_token_tiles(xb8_ref, tok0, MOE_ROWS)
            hmid = (_silu(_dot(x, w1c[...])) * _dot(x, w3c[...])).astype(BF16)
            _store_token_tiles(yb8_ref, tok0, _dot(hmid, w2c[...]))

        @pl.when(jnp.logical_not(used))
        def _():
            rows = pl.ds(tok0 * TOKEN_ROWS, MOE_ROWS * TOKEN_ROWS)
            yb8_ref[rows, :] = jnp.zeros((MOE_ROWS * TOKEN_ROWS, LANES), jnp.uint32)


def _experts(block_expert, n_used, next_expert, slot_parity, xb8, w1, w3, w2):
    nblk = block_expert.shape[0]
    nsteps = nblk // 2
    d = w1.shape[1]
    de = w1.shape[2]
    step_rows = 2 * MOE_ROWS * TOKEN_ROWS
    rows_in = lambda j, be, nu, nx, pa: (jnp.minimum(j, (nu[0] - 1) // 2), 0)
    hbm = pl.BlockSpec(memory_space=pl.ANY)
    return pl.pallas_call(
        _experts_kernel,
        grid_spec=pltpu.PrefetchScalarGridSpec(
            num_scalar_prefetch=4,
            grid=(nsteps,),
            in_specs=[pl.BlockSpec((step_rows, LANES), rows_in), hbm, hbm, hbm],
            out_specs=pl.BlockSpec((step_rows, LANES), lambda j, be, nu, nx, pa: (j, 0)),
            scratch_shapes=[pltpu.VMEM((2, d, de), F32), pltpu.VMEM((2, d, de), F32),
                            pltpu.VMEM((2, de, d), F32), pltpu.SemaphoreType.DMA((2,)),
                            pltpu.VMEM((d, de), BF16), pltpu.VMEM((d, de), BF16),
                            pltpu.VMEM((de, d), BF16)]),
        out_shape=jax.ShapeDtypeStruct((nblk * MOE_ROWS * TOKEN_ROWS, LANES), jnp.uint32),
        compiler_params=_cparams(("arbitrary",)),
        name="experts",
    )(block_expert, n_used, next_expert, slot_parity, xb8, w1, w3, w2)


def _ple_kernel(dest_ref, destn_ref, x1_ref, route_ref, p_ref, ln_ref, wg_ref, wp_ref, yb8_ref,
                out_ref, ya0, ya1, yb0, yb1, sems):
    j = pl.program_id(0)
    last = pl.num_programs(0) - 1
    tm = ROW_TILE
    d = x1_ref.shape[1]

    def start(idx_ref, base, bufs, sem):
        for r in range(tm):
            for slot in range(2):
                pltpu.make_async_copy(_token_tile(yb8_ref, idx_ref[base + 2 * r + slot]),
                                      bufs[slot].at[pl.ds(r * TOKEN_ROWS, TOKEN_ROWS)],
                                      sem).start(priority=slot)

    def wait(bufs, sem):
        for buf in bufs:
            pltpu.make_async_copy(yb8_ref.at[pl.ds(0, tm * TOKEN_ROWS)], buf, sem).wait()

    def compute(bufs, rows):
        x2 = (x1_ref[rows, :]
              + route_ref[rows, 2:3] * _load_token_tiles(bufs[0], 0, tm).astype(F32)
              + route_ref[rows, 3:4] * _load_token_tiles(bufs[1], 0, tm).astype(F32))
        h = (x2 * _rms_scale(x2, d) * ln_ref[...]).astype(BF16)
        pp = _dot(p_ref[rows, :].astype(BF16), wp_ref[...])
        out_ref[rows, :] = x2 + pp * _sigmoid(_dot(h, wg_ref[...]))

    @pl.when(j == 0)
    def _():
        start(dest_ref, 0, (ya0, ya1), sems.at[0])

    wait((ya0, ya1), sems.at[0])
    start(dest_ref, 2 * tm, (yb0, yb1), sems.at[1])
    compute((ya0, ya1), slice(0, tm))

    wait((yb0, yb1), sems.at[1])
    start(destn_ref, 0, (ya0, ya1), sems.at[0])
    compute((yb0, yb1), slice(tm, 2 * tm))

    @pl.when(j == last)
    def _():
        wait((ya0, ya1), sems.at[0])


def _ple(dest_flat, x1, route, p2, ln, wg, wp, yb):
    t, d = x1.shape
    tm = ROW_TILE
    nsteps = t // (2 * tm)
    row = lambda w: pl.BlockSpec((2 * tm, w), lambda j: (j, 0))
    return pl.pallas_call(
        _ple_kernel,
        grid=(nsteps,),
        in_specs=[pl.BlockSpec((4 * tm,), lambda j: (j,), memory_space=pltpu.SMEM),
                  pl.BlockSpec((4 * tm,), lambda j: (jnp.minimum(j + 1, nsteps - 1),),
                               memory_space=pltpu.SMEM),
                  row(d), row(LANES), row(p2.shape[1]), _const_spec(ln.shape),
                  _const_spec(wg.shape), _const_spec(wp.shape),
                  pl.BlockSpec(memory_space=pl.ANY)],
        out_specs=row(d),
        out_shape=jax.ShapeDtypeStruct((t, d), F32),
        scratch_shapes=([pltpu.VMEM((tm * TOKEN_ROWS, LANES), jnp.uint32)] * 4
                        + [pltpu.SemaphoreType.DMA((2,))]),
        compiler_params=_cparams(("arbitrary",)),
        name="ple",
    )(dest_flat, dest_flat, x1, route, p2, ln, wg, wp, yb)


def _head_pad(w, width):
    r = w.shape[0]
    w = w.reshape(r, MLA_HEADS, width)
    return jnp.pad(w, ((0, 0), (0, 0), (0, LANES - width))).reshape(r, MLA_HEADS * LANES)


def _rope_table():
    half = MLA_ROPE // 2
    inv_freq = ROPE_THETA ** (-np.arange(half, dtype=np.float32) / half)
    tab = np.zeros((8, LANES), np.float32)
    tab[0, MLA_NOPE:MLA_NOPE + half] = inv_freq
    tab[0, MLA_NOPE + half:MLA_QK] = inv_freq
    tab[1, MLA_NOPE + half:MLA_QK] = 1.0
    tab[2, MLA_NOPE:MLA_NOPE + half] = -1.0
    freq_t = np.broadcast_to(inv_freq[:, None], (half, LANES))
    return jnp.asarray(tab), jnp.asarray(freq_t)


def _route_tables(ids):
    a = ids.shape[0] * 2
    e = ids.reshape(a)
    onehot = (e[:, None] == jnp.arange(N_EXPERTS, dtype=jnp.int32)[None, :]).astype(jnp.int32)
    csum = jnp.cumsum(onehot, axis=0)
    counts = lax.optimization_barrier(csum[-1])
    pcounts = (counts + MOE_ROWS - 1) // MOE_ROWS * MOE_ROWS
    pends = jnp.cumsum(pcounts)
    pstarts = pends - pcounts
    dest = jnp.sum(onehot * (csum - 1 + pstarts[None, :]), axis=1).astype(jnp.int32)
    nblk = (a + N_EXPERTS * (MOE_ROWS - 1)) // MOE_ROWS
    nblk += nblk % 2
    n_used = (pends[-1] // MOE_ROWS).astype(jnp.int32)
    blk_start = jnp.arange(nblk, dtype=jnp.int32) * MOE_ROWS
    blk_start = jnp.minimum(blk_start, pends[-1] - 1)
    block_expert = jnp.sum((pends[None, :] <= blk_start[:, None]).astype(jnp.int32), axis=1)
    block_expert = jnp.minimum(block_expert, N_EXPERTS - 1).astype(jnp.int32)
    present = counts > 0
    ids_or_end = jnp.where(present, jnp.arange(N_EXPERTS, dtype=jnp.int32), N_EXPERTS)
    first_at_or_after = jnp.flip(lax.cummin(jnp.flip(ids_or_end)))
    next_present = jnp.concatenate([first_at_or_after[1:], jnp.full((1,), N_EXPERTS, jnp.int32)])
    next_present = jnp.where(next_present == N_EXPERTS, -1, next_present)
    parity = (jnp.cumsum(present.astype(jnp.int32)) - 1) % 2
    return (dest, block_expert, n_used.reshape(1), next_present[block_expert].astype(jnp.int32),
            parity[block_expert].astype(jnp.int32), nblk)


def kernel(x, p, positions, ln_mix, w_in, hg_lb, hg_onorm, w_oA, mla_qa_norm, mla_kva_norm, w_uq,
           w_ukv, q_norm, k_norm, w_oB, w_out, ln_moe, w_rg, b_rg, w_re, b_re, w1, w3, w2, ln_ple,
           w_ple_gate, w_ple_proj):
    b, s, d = x.shape
    t = b * s
    depth = w_in.shape[0]
    lb_all = jnp.cumsum(jax.nn.softmax(hg_lb.astype(F32), axis=1), axis=1)
    posf = positions.astype(F32).reshape(t, 1)
    rope_tab, freq_t = _rope_table()
    xc = x.reshape(t, d)

    for layer in range(depth):
        wi = w_in[layer]
        n_hg = 3 * HG_F + 2 * HG_V
        n_mla = MLA_Q_RANK + MLA_KV_RANK + MLA_ROPE
        mla_w = -(-n_mla // LANES) * LANES
        whg = wi[:, :n_hg].astype(BF16)
        wmla = jnp.pad(wi[:, n_hg:n_hg + n_mla], ((0, 0), (0, mla_w - n_mla))).astype(BF16)
        wgate = wi[:, n_hg + n_mla:].astype(BF16)
        lb = lb_all[:, layer, :]

        qs, lf, kk, v, og, mla_in, gates = _inproj(
            xc, ln_mix[layer][None, :], whg, wmla, wgate, lb, mla_qa_norm[layer][None, :],
            mla_kva_norm[layer][None, :])

        r3 = lambda a: a.reshape(b, s, a.shape[-1])
        ya = _hgrn(r3(qs), r3(lf), r3(kk), r3(v), r3(og), hg_onorm[layer][None, :]).reshape(t, HG_V)

        kv_w = MLA_NOPE + MLA_VDIM
        wkv = w_ukv[layer].reshape(MLA_KV_RANK, MLA_HEADS, kv_w)
        ckr_w = mla_w - MLA_Q_RANK
        wq = _head_pad(w_uq[layer], MLA_QK).astype(BF16)
        wk_nope = _head_pad(wkv[:, :, :MLA_NOPE].reshape(MLA_KV_RANK, MLA_HEADS * MLA_NOPE), MLA_NOPE)
        place = np.zeros((ckr_w - MLA_KV_RANK, MLA_HEADS, LANES), np.float32)
        for j in range(MLA_ROPE):
            place[j, :, MLA_NOPE + j] = 1.0
        wk = jnp.concatenate([wk_nope, jnp.asarray(place.reshape(ckr_w - MLA_KV_RANK, -1))],
                             axis=0).astype(BF16)
        wv = jnp.pad(wkv[:, :, MLA_NOPE:].reshape(MLA_KV_RANK, MLA_HEADS * MLA_VDIM),
                     ((0, ckr_w - MLA_KV_RANK), (0, 0))).astype(BF16)
        gq = jnp.pad(q_norm[layer], (0, LANES - MLA_QK))[None, :]
        gk_t = jnp.broadcast_to(jnp.pad(k_norm[layer], (0, LANES - MLA_QK))[:, None], (LANES, LANES))
        qh, kh_t, vh = _mlaprep(mla_in, posf, wq, wk.T, wv, gq, gk_t, rope_tab, freq_t)
        yb = _attn(r3(qh), kh_t, r3(vh)).reshape(t, MLA_HEADS * MLA_VDIM)

        wr = jnp.pad(jnp.concatenate([w_rg[layer], w_re[layer]], axis=1),
                     ((0, 0), (0, LANES - N_GROUPS - N_EXPERTS)))
        wr_hi = wr.astype(BF16)
        wr = jnp.concatenate([wr_hi, (wr - wr_hi.astype(F32)).astype(BF16)], axis=1)
        br = jnp.pad(jnp.concatenate([b_rg[layer], b_re[layer]]),
                     (0, LANES - N_GROUPS - N_EXPERTS))[None, :]
        x1, hm8, route = _merge(xc, ya, yb, gates, w_oA[layer].astype(BF16), w_oB[layer].astype(BF16),
                               w_out[layer].astype(BF16), ln_moe[layer][None, :], wr, br)

        ids = route[:, 0:2].astype(jnp.int32)
        dest, block_expert, n_used, next_expert, slot_parity, nblk = _route_tables(ids)
        xb8 = _dispatch(dest, hm8, jnp.zeros((nblk * MOE_ROWS * TOKEN_ROWS, LANES), jnp.uint32))
        yexp = _experts(block_expert, n_used, next_expert, slot_parity, xb8,
                        w1[layer], w3[layer], w2[layer])

        xc = _ple(dest, x1, route, p[layer].reshape(t, -1), ln_ple[layer][None, :],
                  w_ple_gate[layer].astype(BF16), w_ple_proj[layer].astype(BF16), yexp)
    return xc.reshape(b, s, d)
```

```python
import functools

import numpy as np
import jax
import jax.numpy as jnp
from jax import lax
from jax.experimental import pallas as pl
from jax.experimental.pallas import tpu as pltpu

F32 = jnp.float32
BF16 = jnp.bfloat16

HG_HEADS = 4
HG_KDIM = 128
HG_VDIM = 128
HG_F = HG_HEADS * HG_KDIM
HG_V = HG_HEADS * HG_VDIM
MLA_HEADS = 8
MLA_NOPE = 64
MLA_ROPE = 32
MLA_VDIM = 64
MLA_QK = MLA_NOPE + MLA_ROPE
MLA_Q_RANK = 256
MLA_KV_RANK = 128
ROPE_THETA = 10000.0
N_GROUPS = 8
EXPERTS_PER_GROUP = 8
N_EXPERTS = N_GROUPS * EXPERTS_PER_GROUP
D_EXPERT = 256
EPS = 1e-6

LANES = 128
TOKEN_ROWS = 4
VMEM_LIMIT_BYTES = 56 * 1024 * 1024

ROW_TILE = 256
MERGE_ROW_TILE = 512
INPROJ_ROW_TILE = 256
MLAPREP_ROW_TILE = 512
HG_CHUNK = 64
HG_HEADS_PER_STEP = 2
HG_CHUNKS_PER_MATMUL = 4
ATTN_Q_TILE = 512
ATTN_HEADS_PER_STEP = 8
MOE_ROWS = 256
DISPATCH_TOKENS = 1024
HG_SAFE_LOG_DECAY = -75.0


def _cparams(sem):
    return pltpu.CompilerParams(dimension_semantics=sem, vmem_limit_bytes=VMEM_LIMIT_BYTES)


def _const_spec(shape):
    nd = len(shape)
    return pl.BlockSpec(shape, lambda *_: (0,) * nd)


def _sigmoid(x):
    return 1.0 / (1.0 + jnp.exp(-x))


def _silu(x):
    return x * _sigmoid(x)


def _rms_scale(x, n):
    return lax.rsqrt(jnp.sum(x * x, axis=-1, keepdims=True) * (1.0 / n) + EPS)


def _dot(a, b):
    return jnp.dot(a, b, preferred_element_type=F32)


def _dot_nt(a, b):
    return lax.dot_general(a, b, (((1,), (1,)), ((), ())), preferred_element_type=F32)


def _dot_tn(a, b):
    return lax.dot_general(a, b, (((0,), (0,)), ((), ())), preferred_element_type=F32)


def _chunk_cumsum(a, direction):
    n = a.shape[0]
    in_chunk = lax.broadcasted_iota(jnp.int32, a.shape, 0) % HG_CHUNK
    k = 1
    while k < HG_CHUNK:
        if direction == 0:
            a = a + jnp.where(in_chunk >= k, pltpu.roll(a, k, 0), 0.0)
        else:
            a = a + jnp.where(in_chunk < HG_CHUNK - k, pltpu.roll(a, n - k, 0), 0.0)
        k *= 2
    return a


def _inproj_kernel(x_ref, g_ref, whg_ref, wmla_ref, wgate_ref, lb_ref, nq_ref, nkv_ref,
                   qs_ref, lf_ref, kk_ref, v_ref, og_ref, mla_ref, gates_ref):
    x = x_ref[...]
    d = x.shape[-1]
    h = (x * _rms_scale(x, d) * g_ref[...]).astype(BF16)

    qs_ref[...] = _silu(_dot(h, whg_ref[:, 0:HG_F])).astype(BF16)
    for direction in range(2):
        cols = slice(HG_F * (1 + direction), HG_F * (2 + direction))
        out_cols = slice(HG_F * direction, HG_F * (direction + 1))
        lb = lb_ref[direction:direction + 1, :]
        f = lb + (1.0 - lb) * _sigmoid(_dot(h, whg_ref[:, cols]))
        lf_ref[:, out_cols] = jnp.log(f)
        kk_ref[:, out_cols] = (1.0 - f).astype(BF16)
    v_ref[...] = _dot(h, whg_ref[:, 3 * HG_F:3 * HG_F + HG_V]).astype(BF16)
    og_ref[...] = _silu(_dot(h, whg_ref[:, 3 * HG_F + HG_V:3 * HG_F + 2 * HG_V])).astype(BF16)

    zm = _dot(h, wmla_ref[...])
    cq = zm[:, 0:MLA_Q_RANK]
    mla_ref[:, 0:MLA_Q_RANK] = (cq * _rms_scale(cq, MLA_Q_RANK) * nq_ref[...]).astype(BF16)
    ckv = zm[:, MLA_Q_RANK:MLA_Q_RANK + MLA_KV_RANK]
    mla_ref[:, MLA_Q_RANK:MLA_Q_RANK + MLA_KV_RANK] = (
        ckv * _rms_scale(ckv, MLA_KV_RANK) * nkv_ref[...]).astype(BF16)
    mla_ref[:, MLA_Q_RANK + MLA_KV_RANK:] = zm[:, MLA_Q_RANK + MLA_KV_RANK:].astype(BF16)

    for half in range(2):
        cols = slice(d * half, d * (half + 1))
        gates_ref[:, cols] = _sigmoid(_dot(h, wgate_ref[:, cols])).astype(BF16)


def _inproj(x2, ln_mix, whg, wmla, wgate, lb, nq, nkv):
    t, d = x2.shape
    tm = INPROJ_ROW_TILE
    mla_w = wmla.shape[1]
    row = lambda w: pl.BlockSpec((tm, w), lambda i: (i, 0))
    out_shape = (
        jax.ShapeDtypeStruct((t, HG_F), BF16),
        jax.ShapeDtypeStruct((t, 2 * HG_F), F32),
        jax.ShapeDtypeStruct((t, 2 * HG_F), BF16),
        jax.ShapeDtypeStruct((t, HG_V), BF16),
        jax.ShapeDtypeStruct((t, HG_V), BF16),
        jax.ShapeDtypeStruct((t, mla_w), BF16),
        jax.ShapeDtypeStruct((t, 2 * d), BF16),
    )
    return pl.pallas_call(
        _inproj_kernel,
        grid=(t // tm,),
        in_specs=[row(d), _const_spec((1, d)), _const_spec(whg.shape), _const_spec(wmla.shape),
                  _const_spec(wgate.shape), _const_spec(lb.shape), _const_spec(nq.shape),
                  _const_spec(nkv.shape)],
        out_specs=(row(HG_F), row(2 * HG_F), row(2 * HG_F), row(HG_V), row(HG_V), row(mla_w),
                   row(2 * d)),
        out_shape=out_shape,
        compiler_params=_cparams(("parallel",)),
        name="inproj",
    )(x2, ln_mix, whg, wmla, wgate, lb, nq, nkv)


def _hgrn_kernel(qs_ref, lff_ref, lfb_ref, kf_ref, kb_ref, v_ref, og_ref, onorm_ref, out_ref,
                 state_ref, ofw_ref, obw_ref, qt_scr, u_scr, dec_scr, g_scr, k_scr, v_scr):
    c = HG_CHUNK
    s_len = qs_ref.shape[1]
    n_chunks = s_len // c
    hb = HG_HEADS_PER_STEP
    kd = HG_KDIM

    row = lax.broadcasted_iota(jnp.int32, (c, c), 0)
    col = lax.broadcasted_iota(jnp.int32, (c, c), 1)
    masks = (row >= col, row <= col)
    rowc = lax.broadcasted_iota(jnp.int32, (c, kd), 0)
    o_refs = (ofw_ref, obw_ref)

    def load(j):
        chains = []
        for hh in range(hb):
            lanes = slice(hh * kd, (hh + 1) * kd)
            for direction in range(2):
                chunk = j if direction == 0 else n_chunks - 1 - j
                rows = pl.ds(pl.multiple_of(chunk * c, c), c)
                g = _chunk_cumsum((lff_ref, lfb_ref)[direction][0, rows, lanes], direction)
                total = g[c - 1:c, :] if direction == 0 else g[0:1, :]
                chains.append(dict(
                    idx=hh * 2 + direction, direction=direction, rows=rows, lanes=lanes, g=g,
                    total=total,
                    q=qs_ref[0, rows, lanes].astype(F32),
                    k=(kf_ref, kb_ref)[direction][0, rows, lanes].astype(F32),
                    v=v_ref[0, rows, lanes]))
        return chains

    def finish(ch, o, kdec_t_v_scaled):
        (ofw_ref, obw_ref)[ch["direction"]][ch["rows"], ch["lanes"]] = o
        state_ref[ch["idx"]] = kdec_t_v_scaled

    def fast(ch):
        g = ch["g"]
        qt = (ch["q"] * jnp.exp(g)).astype(BF16)
        kt = (ch["k"] * jnp.exp(-g)).astype(BF16)
        st = state_ref[ch["idx"]]
        sc = jnp.where(masks[ch["direction"]], _dot_nt(qt, kt), 0.0)
        o = _dot_nt(qt, st.astype(BF16)) + _dot(sc.astype(BF16), ch["v"])
        finish(ch, o, (st + _dot_tn(ch["v"], kt)) * jnp.exp(ch["total"]))

    def robust(ch):
        g = ch["g"]
        direction = ch["direction"]
        st = state_ref[ch["idx"]]
        o0 = _dot_nt((ch["q"] * jnp.exp(g)).astype(BF16), st.astype(BF16))
        slot = ch["idx"]
        g_scr[slot] = g
        k_scr[slot] = ch["k"]
        v_scr[slot] = ch["v"].astype(F32)
        q = ch["q"]

        def body(s, acc):
            g_s = g_scr[slot, pl.ds(s, 1), :]
            seen = (rowc >= s) if direction == 0 else (rowc <= s)
            decay = jnp.where(seen, jnp.exp(jnp.minimum(g - g_s, 0.0)), 0.0)
            a = jnp.sum(q * decay * k_scr[slot, pl.ds(s, 1), :], axis=-1, keepdims=True)
            return acc + a * v_scr[slot, pl.ds(s, 1), :]

        o = lax.fori_loop(0, c, body, o0)
        kdec = (ch["k"] * jnp.exp(ch["total"] - g)).astype(BF16)
        finish(ch, o, st * jnp.exp(ch["total"]) + _dot_tn(ch["v"], kdec))

    def step(j, carry):
        chains = load(j)
        lowest = chains[0]["total"]
        for ch in chains[1:]:
            lowest = jnp.minimum(lowest, ch["total"])
        safe = jnp.min(lowest) >= HG_SAFE_LOG_DECAY

        @pl.when(safe)
        def _():
            for ch in chains:
                fast(ch)

        @pl.when(jnp.logical_not(safe))
        def _():
            for ch in chains:
                robust(ch)

        return carry

    grp = HG_CHUNKS_PER_MATMUL if n_chunks % HG_CHUNKS_PER_MATMUL == 0 else 1
    gr = grp * c
    n_groups = n_chunks // grp
    grow = lax.broadcasted_iota(jnp.int32, (gr, gr), 0)
    gcol = lax.broadcasted_iota(jnp.int32, (gr, gr), 1)
    same_chunk = (grow // c) == (gcol // c)
    gmasks = (same_chunk & (grow >= gcol), same_chunk & (grow <= gcol))
    chunk_of_row = lax.broadcasted_iota(jnp.int32, (gr, kd), 0) // c

    def block_diag(a):
        return jnp.concatenate([jnp.where(chunk_of_row == i, a, jnp.zeros_like(a))
                                for i in range(grp)], axis=1)

    def phase_a(gi, lowest):
        rows = pl.ds(pl.multiple_of(gi * gr, gr), gr)
        for hh in range(hb):
            lanes = slice(hh * kd, (hh + 1) * kd)
            v = v_ref[0, rows, lanes]
            v_t = v.T
            q = qs_ref[0, rows, lanes].astype(F32)
            for direction in range(2):
                idx = hh * 2 + direction
                g = _chunk_cumsum((lff_ref, lfb_ref)[direction][0, rows, lanes], direction)
                k = (kf_ref, kb_ref)[direction][0, rows, lanes].astype(F32)
                qt = (q * jnp.exp(g)).astype(BF16)
                kt = (k * jnp.exp(-g)).astype(BF16)
                qt_scr[idx, rows, :] = qt
                sc = jnp.where(gmasks[direction], _dot_nt(qt, kt), 0.0)
                o_refs[direction][rows, lanes] = _dot(sc.astype(BF16), v)
                u = _dot(v_t, block_diag(kt))
                for i in range(grp):
                    u_scr[idx, gi * grp + i] = u[:, i * kd:(i + 1) * kd]
                    edge = i * c + (c - 1 if direction == 0 else 0)
                    total = g[edge:edge + 1, :]
                    lowest = jnp.minimum(lowest, total)
                    dec_scr[idx, pl.ds(gi * grp + i, 1), :] = jnp.exp(total)
        return lowest

    lowest = lax.fori_loop(0, n_groups, phase_a, jnp.zeros((1, kd), F32))
    all_safe = jnp.min(lowest) >= HG_SAFE_LOG_DECAY
    state_ref[...] = jnp.zeros_like(state_ref)

    @pl.when(all_safe)
    def _():
        def phase_b(j, carry):
            for hh in range(hb):
                lanes = slice(hh * kd, (hh + 1) * kd)
                for direction in range(2):
                    idx = hh * 2 + direction
                    gi = j if direction == 0 else n_groups - 1 - j
                    rows = pl.ds(pl.multiple_of(gi * gr, gr), gr)
                    st = state_ref[idx]
                    seen = [None] * grp
                    for i in (range(grp) if direction == 0 else reversed(range(grp))):
                        ci = gi * grp + i
                        seen[i] = st.astype(BF16)
                        st = (st + u_scr[idx, ci]) * dec_scr[idx, pl.ds(ci, 1), :]
                    state_ref[idx] = st
                    o_refs[direction][rows, lanes] += _dot_nt(
                        block_diag(qt_scr[idx, rows, :]), jnp.concatenate(seen, axis=1))
            return carry

        lax.fori_loop(0, n_groups, phase_b, 0)

    @pl.when(jnp.logical_not(all_safe))
    def _():
        lax.fori_loop(0, n_chunks, step, 0)

    blk = 256 if s_len % 256 == 0 else c

    def epilogue(i, carry):
        rows = pl.ds(pl.multiple_of(i * blk, blk), blk)
        for hh in range(hb):
            lanes = slice(hh * kd, (hh + 1) * kd)
            o = ofw_ref[rows, lanes] + obw_ref[rows, lanes]
            y = o * _rms_scale(o, HG_VDIM) * onorm_ref[...]
            out_ref[0, rows, lanes] = (y * og_ref[0, rows, lanes].astype(F32)).astype(BF16)
        return carry

    lax.fori_loop(0, s_len // blk, epilogue, 0)


def _hgrn(qs, lf, kk, v, og, onorm):
    b, s, _ = qs.shape
    hb = HG_HEADS_PER_STEP
    w = hb * HG_KDIM
    n_hsteps = HG_HEADS // hb
    fwd = pl.BlockSpec((1, s, w), lambda bi, hi: (bi, 0, hi))
    bwd = pl.BlockSpec((1, s, w), lambda bi, hi: (bi, 0, n_hsteps + hi))
    return pl.pallas_call(
        _hgrn_kernel,
        grid=(b, n_hsteps),
        in_specs=[fwd, fwd, bwd, fwd, bwd, fwd, fwd, _const_spec(onorm.shape)],
        out_specs=fwd,
        out_shape=jax.ShapeDtypeStruct((b, s, HG_V), BF16),
        scratch_shapes=[
            pltpu.VMEM((2 * hb, HG_VDIM, HG_KDIM), F32),
            pltpu.VMEM((s, w), F32),
            pltpu.VMEM((s, w), F32),
            pltpu.VMEM((2 * hb, s, HG_KDIM), BF16),
            pltpu.VMEM((2 * hb, s // HG_CHUNK, HG_VDIM, HG_KDIM), F32),
            pltpu.VMEM((2 * hb, max(s // HG_CHUNK, 8), HG_KDIM), F32),
            pltpu.VMEM((2 * hb, HG_CHUNK, HG_KDIM), F32),
            pltpu.VMEM((2 * hb, HG_CHUNK, HG_KDIM), F32),
            pltpu.VMEM((2 * hb, HG_CHUNK, HG_VDIM), F32),
        ],
        compiler_params=_cparams(("parallel", "parallel")),
        name="hgrn",
    )(qs, lf, lf, kk, kk, v, og, onorm)


def _mlaprep_kernel(mla_ref, pos_ref, pos_t_ref, wq_ref, wk_t_ref, wv_ref, gq_ref, gk_t_ref,
                    rope_ref, freq_t_ref, q_ref, k_t_ref, v_ref):
    cq = mla_ref[:, 0:MLA_Q_RANK]
    ckr = mla_ref[:, MLA_Q_RANK:]
    half = MLA_ROPE // 2

    ang = pos_ref[...] * rope_ref[0:1, :]
    cos = jnp.cos(ang)
    sin = jnp.sin(ang)
    sin_lo = sin * rope_ref[1:2, :]
    sin_hi = sin * rope_ref[2:3, :]
    qm = _dot(cq, wq_ref[...])
    for hd in range(MLA_HEADS):
        lanes = slice(hd * LANES, (hd + 1) * LANES)
        x = qm[:, lanes]
        y = x * (_rms_scale(x, MLA_QK) * MLA_QK ** -0.5) * gq_ref[...]
        y = y * cos + pltpu.roll(y, half, 1) * sin_lo + pltpu.roll(y, LANES - half, 1) * sin_hi
        q_ref[:, lanes] = y.astype(BF16)

    km_t = _dot_nt(wk_t_ref[...], ckr)
    reps = km_t.shape[1] // LANES
    lane_tile = lambda a: jnp.concatenate([a] * reps, axis=1)
    ang_t = lane_tile(freq_t_ref[...]) * pos_t_ref[...]
    cos_t = jnp.cos(ang_t)
    sin_t = jnp.sin(ang_t)
    gain_t = lane_tile(gk_t_ref[...])
    for hd in range(MLA_HEADS):
        r0 = hd * LANES
        x = km_t[r0:r0 + LANES, :]
        y = x * lax.rsqrt(jnp.sum(x * x, axis=0, keepdims=True) * (1.0 / MLA_QK) + EPS) * gain_t
        x1 = y[MLA_NOPE:MLA_NOPE + half, :]
        x2 = y[MLA_NOPE + half:MLA_QK, :]
        k_t_ref[r0:r0 + MLA_NOPE, :] = y[0:MLA_NOPE, :].astype(BF16)
        k_t_ref[r0 + MLA_NOPE:r0 + MLA_NOPE + half, :] = (x1 * cos_t - x2 * sin_t).astype(BF16)
        k_t_ref[r0 + MLA_NOPE + half:r0 + MLA_QK, :] = (x2 * cos_t + x1 * sin_t).astype(BF16)
        k_t_ref[r0 + MLA_QK:r0 + LANES, :] = jnp.zeros((LANES - MLA_QK, km_t.shape[1]), BF16)

    v_ref[...] = _dot(ckr, wv_ref[...]).astype(BF16)


def _mlaprep(mla_in, posf, wq, wk_t, wv, gq, gk_t, rope_tab, freq_t):
    t = mla_in.shape[0]
    tm = MLAPREP_ROW_TILE
    row = lambda w: pl.BlockSpec((tm, w), lambda i: (i, 0))
    hw = MLA_HEADS * LANES
    vw = MLA_HEADS * MLA_VDIM
    return pl.pallas_call(
        _mlaprep_kernel,
        grid=(t // tm,),
        in_specs=[row(mla_in.shape[1]), row(1), pl.BlockSpec((1, tm), lambda i: (0, i)),
                  _const_spec(wq.shape), _const_spec(wk_t.shape), _const_spec(wv.shape),
                  _const_spec(gq.shape), _const_spec(gk_t.shape), _const_spec(rope_tab.shape),
                  _const_spec(freq_t.shape)],
        out_specs=(row(hw), pl.BlockSpec((hw, tm), lambda i: (0, i)), row(vw)),
        out_shape=(jax.ShapeDtypeStruct((t, hw), BF16), jax.ShapeDtypeStruct((hw, t), BF16),
                   jax.ShapeDtypeStruct((t, vw), BF16)),
        compiler_params=_cparams(("parallel",)),
        name="mlaprep",
    )(mla_in, posf, posf.reshape(1, t), wq, wk_t, wv, gq, gk_t, rope_tab, freq_t)


def _attn_kernel(q_ref, k_t_ref, v_ref, o_ref):
    for j in range(ATTN_HEADS_PER_STEP):
        lanes = slice(j * LANES, (j + 1) * LANES)
        s = _dot(q_ref[0, :, lanes], k_t_ref[lanes, :])
        p = jnp.exp(s - jnp.max(s, axis=-1, keepdims=True))
        l = jnp.sum(p, axis=-1, keepdims=True)
        vl = slice(j * MLA_VDIM, (j + 1) * MLA_VDIM)
        o = _dot(p.astype(BF16), v_ref[0, :, vl])
        o_ref[0, :, vl] = (o / l).astype(BF16)


def _attn(q, k_t, v):
    b, s, _ = q.shape
    tq = min(ATTN_Q_TILE, s)
    nh = ATTN_HEADS_PER_STEP
    return pl.pallas_call(
        _attn_kernel,
        grid=(b, MLA_HEADS // nh, s // tq),
        in_specs=[pl.BlockSpec((1, tq, nh * LANES), lambda bi, hp, i: (bi, i, hp)),
                  pl.BlockSpec((nh * LANES, s), lambda bi, hp, i: (hp, bi)),
                  pl.BlockSpec((1, s, nh * MLA_VDIM), lambda bi, hp, i: (bi, 0, hp))],
        out_specs=pl.BlockSpec((1, tq, nh * MLA_VDIM), lambda bi, hp, i: (bi, i, hp)),
        out_shape=jax.ShapeDtypeStruct((b, s, MLA_HEADS * MLA_VDIM), BF16),
        compiler_params=_cparams(("parallel", "parallel", "arbitrary")),
        name="attn",
    )(q, k_t, v)


def _merge_kernel(x_ref, ya_ref, yb_ref, gates_ref, woa_ref, wob_ref, wout_ref, lnm_ref, wr_ref,
                  br_ref, x1_ref, hm8_ref, route_ref):
    d = x_ref.shape[-1]
    y_a = _dot(ya_ref[...], woa_ref[...])
    y_b = _dot(yb_ref[...], wob_ref[...])
    merged = gates_ref[:, 0:d].astype(F32) * y_a + gates_ref[:, d:2 * d].astype(F32) * y_b
    x1 = x_ref[...] + _dot(merged.astype(BF16), wout_ref[...])
    x1_ref[...] = x1
    hm = x1 * _rms_scale(x1, d) * lnm_ref[...]
    _store_token_tiles(hm8_ref, 0, hm)

    hm_hi = hm.astype(BF16)
    hm_lo = (hm - hm_hi.astype(F32)).astype(BF16)
    hh = _dot(hm_hi, wr_ref[...])
    logits = (hh[:, 0:LANES] + hh[:, LANES:2 * LANES] + _dot(hm_lo, wr_ref[:, 0:LANES])
              + br_ref[...])
    lane = lax.broadcasted_iota(jnp.int32, logits.shape, 1)
    neg = -jnp.inf
    big = jnp.int32(2 ** 30)
    is_group = lane < N_GROUPS
    gl = jnp.where(is_group, logits, neg)
    gmax = jnp.max(gl, axis=-1, keepdims=True)
    p_group = 1.0 / jnp.sum(jnp.where(is_group, jnp.exp(gl - gmax), 0.0), axis=-1, keepdims=True)
    g_sel = jnp.min(jnp.where(gl == gmax, lane, big), axis=-1, keepdims=True)
    lo = N_GROUPS + g_sel * EXPERTS_PER_GROUP
    in_group = (lane >= lo) & (lane < lo + EXPERTS_PER_GROUP)
    el = jnp.where(in_group, logits, neg)
    v1 = jnp.max(el, axis=-1, keepdims=True)
    i1 = jnp.min(jnp.where(el == v1, lane, big), axis=-1, keepdims=True)
    el2 = jnp.where(lane == i1, neg, el)
    v2 = jnp.max(el2, axis=-1, keepdims=True)
    i2 = jnp.min(jnp.where(el2 == v2, lane, big), axis=-1, keepdims=True)
    e21 = jnp.exp(v2 - v1)
    w1 = p_group / (1.0 + e21)
    w2 = w1 * e21
    route = jnp.where(lane == 0, (i1 - N_GROUPS).astype(F32),
                      jnp.where(lane == 1, (i2 - N_GROUPS).astype(F32),
                                jnp.where(lane == 2, w1, jnp.where(lane == 3, w2, 0.0))))
    route_ref[...] = route


def _merge(x2, ya, yb, gates, woa, wob, wout, lnm, wr, br):
    t, d = x2.shape
    tm = MERGE_ROW_TILE
    row = lambda w: pl.BlockSpec((tm, w), lambda i: (i, 0))
    return pl.pallas_call(
        _merge_kernel,
        grid=(t // tm,),
        in_specs=[row(d), row(ya.shape[1]), row(yb.shape[1]), row(2 * d), _const_spec(woa.shape),
                  _const_spec(wob.shape), _const_spec(wout.shape), _const_spec(lnm.shape),
                  _const_spec(wr.shape), _const_spec(br.shape)],
        out_specs=(row(d), pl.BlockSpec((tm * TOKEN_ROWS, LANES), lambda i: (i, 0)), row(LANES)),
        out_shape=(jax.ShapeDtypeStruct((t, d), F32),
                   jax.ShapeDtypeStruct((t * TOKEN_ROWS, LANES), jnp.uint32),
                   jax.ShapeDtypeStruct((t, LANES), F32)),
        compiler_params=_cparams(("parallel",)),
        name="merge",
    )(x2, ya, yb, gates, woa, wob, wout, lnm, wr, br)


_HIGH_HALF = 0xFFFF0000


def _store_token_tiles(ref4, tok0, val):
    n, width = val.shape
    as_bits = lambda a: lax.bitcast_convert_type(a.astype(BF16).astype(F32), jnp.uint32)
    words = (as_bits(val[:, :width // 2]) >> 16) | (as_bits(val[:, width // 2:]) & jnp.uint32(_HIGH_HALF))
    for j in range(TOKEN_ROWS):
        ref4[pl.ds(tok0 * TOKEN_ROWS + j, n, stride=TOKEN_ROWS), :] = words[:, j * LANES:(j + 1) * LANES]


def _load_token_tiles(ref4, tok0, n):
    words = jnp.concatenate(
        [ref4[pl.ds(tok0 * TOKEN_ROWS + j, n, stride=TOKEN_ROWS), :] for j in range(TOKEN_ROWS)],
        axis=1)
    low = lax.bitcast_convert_type(words << 16, F32).astype(BF16)
    high = lax.bitcast_convert_type(words & jnp.uint32(_HIGH_HALF), F32).astype(BF16)
    return jnp.concatenate([low, high], axis=1)


def _token_tile(ref8, tok):
    return ref8.at[pl.ds(pl.multiple_of(tok * TOKEN_ROWS, TOKEN_ROWS), TOKEN_ROWS)]


def _dispatch_kernel(dest_ref, hm8_ref, xb_in_ref, xb8_ref, sem):
    del xb_in_ref
    n_tok = DISPATCH_TOKENS
    for r in range(n_tok):
        src = hm8_ref.at[pl.ds(r * TOKEN_ROWS, TOKEN_ROWS)]
        for slot in range(2):
            pltpu.make_async_copy(src, _token_tile(xb8_ref, dest_ref[2 * r + slot]),
                                  sem).start(priority=slot)
    for _ in range(2):
        pltpu.make_async_copy(hm8_ref, xb8_ref.at[pl.ds(0, n_tok * TOKEN_ROWS)], sem).wait()


def _dispatch(dest_flat, hm8, xb_init):
    t = hm8.shape[0] // TOKEN_ROWS
    return pl.pallas_call(
        _dispatch_kernel,
        grid=(t // DISPATCH_TOKENS,),
        in_specs=[pl.BlockSpec((2 * DISPATCH_TOKENS,), lambda i: (i,), memory_space=pltpu.SMEM),
                  pl.BlockSpec((DISPATCH_TOKENS * TOKEN_ROWS, LANES), lambda i: (i, 0)),
                  pl.BlockSpec(memory_space=pl.ANY)],
        out_specs=pl.BlockSpec(memory_space=pl.ANY),
        out_shape=jax.ShapeDtypeStruct(xb_init.shape, xb_init.dtype),
        scratch_shapes=[pltpu.SemaphoreType.DMA(())],
        input_output_aliases={2: 0},
        compiler_params=_cparams(("arbitrary",)),
        name="dispatch",
    )(dest_flat, hm8, xb_init)


def _experts_kernel(be_ref, nu_ref, nxt_ref, par_ref, xb8_ref, w1_hbm, w3_hbm, w2_hbm, yb8_ref,
                    w1f, w3f, w2f, sems, w1c, w3c, w2c):
    j = pl.program_id(0)

    def fetch(e, slot):
        return [pltpu.make_async_copy(src.at[e], dst.at[slot], sems.at[slot])
                for src, dst in ((w1_hbm, w1f), (w3_hbm, w3f), (w2_hbm, w2f))]

    for half in range(2):
        blk = 2 * j + half
        used = blk < nu_ref[0]
        new_expert = jnp.logical_or(blk == 0, be_ref[blk] != be_ref[jnp.maximum(blk - 1, 0)])
        tok0 = half * MOE_ROWS

        @pl.when(jnp.logical_and(used, new_expert))
        def _():
            slot = par_ref[blk]

            @pl.when(blk == 0)
            def _():
                for cp in fetch(be_ref[0], 0):
                    cp.start()

            for cp in fetch(be_ref[blk], slot):
                cp.wait()

            @pl.when(nxt_ref[blk] >= 0)
            def _():
                for cp in fetch(nxt_ref[blk], 1 - slot):
                    cp.start()

            w1c[...] = w1f[slot].astype(BF16)
            w3c[...] = w3f[slot].astype(BF16)
            w2c[...] = w2f[slot].astype(BF16)

        @pl.when(used)
        def _():
            x = _load_token_tiles(xb8_ref, tok0, MOE_ROWS)
            hmid = (_silu(_dot(x, w1c[...])) * _dot(x, w3c[...])).astype(BF16)
            _store_token_tiles(yb8_ref, tok0, _dot(hmid, w2c[...]))

        @pl.when(jnp.logical_not(used))
        def _():
            rows = pl.ds(tok0 * TOKEN_ROWS, MOE_ROWS * TOKEN_ROWS)
            yb8_ref[rows, :] = jnp.zeros((MOE_ROWS * TOKEN_ROWS, LANES), jnp.uint32)


def _experts(block_expert, n_used, next_expert, slot_parity, xb8, w1, w3, w2):
    nblk = block_expert.shape[0]
    nsteps = nblk // 2
    d = w1.shape[1]
    de = w1.shape[2]
    step_rows = 2 * MOE_ROWS * TOKEN_ROWS
    rows_in = lambda j, be, nu, nx, pa: (jnp.minimum(j, (nu[0] - 1) // 2), 0)
    hbm = pl.BlockSpec(memory_space=pl.ANY)
    return pl.pallas_call(
        _experts_kernel,
        grid_spec=pltpu.PrefetchScalarGridSpec(
            num_scalar_prefetch=4,
            grid=(nsteps,),
            in_specs=[pl.BlockSpec((step_rows, LANES), rows_in), hbm, hbm, hbm],
            out_specs=pl.BlockSpec((step_rows, LANES), lambda j, be, nu, nx, pa: (j, 0)),
            scratch_shapes=[pltpu.VMEM((2, d, de), F32), pltpu.VMEM((2, d, de), F32),
                            pltpu.VMEM((2, de, d), F32), pltpu.SemaphoreType.DMA((2,)),
                            pltpu.VMEM((d, de), BF16), pltpu.VMEM((d, de), BF16),
                            pltpu.VMEM((de, d), BF16)]),
        out_shape=jax.ShapeDtypeStruct((nblk * MOE_ROWS * TOKEN_ROWS, LANES), jnp.uint32),
        compiler_params=_cparams(("arbitrary",)),
        name="experts",
    )(block_expert, n_used, next_expert, slot_parity, xb8, w1, w3, w2)


def _ple_kernel(dest_ref, destn_ref, x1_ref, route_ref, p_ref, ln_ref, wg_ref, wp_ref, yb8_ref,
                out_ref, ya0, ya1, yb0, yb1, sems):
    j = pl.program_id(0)
    last = pl.num_programs(0) - 1
    tm = ROW_TILE
    d = x1_ref.shape[1]

    def start(idx_ref, base, bufs, sem):
        for r in range(tm):
            for slot in range(2):
                pltpu.make_async_copy(_token_tile(yb8_ref, idx_ref[base + 2 * r + slot]),
                                      bufs[slot].at[pl.ds(r * TOKEN_ROWS, TOKEN_ROWS)],
                                      sem).start(priority=slot)

    def wait(bufs, sem):
        for buf in bufs:
            pltpu.make_async_copy(yb8_ref.at[pl.ds(0, tm * TOKEN_ROWS)], buf, sem).wait()

    def compute(bufs, rows):
        x2 = (x1_ref[rows, :]
              + route_ref[rows, 2:3] * _load_token_tiles(bufs[0], 0, tm).astype(F32)
              + route_ref[rows, 3:4] * _load_token_tiles(bufs[1], 0, tm).astype(F32))
        h = (x2 * _rms_scale(x2, d) * ln_ref[...]).astype(BF16)
        pp = _dot(p_ref[rows, :].astype(BF16), wp_ref[...])
        out_ref[rows, :] = x2 + pp * _sigmoid(_dot(h, wg_ref[...]))

    @pl.when(j == 0)
    def _():
        start(dest_ref, 0, (ya0, ya1), sems.at[0])

    wait((ya0, ya1), sems.at[0])
    start(dest_ref, 2 * tm, (yb0, yb1), sems.at[1])
    compute((ya0, ya1), slice(0, tm))

    wait((yb0, yb1), sems.at[1])
    start(destn_ref, 0, (ya0, ya1), sems.at[0])
    compute((yb0, yb1), slice(tm, 2 * tm))

    @pl.when(j == last)
    def _():
        wait((ya0, ya1), sems.at[0])


def _ple(dest_flat, x1, route, p2, ln, wg, wp, yb):
    t, d = x1.shape
    tm = ROW_TILE
    nsteps = t // (2 * tm)
    row = lambda w: pl.BlockSpec((2 * tm, w), lambda j: (j, 0))
    return pl.pallas_call(
        _ple_kernel,
        grid=(nsteps,),
        in_specs=[pl.BlockSpec((4 * tm,), lambda j: (j,), memory_space=pltpu.SMEM),
                  pl.BlockSpec((4 * tm,), lambda j: (jnp.minimum(j + 1, nsteps - 1),),
                               memory_space=pltpu.SMEM),
                  row(d), row(LANES), row(p2.shape[1]), _const_spec(ln.shape),
                  _const_spec(wg.shape), _const_spec(wp.shape),
                  pl.BlockSpec(memory_space=pl.ANY)],
        out_specs=row(d),
        out_shape=jax.ShapeDtypeStruct((t, d), F32),
        scratch_shapes=([pltpu.VMEM((tm * TOKEN_ROWS, LANES), jnp.uint32)] * 4
                        + [pltpu.SemaphoreType.DMA((2,))]),
        compiler_params=_cparams(("arbitrary",)),
        name="ple",
    )(dest_flat, dest_flat, x1, route, p2, ln, wg, wp, yb)


def _head_pad(w, width):
    r = w.shape[0]
    w = w.reshape(r, MLA_HEADS, width)
    return jnp.pad(w, ((0, 0), (0, 0), (0, LANES - width))).reshape(r, MLA_HEADS * LANES)


def _rope_table():
    half = MLA_ROPE // 2
    inv_freq = ROPE_THETA ** (-np.arange(half, dtype=np.float32) / half)
    tab = np.zeros((8, LANES), np.float32)
    tab[0, MLA_NOPE:MLA_NOPE + half] = inv_freq
    tab[0, MLA_NOPE + half:MLA_QK] = inv_freq
    tab[1, MLA_NOPE + half:MLA_QK] = 1.0
    tab[2, MLA_NOPE:MLA_NOPE + half] = -1.0
    freq_t = np.broadcast_to(inv_freq[:, None], (half, LANES))
    return jnp.asarray(tab), jnp.asarray(freq_t)


def _route_tables(ids):
    a = ids.shape[0] * 2
    e = ids.reshape(a)
    onehot = (e[:, None] == jnp.arange(N_EXPERTS, dtype=jnp.int32)[None, :]).astype(jnp.int32)
    csum = jnp.cumsum(onehot, axis=0)
    counts = csum[-1]
    rank = jnp.sum(csum * onehot, axis=1) - 1
    pcounts = (counts + MOE_ROWS - 1) // MOE_ROWS * MOE_ROWS
    pends = jnp.cumsum(pcounts)
    pstarts = pends - pcounts
    dest = (jnp.sum(onehot * pstarts[None, :], axis=1) + rank).astype(jnp.int32)
    nblk = (a + N_EXPERTS * (MOE_ROWS - 1)) // MOE_ROWS
    nblk += nblk % 2
    n_used = (pends[-1] // MOE_ROWS).astype(jnp.int32)
    blk_start = jnp.arange(nblk, dtype=jnp.int32) * MOE_ROWS
    blk_start = jnp.minimum(blk_start, pends[-1] - 1)
    block_expert = jnp.sum((pends[None, :] <= blk_start[:, None]).astype(jnp.int32), axis=1)
    block_expert = jnp.minimum(block_expert, N_EXPERTS - 1).astype(jnp.int32)
    block_expert, n_used = lax.optimization_barrier((block_expert, n_used))
    blk = jnp.arange(nblk, dtype=jnp.int32)
    prev = jnp.concatenate([jnp.full((1,), -1, jnp.int32), block_expert[:-1]])
    starts_expert = (blk < n_used) & (block_expert != prev)
    slot_parity = (jnp.cumsum(starts_expert.astype(jnp.int32)) - 1) % 2
    start_at_or_after = jnp.flip(lax.cummin(jnp.flip(jnp.where(starts_expert, blk, nblk))))
    next_start = jnp.concatenate([start_at_or_after[1:], jnp.full((1,), nblk, jnp.int32)])
    next_expert = jnp.where(next_start < nblk, block_expert[jnp.minimum(next_start, nblk - 1)], -1)
    return (dest, block_expert, n_used.reshape(1), next_expert.astype(jnp.int32),
            slot_parity.astype(jnp.int32), nblk)


def kernel(x, p, positions, ln_mix, w_in, hg_lb, hg_onorm, w_oA, mla_qa_norm, mla_kva_norm, w_uq,
           w_ukv, q_norm, k_norm, w_oB, w_out, ln_moe, w_rg, b_rg, w_re, b_re, w1, w3, w2, ln_ple,
           w_ple_gate, w_ple_proj):
    b, s, d = x.shape
    t = b * s
    depth = w_in.shape[0]
    lb_all = jnp.cumsum(jax.nn.softmax(hg_lb.astype(F32), axis=1), axis=1)
    posf = positions.astype(F32).reshape(t, 1)
    rope_tab, freq_t = _rope_table()
    xc = x.reshape(t, d)

    for layer in range(depth):
        wi = w_in[layer]
        n_hg = 3 * HG_F + 2 * HG_V
        n_mla = MLA_Q_RANK + MLA_KV_RANK + MLA_ROPE
        mla_w = -(-n_mla // LANES) * LANES
        whg = wi[:, :n_hg].astype(BF16)
        wmla = jnp.pad(wi[:, n_hg:n_hg + n_mla], ((0, 0), (0, mla_w - n_mla))).astype(BF16)
        wgate = wi[:, n_hg + n_mla:].astype(BF16)
        lb = lb_all[:, layer, :]

        qs, lf, kk, v, og, mla_in, gates = _inproj(
            xc, ln_mix[layer][None, :], whg, wmla, wgate, lb, mla_qa_norm[layer][None, :],
            mla_kva_norm[layer][None, :])

        r3 = lambda a: a.reshape(b, s, a.shape[-1])
        ya = _hgrn(r3(qs), r3(lf), r3(kk), r3(v), r3(og), hg_onorm[layer][None, :]).reshape(t, HG_V)

        kv_w = MLA_NOPE + MLA_VDIM
        wkv = w_ukv[layer].reshape(MLA_KV_RANK, MLA_HEADS, kv_w)
        ckr_w = mla_w - MLA_Q_RANK
        wq = _head_pad(w_uq[layer], MLA_QK).astype(BF16)
        wk_nope = _head_pad(wkv[:, :, :MLA_NOPE].reshape(MLA_KV_RANK, MLA_HEADS * MLA_NOPE), MLA_NOPE)
        place = np.zeros((ckr_w - MLA_KV_RANK, MLA_HEADS, LANES), np.float32)
        for j in range(MLA_ROPE):
            place[j, :, MLA_NOPE + j] = 1.0
        wk = jnp.concatenate([wk_nope, jnp.asarray(place.reshape(ckr_w - MLA_KV_RANK, -1))],
                             axis=0).astype(BF16)
        wv = jnp.pad(wkv[:, :, MLA_NOPE:].reshape(MLA_KV_RANK, MLA_HEADS * MLA_VDIM),
                     ((0, ckr_w - MLA_KV_RANK), (0, 0))).astype(BF16)
        gq = jnp.pad(q_norm[layer], (0, LANES - MLA_QK))[None, :]
        gk_t = jnp.broadcast_to(jnp.pad(k_norm[layer], (0, LANES - MLA_QK))[:, None], (LANES, LANES))
        qh, kh_t, vh = _mlaprep(mla_in, posf, wq, wk.T, wv, gq, gk_t, rope_tab, freq_t)
        yb = _attn(r3(qh), kh_t, r3(vh)).reshape(t, MLA_HEADS * MLA_VDIM)

        wr = jnp.pad(jnp.concatenate([w_rg[layer], w_re[layer]], axis=1),
                     ((0, 0), (0, LANES - N_GROUPS - N_EXPERTS)))
        wr_hi = wr.astype(BF16)
        wr = jnp.concatenate([wr_hi, (wr - wr_hi.astype(F32)).astype(BF16)], axis=1)
        br = jnp.pad(jnp.concatenate([b_rg[layer], b_re[layer]]),
                     (0, LANES - N_GROUPS - N_EXPERTS))[None, :]
        x1, hm8, route = _merge(xc, ya, yb, gates, w_oA[layer].astype(BF16), w_oB[layer].astype(BF16),
                               w_out[layer].astype(BF16), ln_moe[layer][None, :], wr, br)

        ids = route[:, 0:2].astype(jnp.int32)
        dest, block_expert, n_used, next_expert, slot_parity, nblk = _route_tables(ids)
        xb8 = _dispatch(dest, hm8, jnp.zeros((nblk * MOE_ROWS * TOKEN_ROWS, LANES), jnp.uint32))
        yexp = _experts(block_expert, n_used, next_expert, slot_parity, xb8,
                        w1[layer], w3[layer], w2[layer])

        xc = _ple(dest, x1, route, p[layer].reshape(t, -1), ln_ple[layer][None, :],
                  w_ple_gate[layer].astype(BF16), w_ple_proj[layer].astype(BF16), yexp)
    return xc.reshape(b, s, d)
```

```python
import functools

import numpy as np
import jax
import jax.numpy as jnp
from jax import lax
from jax.experimental import pallas as pl
from jax.experimental.pallas import tpu as pltpu

F32 = jnp.float32
BF16 = jnp.bfloat16

HG_HEADS = 4
HG_KDIM = 128
HG_VDIM = 128
HG_F = HG_HEADS * HG_KDIM
HG_V = HG_HEADS * HG_VDIM
MLA_HEADS = 8
MLA_NOPE = 64
MLA_ROPE = 32
MLA_VDIM = 64
MLA_QK = MLA_NOPE + MLA_ROPE
MLA_Q_RANK = 256
MLA_KV_RANK = 128
ROPE_THETA = 10000.0
N_GROUPS = 8
EXPERTS_PER_GROUP = 8
N_EXPERTS = N_GROUPS * EXPERTS_PER_GROUP
D_EXPERT = 256
EPS = 1e-6

LANES = 128
TOKEN_ROWS = 4
VMEM_LIMIT_BYTES = 56 * 1024 * 1024

PLE_ROW_TILE = 256
MERGE_ROW_TILE = 512
INPROJ_ROW_TILE = 256
MLAPREP_ROW_TILE = 512
HG_CHUNK = 64
HG_HEADS_PER_STEP = 2
HG_FAST_CHUNK = 128
HG_CHUNKS_PER_MATMUL = 2
ATTN_Q_TILE = 512
ATTN_HEADS_PER_STEP = 8
MOE_ROWS = 256
DISPATCH_TOKENS = 1024
HG_SAFE_LOG_DECAY = -75.0


def _cparams(sem):
    return pltpu.CompilerParams(dimension_semantics=sem, vmem_limit_bytes=VMEM_LIMIT_BYTES)


def _const_spec(shape):
    nd = len(shape)
    return pl.BlockSpec(shape, lambda *_: (0,) * nd)


def _sigmoid(x):
    return 1.0 / (1.0 + jnp.exp(-x))


def _silu(x):
    return x * _sigmoid(x)


def _rms_scale(x, n):
    return lax.rsqrt(jnp.sum(x * x, axis=-1, keepdims=True) * (1.0 / n) + EPS)


def _dot(a, b):
    return jnp.dot(a, b, preferred_element_type=F32)


def _dot_nt(a, b):
    return lax.dot_general(a, b, (((1,), (1,)), ((), ())), preferred_element_type=F32)


def _dot_tn(a, b):
    return lax.dot_general(a, b, (((0,), (0,)), ((), ())), preferred_element_type=F32)


def _chunk_cumsum(a, direction, chunk):
    n = a.shape[0]
    in_chunk = lax.broadcasted_iota(jnp.int32, a.shape, 0) % chunk
    k = 1
    while k < chunk:
        if direction == 0:
            a = a + jnp.where(in_chunk >= k, pltpu.roll(a, k, 0), 0.0)
        else:
            a = a + jnp.where(in_chunk < chunk - k, pltpu.roll(a, n - k, 0), 0.0)
        k *= 2
    return a


def _inproj_kernel(x_ref, g_ref, whg_ref, wmla_ref, wgate_ref, lb_ref, nq_ref, nkv_ref,
                   qs_ref, lf_ref, kk_ref, v_ref, og_ref, mla_ref, gates_ref):
    x = x_ref[...]
    d = x.shape[-1]
    h = (x * _rms_scale(x, d) * g_ref[...]).astype(BF16)

    qs_ref[...] = _silu(_dot(h, whg_ref[:, 0:HG_F])).astype(BF16)
    for direction in range(2):
        cols = slice(HG_F * (1 + direction), HG_F * (2 + direction))
        out_cols = slice(HG_F * direction, HG_F * (direction + 1))
        lb = lb_ref[direction:direction + 1, :]
        f = lb + (1.0 - lb) * _sigmoid(_dot(h, whg_ref[:, cols]))
        lf_ref[:, out_cols] = jnp.log(f)
        kk_ref[:, out_cols] = (1.0 - f).astype(BF16)
    v_ref[...] = _dot(h, whg_ref[:, 3 * HG_F:3 * HG_F + HG_V]).astype(BF16)
    og_ref[...] = _silu(_dot(h, whg_ref[:, 3 * HG_F + HG_V:3 * HG_F + 2 * HG_V])).astype(BF16)

    zm = _dot(h, wmla_ref[...])
    cq = zm[:, 0:MLA_Q_RANK]
    mla_ref[:, 0:MLA_Q_RANK] = (cq * _rms_scale(cq, MLA_Q_RANK) * nq_ref[...]).astype(BF16)
    ckv = zm[:, MLA_Q_RANK:MLA_Q_RANK + MLA_KV_RANK]
    mla_ref[:, MLA_Q_RANK:MLA_Q_RANK + MLA_KV_RANK] = (
        ckv * _rms_scale(ckv, MLA_KV_RANK) * nkv_ref[...]).astype(BF16)
    mla_ref[:, MLA_Q_RANK + MLA_KV_RANK:] = zm[:, MLA_Q_RANK + MLA_KV_RANK:].astype(BF16)

    for half in range(2):
        cols = slice(d * half, d * (half + 1))
        gates_ref[:, cols] = _sigmoid(_dot(h, wgate_ref[:, cols])).astype(BF16)


def _inproj(x2, ln_mix, whg, wmla, wgate, lb, nq, nkv):
    t, d = x2.shape
    tm = INPROJ_ROW_TILE
    mla_w = wmla.shape[1]
    row = lambda w: pl.BlockSpec((tm, w), lambda i: (i, 0))
    out_shape = (
        jax.ShapeDtypeStruct((t, HG_F), BF16),
        jax.ShapeDtypeStruct((t, 2 * HG_F), F32),
        jax.ShapeDtypeStruct((t, 2 * HG_F), BF16),
        jax.ShapeDtypeStruct((t, HG_V), BF16),
        jax.ShapeDtypeStruct((t, HG_V), BF16),
        jax.ShapeDtypeStruct((t, mla_w), BF16),
        jax.ShapeDtypeStruct((t, 2 * d), BF16),
    )
    return pl.pallas_call(
        _inproj_kernel,
        grid=(t // tm,),
        in_specs=[row(d), _const_spec((1, d)), _const_spec(whg.shape), _const_spec(wmla.shape),
                  _const_spec(wgate.shape), _const_spec(lb.shape), _const_spec(nq.shape),
                  _const_spec(nkv.shape)],
        out_specs=(row(HG_F), row(2 * HG_F), row(2 * HG_F), row(HG_V), row(HG_V), row(mla_w),
                   row(2 * d)),
        out_shape=out_shape,
        compiler_params=_cparams(("parallel",)),
        name="inproj",
    )(x2, ln_mix, whg, wmla, wgate, lb, nq, nkv)


def _hgrn_kernel(qs_ref, lff_ref, lfb_ref, kf_ref, kb_ref, v_ref, og_ref, onorm_ref, out_ref,
                 state_ref, ofw_ref, obw_ref, qt_scr, u_scr, vec_scr, g_scr, k_scr, v_scr):
    c = HG_CHUNK
    s_len = qs_ref.shape[1]
    n_chunks = s_len // c
    hb = HG_HEADS_PER_STEP
    kd = HG_KDIM

    row = lax.broadcasted_iota(jnp.int32, (c, c), 0)
    col = lax.broadcasted_iota(jnp.int32, (c, c), 1)
    masks = (row >= col, row <= col)
    rowc = lax.broadcasted_iota(jnp.int32, (c, kd), 0)
    o_refs = (ofw_ref, obw_ref)

    def load(j):
        chains = []
        for hh in range(hb):
            lanes = slice(hh * kd, (hh + 1) * kd)
            for direction in range(2):
                chunk = j if direction == 0 else n_chunks - 1 - j
                rows = pl.ds(pl.multiple_of(chunk * c, c), c)
                g = _chunk_cumsum((lff_ref, lfb_ref)[direction][0, rows, lanes], direction, c)
                total = g[c - 1:c, :] if direction == 0 else g[0:1, :]
                chains.append(dict(
                    idx=hh * 2 + direction, direction=direction, rows=rows, lanes=lanes, g=g,
                    total=total,
                    q=qs_ref[0, rows, lanes].astype(F32),
                    k=(kf_ref, kb_ref)[direction][0, rows, lanes].astype(F32),
                    v=v_ref[0, rows, lanes]))
        return chains

    def finish(ch, o, kdec_t_v_scaled):
        (ofw_ref, obw_ref)[ch["direction"]][ch["rows"], ch["lanes"]] = o
        state_ref[ch["idx"]] = kdec_t_v_scaled

    def fast(ch):
        g = ch["g"]
        qt = (ch["q"] * jnp.exp(g)).astype(BF16)
        kt = (ch["k"] * jnp.exp(-g)).astype(BF16)
        st = state_ref[ch["idx"]]
        sc = jnp.where(masks[ch["direction"]], _dot_nt(qt, kt), 0.0)
        o = _dot_nt(qt, st.astype(BF16)) + _dot(sc.astype(BF16), ch["v"])
        finish(ch, o, (st + _dot_tn(ch["v"], kt)) * jnp.exp(ch["total"]))

    def robust(ch):
        g = ch["g"]
        direction = ch["direction"]
        st = state_ref[ch["idx"]]
        o0 = _dot_nt((ch["q"] * jnp.exp(g)).astype(BF16), st.astype(BF16))
        slot = ch["idx"]
        g_scr[slot] = g
        k_scr[slot] = ch["k"]
        v_scr[slot] = ch["v"].astype(F32)
        q = ch["q"]

        def body(s, acc):
            g_s = g_scr[slot, pl.ds(s, 1), :]
            seen = (rowc >= s) if direction == 0 else (rowc <= s)
            decay = jnp.where(seen, jnp.exp(jnp.minimum(g - g_s, 0.0)), 0.0)
            a = jnp.sum(q * decay * k_scr[slot, pl.ds(s, 1), :], axis=-1, keepdims=True)
            return acc + a * v_scr[slot, pl.ds(s, 1), :]

        o = lax.fori_loop(0, c, body, o0)
        kdec = (ch["k"] * jnp.exp(ch["total"] - g)).astype(BF16)
        finish(ch, o, st * jnp.exp(ch["total"]) + _dot_tn(ch["v"], kdec))

    def step(j, carry):
        chains = load(j)
        lowest = chains[0]["total"]
        for ch in chains[1:]:
            lowest = jnp.minimum(lowest, ch["total"])
        safe = jnp.min(lowest) >= HG_SAFE_LOG_DECAY

        @pl.when(safe)
        def _():
            for ch in chains:
                fast(ch)

        @pl.when(jnp.logical_not(safe))
        def _():
            for ch in chains:
                robust(ch)

        return carry

    fc = HG_FAST_CHUNK if s_len % (HG_FAST_CHUNK * HG_CHUNKS_PER_MATMUL) == 0 else c
    grp = HG_CHUNKS_PER_MATMUL if (s_len // fc) % HG_CHUNKS_PER_MATMUL == 0 else 1
    gr = grp * fc
    n_groups = s_len // gr
    mid = fc // 2
    grow = lax.broadcasted_iota(jnp.int32, (gr, gr), 0)
    gcol = lax.broadcasted_iota(jnp.int32, (gr, gr), 1)
    same_chunk = (grow // fc) == (gcol // fc)
    gmasks = (same_chunk & (grow >= gcol), same_chunk & (grow <= gcol))
    chunk_of_row = lax.broadcasted_iota(jnp.int32, (gr, kd), 0) // fc

    def block_diag(a):
        return jnp.concatenate([jnp.where(chunk_of_row == i, a, jnp.zeros_like(a))
                                for i in range(grp)], axis=1)

    def phase_a(gi, lowest):
        rows = pl.ds(pl.multiple_of(gi * gr, gr), gr)
        for hh in range(hb):
            lanes = slice(hh * kd, (hh + 1) * kd)
            v = v_ref[0, rows, lanes]
            v_t = v.T
            q = qs_ref[0, rows, lanes].astype(F32)
            for direction in range(2):
                idx = hh * 2 + direction
                g = _chunk_cumsum((lff_ref, lfb_ref)[direction][0, rows, lanes], direction, fc)
                k = (kf_ref, kb_ref)[direction][0, rows, lanes].astype(F32)
                g_mid = [g[i * fc + mid:i * fc + mid + 1, :] for i in range(grp)]
                centred = g - jnp.concatenate([jnp.broadcast_to(r, (fc, kd)) for r in g_mid], axis=0)
                qt = (q * jnp.exp(centred)).astype(BF16)
                kt = (k * jnp.exp(-centred)).astype(BF16)
                qt_scr[idx, rows, :] = qt
                sc = jnp.where(gmasks[direction], _dot_nt(qt, kt), 0.0)
                o_refs[direction][rows, lanes] = _dot(sc.astype(BF16), v)
                u = _dot(v_t, block_diag(kt))
                for i in range(grp):
                    ci = gi * grp + i
                    u_scr[idx, ci] = u[:, i * kd:(i + 1) * kd]
                    edge = i * fc + (fc - 1 if direction == 0 else 0)
                    total = g[edge:edge + 1, :]
                    lowest = jnp.minimum(lowest, jnp.minimum(g_mid[i], total - g_mid[i]))
                    vec_scr[idx, 0, pl.ds(ci, 1), :] = jnp.exp(total)
                    vec_scr[idx, 1, pl.ds(ci, 1), :] = jnp.exp(g_mid[i])
                    vec_scr[idx, 2, pl.ds(ci, 1), :] = jnp.exp(total - g_mid[i])
        return lowest

    lowest = lax.fori_loop(0, n_groups, phase_a, jnp.zeros((1, kd), F32))
    all_safe = jnp.min(lowest) >= HG_SAFE_LOG_DECAY
    state_ref[...] = jnp.zeros_like(state_ref)

    @pl.when(all_safe)
    def _():
        def phase_b(j, carry):
            for hh in range(hb):
                lanes = slice(hh * kd, (hh + 1) * kd)
                for direction in range(2):
                    idx = hh * 2 + direction
                    gi = j if direction == 0 else n_groups - 1 - j
                    rows = pl.ds(pl.multiple_of(gi * gr, gr), gr)
                    st = state_ref[idx]
                    seen = [None] * grp
                    for i in (range(grp) if direction == 0 else reversed(range(grp))):
                        ci = gi * grp + i
                        at = pl.ds(ci, 1)
                        seen[i] = (st * vec_scr[idx, 1, at, :]).astype(BF16)
                        st = st * vec_scr[idx, 0, at, :] + u_scr[idx, ci] * vec_scr[idx, 2, at, :]
                    state_ref[idx] = st
                    o_refs[direction][rows, lanes] += _dot_nt(
                        block_diag(qt_scr[idx, rows, :]), jnp.concatenate(seen, axis=1))
            return carry

        lax.fori_loop(0, n_groups, phase_b, 0)

    @pl.when(jnp.logical_not(all_safe))
    def _():
        lax.fori_loop(0, n_chunks, step, 0)

    blk = 256 if s_len % 256 == 0 else c

    def epilogue(i, carry):
        rows = pl.ds(pl.multiple_of(i * blk, blk), blk)
        for hh in range(hb):
            lanes = slice(hh * kd, (hh + 1) * kd)
            o = ofw_ref[rows, lanes] + obw_ref[rows, lanes]
            y = o * _rms_scale(o, HG_VDIM) * onorm_ref[...]
            out_ref[0, rows, lanes] = (y * og_ref[0, rows, lanes].astype(F32)).astype(BF16)
        return carry

    lax.fori_loop(0, s_len // blk, epilogue, 0)


def _hgrn(qs, lf, kk, v, og, onorm):
    b, s, _ = qs.shape
    hb = HG_HEADS_PER_STEP
    w = hb * HG_KDIM
    n_hsteps = HG_HEADS // hb
    fwd = pl.BlockSpec((1, s, w), lambda bi, hi: (bi, 0, hi))
    bwd = pl.BlockSpec((1, s, w), lambda bi, hi: (bi, 0, n_hsteps + hi))
    return pl.pallas_call(
        _hgrn_kernel,
        grid=(b, n_hsteps),
        in_specs=[fwd, fwd, bwd, fwd, bwd, fwd, fwd, _const_spec(onorm.shape)],
        out_specs=fwd,
        out_shape=jax.ShapeDtypeStruct((b, s, HG_V), BF16),
        scratch_shapes=[
            pltpu.VMEM((2 * hb, HG_VDIM, HG_KDIM), F32),
            pltpu.VMEM((s, w), F32),
            pltpu.VMEM((s, w), F32),
            pltpu.VMEM((2 * hb, s, HG_KDIM), BF16),
            pltpu.VMEM((2 * hb, s // HG_CHUNK, HG_VDIM, HG_KDIM), F32),
            pltpu.VMEM((2 * hb, 3, max(s // HG_CHUNK, 8), HG_KDIM), F32),
            pltpu.VMEM((2 * hb, HG_CHUNK, HG_KDIM), F32),
            pltpu.VMEM((2 * hb, HG_CHUNK, HG_KDIM), F32),
            pltpu.VMEM((2 * hb, HG_CHUNK, HG_VDIM), F32),
        ],
        compiler_params=_cparams(("parallel", "parallel")),
        name="hgrn",
    )(qs, lf, lf, kk, kk, v, og, onorm)


def _mlaprep_kernel(mla_ref, pos_ref, pos_t_ref, wq_ref, wk_t_ref, wv_ref, gq_ref, gk_t_ref,
                    rope_ref, freq_t_ref, q_ref, k_t_ref, v_ref):
    cq = mla_ref[:, 0:MLA_Q_RANK]
    ckr = mla_ref[:, MLA_Q_RANK:]
    half = MLA_ROPE // 2

    ang = pos_ref[...] * rope_ref[0:1, :]
    cos = jnp.cos(ang)
    sin = jnp.sin(ang)
    sin_lo = sin * rope_ref[1:2, :]
    sin_hi = sin * rope_ref[2:3, :]
    qm = _dot(cq, wq_ref[...])
    for hd in range(MLA_HEADS):
        lanes = slice(hd * LANES, (hd + 1) * LANES)
        x = qm[:, lanes]
        y = x * (_rms_scale(x, MLA_QK) * MLA_QK ** -0.5) * gq_ref[...]
        y = y * cos + pltpu.roll(y, half, 1) * sin_lo + pltpu.roll(y, LANES - half, 1) * sin_hi
        q_ref[:, lanes] = y.astype(BF16)

    km_t = _dot_nt(wk_t_ref[...], ckr)
    reps = km_t.shape[1] // LANES
    lane_tile = lambda a: jnp.concatenate([a] * reps, axis=1)
    ang_t = lane_tile(freq_t_ref[...]) * pos_t_ref[...]
    cos_t = jnp.cos(ang_t)
    sin_t = jnp.sin(ang_t)
    gain_t = lane_tile(gk_t_ref[...])
    for hd in range(MLA_HEADS):
        r0 = hd * LANES
        x = km_t[r0:r0 + LANES, :]
        y = x * lax.rsqrt(jnp.sum(x * x, axis=0, keepdims=True) * (1.0 / MLA_QK) + EPS) * gain_t
        x1 = y[MLA_NOPE:MLA_NOPE + half, :]
        x2 = y[MLA_NOPE + half:MLA_QK, :]
        k_t_ref[r0:r0 + MLA_NOPE, :] = y[0:MLA_NOPE, :].astype(BF16)
        k_t_ref[r0 + MLA_NOPE:r0 + MLA_NOPE + half, :] = (x1 * cos_t - x2 * sin_t).astype(BF16)
        k_t_ref[r0 + MLA_NOPE + half:r0 + MLA_QK, :] = (x2 * cos_t + x1 * sin_t).astype(BF16)
        k_t_ref[r0 + MLA_QK:r0 + LANES, :] = jnp.zeros((LANES - MLA_QK, km_t.shape[1]), BF16)

    v_ref[...] = _dot(ckr, wv_ref[...]).astype(BF16)


def _mlaprep(mla_in, posf, wq, wk_t, wv, gq, gk_t, rope_tab, freq_t):
    t = mla_in.shape[0]
    tm = MLAPREP_ROW_TILE
    row = lambda w: pl.BlockSpec((tm, w), lambda i: (i, 0))
    hw = MLA_HEADS * LANES
    vw = MLA_HEADS * MLA_VDIM
    return pl.pallas_call(
        _mlaprep_kernel,
        grid=(t // tm,),
        in_specs=[row(mla_in.shape[1]), row(1), pl.BlockSpec((1, tm), lambda i: (0, i)),
                  _const_spec(wq.shape), _const_spec(wk_t.shape), _const_spec(wv.shape),
                  _const_spec(gq.shape), _const_spec(gk_t.shape), _const_spec(rope_tab.shape),
                  _const_spec(freq_t.shape)],
        out_specs=(row(hw), pl.BlockSpec((hw, tm), lambda i: (0, i)), row(vw)),
        out_shape=(jax.ShapeDtypeStruct((t, hw), BF16), jax.ShapeDtypeStruct((hw, t), BF16),
                   jax.ShapeDtypeStruct((t, vw), BF16)),
        compiler_params=_cparams(("parallel",)),
        name="mlaprep",
    )(mla_in, posf, posf.reshape(1, t), wq, wk_t, wv, gq, gk_t, rope_tab, freq_t)


def _attn_kernel(q_ref, k_t_ref, v_ref, o_ref):
    for j in range(ATTN_HEADS_PER_STEP):
        lanes = slice(j * LANES, (j + 1) * LANES)
        s = _dot(q_ref[0, :, lanes], k_t_ref[lanes, :])
        p = jnp.exp(s - jnp.max(s, axis=-1, keepdims=True))
        l = jnp.sum(p, axis=-1, keepdims=True)
        vl = slice(j * MLA_VDIM, (j + 1) * MLA_VDIM)
        o = _dot(p.astype(BF16), v_ref[0, :, vl])
        o_ref[0, :, vl] = (o / l).astype(BF16)


def _attn(q, k_t, v):
    b, s, _ = q.shape
    tq = min(ATTN_Q_TILE, s)
    nh = ATTN_HEADS_PER_STEP
    return pl.pallas_call(
        _attn_kernel,
        grid=(b, MLA_HEADS // nh, s // tq),
        in_specs=[pl.BlockSpec((1, tq, nh * LANES), lambda bi, hp, i: (bi, i, hp)),
                  pl.BlockSpec((nh * LANES, s), lambda bi, hp, i: (hp, bi)),
                  pl.BlockSpec((1, s, nh * MLA_VDIM), lambda bi, hp, i: (bi, 0, hp))],
        out_specs=pl.BlockSpec((1, tq, nh * MLA_VDIM), lambda bi, hp, i: (bi, i, hp)),
        out_shape=jax.ShapeDtypeStruct((b, s, MLA_HEADS * MLA_VDIM), BF16),
        compiler_params=_cparams(("parallel", "parallel", "arbitrary")),
        name="attn",
    )(q, k_t, v)


def _merge_kernel(x_ref, ya_ref, yb_ref, gates_ref, woa_ref, wob_ref, wout_ref, lnm_ref, wr_ref,
                  br_ref, x1_ref, hm8_ref, route_ref):
    d = x_ref.shape[-1]
    y_a = _dot(ya_ref[...], woa_ref[...])
    y_b = _dot(yb_ref[...], wob_ref[...])
    merged = gates_ref[:, 0:d].astype(F32) * y_a + gates_ref[:, d:2 * d].astype(F32) * y_b
    x1 = x_ref[...] + _dot(merged.astype(BF16), wout_ref[...])
    x1_ref[...] = x1
    hm = x1 * _rms_scale(x1, d) * lnm_ref[...]
    _store_token_tiles(hm8_ref, 0, hm)

    hm_hi = hm.astype(BF16)
    hm_lo = (hm - hm_hi.astype(F32)).astype(BF16)
    hh = _dot(hm_hi, wr_ref[...])
    logits = (hh[:, 0:LANES] + hh[:, LANES:2 * LANES] + _dot(hm_lo, wr_ref[:, 0:LANES])
              + br_ref[...])
    lane = lax.broadcasted_iota(jnp.int32, logits.shape, 1)
    neg = -jnp.inf
    big = jnp.int32(2 ** 30)
    is_group = lane < N_GROUPS
    gl = jnp.where(is_group, logits, neg)
    gmax = jnp.max(gl, axis=-1, keepdims=True)
    p_group = 1.0 / jnp.sum(jnp.where(is_group, jnp.exp(gl - gmax), 0.0), axis=-1, keepdims=True)
    g_sel = jnp.min(jnp.where(gl == gmax, lane, big), axis=-1, keepdims=True)
    lo = N_GROUPS + g_sel * EXPERTS_PER_GROUP
    in_group = (lane >= lo) & (lane < lo + EXPERTS_PER_GROUP)
    el = jnp.where(in_group, logits, neg)
    v1 = jnp.max(el, axis=-1, keepdims=True)
    i1 = jnp.min(jnp.where(el == v1, lane, big), axis=-1, keepdims=True)
    el2 = jnp.where(lane == i1, neg, el)
    v2 = jnp.max(el2, axis=-1, keepdims=True)
    i2 = jnp.min(jnp.where(el2 == v2, lane, big), axis=-1, keepdims=True)
    e21 = jnp.exp(v2 - v1)
    w1 = p_group / (1.0 + e21)
    w2 = w1 * e21
    route = jnp.where(lane == 0, (i1 - N_GROUPS).astype(F32),
                      jnp.where(lane == 1, (i2 - N_GROUPS).astype(F32),
                                jnp.where(lane == 2, w1, jnp.where(lane == 3, w2, 0.0))))
    route_ref[...] = route


def _merge(x2, ya, yb, gates, woa, wob, wout, lnm, wr, br):
    t, d = x2.shape
    tm = MERGE_ROW_TILE
    row = lambda w: pl.BlockSpec((tm, w), lambda i: (i, 0))
    return pl.pallas_call(
        _merge_kernel,
        grid=(t // tm,),
        in_specs=[row(d), row(ya.shape[1]), row(yb.shape[1]), row(2 * d), _const_spec(woa.shape),
                  _const_spec(wob.shape), _const_spec(wout.shape), _const_spec(lnm.shape),
                  _const_spec(wr.shape), _const_spec(br.shape)],
        out_specs=(row(d), pl.BlockSpec((tm * TOKEN_ROWS, LANES), lambda i: (i, 0)), row(LANES)),
        out_shape=(jax.ShapeDtypeStruct((t, d), F32),
                   jax.ShapeDtypeStruct((t * TOKEN_ROWS, LANES), jnp.uint32),
                   jax.ShapeDtypeStruct((t, LANES), F32)),
        compiler_params=_cparams(("parallel",)),
        name="merge",
    )(x2, ya, yb, gates, woa, wob, wout, lnm, wr, br)


_HIGH_HALF = 0xFFFF0000


def _store_token_tiles(ref4, tok0, val):
    n, width = val.shape
    as_bits = lambda a: lax.bitcast_convert_type(a.astype(BF16).astype(F32), jnp.uint32)
    words = (as_bits(val[:, :width // 2]) >> 16) | (as_bits(val[:, width // 2:]) & jnp.uint32(_HIGH_HALF))
    for j in range(TOKEN_ROWS):
        ref4[pl.ds(tok0 * TOKEN_ROWS + j, n, stride=TOKEN_ROWS), :] = words[:, j * LANES:(j + 1) * LANES]


def _load_token_tiles(ref4, tok0, n):
    words = jnp.concatenate(
        [ref4[pl.ds(tok0 * TOKEN_ROWS + j, n, stride=TOKEN_ROWS), :] for j in range(TOKEN_ROWS)],
        axis=1)
    low = lax.bitcast_convert_type(words << 16, F32).astype(BF16)
    high = lax.bitcast_convert_type(words & jnp.uint32(_HIGH_HALF), F32).astype(BF16)
    return jnp.concatenate([low, high], axis=1)


def _token_tile(ref8, tok):
    return ref8.at[pl.ds(pl.multiple_of(tok * TOKEN_ROWS, TOKEN_ROWS), TOKEN_ROWS)]


def _dispatch_kernel(dest_ref, hm8_ref, xb_in_ref, xb8_ref, sem):
    del xb_in_ref
    n_tok = DISPATCH_TOKENS
    for r in range(n_tok):
        src = hm8_ref.at[pl.ds(r * TOKEN_ROWS, TOKEN_ROWS)]
        for slot in range(2):
            pltpu.make_async_copy(src, _token_tile(xb8_ref, dest_ref[2 * r + slot]),
                                  sem).start(priority=slot)
    for _ in range(2):
        pltpu.make_async_copy(hm8_ref, xb8_ref.at[pl.ds(0, n_tok * TOKEN_ROWS)], sem).wait()


def _dispatch(dest_flat, hm8, xb_init):
    t = hm8.shape[0] // TOKEN_ROWS
    return pl.pallas_call(
        _dispatch_kernel,
        grid=(t // DISPATCH_TOKENS,),
        in_specs=[pl.BlockSpec((2 * DISPATCH_TOKENS,), lambda i: (i,), memory_space=pltpu.SMEM),
                  pl.BlockSpec((DISPATCH_TOKENS * TOKEN_ROWS, LANES), lambda i: (i, 0)),
                  pl.BlockSpec(memory_space=pl.ANY)],
        out_specs=pl.BlockSpec(memory_space=pl.ANY),
        out_shape=jax.ShapeDtypeStruct(xb_init.shape, xb_init.dtype),
        scratch_shapes=[pltpu.SemaphoreType.DMA(())],
        input_output_aliases={2: 0},
        compiler_params=_cparams(("arbitrary",)),
        name="dispatch",
    )(dest_flat, hm8, xb_init)


def _experts_kernel(be_ref, nu_ref, nxt_ref, par_ref, xb8_ref, w1_hbm, w3_hbm, w2_hbm, yb8_ref,
                    w1f, w3f, w2f, sems, w1c, w3c, w2c):
    j = pl.program_id(0)

    def fetch(e, slot):
        return [pltpu.make_async_copy(src.at[e], dst.at[slot], sems.at[slot])
                for src, dst in ((w1_hbm, w1f), (w3_hbm, w3f), (w2_hbm, w2f))]

    for half in range(2):
        blk = 2 * j + half
        used = blk < nu_ref[0]
        new_expert = jnp.logical_or(blk == 0, be_ref[blk] != be_ref[jnp.maximum(blk - 1, 0)])
        tok0 = half * MOE_ROWS

        @pl.when(jnp.logical_and(used, new_expert))
        def _():
            slot = par_ref[blk]

            @pl.when(blk == 0)
            def _():
                for cp in fetch(be_ref[0], 0):
                    cp.start()

            for cp in fetch(be_ref[blk], slot):
                cp.wait()

            @pl.when(nxt_ref[blk] >= 0)
            def _():
                for cp in fetch(nxt_ref[blk], 1 - slot):
                    cp.start()

            w1c[...] = w1f[slot].astype(BF16)
            w3c[...] = w3f[slot].astype(BF16)
            w2c[...] = w2f[slot].astype(BF16)

        @pl.when(used)
        def _():
            x = _load_token_tiles(xb8_ref, tok0, MOE_ROWS)
            hmid = (_silu(_dot(x, w1c[...])) * _dot(x, w3c[...])).astype(BF16)
            _store_token_tiles(yb8_ref, tok0, _dot(hmid, w2c[...]))

        @pl.when(jnp.logical_not(used))
        def _():
            rows = pl.ds(tok0 * TOKEN_ROWS, MOE_ROWS * TOKEN_ROWS)
            yb8_ref[rows, :] = jnp.zeros((MOE_ROWS * TOKEN_ROWS, LANES), jnp.uint32)


def _experts(block_expert, n_used, next_expert, slot_parity, xb8, w1, w3, w2):
    nblk = block_expert.shape[0]
    nsteps = nblk // 2
    d = w1.shape[1]
    de = w1.shape[2]
    step_rows = 2 * MOE_ROWS * TOKEN_ROWS
    rows_in = lambda j, be, nu, nx, pa: (jnp.minimum(j, (nu[0] - 1) // 2), 0)
    hbm = pl.BlockSpec(memory_space=pl.ANY)
    return pl.pallas_call(
        _experts_kernel,
        grid_spec=pltpu.PrefetchScalarGridSpec(
            num_scalar_prefetch=4,
            grid=(nsteps,),
            in_specs=[pl.BlockSpec((step_rows, LANES), rows_in), hbm, hbm, hbm],
            out_specs=pl.BlockSpec((step_rows, LANES), lambda j, be, nu, nx, pa: (j, 0)),
            scratch_shapes=[pltpu.VMEM((2, d, de), F32), pltpu.VMEM((2, d, de), F32),
                            pltpu.VMEM((2, de, d), F32), pltpu.SemaphoreType.DMA((2,)),
                            pltpu.VMEM((d, de), BF16), pltpu.VMEM((d, de), BF16),
                            pltpu.VMEM((de, d), BF16)]),
        out_shape=jax.ShapeDtypeStruct((nblk * MOE_ROWS * TOKEN_ROWS, LANES), jnp.uint32),
        compiler_params=_cparams(("arbitrary",)),
        name="experts",
    )(block_expert, n_used, next_expert, slot_parity, xb8, w1, w3, w2)


def _ple_kernel(dest_ref, destn_ref, x1_ref, route_ref, p_ref, ln_ref, wg_ref, wp_ref, yb8_ref,
                out_ref, ya0, ya1, yb0, yb1, sems):
    j = pl.program_id(0)
    last = pl.num_programs(0) - 1
    tm = PLE_ROW_TILE
    d = x1_ref.shape[1]

    def start(idx_ref, base, bufs, sem):
        for r in range(tm):
            for slot in range(2):
                pltpu.make_async_copy(_token_tile(yb8_ref, idx_ref[base + 2 * r + slot]),
                                      bufs[slot].at[pl.ds(r * TOKEN_ROWS, TOKEN_ROWS)],
                                      sem).start(priority=slot)

    def wait(bufs, sem):
        for buf in bufs:
            pltpu.make_async_copy(yb8_ref.at[pl.ds(0, tm * TOKEN_ROWS)], buf, sem).wait()

    def compute(bufs, rows):
        x2 = (x1_ref[rows, :]
              + route_ref[rows, 2:3] * _load_token_tiles(bufs[0], 0, tm).astype(F32)
              + route_ref[rows, 3:4] * _load_token_tiles(bufs[1], 0, tm).astype(F32))
        h = (x2 * _rms_scale(x2, d) * ln_ref[...]).astype(BF16)
        pp = _dot(p_ref[rows, :].astype(BF16), wp_ref[...])
        out_ref[rows, :] = x2 + pp * _sigmoid(_dot(h, wg_ref[...]))

    @pl.when(j == 0)
    def _():
        start(dest_ref, 0, (ya0, ya1), sems.at[0])

    wait((ya0, ya1), sems.at[0])
    start(dest_ref, 2 * tm, (yb0, yb1), sems.at[1])
    compute((ya0, ya1), slice(0, tm))

    wait((yb0, yb1), sems.at[1])
    start(destn_ref, 0, (ya0, ya1), sems.at[0])
    compute((yb0, yb1), slice(tm, 2 * tm))

    @pl.when(j == last)
    def _():
        wait((ya0, ya1), sems.at[0])


def _ple(dest_flat, x1, route, p2, ln, wg, wp, yb):
    t, d = x1.shape
    tm = PLE_ROW_TILE
    nsteps = t // (2 * tm)
    row = lambda w: pl.BlockSpec((2 * tm, w), lambda j: (j, 0))
    return pl.pallas_call(
        _ple_kernel,
        grid=(nsteps,),
        in_specs=[pl.BlockSpec((4 * tm,), lambda j: (j,), memory_space=pltpu.SMEM),
                  pl.BlockSpec((4 * tm,), lambda j: (jnp.minimum(j + 1, nsteps - 1),),
                               memory_space=pltpu.SMEM),
                  row(d), row(LANES), row(p2.shape[1]), _const_spec(ln.shape),
                  _const_spec(wg.shape), _const_spec(wp.shape),
                  pl.BlockSpec(memory_space=pl.ANY)],
        out_specs=row(d),
        out_shape=jax.ShapeDtypeStruct((t, d), F32),
        scratch_shapes=([pltpu.VMEM((tm * TOKEN_ROWS, LANES), jnp.uint32)] * 4
                        + [pltpu.SemaphoreType.DMA((2,))]),
        compiler_params=_cparams(("arbitrary",)),
        name="ple",
    )(dest_flat, dest_flat, x1, route, p2, ln, wg, wp, yb)


def _head_pad(w, width):
    r = w.shape[0]
    w = w.reshape(r, MLA_HEADS, width)
    return jnp.pad(w, ((0, 0), (0, 0), (0, LANES - width))).reshape(r, MLA_HEADS * LANES)


def _rope_table():
    half = MLA_ROPE // 2
    inv_freq = ROPE_THETA ** (-np.arange(half, dtype=np.float32) / half)
    tab = np.zeros((8, LANES), np.float32)
    tab[0, MLA_NOPE:MLA_NOPE + half] = inv_freq
    tab[0, MLA_NOPE + half:MLA_QK] = inv_freq
    tab[1, MLA_NOPE + half:MLA_QK] = 1.0
    tab[2, MLA_NOPE:MLA_NOPE + half] = -1.0
    freq_t = np.broadcast_to(inv_freq[:, None], (half, LANES))
    return jnp.asarray(tab), jnp.asarray(freq_t)


def _route_tables(ids):
    a = ids.shape[0] * 2
    e = ids.reshape(a)
    onehot = (e[:, None] == jnp.arange(N_EXPERTS, dtype=jnp.int32)[None, :]).astype(jnp.int32)
    csum = jnp.cumsum(onehot, axis=0)
    counts = csum[-1]
    rank = jnp.sum(csum * onehot, axis=1) - 1
    pcounts = (counts + MOE_ROWS - 1) // MOE_ROWS * MOE_ROWS
    pends = jnp.cumsum(pcounts)
    pstarts = pends - pcounts
    dest = (jnp.sum(onehot * pstarts[None, :], axis=1) + rank).astype(jnp.int32)
    nblk = (a + N_EXPERTS * (MOE_ROWS - 1)) // MOE_ROWS
    nblk += nblk % 2
    n_used = (pends[-1] // MOE_ROWS).astype(jnp.int32)
    blk_start = jnp.arange(nblk, dtype=jnp.int32) * MOE_ROWS
    blk_start = jnp.minimum(blk_start, pends[-1] - 1)
    block_expert = jnp.sum((pends[None, :] <= blk_start[:, None]).astype(jnp.int32), axis=1)
    block_expert = jnp.minimum(block_expert, N_EXPERTS - 1).astype(jnp.int32)
    block_expert, n_used = lax.optimization_barrier((block_expert, n_used))
    blk = jnp.arange(nblk, dtype=jnp.int32)
    prev = jnp.concatenate([jnp.full((1,), -1, jnp.int32), block_expert[:-1]])
    starts_expert = (blk < n_used) & (block_expert != prev)
    slot_parity = (jnp.cumsum(starts_expert.astype(jnp.int32)) - 1) % 2
    start_at_or_after = jnp.flip(lax.cummin(jnp.flip(jnp.where(starts_expert, blk, nblk))))
    next_start = jnp.concatenate([start_at_or_after[1:], jnp.full((1,), nblk, jnp.int32)])
    next_expert = jnp.where(next_start < nblk, block_expert[jnp.minimum(next_start, nblk - 1)], -1)
    return (dest, block_expert, n_used.reshape(1), next_expert.astype(jnp.int32),
            slot_parity.astype(jnp.int32), nblk)


def kernel(x, p, positions, ln_mix, w_in, hg_lb, hg_onorm, w_oA, mla_qa_norm, mla_kva_norm, w_uq,
           w_ukv, q_norm, k_norm, w_oB, w_out, ln_moe, w_rg, b_rg, w_re, b_re, w1, w3, w2, ln_ple,
           w_ple_gate, w_ple_proj):
    b, s, d = x.shape
    t = b * s
    depth = w_in.shape[0]
    lb_all = jnp.cumsum(jax.nn.softmax(hg_lb.astype(F32), axis=1), axis=1)
    posf = positions.astype(F32).reshape(t, 1)
    rope_tab, freq_t = _rope_table()
    xc = x.reshape(t, d)

    for layer in range(depth):
        wi = w_in[layer]
        n_hg = 3 * HG_F + 2 * HG_V
        n_mla = MLA_Q_RANK + MLA_KV_RANK + MLA_ROPE
        mla_w = -(-n_mla // LANES) * LANES
        whg = wi[:, :n_hg].astype(BF16)
        wmla = jnp.pad(wi[:, n_hg:n_hg + n_mla], ((0, 0), (0, mla_w - n_mla))).astype(BF16)
        wgate = wi[:, n_hg + n_mla:].astype(BF16)
        lb = lb_all[:, layer, :]

        qs, lf, kk, v, og, mla_in, gates = _inproj(
            xc, ln_mix[layer][None, :], whg, wmla, wgate, lb, mla_qa_norm[layer][None, :],
            mla_kva_norm[layer][None, :])

        r3 = lambda a: a.reshape(b, s, a.shape[-1])
        ya = _hgrn(r3(qs), r3(lf), r3(kk), r3(v), r3(og), hg_onorm[layer][None, :]).reshape(t, HG_V)

        kv_w = MLA_NOPE + MLA_VDIM
        wkv = w_ukv[layer].reshape(MLA_KV_RANK, MLA_HEADS, kv_w)
        ckr_w = mla_w - MLA_Q_RANK
        wq = _head_pad(w_uq[layer], MLA_QK).astype(BF16)
        wk_nope = _head_pad(wkv[:, :, :MLA_NOPE].reshape(MLA_KV_RANK, MLA_HEADS * MLA_NOPE), MLA_NOPE)
        place = np.zeros((ckr_w - MLA_KV_RANK, MLA_HEADS, LANES), np.float32)
        for j in range(MLA_ROPE):
            place[j, :, MLA_NOPE + j] = 1.0
        wk = jnp.concatenate([wk_nope, jnp.asarray(place.reshape(ckr_w - MLA_KV_RANK, -1))],
                             axis=0).astype(BF16)
        wv = jnp.pad(wkv[:, :, MLA_NOPE:].reshape(MLA_KV_RANK, MLA_HEADS * MLA_VDIM),
                     ((0, ckr_w - MLA_KV_RANK), (0, 0))).astype(BF16)
        gq = jnp.pad(q_norm[layer], (0, LANES - MLA_QK))[None, :]
        gk_t = jnp.broadcast_to(jnp.pad(k_norm[layer], (0, LANES - MLA_QK))[:, None], (LANES, LANES))
        qh, kh_t, vh = _mlaprep(mla_in, posf, wq, wk.T, wv, gq, gk_t, rope_tab, freq_t)
        yb = _attn(r3(qh), kh_t, r3(vh)).reshape(t, MLA_HEADS * MLA_VDIM)

        wr = jnp.pad(jnp.concatenate([w_rg[layer], w_re[layer]], axis=1),
                     ((0, 0), (0, LANES - N_GROUPS - N_EXPERTS)))
        wr_hi = wr.astype(BF16)
        wr = jnp.concatenate([wr_hi, (wr - wr_hi.astype(F32)).astype(BF16)], axis=1)
        br = jnp.pad(jnp.concatenate([b_rg[layer], b_re[layer]]),
                     (0, LANES - N_GROUPS - N_EXPERTS))[None, :]
        x1, hm8, route = _merge(xc, ya, yb, gates, w_oA[layer].astype(BF16), w_oB[layer].astype(BF16),
                               w_out[layer].astype(BF16), ln_moe[layer][None, :], wr, br)

        ids = route[:, 0:2].astype(jnp.int32)
        dest, block_expert, n_used, next_expert, slot_parity, nblk = _route_tables(ids)
        xb8 = _dispatch(dest, hm8, jnp.zeros((nblk * MOE_ROWS * TOKEN_ROWS, LANES), jnp.uint32))
        yexp = _experts(block_expert, n_used, next_expert, slot_parity, xb8,
                        w1[layer], w3[layer], w2[layer])

        xc = _ple(dest, x1, route, p[layer].reshape(t, -1), ln_ple[layer][None, :],
                  w_ple_gate[layer].astype(BF16), w_ple_proj[layer].astype(BF16), yexp)
    return xc.reshape(b, s, d)
```

```python
import functools

import numpy as np
import jax
import jax.numpy as jnp
from jax import lax
from jax.experimental import pallas as pl
from jax.experimental.pallas import tpu as pltpu

F32 = jnp.float32
BF16 = jnp.bfloat16

HG_HEADS = 4
HG_KDIM = 128
HG_VDIM = 128
HG_F = HG_HEADS * HG_KDIM
HG_V = HG_HEADS * HG_VDIM
MLA_HEADS = 8
MLA_NOPE = 64
MLA_ROPE = 32
MLA_VDIM = 64
MLA_QK = MLA_NOPE + MLA_ROPE
MLA_Q_RANK = 256
MLA_KV_RANK = 128
ROPE_THETA = 10000.0
N_GROUPS = 8
EXPERTS_PER_GROUP = 8
N_EXPERTS = N_GROUPS * EXPERTS_PER_GROUP
D_EXPERT = 256
EPS = 1e-6

LANES = 128
TOKEN_ROWS = 4
VMEM_LIMIT_BYTES = 56 * 1024 * 1024

PLE_ROW_TILE = 256
MERGE_ROW_TILE = 512
INPROJ_ROW_TILE = 256
MLAPREP_ROW_TILE = 512
HG_CHUNK = 64
HG_HEADS_PER_STEP = 2
HG_FAST_CHUNK = 128
HG_CHUNKS_PER_MATMUL = 2
ATTN_Q_TILE = 512
ATTN_HEADS_PER_STEP = 8
MOE_ROWS = 256
DISPATCH_TOKENS = 1024
HG_SAFE_LOG_DECAY = -75.0


def _cparams(sem):
    return pltpu.CompilerParams(dimension_semantics=sem, vmem_limit_bytes=VMEM_LIMIT_BYTES)


def _const_spec(shape):
    nd = len(shape)
    return pl.BlockSpec(shape, lambda *_: (0,) * nd)


def _sigmoid(x):
    return 1.0 / (1.0 + jnp.exp(-x))


def _silu(x):
    return x * _sigmoid(x)


def _rms_scale(x, n):
    return lax.rsqrt(jnp.sum(x * x, axis=-1, keepdims=True) * (1.0 / n) + EPS)


def _dot(a, b):
    return jnp.dot(a, b, preferred_element_type=F32)


def _dot_nt(a, b):
    return lax.dot_general(a, b, (((1,), (1,)), ((), ())), preferred_element_type=F32)


def _dot_tn(a, b):
    return lax.dot_general(a, b, (((0,), (0,)), ((), ())), preferred_element_type=F32)


def _chunk_cumsum(a, direction, chunk):
    n = a.shape[0]
    in_chunk = lax.broadcasted_iota(jnp.int32, a.shape, 0) % chunk
    k = 1
    while k < chunk:
        if direction == 0:
            a = a + jnp.where(in_chunk >= k, pltpu.roll(a, k, 0), 0.0)
        else:
            a = a + jnp.where(in_chunk < chunk - k, pltpu.roll(a, n - k, 0), 0.0)
        k *= 2
    return a


def _inproj_kernel(x_ref, g_ref, whg_ref, wmla_ref, wgate_ref, lb_ref, nq_ref, nkv_ref,
                   qs_ref, lf_ref, kk_ref, v_ref, og_ref, mla_ref, gates_ref):
    x = x_ref[...]
    d = x.shape[-1]
    h = (x * _rms_scale(x, d) * g_ref[...]).astype(BF16)

    qs_ref[...] = _silu(_dot(h, whg_ref[:, 0:HG_F])).astype(BF16)
    for direction in range(2):
        cols = slice(HG_F * (1 + direction), HG_F * (2 + direction))
        out_cols = slice(HG_F * direction, HG_F * (direction + 1))
        lb = lb_ref[direction:direction + 1, :]
        f = lb + (1.0 - lb) * _sigmoid(_dot(h, whg_ref[:, cols]))
        lf_ref[:, out_cols] = jnp.log(f)
        kk_ref[:, out_cols] = (1.0 - f).astype(BF16)
    v_ref[...] = _dot(h, whg_ref[:, 3 * HG_F:3 * HG_F + HG_V]).astype(BF16)
    og_ref[...] = _silu(_dot(h, whg_ref[:, 3 * HG_F + HG_V:3 * HG_F + 2 * HG_V])).astype(BF16)

    zm = _dot(h, wmla_ref[...])
    cq = zm[:, 0:MLA_Q_RANK]
    mla_ref[:, 0:MLA_Q_RANK] = (cq * _rms_scale(cq, MLA_Q_RANK) * nq_ref[...]).astype(BF16)
    ckv = zm[:, MLA_Q_RANK:MLA_Q_RANK + MLA_KV_RANK]
    mla_ref[:, MLA_Q_RANK:MLA_Q_RANK + MLA_KV_RANK] = (
        ckv * _rms_scale(ckv, MLA_KV_RANK) * nkv_ref[...]).astype(BF16)
    mla_ref[:, MLA_Q_RANK + MLA_KV_RANK:] = zm[:, MLA_Q_RANK + MLA_KV_RANK:].astype(BF16)

    for half in range(2):
        cols = slice(d * half, d * (half + 1))
        gates_ref[:, cols] = _sigmoid(_dot(h, wgate_ref[:, cols])).astype(BF16)


def _inproj(x2, ln_mix, whg, wmla, wgate, lb, nq, nkv):
    t, d = x2.shape
    tm = INPROJ_ROW_TILE
    mla_w = wmla.shape[1]
    row = lambda w: pl.BlockSpec((tm, w), lambda i: (i, 0))
    out_shape = (
        jax.ShapeDtypeStruct((t, HG_F), BF16),
        jax.ShapeDtypeStruct((t, 2 * HG_F), F32),
        jax.ShapeDtypeStruct((t, 2 * HG_F), BF16),
        jax.ShapeDtypeStruct((t, HG_V), BF16),
        jax.ShapeDtypeStruct((t, HG_V), BF16),
        jax.ShapeDtypeStruct((t, mla_w), BF16),
        jax.ShapeDtypeStruct((t, 2 * d), BF16),
    )
    return pl.pallas_call(
        _inproj_kernel,
        grid=(t // tm,),
        in_specs=[row(d), _const_spec((1, d)), _const_spec(whg.shape), _const_spec(wmla.shape),
                  _const_spec(wgate.shape), _const_spec(lb.shape), _const_spec(nq.shape),
                  _const_spec(nkv.shape)],
        out_specs=(row(HG_F), row(2 * HG_F), row(2 * HG_F), row(HG_V), row(HG_V), row(mla_w),
                   row(2 * d)),
        out_shape=out_shape,
        compiler_params=_cparams(("parallel",)),
        name="inproj",
    )(x2, ln_mix, whg, wmla, wgate, lb, nq, nkv)


def _hgrn_kernel(qs_ref, lff_ref, lfb_ref, kf_ref, kb_ref, v_ref, og_ref, onorm_ref, out_ref,
                 state_ref, ofw_ref, obw_ref, qt_scr, u_scr, vec_scr, g_scr, k_scr, v_scr):
    c = HG_CHUNK
    s_len = qs_ref.shape[1]
    n_chunks = s_len // c
    hb = HG_HEADS_PER_STEP
    kd = HG_KDIM

    row = lax.broadcasted_iota(jnp.int32, (c, c), 0)
    col = lax.broadcasted_iota(jnp.int32, (c, c), 1)
    masks = (row >= col, row <= col)
    rowc = lax.broadcasted_iota(jnp.int32, (c, kd), 0)
    o_refs = (ofw_ref, obw_ref)

    def load(j):
        chains = []
        for hh in range(hb):
            lanes = slice(hh * kd, (hh + 1) * kd)
            for direction in range(2):
                chunk = j if direction == 0 else n_chunks - 1 - j
                rows = pl.ds(pl.multiple_of(chunk * c, c), c)
                g = _chunk_cumsum((lff_ref, lfb_ref)[direction][0, rows, lanes], direction, c)
                total = g[c - 1:c, :] if direction == 0 else g[0:1, :]
                chains.append(dict(
                    idx=hh * 2 + direction, direction=direction, rows=rows, lanes=lanes, g=g,
                    total=total,
                    q=qs_ref[0, rows, lanes].astype(F32),
                    k=(kf_ref, kb_ref)[direction][0, rows, lanes].astype(F32),
                    v=v_ref[0, rows, lanes]))
        return chains

    def finish(ch, o, kdec_t_v_scaled):
        (ofw_ref, obw_ref)[ch["direction"]][ch["rows"], ch["lanes"]] = o
        state_ref[ch["idx"]] = kdec_t_v_scaled

    def fast(ch):
        g = ch["g"]
        qt = (ch["q"] * jnp.exp(g)).astype(BF16)
        kt = (ch["k"] * jnp.exp(-g)).astype(BF16)
        st = state_ref[ch["idx"]]
        sc = jnp.where(masks[ch["direction"]], _dot_nt(qt, kt), 0.0)
        o = _dot_nt(qt, st.astype(BF16)) + _dot(sc.astype(BF16), ch["v"])
        finish(ch, o, (st + _dot_tn(ch["v"], kt)) * jnp.exp(ch["total"]))

    def robust(ch):
        g = ch["g"]
        direction = ch["direction"]
        st = state_ref[ch["idx"]]
        o0 = _dot_nt((ch["q"] * jnp.exp(g)).astype(BF16), st.astype(BF16))
        slot = ch["idx"]
        g_scr[slot] = g
        k_scr[slot] = ch["k"]
        v_scr[slot] = ch["v"].astype(F32)
        q = ch["q"]

        def body(s, acc):
            g_s = g_scr[slot, pl.ds(s, 1), :]
            seen = (rowc >= s) if direction == 0 else (rowc <= s)
            decay = jnp.where(seen, jnp.exp(jnp.minimum(g - g_s, 0.0)), 0.0)
            a = jnp.sum(q * decay * k_scr[slot, pl.ds(s, 1), :], axis=-1, keepdims=True)
            return acc + a * v_scr[slot, pl.ds(s, 1), :]

        o = lax.fori_loop(0, c, body, o0)
        kdec = (ch["k"] * jnp.exp(ch["total"] - g)).astype(BF16)
        finish(ch, o, st * jnp.exp(ch["total"]) + _dot_tn(ch["v"], kdec))

    def step(j, carry):
        chains = load(j)
        lowest = chains[0]["total"]
        for ch in chains[1:]:
            lowest = jnp.minimum(lowest, ch["total"])
        safe = jnp.min(lowest) >= HG_SAFE_LOG_DECAY

        @pl.when(safe)
        def _():
            for ch in chains:
                fast(ch)

        @pl.when(jnp.logical_not(safe))
        def _():
            for ch in chains:
                robust(ch)

        return carry

    fc = HG_FAST_CHUNK if s_len % (HG_FAST_CHUNK * HG_CHUNKS_PER_MATMUL) == 0 else c
    grp = HG_CHUNKS_PER_MATMUL if (s_len // fc) % HG_CHUNKS_PER_MATMUL == 0 else 1
    gr = grp * fc
    n_groups = s_len // gr
    mid = fc // 2
    grow = lax.broadcasted_iota(jnp.int32, (gr, gr), 0)
    gcol = lax.broadcasted_iota(jnp.int32, (gr, gr), 1)
    same_chunk = (grow // fc) == (gcol // fc)
    gmasks = (same_chunk & (grow >= gcol), same_chunk & (grow <= gcol))
    chunk_of_row = lax.broadcasted_iota(jnp.int32, (gr, kd), 0) // fc

    def block_diag(a):
        return jnp.concatenate([jnp.where(chunk_of_row == i, a, jnp.zeros_like(a))
                                for i in range(grp)], axis=1)

    def phase_a(gi, lowest):
        rows = pl.ds(pl.multiple_of(gi * gr, gr), gr)
        for hh in range(hb):
            lanes = slice(hh * kd, (hh + 1) * kd)
            v = v_ref[0, rows, lanes]
            v_t = v.T
            q = qs_ref[0, rows, lanes].astype(F32)
            for direction in range(2):
                idx = hh * 2 + direction
                g = _chunk_cumsum((lff_ref, lfb_ref)[direction][0, rows, lanes], direction, fc)
                k = (kf_ref, kb_ref)[direction][0, rows, lanes].astype(F32)
                g_mid = [g[i * fc + mid:i * fc + mid + 1, :] for i in range(grp)]
                centred = g - jnp.concatenate([jnp.broadcast_to(r, (fc, kd)) for r in g_mid], axis=0)
                qt = (q * jnp.exp(centred)).astype(BF16)
                kt = (k * jnp.exp(-centred)).astype(BF16)
                qt_scr[idx, rows, :] = qt
                sc = jnp.where(gmasks[direction], _dot_nt(qt, kt), 0.0)
                o_refs[direction][rows, lanes] = _dot(sc.astype(BF16), v)
                u = _dot(v_t, block_diag(kt))
                for i in range(grp):
                    ci = gi * grp + i
                    u_scr[idx, ci] = u[:, i * kd:(i + 1) * kd]
                    edge = i * fc + (fc - 1 if direction == 0 else 0)
                    total = g[edge:edge + 1, :]
                    lowest = jnp.minimum(lowest, jnp.minimum(g_mid[i], total - g_mid[i]))
                    vec_scr[idx, 0, pl.ds(ci, 1), :] = jnp.exp(total)
                    vec_scr[idx, 1, pl.ds(ci, 1), :] = jnp.exp(g_mid[i])
                    vec_scr[idx, 2, pl.ds(ci, 1), :] = jnp.exp(total - g_mid[i])
        return lowest

    lowest = lax.fori_loop(0, n_groups, phase_a, jnp.zeros((1, kd), F32))
    all_safe = jnp.min(lowest) >= HG_SAFE_LOG_DECAY
    state_ref[...] = jnp.zeros_like(state_ref)

    @pl.when(all_safe)
    def _():
        def phase_b(j, carry):
            for hh in range(hb):
                lanes = slice(hh * kd, (hh + 1) * kd)
                for direction in range(2):
                    idx = hh * 2 + direction
                    gi = j if direction == 0 else n_groups - 1 - j
                    rows = pl.ds(pl.multiple_of(gi * gr, gr), gr)
                    st = state_ref[idx]
                    seen = [None] * grp
                    for i in (range(grp) if direction == 0 else reversed(range(grp))):
                        ci = gi * grp + i
                        at = pl.ds(ci, 1)
                        seen[i] = (st * vec_scr[idx, 1, at, :]).astype(BF16)
                        st = st * vec_scr[idx, 0, at, :] + u_scr[idx, ci] * vec_scr[idx, 2, at, :]
                    state_ref[idx] = st
                    o_refs[direction][rows, lanes] += _dot_nt(
                        block_diag(qt_scr[idx, rows, :]), jnp.concatenate(seen, axis=1))
            return carry

        lax.fori_loop(0, n_groups, phase_b, 0)

    @pl.when(jnp.logical_not(all_safe))
    def _():
        lax.fori_loop(0, n_chunks, step, 0)

    blk = 256 if s_len % 256 == 0 else c

    def epilogue(i, carry):
        rows = pl.ds(pl.multiple_of(i * blk, blk), blk)
        for hh in range(hb):
            lanes = slice(hh * kd, (hh + 1) * kd)
            o = ofw_ref[rows, lanes] + obw_ref[rows, lanes]
            y = o * _rms_scale(o, HG_VDIM) * onorm_ref[...]
            out_ref[0, rows, lanes] = (y * og_ref[0, rows, lanes].astype(F32)).astype(BF16)
        return carry

    lax.fori_loop(0, s_len // blk, epilogue, 0)


def _hgrn(qs, lf, kk, v, og, onorm):
    b, s, _ = qs.shape
    hb = HG_HEADS_PER_STEP
    w = hb * HG_KDIM
    n_hsteps = HG_HEADS // hb
    fwd = pl.BlockSpec((1, s, w), lambda bi, hi: (bi, 0, hi))
    bwd = pl.BlockSpec((1, s, w), lambda bi, hi: (bi, 0, n_hsteps + hi))
    return pl.pallas_call(
        _hgrn_kernel,
        grid=(b, n_hsteps),
        in_specs=[fwd, fwd, bwd, fwd, bwd, fwd, fwd, _const_spec(onorm.shape)],
        out_specs=fwd,
        out_shape=jax.ShapeDtypeStruct((b, s, HG_V), BF16),
        scratch_shapes=[
            pltpu.VMEM((2 * hb, HG_VDIM, HG_KDIM), F32),
            pltpu.VMEM((s, w), F32),
            pltpu.VMEM((s, w), F32),
            pltpu.VMEM((2 * hb, s, HG_KDIM), BF16),
            pltpu.VMEM((2 * hb, s // HG_CHUNK, HG_VDIM, HG_KDIM), F32),
            pltpu.VMEM((2 * hb, 3, max(s // HG_CHUNK, 8), HG_KDIM), F32),
            pltpu.VMEM((2 * hb, HG_CHUNK, HG_KDIM), F32),
            pltpu.VMEM((2 * hb, HG_CHUNK, HG_KDIM), F32),
            pltpu.VMEM((2 * hb, HG_CHUNK, HG_VDIM), F32),
        ],
        compiler_params=_cparams(("parallel", "parallel")),
        name="hgrn",
    )(qs, lf, lf, kk, kk, v, og, onorm)


def _mlaprep_kernel(mla_ref, pos_t_ref, wq_t_ref, wk_t_ref, wv_ref, gq_t_ref, gk_t_ref, freq_t_ref,
                    q_t_ref, k_t_ref, v_ref):
    cq = mla_ref[:, 0:MLA_Q_RANK]
    ckr = mla_ref[:, MLA_Q_RANK:]
    half = MLA_ROPE // 2
    tm = mla_ref.shape[0]

    lane_tile = lambda a: jnp.concatenate([a] * (tm // LANES), axis=1)
    ang_t = lane_tile(freq_t_ref[...]) * pos_t_ref[...]
    cos_t = jnp.cos(ang_t)
    sin_t = jnp.sin(ang_t)

    def norm_rope_t(x_t, gain_ref, out_ref):
        gain_t = lane_tile(gain_ref[...])
        for hd in range(MLA_HEADS):
            r0 = hd * LANES
            x = x_t[r0:r0 + LANES, :]
            y = x * lax.rsqrt(jnp.sum(x * x, axis=0, keepdims=True) * (1.0 / MLA_QK) + EPS) * gain_t
            x1 = y[MLA_NOPE:MLA_NOPE + half, :]
            x2 = y[MLA_NOPE + half:MLA_QK, :]
            out_ref[r0:r0 + MLA_NOPE, :] = y[0:MLA_NOPE, :].astype(BF16)
            out_ref[r0 + MLA_NOPE:r0 + MLA_NOPE + half, :] = (x1 * cos_t - x2 * sin_t).astype(BF16)
            out_ref[r0 + MLA_NOPE + half:r0 + MLA_QK, :] = (x2 * cos_t + x1 * sin_t).astype(BF16)
            out_ref[r0 + MLA_QK:r0 + LANES, :] = jnp.zeros((LANES - MLA_QK, tm), BF16)

    norm_rope_t(_dot_nt(wq_t_ref[...], cq), gq_t_ref, q_t_ref)
    norm_rope_t(_dot_nt(wk_t_ref[...], ckr), gk_t_ref, k_t_ref)
    v_ref[...] = _dot(ckr, wv_ref[...]).astype(BF16)


def _mlaprep(mla_in, pos_t, wq_t, wk_t, wv, gq_t, gk_t, freq_t):
    t = mla_in.shape[0]
    tm = MLAPREP_ROW_TILE
    row = lambda w: pl.BlockSpec((tm, w), lambda i: (i, 0))
    col = lambda h: pl.BlockSpec((h, tm), lambda i: (0, i))
    hw = MLA_HEADS * LANES
    vw = MLA_HEADS * MLA_VDIM
    return pl.pallas_call(
        _mlaprep_kernel,
        grid=(t // tm,),
        in_specs=[row(mla_in.shape[1]), col(1), _const_spec(wq_t.shape), _const_spec(wk_t.shape),
                  _const_spec(wv.shape), _const_spec(gq_t.shape), _const_spec(gk_t.shape),
                  _const_spec(freq_t.shape)],
        out_specs=(col(hw), col(hw), row(vw)),
        out_shape=(jax.ShapeDtypeStruct((hw, t), BF16), jax.ShapeDtypeStruct((hw, t), BF16),
                   jax.ShapeDtypeStruct((t, vw), BF16)),
        compiler_params=_cparams(("parallel",)),
        name="mlaprep",
    )(mla_in, pos_t, wq_t, wk_t, wv, gq_t, gk_t, freq_t)


def _attn_kernel(q_t_ref, k_t_ref, v_ref, o_ref):
    for j in range(ATTN_HEADS_PER_STEP):
        rows = slice(j * LANES, (j + 1) * LANES)
        s = _dot_tn(q_t_ref[rows, :], k_t_ref[rows, :])
        p = jnp.exp(s - jnp.max(s, axis=-1, keepdims=True))
        l = jnp.sum(p, axis=-1, keepdims=True)
        vl = slice(j * MLA_VDIM, (j + 1) * MLA_VDIM)
        o = _dot(p.astype(BF16), v_ref[0, :, vl])
        o_ref[0, :, vl] = (o / l).astype(BF16)


def _attn(q_t, k_t, v):
    b, s, _ = v.shape
    tq = min(ATTN_Q_TILE, s)
    nh = ATTN_HEADS_PER_STEP
    n_q = s // tq
    return pl.pallas_call(
        _attn_kernel,
        grid=(b, MLA_HEADS // nh, n_q),
        in_specs=[pl.BlockSpec((nh * LANES, tq), lambda bi, hp, i: (hp, bi * n_q + i)),
                  pl.BlockSpec((nh * LANES, s), lambda bi, hp, i: (hp, bi)),
                  pl.BlockSpec((1, s, nh * MLA_VDIM), lambda bi, hp, i: (bi, 0, hp))],
        out_specs=pl.BlockSpec((1, tq, nh * MLA_VDIM), lambda bi, hp, i: (bi, i, hp)),
        out_shape=jax.ShapeDtypeStruct((b, s, MLA_HEADS * MLA_VDIM), BF16),
        compiler_params=_cparams(("parallel", "parallel", "arbitrary")),
        name="attn",
    )(q_t, k_t, v)


def _merge_kernel(x_ref, ya_ref, yb_ref, gates_ref, woa_ref, wob_ref, wout_ref, lnm_ref, wr_ref,
                  br_ref, x1_ref, hm8_ref, route_ref):
    d = x_ref.shape[-1]
    y_a = _dot(ya_ref[...], woa_ref[...])
    y_b = _dot(yb_ref[...], wob_ref[...])
    merged = gates_ref[:, 0:d].astype(F32) * y_a + gates_ref[:, d:2 * d].astype(F32) * y_b
    x1 = x_ref[...] + _dot(merged.astype(BF16), wout_ref[...])
    x1_ref[...] = x1
    hm = x1 * _rms_scale(x1, d) * lnm_ref[...]
    _store_token_tiles(hm8_ref, 0, hm)

    hm_hi = hm.astype(BF16)
    hm_lo = (hm - hm_hi.astype(F32)).astype(BF16)
    hh = _dot(hm_hi, wr_ref[...])
    logits = (hh[:, 0:LANES] + hh[:, LANES:2 * LANES] + _dot(hm_lo, wr_ref[:, 0:LANES])
              + br_ref[...])
    lane = lax.broadcasted_iota(jnp.int32, logits.shape, 1)
    neg = -jnp.inf
    big = jnp.int32(2 ** 30)
    is_group = lane < N_GROUPS
    gl = jnp.where(is_group, logits, neg)
    gmax = jnp.max(gl, axis=-1, keepdims=True)
    p_group = 1.0 / jnp.sum(jnp.where(is_group, jnp.exp(gl - gmax), 0.0), axis=-1, keepdims=True)
    g_sel = jnp.min(jnp.where(gl == gmax, lane, big), axis=-1, keepdims=True)
    lo = N_GROUPS + g_sel * EXPERTS_PER_GROUP
    in_group = (lane >= lo) & (lane < lo + EXPERTS_PER_GROUP)
    el = jnp.where(in_group, logits, neg)
    v1 = jnp.max(el, axis=-1, keepdims=True)
    i1 = jnp.min(jnp.where(el == v1, lane, big), axis=-1, keepdims=True)
    el2 = jnp.where(lane == i1, neg, el)
    v2 = jnp.max(el2, axis=-1, keepdims=True)
    i2 = jnp.min(jnp.where(el2 == v2, lane, big), axis=-1, keepdims=True)
    e21 = jnp.exp(v2 - v1)
    w1 = p_group / (1.0 + e21)
    w2 = w1 * e21
    route = jnp.where(lane == 0, (i1 - N_GROUPS).astype(F32),
                      jnp.where(lane == 1, (i2 - N_GROUPS).astype(F32),
                                jnp.where(lane == 2, w1, jnp.where(lane == 3, w2, 0.0))))
    route_ref[...] = route


def _merge(x2, ya, yb, gates, woa, wob, wout, lnm, wr, br):
    t, d = x2.shape
    tm = MERGE_ROW_TILE
    row = lambda w: pl.BlockSpec((tm, w), lambda i: (i, 0))
    return pl.pallas_call(
        _merge_kernel,
        grid=(t // tm,),
        in_specs=[row(d), row(ya.shape[1]), row(yb.shape[1]), row(2 * d), _const_spec(woa.shape),
                  _const_spec(wob.shape), _const_spec(wout.shape), _const_spec(lnm.shape),
                  _const_spec(wr.shape), _const_spec(br.shape)],
        out_specs=(row(d), pl.BlockSpec((tm * TOKEN_ROWS, LANES), lambda i: (i, 0)), row(LANES)),
        out_shape=(jax.ShapeDtypeStruct((t, d), F32),
                   jax.ShapeDtypeStruct((t * TOKEN_ROWS, LANES), jnp.uint32),
                   jax.ShapeDtypeStruct((t, LANES), F32)),
        compiler_params=_cparams(("parallel",)),
        name="merge",
    )(x2, ya, yb, gates, woa, wob, wout, lnm, wr, br)


_HIGH_HALF = 0xFFFF0000


def _store_token_tiles(ref4, tok0, val):
    n, width = val.shape
    as_bits = lambda a: lax.bitcast_convert_type(a.astype(BF16).astype(F32), jnp.uint32)
    words = (as_bits(val[:, :width // 2]) >> 16) | (as_bits(val[:, width // 2:]) & jnp.uint32(_HIGH_HALF))
    for j in range(TOKEN_ROWS):
        ref4[pl.ds(tok0 * TOKEN_ROWS + j, n, stride=TOKEN_ROWS), :] = words[:, j * LANES:(j + 1) * LANES]


def _load_token_tiles(ref4, tok0, n):
    words = jnp.concatenate(
        [ref4[pl.ds(tok0 * TOKEN_ROWS + j, n, stride=TOKEN_ROWS), :] for j in range(TOKEN_ROWS)],
        axis=1)
    low = lax.bitcast_convert_type(words << 16, F32).astype(BF16)
    high = lax.bitcast_convert_type(words & jnp.uint32(_HIGH_HALF), F32).astype(BF16)
    return jnp.concatenate([low, high], axis=1)


def _token_tile(ref8, tok):
    return ref8.at[pl.ds(pl.multiple_of(tok * TOKEN_ROWS, TOKEN_ROWS), TOKEN_ROWS)]


def _dispatch_kernel(dest_ref, hm8_ref, xb_in_ref, xb8_ref, sem):
    del xb_in_ref
    n_tok = DISPATCH_TOKENS
    for r in range(n_tok):
        src = hm8_ref.at[pl.ds(r * TOKEN_ROWS, TOKEN_ROWS)]
        for slot in range(2):
            pltpu.make_async_copy(src, _token_tile(xb8_ref, dest_ref[2 * r + slot]),
                                  sem).start(priority=slot)
    for _ in range(2):
        pltpu.make_async_copy(hm8_ref, xb8_ref.at[pl.ds(0, n_tok * TOKEN_ROWS)], sem).wait()


def _dispatch(dest_flat, hm8, xb_init):
    t = hm8.shape[0] // TOKEN_ROWS
    return pl.pallas_call(
        _dispatch_kernel,
        grid=(t // DISPATCH_TOKENS,),
        in_specs=[pl.BlockSpec((2 * DISPATCH_TOKENS,), lambda i: (i,), memory_space=pltpu.SMEM),
                  pl.BlockSpec((DISPATCH_TOKENS * TOKEN_ROWS, LANES), lambda i: (i, 0)),
                  pl.BlockSpec(memory_space=pl.ANY)],
        out_specs=pl.BlockSpec(memory_space=pl.ANY),
        out_shape=jax.ShapeDtypeStruct(xb_init.shape, xb_init.dtype),
        scratch_shapes=[pltpu.SemaphoreType.DMA(())],
        input_output_aliases={2: 0},
        compiler_params=_cparams(("arbitrary",)),
        name="dispatch",
    )(dest_flat, hm8, xb_init)


def _experts_kernel(be_ref, nu_ref, nxt_ref, par_ref, xb8_ref, w1_hbm, w3_hbm, w2_hbm, yb8_ref,
                    w1f, w3f, w2f, sems, w1c, w3c, w2c):
    j = pl.program_id(0)

    def fetch(e, slot):
        return [pltpu.make_async_copy(src.at[e], dst.at[slot], sems.at[slot])
                for src, dst in ((w1_hbm, w1f), (w3_hbm, w3f), (w2_hbm, w2f))]

    for half in range(2):
        blk = 2 * j + half
        used = blk < nu_ref[0]
        new_expert = jnp.logical_or(blk == 0, be_ref[blk] != be_ref[jnp.maximum(blk - 1, 0)])
        tok0 = half * MOE_ROWS

        @pl.when(jnp.logical_and(used, new_expert))
        def _():
            slot = par_ref[blk]

            @pl.when(blk == 0)
            def _():
                for cp in fetch(be_ref[0], 0):
                    cp.start()

            for cp in fetch(be_ref[blk], slot):
                cp.wait()

            @pl.when(nxt_ref[blk] >= 0)
            def _():
                for cp in fetch(nxt_ref[blk], 1 - slot):
                    cp.start()

            w1c[...] = w1f[slot].astype(BF16)
            w3c[...] = w3f[slot].astype(BF16)
            w2c[...] = w2f[slot].astype(BF16)

        @pl.when(used)
        def _():
            x = _load_token_tiles(xb8_ref, tok0, MOE_ROWS)
            hmid = (_silu(_dot(x, w1c[...])) * _dot(x, w3c[...])).astype(BF16)
            _store_token_tiles(yb8_ref, tok0, _dot(hmid, w2c[...]))

        @pl.when(jnp.logical_not(used))
        def _():
            rows = pl.ds(tok0 * TOKEN_ROWS, MOE_ROWS * TOKEN_ROWS)
            yb8_ref[rows, :] = jnp.zeros((MOE_ROWS * TOKEN_ROWS, LANES), jnp.uint32)


def _experts(block_expert, n_used, next_expert, slot_parity, xb8, w1, w3, w2):
    nblk = block_expert.shape[0]
    nsteps = nblk // 2
    d = w1.shape[1]
    de = w1.shape[2]
    step_rows = 2 * MOE_ROWS * TOKEN_ROWS
    rows_in = lambda j, be, nu, nx, pa: (jnp.minimum(j, (nu[0] - 1) // 2), 0)
    hbm = pl.BlockSpec(memory_space=pl.ANY)
    return pl.pallas_call(
        _experts_kernel,
        grid_spec=pltpu.PrefetchScalarGridSpec(
            num_scalar_prefetch=4,
            grid=(nsteps,),
            in_specs=[pl.BlockSpec((step_rows, LANES), rows_in), hbm, hbm, hbm],
            out_specs=pl.BlockSpec((step_rows, LANES), lambda j, be, nu, nx, pa: (j, 0)),
            scratch_shapes=[pltpu.VMEM((2, d, de), F32), pltpu.VMEM((2, d, de), F32),
                            pltpu.VMEM((2, de, d), F32), pltpu.SemaphoreType.DMA((2,)),
                            pltpu.VMEM((d, de), BF16), pltpu.VMEM((d, de), BF16),
                            pltpu.VMEM((de, d), BF16)]),
        out_shape=jax.ShapeDtypeStruct((nblk * MOE_ROWS * TOKEN_ROWS, LANES), jnp.uint32),
        compiler_params=_cparams(("arbitrary",)),
        name="experts",
    )(block_expert, n_used, next_expert, slot_parity, xb8, w1, w3, w2)


def _ple_kernel(dest_ref, destn_ref, x1_ref, route_ref, p_ref, ln_ref, wg_ref, wp_ref, yb8_ref,
                out_ref, ya0, ya1, yb0, yb1, sems):
    j = pl.program_id(0)
    last = pl.num_programs(0) - 1
    tm = PLE_ROW_TILE
    d = x1_ref.shape[1]

    def start(idx_ref, base, bufs, sem):
        for r in range(tm):
            for slot in range(2):
                pltpu.make_async_copy(_token_tile(yb8_ref, idx_ref[base + 2 * r + slot]),
                                      bufs[slot].at[pl.ds(r * TOKEN_ROWS, TOKEN_ROWS)],
                                      sem).start(priority=slot)

    def wait(bufs, sem):
        for buf in bufs:
            pltpu.make_async_copy(yb8_ref.at[pl.ds(0, tm * TOKEN_ROWS)], buf, sem).wait()

    def compute(bufs, rows):
        x2 = (x1_ref[rows, :]
              + route_ref[rows, 2:3] * _load_token_tiles(bufs[0], 0, tm).astype(F32)
              + route_ref[rows, 3:4] * _load_token_tiles(bufs[1], 0, tm).astype(F32))
        h = (x2 * _rms_scale(x2, d) * ln_ref[...]).astype(BF16)
        pp = _dot(p_ref[rows, :].astype(BF16), wp_ref[...])
        out_ref[rows, :] = x2 + pp * _sigmoid(_dot(h, wg_ref[...]))

    @pl.when(j == 0)
    def _():
        start(dest_ref, 0, (ya0, ya1), sems.at[0])

    wait((ya0, ya1), sems.at[0])
    start(dest_ref, 2 * tm, (yb0, yb1), sems.at[1])
    compute((ya0, ya1), slice(0, tm))

    wait((yb0, yb1), sems.at[1])
    start(destn_ref, 0, (ya0, ya1), sems.at[0])
    compute((yb0, yb1), slice(tm, 2 * tm))

    @pl.when(j == last)
    def _():
        wait((ya0, ya1), sems.at[0])


def _ple(dest_flat, x1, route, p2, ln, wg, wp, yb):
    t, d = x1.shape
    tm = PLE_ROW_TILE
    nsteps = t // (2 * tm)
    row = lambda w: pl.BlockSpec((2 * tm, w), lambda j: (j, 0))
    return pl.pallas_call(
        _ple_kernel,
        grid=(nsteps,),
        in_specs=[pl.BlockSpec((4 * tm,), lambda j: (j,), memory_space=pltpu.SMEM),
                  pl.BlockSpec((4 * tm,), lambda j: (jnp.minimum(j + 1, nsteps - 1),),
                               memory_space=pltpu.SMEM),
                  row(d), row(LANES), row(p2.shape[1]), _const_spec(ln.shape),
                  _const_spec(wg.shape), _const_spec(wp.shape),
                  pl.BlockSpec(memory_space=pl.ANY)],
        out_specs=row(d),
        out_shape=jax.ShapeDtypeStruct((t, d), F32),
        scratch_shapes=([pltpu.VMEM((tm * TOKEN_ROWS, LANES), jnp.uint32)] * 4
                        + [pltpu.SemaphoreType.DMA((2,))]),
        compiler_params=_cparams(("arbitrary",)),
        name="ple",
    )(dest_flat, dest_flat, x1, route, p2, ln, wg, wp, yb)


def _head_pad(w, width):
    r = w.shape[0]
    w = w.reshape(r, MLA_HEADS, width)
    return jnp.pad(w, ((0, 0), (0, 0), (0, LANES - width))).reshape(r, MLA_HEADS * LANES)


def _rope_table():
    half = MLA_ROPE // 2
    inv_freq = ROPE_THETA ** (-np.arange(half, dtype=np.float32) / half)
    return jnp.asarray(np.broadcast_to(inv_freq[:, None], (half, LANES)))


def _gain_t(gain, scale):
    return jnp.broadcast_to((jnp.pad(gain, (0, LANES - MLA_QK)) * scale)[:, None], (LANES, LANES))


def _route_tables(ids):
    a = ids.shape[0] * 2
    e = ids.reshape(a)
    onehot = (e[:, None] == jnp.arange(N_EXPERTS, dtype=jnp.int32)[None, :]).astype(jnp.int32)
    csum = jnp.cumsum(onehot, axis=0)
    counts = csum[-1]
    rank = jnp.sum(csum * onehot, axis=1) - 1
    pcounts = (counts + MOE_ROWS - 1) // MOE_ROWS * MOE_ROWS
    pends = jnp.cumsum(pcounts)
    pstarts = pends - pcounts
    dest = (jnp.sum(onehot * pstarts[None, :], axis=1) + rank).astype(jnp.int32)
    nblk = (a + N_EXPERTS * (MOE_ROWS - 1)) // MOE_ROWS
    nblk += nblk % 2
    n_used = (pends[-1] // MOE_ROWS).astype(jnp.int32)
    blk_start = jnp.arange(nblk, dtype=jnp.int32) * MOE_ROWS
    blk_start = jnp.minimum(blk_start, pends[-1] - 1)
    block_expert = jnp.sum((pends[None, :] <= blk_start[:, None]).astype(jnp.int32), axis=1)
    block_expert = jnp.minimum(block_expert, N_EXPERTS - 1).astype(jnp.int32)
    block_expert, n_used = lax.optimization_barrier((block_expert, n_used))
    blk = jnp.arange(nblk, dtype=jnp.int32)
    prev = jnp.concatenate([jnp.full((1,), -1, jnp.int32), block_expert[:-1]])
    starts_expert = (blk < n_used) & (block_expert != prev)
    slot_parity = (jnp.cumsum(starts_expert.astype(jnp.int32)) - 1) % 2
    start_at_or_after = jnp.flip(lax.cummin(jnp.flip(jnp.where(starts_expert, blk, nblk))))
    next_start = jnp.concatenate([start_at_or_after[1:], jnp.full((1,), nblk, jnp.int32)])
    next_expert = jnp.where(next_start < nblk, block_expert[jnp.minimum(next_start, nblk - 1)], -1)
    return (dest, block_expert, n_used.reshape(1), next_expert.astype(jnp.int32),
            slot_parity.astype(jnp.int32), nblk)


def kernel(x, p, positions, ln_mix, w_in, hg_lb, hg_onorm, w_oA, mla_qa_norm, mla_kva_norm, w_uq,
           w_ukv, q_norm, k_norm, w_oB, w_out, ln_moe, w_rg, b_rg, w_re, b_re, w1, w3, w2, ln_ple,
           w_ple_gate, w_ple_proj):
    b, s, d = x.shape
    t = b * s
    depth = w_in.shape[0]
    lb_all = jnp.cumsum(jax.nn.softmax(hg_lb.astype(F32), axis=1), axis=1)
    pos_t = positions.astype(F32).reshape(1, t)
    freq_t = _rope_table()
    xc = x.reshape(t, d)

    for layer in range(depth):
        wi = w_in[layer]
        n_hg = 3 * HG_F + 2 * HG_V
        n_mla = MLA_Q_RANK + MLA_KV_RANK + MLA_ROPE
        mla_w = -(-n_mla // LANES) * LANES
        whg = wi[:, :n_hg].astype(BF16)
        wmla = jnp.pad(wi[:, n_hg:n_hg + n_mla], ((0, 0), (0, mla_w - n_mla))).astype(BF16)
        wgate = wi[:, n_hg + n_mla:].astype(BF16)
        lb = lb_all[:, layer, :]

        qs, lf, kk, v, og, mla_in, gates = _inproj(
            xc, ln_mix[layer][None, :], whg, wmla, wgate, lb, mla_qa_norm[layer][None, :],
            mla_kva_norm[layer][None, :])

        r3 = lambda a: a.reshape(b, s, a.shape[-1])
        ya = _hgrn(r3(qs), r3(lf), r3(kk), r3(v), r3(og), hg_onorm[layer][None, :]).reshape(t, HG_V)

        kv_w = MLA_NOPE + MLA_VDIM
        wkv = w_ukv[layer].reshape(MLA_KV_RANK, MLA_HEADS, kv_w)
        ckr_w = mla_w - MLA_Q_RANK
        wq = _head_pad(w_uq[layer], MLA_QK).astype(BF16)
        wk_nope = _head_pad(wkv[:, :, :MLA_NOPE].reshape(MLA_KV_RANK, MLA_HEADS * MLA_NOPE), MLA_NOPE)
        place = np.zeros((ckr_w - MLA_KV_RANK, MLA_HEADS, LANES), np.float32)
        for j in range(MLA_ROPE):
            place[j, :, MLA_NOPE + j] = 1.0
        wk = jnp.concatenate([wk_nope, jnp.asarray(place.reshape(ckr_w - MLA_KV_RANK, -1))],
                             axis=0).astype(BF16)
        wv = jnp.pad(wkv[:, :, MLA_NOPE:].reshape(MLA_KV_RANK, MLA_HEADS * MLA_VDIM),
                     ((0, ckr_w - MLA_KV_RANK), (0, 0))).astype(BF16)
        qh_t, kh_t, vh = _mlaprep(mla_in, pos_t, wq.T, wk.T, wv, _gain_t(q_norm[layer], MLA_QK ** -0.5),
                                  _gain_t(k_norm[layer], 1.0), freq_t)
        yb = _attn(qh_t, kh_t, r3(vh)).reshape(t, MLA_HEADS * MLA_VDIM)

        wr = jnp.pad(jnp.concatenate([w_rg[layer], w_re[layer]], axis=1),
                     ((0, 0), (0, LANES - N_GROUPS - N_EXPERTS)))
        wr_hi = wr.astype(BF16)
        wr = jnp.concatenate([wr_hi, (wr - wr_hi.astype(F32)).astype(BF16)], axis=1)
        br = jnp.pad(jnp.concatenate([b_rg[layer], b_re[layer]]),
                     (0, LANES - N_GROUPS - N_EXPERTS))[None, :]
        x1, hm8, route = _merge(xc, ya, yb, gates, w_oA[layer].astype(BF16), w_oB[layer].astype(BF16),
                               w_out[layer].astype(BF16), ln_moe[layer][None, :], wr, br)

        ids = route[:, 0:2].astype(jnp.int32)
        dest, block_expert, n_used, next_expert, slot_parity, nblk = _route_tables(ids)
        xb8 = _dispatch(dest, hm8, jnp.zeros((nblk * MOE_ROWS * TOKEN_ROWS, LANES), jnp.uint32))
        yexp = _experts(block_expert, n_used, next_expert, slot_parity, xb8,
                        w1[layer], w3[layer], w2[layer])

        xc = _ple(dest, x1, route, p[layer].reshape(t, -1), ln_ple[layer][None, :],
                  w_ple_gate[layer].astype(BF16), w_ple_proj[layer].astype(BF16), yexp)
    return xc.reshape(b, s, d)
```

```python
import functools

import numpy as np
import jax
import jax.numpy as jnp
from jax import lax
from jax.experimental import pallas as pl
from jax.experimental.pallas import tpu as pltpu

F32 = jnp.float32
BF16 = jnp.bfloat16

HG_HEADS = 4
HG_KDIM = 128
HG_VDIM = 128
HG_F = HG_HEADS * HG_KDIM
HG_V = HG_HEADS * HG_VDIM
MLA_HEADS = 8
MLA_NOPE = 64
MLA_ROPE = 32
MLA_VDIM = 64
MLA_QK = MLA_NOPE + MLA_ROPE
MLA_Q_RANK = 256
MLA_KV_RANK = 128
ROPE_THETA = 10000.0
N_GROUPS = 8
EXPERTS_PER_GROUP = 8
N_EXPERTS = N_GROUPS * EXPERTS_PER_GROUP
D_EXPERT = 256
EPS = 1e-6

LANES = 128
ROUTE_ROWS = 8
TOKEN_ROWS = 4
VMEM_LIMIT_BYTES = 56 * 1024 * 1024

PLE_ROW_TILE = 256
MERGE_ROW_TILE = 512
INPROJ_ROW_TILE = 256
MLAPREP_ROW_TILE = 512
HG_CHUNK = 64
HG_HEADS_PER_STEP = 2
HG_FAST_CHUNK = 128
HG_CHUNKS_PER_MATMUL = 2
ATTN_Q_TILE = 512
ATTN_HEADS_PER_STEP = 8
MOE_ROWS = 256
DISPATCH_TOKENS = 1024
HG_SAFE_LOG_DECAY = -75.0


def _cparams(sem):
    return pltpu.CompilerParams(dimension_semantics=sem, vmem_limit_bytes=VMEM_LIMIT_BYTES)


def _const_spec(shape):
    nd = len(shape)
    return pl.BlockSpec(shape, lambda *_: (0,) * nd)


def _sigmoid(x):
    return 1.0 / (1.0 + jnp.exp(-x))


def _silu(x):
    return x * _sigmoid(x)


def _rms_scale(x, n):
    return lax.rsqrt(jnp.sum(x * x, axis=-1, keepdims=True) * (1.0 / n) + EPS)


def _dot(a, b):
    return jnp.dot(a, b, preferred_element_type=F32)


def _dot_nt(a, b):
    return lax.dot_general(a, b, (((1,), (1,)), ((), ())), preferred_element_type=F32)


def _dot_tn(a, b):
    return lax.dot_general(a, b, (((0,), (0,)), ((), ())), preferred_element_type=F32)


def _chunk_cumsum(a, direction, chunk):
    n = a.shape[0]
    in_chunk = lax.broadcasted_iota(jnp.int32, a.shape, 0) % chunk
    k = 1
    while k < chunk:
        if direction == 0:
            a = a + jnp.where(in_chunk >= k, pltpu.roll(a, k, 0), 0.0)
        else:
            a = a + jnp.where(in_chunk < chunk - k, pltpu.roll(a, n - k, 0), 0.0)
        k *= 2
    return a


def _inproj_kernel(x_ref, g_ref, whg_ref, wmla_ref, wgate_ref, lb_ref, nq_ref, nkv_ref,
                   qs_ref, lf_ref, kk_ref, v_ref, og_ref, mla_ref, gates_ref):
    x = x_ref[...]
    d = x.shape[-1]
    h = (x * _rms_scale(x, d) * g_ref[...]).astype(BF16)

    qs_ref[...] = _silu(_dot(h, whg_ref[:, 0:HG_F])).astype(BF16)
    for direction in range(2):
        cols = slice(HG_F * (1 + direction), HG_F * (2 + direction))
        out_cols = slice(HG_F * direction, HG_F * (direction + 1))
        lb = lb_ref[direction:direction + 1, :]
        f = lb + (1.0 - lb) * _sigmoid(_dot(h, whg_ref[:, cols]))
        lf_ref[:, out_cols] = jnp.log(f)
        kk_ref[:, out_cols] = (1.0 - f).astype(BF16)
    v_ref[...] = _dot(h, whg_ref[:, 3 * HG_F:3 * HG_F + HG_V]).astype(BF16)
    og_ref[...] = _silu(_dot(h, whg_ref[:, 3 * HG_F + HG_V:3 * HG_F + 2 * HG_V])).astype(BF16)

    zm = _dot(h, wmla_ref[...])
    cq = zm[:, 0:MLA_Q_RANK]
    mla_ref[:, 0:MLA_Q_RANK] = (cq * _rms_scale(cq, MLA_Q_RANK) * nq_ref[...]).astype(BF16)
    ckv = zm[:, MLA_Q_RANK:MLA_Q_RANK + MLA_KV_RANK]
    mla_ref[:, MLA_Q_RANK:MLA_Q_RANK + MLA_KV_RANK] = (
        ckv * _rms_scale(ckv, MLA_KV_RANK) * nkv_ref[...]).astype(BF16)
    mla_ref[:, MLA_Q_RANK + MLA_KV_RANK:] = zm[:, MLA_Q_RANK + MLA_KV_RANK:].astype(BF16)

    for half in range(2):
        cols = slice(d * half, d * (half + 1))
        gates_ref[:, cols] = _sigmoid(_dot(h, wgate_ref[:, cols])).astype(BF16)


def _inproj(x2, ln_mix, whg, wmla, wgate, lb, nq, nkv):
    t, d = x2.shape
    tm = INPROJ_ROW_TILE
    mla_w = wmla.shape[1]
    row = lambda w: pl.BlockSpec((tm, w), lambda i: (i, 0))
    out_shape = (
        jax.ShapeDtypeStruct((t, HG_F), BF16),
        jax.ShapeDtypeStruct((t, 2 * HG_F), F32),
        jax.ShapeDtypeStruct((t, 2 * HG_F), BF16),
        jax.ShapeDtypeStruct((t, HG_V), BF16),
        jax.ShapeDtypeStruct((t, HG_V), BF16),
        jax.ShapeDtypeStruct((t, mla_w), BF16),
        jax.ShapeDtypeStruct((t, 2 * d), BF16),
    )
    return pl.pallas_call(
        _inproj_kernel,
        grid=(t // tm,),
        in_specs=[row(d), _const_spec((1, d)), _const_spec(whg.shape), _const_spec(wmla.shape),
                  _const_spec(wgate.shape), _const_spec(lb.shape), _const_spec(nq.shape),
                  _const_spec(nkv.shape)],
        out_specs=(row(HG_F), row(2 * HG_F), row(2 * HG_F), row(HG_V), row(HG_V), row(mla_w),
                   row(2 * d)),
        out_shape=out_shape,
        compiler_params=_cparams(("parallel",)),
        name="inproj",
    )(x2, ln_mix, whg, wmla, wgate, lb, nq, nkv)


def _hgrn_kernel(qs_ref, lff_ref, lfb_ref, kf_ref, kb_ref, v_ref, og_ref, onorm_ref, out_ref,
                 state_ref, ofw_ref, obw_ref, qt_scr, u_scr, vec_scr, g_scr, k_scr, v_scr):
    c = HG_CHUNK
    s_len = qs_ref.shape[1]
    n_chunks = s_len // c
    hb = HG_HEADS_PER_STEP
    kd = HG_KDIM

    row = lax.broadcasted_iota(jnp.int32, (c, c), 0)
    col = lax.broadcasted_iota(jnp.int32, (c, c), 1)
    masks = (row >= col, row <= col)
    rowc = lax.broadcasted_iota(jnp.int32, (c, kd), 0)
    o_refs = (ofw_ref, obw_ref)

    def load(j):
        chains = []
        for hh in range(hb):
            lanes = slice(hh * kd, (hh + 1) * kd)
            for direction in range(2):
                chunk = j if direction == 0 else n_chunks - 1 - j
                rows = pl.ds(pl.multiple_of(chunk * c, c), c)
                g = _chunk_cumsum((lff_ref, lfb_ref)[direction][0, rows, lanes], direction, c)
                total = g[c - 1:c, :] if direction == 0 else g[0:1, :]
                chains.append(dict(
                    idx=hh * 2 + direction, direction=direction, rows=rows, lanes=lanes, g=g,
                    total=total,
                    q=qs_ref[0, rows, lanes].astype(F32),
                    k=(kf_ref, kb_ref)[direction][0, rows, lanes].astype(F32),
                    v=v_ref[0, rows, lanes]))
        return chains

    def finish(ch, o, kdec_t_v_scaled):
        (ofw_ref, obw_ref)[ch["direction"]][ch["rows"], ch["lanes"]] = o
        state_ref[ch["idx"]] = kdec_t_v_scaled

    def fast(ch):
        g = ch["g"]
        qt = (ch["q"] * jnp.exp(g)).astype(BF16)
        kt = (ch["k"] * jnp.exp(-g)).astype(BF16)
        st = state_ref[ch["idx"]]
        sc = jnp.where(masks[ch["direction"]], _dot_nt(qt, kt), 0.0)
        o = _dot_nt(qt, st.astype(BF16)) + _dot(sc.astype(BF16), ch["v"])
        finish(ch, o, (st + _dot_tn(ch["v"], kt)) * jnp.exp(ch["total"]))

    def robust(ch):
        g = ch["g"]
        direction = ch["direction"]
        st = state_ref[ch["idx"]]
        o0 = _dot_nt((ch["q"] * jnp.exp(g)).astype(BF16), st.astype(BF16))
        slot = ch["idx"]
        g_scr[slot] = g
        k_scr[slot] = ch["k"]
        v_scr[slot] = ch["v"].astype(F32)
        q = ch["q"]

        def body(s, acc):
            g_s = g_scr[slot, pl.ds(s, 1), :]
            seen = (rowc >= s) if direction == 0 else (rowc <= s)
            decay = jnp.where(seen, jnp.exp(jnp.minimum(g - g_s, 0.0)), 0.0)
            a = jnp.sum(q * decay * k_scr[slot, pl.ds(s, 1), :], axis=-1, keepdims=True)
            return acc + a * v_scr[slot, pl.ds(s, 1), :]

        o = lax.fori_loop(0, c, body, o0)
        kdec = (ch["k"] * jnp.exp(ch["total"] - g)).astype(BF16)
        finish(ch, o, st * jnp.exp(ch["total"]) + _dot_tn(ch["v"], kdec))

    def step(j, carry):
        chains = load(j)
        lowest = chains[0]["total"]
        for ch in chains[1:]:
            lowest = jnp.minimum(lowest, ch["total"])
        safe = jnp.min(lowest) >= HG_SAFE_LOG_DECAY

        @pl.when(safe)
        def _():
            for ch in chains:
                fast(ch)

        @pl.when(jnp.logical_not(safe))
        def _():
            for ch in chains:
                robust(ch)

        return carry

    fc = HG_FAST_CHUNK if s_len % (HG_FAST_CHUNK * HG_CHUNKS_PER_MATMUL) == 0 else c
    grp = HG_CHUNKS_PER_MATMUL if (s_len // fc) % HG_CHUNKS_PER_MATMUL == 0 else 1
    gr = grp * fc
    n_groups = s_len // gr
    mid = fc // 2
    grow = lax.broadcasted_iota(jnp.int32, (gr, gr), 0)
    gcol = lax.broadcasted_iota(jnp.int32, (gr, gr), 1)
    same_chunk = (grow // fc) == (gcol // fc)
    gmasks = (same_chunk & (grow >= gcol), same_chunk & (grow <= gcol))
    chunk_of_row = lax.broadcasted_iota(jnp.int32, (gr, kd), 0) // fc

    def block_diag(a):
        return jnp.concatenate([jnp.where(chunk_of_row == i, a, jnp.zeros_like(a))
                                for i in range(grp)], axis=1)

    def phase_a(gi, lowest):
        rows = pl.ds(pl.multiple_of(gi * gr, gr), gr)
        for hh in range(hb):
            lanes = slice(hh * kd, (hh + 1) * kd)
            v = v_ref[0, rows, lanes]
            v_t = v.T
            q = qs_ref[0, rows, lanes].astype(F32)
            for direction in range(2):
                idx = hh * 2 + direction
                g = _chunk_cumsum((lff_ref, lfb_ref)[direction][0, rows, lanes], direction, fc)
                k = (kf_ref, kb_ref)[direction][0, rows, lanes].astype(F32)
                g_mid = [g[i * fc + mid:i * fc + mid + 1, :] for i in range(grp)]
                centred = g - jnp.concatenate([jnp.broadcast_to(r, (fc, kd)) for r in g_mid], axis=0)
                qt = (q * jnp.exp(centred)).astype(BF16)
                kt = (k * jnp.exp(-centred)).astype(BF16)
                qt_scr[idx, rows, :] = qt
                sc = jnp.where(gmasks[direction], _dot_nt(qt, kt), 0.0)
                o_refs[direction][rows, lanes] = _dot(sc.astype(BF16), v)
                u = _dot(v_t, block_diag(kt))
                for i in range(grp):
                    ci = gi * grp + i
                    u_scr[idx, ci] = u[:, i * kd:(i + 1) * kd]
                    edge = i * fc + (fc - 1 if direction == 0 else 0)
                    total = g[edge:edge + 1, :]
                    lowest = jnp.minimum(lowest, jnp.minimum(g_mid[i], total - g_mid[i]))
                    vec_scr[idx, 0, pl.ds(ci, 1), :] = jnp.exp(total)
                    vec_scr[idx, 1, pl.ds(ci, 1), :] = jnp.exp(g_mid[i])
                    vec_scr[idx, 2, pl.ds(ci, 1), :] = jnp.exp(total - g_mid[i])
        return lowest

    lowest = lax.fori_loop(0, n_groups, phase_a, jnp.zeros((1, kd), F32))
    all_safe = jnp.min(lowest) >= HG_SAFE_LOG_DECAY
    state_ref[...] = jnp.zeros_like(state_ref)

    @pl.when(all_safe)
    def _():
        def phase_b(j, carry):
            for hh in range(hb):
                lanes = slice(hh * kd, (hh + 1) * kd)
                for direction in range(2):
                    idx = hh * 2 + direction
                    gi = j if direction == 0 else n_groups - 1 - j
                    rows = pl.ds(pl.multiple_of(gi * gr, gr), gr)
                    st = state_ref[idx]
                    seen = [None] * grp
                    for i in (range(grp) if direction == 0 else reversed(range(grp))):
                        ci = gi * grp + i
                        at = pl.ds(ci, 1)
                        seen[i] = (st * vec_scr[idx, 1, at, :]).astype(BF16)
                        st = st * vec_scr[idx, 0, at, :] + u_scr[idx, ci] * vec_scr[idx, 2, at, :]
                    state_ref[idx] = st
                    o_refs[direction][rows, lanes] += _dot_nt(
                        block_diag(qt_scr[idx, rows, :]), jnp.concatenate(seen, axis=1))
            return carry

        lax.fori_loop(0, n_groups, phase_b, 0)

    @pl.when(jnp.logical_not(all_safe))
    def _():
        lax.fori_loop(0, n_chunks, step, 0)

    blk = 256 if s_len % 256 == 0 else c

    def epilogue(i, carry):
        rows = pl.ds(pl.multiple_of(i * blk, blk), blk)
        for hh in range(hb):
            lanes = slice(hh * kd, (hh + 1) * kd)
            o = ofw_ref[rows, lanes] + obw_ref[rows, lanes]
            y = o * _rms_scale(o, HG_VDIM) * onorm_ref[...]
            out_ref[0, rows, lanes] = (y * og_ref[0, rows, lanes].astype(F32)).astype(BF16)
        return carry

    lax.fori_loop(0, s_len // blk, epilogue, 0)


def _hgrn(qs, lf, kk, v, og, onorm):
    b, s, _ = qs.shape
    hb = HG_HEADS_PER_STEP
    w = hb * HG_KDIM
    n_hsteps = HG_HEADS // hb
    fwd = pl.BlockSpec((1, s, w), lambda bi, hi: (bi, 0, hi))
    bwd = pl.BlockSpec((1, s, w), lambda bi, hi: (bi, 0, n_hsteps + hi))
    return pl.pallas_call(
        _hgrn_kernel,
        grid=(b, n_hsteps),
        in_specs=[fwd, fwd, bwd, fwd, bwd, fwd, fwd, _const_spec(onorm.shape)],
        out_specs=fwd,
        out_shape=jax.ShapeDtypeStruct((b, s, HG_V), BF16),
        scratch_shapes=[
            pltpu.VMEM((2 * hb, HG_VDIM, HG_KDIM), F32),
            pltpu.VMEM((s, w), F32),
            pltpu.VMEM((s, w), F32),
            pltpu.VMEM((2 * hb, s, HG_KDIM), BF16),
            pltpu.VMEM((2 * hb, s // HG_CHUNK, HG_VDIM, HG_KDIM), F32),
            pltpu.VMEM((2 * hb, 3, max(s // HG_CHUNK, 8), HG_KDIM), F32),
            pltpu.VMEM((2 * hb, HG_CHUNK, HG_KDIM), F32),
            pltpu.VMEM((2 * hb, HG_CHUNK, HG_KDIM), F32),
            pltpu.VMEM((2 * hb, HG_CHUNK, HG_VDIM), F32),
        ],
        compiler_params=_cparams(("parallel", "parallel")),
        name="hgrn",
    )(qs, lf, lf, kk, kk, v, og, onorm)


def _mlaprep_kernel(mla_ref, pos_t_ref, wq_t_ref, wk_t_ref, wv_ref, gq_t_ref, gk_t_ref, freq_t_ref,
                    q_t_ref, k_t_ref, v_ref):
    cq = mla_ref[:, 0:MLA_Q_RANK]
    ckr = mla_ref[:, MLA_Q_RANK:]
    half = MLA_ROPE // 2
    tm = mla_ref.shape[0]

    lane_tile = lambda a: jnp.concatenate([a] * (tm // LANES), axis=1)
    ang_t = lane_tile(freq_t_ref[...]) * pos_t_ref[...]
    cos_t = jnp.cos(ang_t)
    sin_t = jnp.sin(ang_t)

    def norm_rope_t(x_t, gain_ref, out_ref):
        gain_t = lane_tile(gain_ref[...])
        for hd in range(MLA_HEADS):
            r0 = hd * LANES
            x = x_t[r0:r0 + LANES, :]
            y = x * lax.rsqrt(jnp.sum(x * x, axis=0, keepdims=True) * (1.0 / MLA_QK) + EPS) * gain_t
            x1 = y[MLA_NOPE:MLA_NOPE + half, :]
            x2 = y[MLA_NOPE + half:MLA_QK, :]
            out_ref[r0:r0 + MLA_NOPE, :] = y[0:MLA_NOPE, :].astype(BF16)
            out_ref[r0 + MLA_NOPE:r0 + MLA_NOPE + half, :] = (x1 * cos_t - x2 * sin_t).astype(BF16)
            out_ref[r0 + MLA_NOPE + half:r0 + MLA_QK, :] = (x2 * cos_t + x1 * sin_t).astype(BF16)
            out_ref[r0 + MLA_QK:r0 + LANES, :] = jnp.zeros((LANES - MLA_QK, tm), BF16)

    norm_rope_t(_dot_nt(wq_t_ref[...], cq), gq_t_ref, q_t_ref)
    norm_rope_t(_dot_nt(wk_t_ref[...], ckr), gk_t_ref, k_t_ref)
    v_ref[...] = _dot(ckr, wv_ref[...]).astype(BF16)


def _mlaprep(mla_in, pos_t, wq_t, wk_t, wv, gq_t, gk_t, freq_t):
    t = mla_in.shape[0]
    tm = MLAPREP_ROW_TILE
    row = lambda w: pl.BlockSpec((tm, w), lambda i: (i, 0))
    col = lambda h: pl.BlockSpec((h, tm), lambda i: (0, i))
    hw = MLA_HEADS * LANES
    vw = MLA_HEADS * MLA_VDIM
    return pl.pallas_call(
        _mlaprep_kernel,
        grid=(t // tm,),
        in_specs=[row(mla_in.shape[1]), col(1), _const_spec(wq_t.shape), _const_spec(wk_t.shape),
                  _const_spec(wv.shape), _const_spec(gq_t.shape), _const_spec(gk_t.shape),
                  _const_spec(freq_t.shape)],
        out_specs=(col(hw), col(hw), row(vw)),
        out_shape=(jax.ShapeDtypeStruct((hw, t), BF16), jax.ShapeDtypeStruct((hw, t), BF16),
                   jax.ShapeDtypeStruct((t, vw), BF16)),
        compiler_params=_cparams(("parallel",)),
        name="mlaprep",
    )(mla_in, pos_t, wq_t, wk_t, wv, gq_t, gk_t, freq_t)


def _attn_kernel(q_t_ref, k_t_ref, v_ref, o_ref):
    for j in range(ATTN_HEADS_PER_STEP):
        rows = slice(j * LANES, (j + 1) * LANES)
        s = _dot_tn(q_t_ref[rows, :], k_t_ref[rows, :])
        p = jnp.exp(s - jnp.max(s, axis=-1, keepdims=True))
        l = jnp.sum(p, axis=-1, keepdims=True)
        vl = slice(j * MLA_VDIM, (j + 1) * MLA_VDIM)
        o = _dot(p.astype(BF16), v_ref[0, :, vl])
        o_ref[0, :, vl] = (o / l).astype(BF16)


def _attn(q_t, k_t, v):
    b, s, _ = v.shape
    tq = min(ATTN_Q_TILE, s)
    nh = ATTN_HEADS_PER_STEP
    n_q = s // tq
    return pl.pallas_call(
        _attn_kernel,
        grid=(b, MLA_HEADS // nh, n_q),
        in_specs=[pl.BlockSpec((nh * LANES, tq), lambda bi, hp, i: (hp, bi * n_q + i)),
                  pl.BlockSpec((nh * LANES, s), lambda bi, hp, i: (hp, bi)),
                  pl.BlockSpec((1, s, nh * MLA_VDIM), lambda bi, hp, i: (bi, 0, hp))],
        out_specs=pl.BlockSpec((1, tq, nh * MLA_VDIM), lambda bi, hp, i: (bi, i, hp)),
        out_shape=jax.ShapeDtypeStruct((b, s, MLA_HEADS * MLA_VDIM), BF16),
        compiler_params=_cparams(("parallel", "parallel", "arbitrary")),
        name="attn",
    )(q_t, k_t, v)


def _merge_kernel(x_ref, ya_ref, yb_ref, gates_ref, woa_ref, wob_ref, wout_ref, lnm_ref, wr_t_ref,
                  br_t_ref, x1_ref, hm8_ref, route_t_ref):
    d = x_ref.shape[-1]
    y_a = _dot(ya_ref[...], woa_ref[...])
    y_b = _dot(yb_ref[...], wob_ref[...])
    merged = gates_ref[:, 0:d].astype(F32) * y_a + gates_ref[:, d:2 * d].astype(F32) * y_b
    x1 = x_ref[...] + _dot(merged.astype(BF16), wout_ref[...])
    x1_ref[...] = x1
    hm = x1 * _rms_scale(x1, d) * lnm_ref[...]
    _store_token_tiles(hm8_ref, 0, hm)

    tm = hm.shape[0]
    hm_hi = hm.astype(BF16)
    hm_lo = (hm - hm_hi.astype(F32)).astype(BF16)
    hh = _dot_nt(wr_t_ref[...], hm_hi)
    bias = jnp.concatenate([br_t_ref[...]] * (tm // LANES), axis=1)
    logits = hh[0:LANES] + hh[LANES:2 * LANES] + _dot_nt(wr_t_ref[0:LANES, :], hm_lo) + bias
    neg = -jnp.inf
    gl = logits[0:N_GROUPS]
    gmax = jnp.max(gl, axis=0, keepdims=True)
    p_group = 1.0 / jnp.sum(jnp.exp(gl - gmax), axis=0, keepdims=True)
    g_row = lax.broadcasted_iota(jnp.int32, gl.shape, 0)
    g_sel = jnp.min(jnp.where(gl == gmax, g_row, N_GROUPS), axis=0, keepdims=True)
    el_all = logits[N_GROUPS:N_GROUPS + N_EXPERTS]
    e_row = lax.broadcasted_iota(jnp.int32, el_all.shape, 0)
    el = jnp.where(e_row // EXPERTS_PER_GROUP == g_sel, el_all, neg)
    v1 = jnp.max(el, axis=0, keepdims=True)
    i1 = jnp.min(jnp.where(el == v1, e_row, N_EXPERTS), axis=0, keepdims=True)
    el2 = jnp.where(e_row == i1, neg, el)
    v2 = jnp.max(el2, axis=0, keepdims=True)
    i2 = jnp.min(jnp.where(el2 == v2, e_row, N_EXPERTS), axis=0, keepdims=True)
    e21 = jnp.exp(v2 - v1)
    w1 = p_group / (1.0 + e21)
    w2 = w1 * e21
    out_row = lax.broadcasted_iota(jnp.int32, route_t_ref.shape, 0)
    route_t_ref[...] = jnp.where(out_row == 0, i1.astype(F32),
                                 jnp.where(out_row == 1, i2.astype(F32),
                                           jnp.where(out_row == 2, w1,
                                                     jnp.where(out_row == 3, w2, 0.0))))


def _merge(x2, ya, yb, gates, woa, wob, wout, lnm, wr, br):
    t, d = x2.shape
    tm = MERGE_ROW_TILE
    row = lambda w: pl.BlockSpec((tm, w), lambda i: (i, 0))
    return pl.pallas_call(
        _merge_kernel,
        grid=(t // tm,),
        in_specs=[row(d), row(ya.shape[1]), row(yb.shape[1]), row(2 * d), _const_spec(woa.shape),
                  _const_spec(wob.shape), _const_spec(wout.shape), _const_spec(lnm.shape),
                  _const_spec(wr.shape), _const_spec(br.shape)],
        out_specs=(row(d), pl.BlockSpec((tm * TOKEN_ROWS, LANES), lambda i: (i, 0)),
                   pl.BlockSpec((ROUTE_ROWS, tm), lambda i: (0, i))),
        out_shape=(jax.ShapeDtypeStruct((t, d), F32),
                   jax.ShapeDtypeStruct((t * TOKEN_ROWS, LANES), jnp.uint32),
                   jax.ShapeDtypeStruct((ROUTE_ROWS, t), F32)),
        compiler_params=_cparams(("parallel",)),
        name="merge",
    )(x2, ya, yb, gates, woa, wob, wout, lnm, wr, br)


_HIGH_HALF = 0xFFFF0000


def _store_token_tiles(ref4, tok0, val):
    n, width = val.shape
    as_bits = lambda a: lax.bitcast_convert_type(a.astype(BF16).astype(F32), jnp.uint32)
    words = (as_bits(val[:, :width // 2]) >> 16) | (as_bits(val[:, width // 2:]) & jnp.uint32(_HIGH_HALF))
    for j in range(TOKEN_ROWS):
        ref4[pl.ds(tok0 * TOKEN_ROWS + j, n, stride=TOKEN_ROWS), :] = words[:, j * LANES:(j + 1) * LANES]


def _load_token_tiles(ref4, tok0, n):
    words = jnp.concatenate(
        [ref4[pl.ds(tok0 * TOKEN_ROWS + j, n, stride=TOKEN_ROWS), :] for j in range(TOKEN_ROWS)],
        axis=1)
    low = lax.bitcast_convert_type(words << 16, F32).astype(BF16)
    high = lax.bitcast_convert_type(words & jnp.uint32(_HIGH_HALF), F32).astype(BF16)
    return jnp.concatenate([low, high], axis=1)


def _token_tile(ref8, tok):
    return ref8.at[pl.ds(pl.multiple_of(tok * TOKEN_ROWS, TOKEN_ROWS), TOKEN_ROWS)]


def _dispatch_kernel(dest_ref, hm8_ref, xb_in_ref, xb8_ref, sem):
    del xb_in_ref
    n_tok = DISPATCH_TOKENS
    for r in range(n_tok):
        src = hm8_ref.at[pl.ds(r * TOKEN_ROWS, TOKEN_ROWS)]
        for slot in range(2):
            pltpu.make_async_copy(src, _token_tile(xb8_ref, dest_ref[2 * r + slot]),
                                  sem).start(priority=slot)
    for _ in range(2):
        pltpu.make_async_copy(hm8_ref, xb8_ref.at[pl.ds(0, n_tok * TOKEN_ROWS)], sem).wait()


def _dispatch(dest_flat, hm8, xb_init):
    t = hm8.shape[0] // TOKEN_ROWS
    return pl.pallas_call(
        _dispatch_kernel,
        grid=(t // DISPATCH_TOKENS,),
        in_specs=[pl.BlockSpec((2 * DISPATCH_TOKENS,), lambda i: (i,), memory_space=pltpu.SMEM),
                  pl.BlockSpec((DISPATCH_TOKENS * TOKEN_ROWS, LANES), lambda i: (i, 0)),
                  pl.BlockSpec(memory_space=pl.ANY)],
        out_specs=pl.BlockSpec(memory_space=pl.ANY),
        out_shape=jax.ShapeDtypeStruct(xb_init.shape, xb_init.dtype),
        scratch_shapes=[pltpu.SemaphoreType.DMA(())],
        input_output_aliases={2: 0},
        compiler_params=_cparams(("arbitrary",)),
        name="dispatch",
    )(dest_flat, hm8, xb_init)


def _experts_kernel(be_ref, nu_ref, nxt_ref, par_ref, xb8_ref, w1_hbm, w3_hbm, w2_hbm, yb8_ref,
                    w1f, w3f, w2f, sems, w1c, w3c, w2c):
    j = pl.program_id(0)

    def fetch(e, slot):
        return [pltpu.make_async_copy(src.at[e], dst.at[slot], sems.at[slot])
                for src, dst in ((w1_hbm, w1f), (w3_hbm, w3f), (w2_hbm, w2f))]

    for half in range(2):
        blk = 2 * j + half
        used = blk < nu_ref[0]
        new_expert = jnp.logical_or(blk == 0, be_ref[blk] != be_ref[jnp.maximum(blk - 1, 0)])
        tok0 = half * MOE_ROWS

        @pl.when(jnp.logical_and(used, new_expert))
        def _():
            slot = par_ref[blk]

            @pl.when(blk == 0)
            def _():
                for cp in fetch(be_ref[0], 0):
                    cp.start()

            for cp in fetch(be_ref[blk], slot):
                cp.wait()

            @pl.when(nxt_ref[blk] >= 0)
            def _():
                for cp in fetch(nxt_ref[blk], 1 - slot):
                    cp.start()

            w1c[...] = w1f[slot].astype(BF16)
            w3c[...] = w3f[slot].astype(BF16)
            w2c[...] = w2f[slot].astype(BF16)

        @pl.when(used)
        def _():
            x = _load_token_tiles(xb8_ref, tok0, MOE_ROWS)
            hmid = (_silu(_dot(x, w1c[...])) * _dot(x, w3c[...])).astype(BF16)
            _store_token_tiles(yb8_ref, tok0, _dot(hmid, w2c[...]))

        @pl.when(jnp.logical_not(used))
        def _():
            rows = pl.ds(tok0 * TOKEN_ROWS, MOE_ROWS * TOKEN_ROWS)
            yb8_ref[rows, :] = jnp.zeros((MOE_ROWS * TOKEN_ROWS, LANES), jnp.uint32)


def _experts(block_expert, n_used, next_expert, slot_parity, xb8, w1, w3, w2):
    nblk = block_expert.shape[0]
    nsteps = nblk // 2
    d = w1.shape[1]
    de = w1.shape[2]
    step_rows = 2 * MOE_ROWS * TOKEN_ROWS
    rows_in = lambda j, be, nu, nx, pa: (jnp.minimum(j, (nu[0] - 1) // 2), 0)
    hbm = pl.BlockSpec(memory_space=pl.ANY)
    return pl.pallas_call(
        _experts_kernel,
        grid_spec=pltpu.PrefetchScalarGridSpec(
            num_scalar_prefetch=4,
            grid=(nsteps,),
            in_specs=[pl.BlockSpec((step_rows, LANES), rows_in), hbm, hbm, hbm],
            out_specs=pl.BlockSpec((step_rows, LANES), lambda j, be, nu, nx, pa: (j, 0)),
            scratch_shapes=[pltpu.VMEM((2, d, de), F32), pltpu.VMEM((2, d, de), F32),
                            pltpu.VMEM((2, de, d), F32), pltpu.SemaphoreType.DMA((2,)),
                            pltpu.VMEM((d, de), BF16), pltpu.VMEM((d, de), BF16),
                            pltpu.VMEM((de, d), BF16)]),
        out_shape=jax.ShapeDtypeStruct((nblk * MOE_ROWS * TOKEN_ROWS, LANES), jnp.uint32),
        compiler_params=_cparams(("arbitrary",)),
        name="experts",
    )(block_expert, n_used, next_expert, slot_parity, xb8, w1, w3, w2)


def _ple_kernel(dest_ref, destn_ref, x1_ref, route_ref, p_ref, ln_ref, wg_ref, wp_ref, yb8_ref,
                out_ref, ya0, ya1, yb0, yb1, sems):
    j = pl.program_id(0)
    last = pl.num_programs(0) - 1
    tm = PLE_ROW_TILE
    d = x1_ref.shape[1]

    def start(idx_ref, base, bufs, sem):
        for r in range(tm):
            for slot in range(2):
                pltpu.make_async_copy(_token_tile(yb8_ref, idx_ref[base + 2 * r + slot]),
                                      bufs[slot].at[pl.ds(r * TOKEN_ROWS, TOKEN_ROWS)],
                                      sem).start(priority=slot)

    def wait(bufs, sem):
        for buf in bufs:
            pltpu.make_async_copy(yb8_ref.at[pl.ds(0, tm * TOKEN_ROWS)], buf, sem).wait()

    def compute(bufs, rows):
        x2 = (x1_ref[rows, :]
              + route_ref[rows, 0:1] * _load_token_tiles(bufs[0], 0, tm).astype(F32)
              + route_ref[rows, 1:2] * _load_token_tiles(bufs[1], 0, tm).astype(F32))
        h = (x2 * _rms_scale(x2, d) * ln_ref[...]).astype(BF16)
        pp = _dot(p_ref[rows, :].astype(BF16), wp_ref[...])
        out_ref[rows, :] = x2 + pp * _sigmoid(_dot(h, wg_ref[...]))

    @pl.when(j == 0)
    def _():
        start(dest_ref, 0, (ya0, ya1), sems.at[0])

    wait((ya0, ya1), sems.at[0])
    start(dest_ref, 2 * tm, (yb0, yb1), sems.at[1])
    compute((ya0, ya1), slice(0, tm))

    wait((yb0, yb1), sems.at[1])
    start(destn_ref, 0, (ya0, ya1), sems.at[0])
    compute((yb0, yb1), slice(tm, 2 * tm))

    @pl.when(j == last)
    def _():
        wait((ya0, ya1), sems.at[0])


def _ple(dest_flat, x1, route, p2, ln, wg, wp, yb):
    t, d = x1.shape
    tm = PLE_ROW_TILE
    nsteps = t // (2 * tm)
    row = lambda w: pl.BlockSpec((2 * tm, w), lambda j: (j, 0))
    return pl.pallas_call(
        _ple_kernel,
        grid=(nsteps,),
        in_specs=[pl.BlockSpec((4 * tm,), lambda j: (j,), memory_space=pltpu.SMEM),
                  pl.BlockSpec((4 * tm,), lambda j: (jnp.minimum(j + 1, nsteps - 1),),
                               memory_space=pltpu.SMEM),
                  row(d), row(route.shape[1]), row(p2.shape[1]), _const_spec(ln.shape),
                  _const_spec(wg.shape), _const_spec(wp.shape),
                  pl.BlockSpec(memory_space=pl.ANY)],
        out_specs=row(d),
        out_shape=jax.ShapeDtypeStruct((t, d), F32),
        scratch_shapes=([pltpu.VMEM((tm * TOKEN_ROWS, LANES), jnp.uint32)] * 4
                        + [pltpu.SemaphoreType.DMA((2,))]),
        compiler_params=_cparams(("arbitrary",)),
        name="ple",
    )(dest_flat, dest_flat, x1, route, p2, ln, wg, wp, yb)


def _head_pad(w, width):
    r = w.shape[0]
    w = w.reshape(r, MLA_HEADS, width)
    return jnp.pad(w, ((0, 0), (0, 0), (0, LANES - width))).reshape(r, MLA_HEADS * LANES)


def _rope_table():
    half = MLA_ROPE // 2
    inv_freq = ROPE_THETA ** (-np.arange(half, dtype=np.float32) / half)
    return jnp.asarray(np.broadcast_to(inv_freq[:, None], (half, LANES)))


def _gain_t(gain, scale):
    return jnp.broadcast_to((jnp.pad(gain, (0, LANES - MLA_QK)) * scale)[:, None], (LANES, LANES))


def _route_tables(ids):
    a = ids.shape[0] * 2
    e = ids.reshape(a)
    onehot = (e[:, None] == jnp.arange(N_EXPERTS, dtype=jnp.int32)[None, :]).astype(jnp.int32)
    csum = jnp.cumsum(onehot, axis=0)
    counts = csum[-1]
    rank = jnp.sum(csum * onehot, axis=1) - 1
    pcounts = (counts + MOE_ROWS - 1) // MOE_ROWS * MOE_ROWS
    pends = jnp.cumsum(pcounts)
    pstarts = pends - pcounts
    dest = (jnp.sum(onehot * pstarts[None, :], axis=1) + rank).astype(jnp.int32)
    nblk = (a + N_EXPERTS * (MOE_ROWS - 1)) // MOE_ROWS
    nblk += nblk % 2
    n_used = (pends[-1] // MOE_ROWS).astype(jnp.int32)
    blk_start = jnp.arange(nblk, dtype=jnp.int32) * MOE_ROWS
    blk_start = jnp.minimum(blk_start, pends[-1] - 1)
    block_expert = jnp.sum((pends[None, :] <= blk_start[:, None]).astype(jnp.int32), axis=1)
    block_expert = jnp.minimum(block_expert, N_EXPERTS - 1).astype(jnp.int32)
    block_expert, n_used = lax.optimization_barrier((block_expert, n_used))
    blk = jnp.arange(nblk, dtype=jnp.int32)
    prev = jnp.concatenate([jnp.full((1,), -1, jnp.int32), block_expert[:-1]])
    starts_expert = (blk < n_used) & (block_expert != prev)
    slot_parity = (jnp.cumsum(starts_expert.astype(jnp.int32)) - 1) % 2
    start_at_or_after = jnp.flip(lax.cummin(jnp.flip(jnp.where(starts_expert, blk, nblk))))
    next_start = jnp.concatenate([start_at_or_after[1:], jnp.full((1,), nblk, jnp.int32)])
    next_expert = jnp.where(next_start < nblk, block_expert[jnp.minimum(next_start, nblk - 1)], -1)
    return (dest, block_expert, n_used.reshape(1), next_expert.astype(jnp.int32),
            slot_parity.astype(jnp.int32), nblk)


def kernel(x, p, positions, ln_mix, w_in, hg_lb, hg_onorm, w_oA, mla_qa_norm, mla_kva_norm, w_uq,
           w_ukv, q_norm, k_norm, w_oB, w_out, ln_moe, w_rg, b_rg, w_re, b_re, w1, w3, w2, ln_ple,
           w_ple_gate, w_ple_proj):
    b, s, d = x.shape
    t = b * s
    depth = w_in.shape[0]
    lb_all = jnp.cumsum(jax.nn.softmax(hg_lb.astype(F32), axis=1), axis=1)
    pos_t = positions.astype(F32).reshape(1, t)
    freq_t = _rope_table()
    xc = x.reshape(t, d)

    for layer in range(depth):
        wi = w_in[layer]
        n_hg = 3 * HG_F + 2 * HG_V
        n_mla = MLA_Q_RANK + MLA_KV_RANK + MLA_ROPE
        mla_w = -(-n_mla // LANES) * LANES
        whg = wi[:, :n_hg].astype(BF16)
        wmla = jnp.pad(wi[:, n_hg:n_hg + n_mla], ((0, 0), (0, mla_w - n_mla))).astype(BF16)
        wgate = wi[:, n_hg + n_mla:].astype(BF16)
        lb = lb_all[:, layer, :]

        qs, lf, kk, v, og, mla_in, gates = _inproj(
            xc, ln_mix[layer][None, :], whg, wmla, wgate, lb, mla_qa_norm[layer][None, :],
            mla_kva_norm[layer][None, :])

        r3 = lambda a: a.reshape(b, s, a.shape[-1])
        ya = _hgrn(r3(qs), r3(lf), r3(kk), r3(v), r3(og), hg_onorm[layer][None, :]).reshape(t, HG_V)

        kv_w = MLA_NOPE + MLA_VDIM
        wkv = w_ukv[layer].reshape(MLA_KV_RANK, MLA_HEADS, kv_w)
        ckr_w = mla_w - MLA_Q_RANK
        wq = _head_pad(w_uq[layer], MLA_QK).astype(BF16)
        wk_nope = _head_pad(wkv[:, :, :MLA_NOPE].reshape(MLA_KV_RANK, MLA_HEADS * MLA_NOPE), MLA_NOPE)
        place = np.zeros((ckr_w - MLA_KV_RANK, MLA_HEADS, LANES), np.float32)
        for j in range(MLA_ROPE):
            place[j, :, MLA_NOPE + j] = 1.0
        wk = jnp.concatenate([wk_nope, jnp.asarray(place.reshape(ckr_w - MLA_KV_RANK, -1))],
                             axis=0).astype(BF16)
        wv = jnp.pad(wkv[:, :, MLA_NOPE:].reshape(MLA_KV_RANK, MLA_HEADS * MLA_VDIM),
                     ((0, ckr_w - MLA_KV_RANK), (0, 0))).astype(BF16)
        qh_t, kh_t, vh = _mlaprep(mla_in, pos_t, wq.T, wk.T, wv, _gain_t(q_norm[layer], MLA_QK ** -0.5),
                                  _gain_t(k_norm[layer], 1.0), freq_t)
        yb = _attn(qh_t, kh_t, r3(vh)).reshape(t, MLA_HEADS * MLA_VDIM)

        wr = jnp.pad(jnp.concatenate([w_rg[layer], w_re[layer]], axis=1),
                     ((0, 0), (0, LANES - N_GROUPS - N_EXPERTS)))
        wr_hi = wr.astype(BF16)
        wr_t = jnp.concatenate([wr_hi, (wr - wr_hi.astype(F32)).astype(BF16)], axis=1).T
        br_t = jnp.broadcast_to(jnp.pad(jnp.concatenate([b_rg[layer], b_re[layer]]),
                                        (0, LANES - N_GROUPS - N_EXPERTS))[:, None], (LANES, LANES))
        x1, hm8, route_t = _merge(xc, ya, yb, gates, w_oA[layer].astype(BF16),
                                  w_oB[layer].astype(BF16), w_out[layer].astype(BF16),
                                  ln_moe[layer][None, :], wr_t, br_t)

        ids = route_t[0:2].astype(jnp.int32).T
        route = route_t[2:4].T
        dest, block_expert, n_used, next_expert, slot_parity, nblk = _route_tables(ids)
        xb8 = _dispatch(dest, hm8, jnp.zeros((nblk * MOE_ROWS * TOKEN_ROWS, LANES), jnp.uint32))
        yexp = _experts(block_expert, n_used, next_expert, slot_parity, xb8,
                        w1[layer], w3[layer], w2[layer])

        xc = _ple(dest, x1, route, p[layer].reshape(t, -1), ln_ple[layer][None, :],
                  w_ple_gate[layer].astype(BF16), w_ple_proj[layer].astype(BF16), yexp)
    return xc.reshape(b, s, d)
```

```python
import functools

import numpy as np
import jax
import jax.numpy as jnp
from jax import lax
from jax.experimental import pallas as pl
from jax.experimental.pallas import tpu as pltpu

F32 = jnp.float32
BF16 = jnp.bfloat16

HG_HEADS = 4
HG_KDIM = 128
HG_VDIM = 128
HG_F = HG_HEADS * HG_KDIM
HG_V = HG_HEADS * HG_VDIM
MLA_HEADS = 8
MLA_NOPE = 64
MLA_ROPE = 32
MLA_VDIM = 64
MLA_QK = MLA_NOPE + MLA_ROPE
MLA_Q_RANK = 256
MLA_KV_RANK = 128
ROPE_THETA = 10000.0
N_GROUPS = 8
EXPERTS_PER_GROUP = 8
N_EXPERTS = N_GROUPS * EXPERTS_PER_GROUP
D_EXPERT = 256
EPS = 1e-6

LANES = 128
ROUTE_ROWS = 8
TOKEN_ROWS = 4
VMEM_LIMIT_BYTES = 56 * 1024 * 1024

PLE_ROW_TILE = 256
MERGE_ROW_TILE = 512
INPROJ_ROW_TILE = 256
MLAPREP_ROW_TILE = 512
HG_CHUNK = 64
HG_HEADS_PER_STEP = 2
HG_FAST_CHUNK = 128
HG_CHUNKS_PER_MATMUL = 2
ATTN_Q_TILE = 512
ATTN_HEADS_PER_STEP = 8
MOE_ROWS = 256
DISPATCH_TOKENS = 1024
HG_SAFE_LOG_DECAY = -75.0


def _cparams(sem):
    return pltpu.CompilerParams(dimension_semantics=sem, vmem_limit_bytes=VMEM_LIMIT_BYTES)


def _const_spec(shape):
    nd = len(shape)
    return pl.BlockSpec(shape, lambda *_: (0,) * nd)


def _sigmoid(x):
    return 1.0 / (1.0 + jnp.exp(-x))


def _silu(x):
    return x * _sigmoid(x)


def _rms_scale(x, n):
    return lax.rsqrt(jnp.sum(x * x, axis=-1, keepdims=True) * (1.0 / n) + EPS)


def _dot(a, b):
    return jnp.dot(a, b, preferred_element_type=F32)


def _dot_nt(a, b):
    return lax.dot_general(a, b, (((1,), (1,)), ((), ())), preferred_element_type=F32)


def _dot_tn(a, b):
    return lax.dot_general(a, b, (((0,), (0,)), ((), ())), preferred_element_type=F32)


def _chunk_cumsum(a, direction, chunk):
    n = a.shape[0]
    in_chunk = lax.broadcasted_iota(jnp.int32, a.shape, 0) % chunk
    k = 1
    while k < chunk:
        if direction == 0:
            a = a + jnp.where(in_chunk >= k, pltpu.roll(a, k, 0), 0.0)
        else:
            a = a + jnp.where(in_chunk < chunk - k, pltpu.roll(a, n - k, 0), 0.0)
        k *= 2
    return a


def _inproj_kernel(x_ref, g_ref, whg_ref, wmla_ref, wgate_ref, lb_ref, nq_ref, nkv_ref,
                   qs_ref, lf_ref, kk_ref, v_ref, og_ref, mla_ref, gates_ref):
    x = x_ref[...]
    d = x.shape[-1]
    h = (x * _rms_scale(x, d) * g_ref[...]).astype(BF16)

    qs_ref[...] = _silu(_dot(h, whg_ref[:, 0:HG_F])).astype(BF16)
    for direction in range(2):
        cols = slice(HG_F * (1 + direction), HG_F * (2 + direction))
        out_cols = slice(HG_F * direction, HG_F * (direction + 1))
        lb = lb_ref[direction:direction + 1, :]
        f = lb + (1.0 - lb) * _sigmoid(_dot(h, whg_ref[:, cols]))
        lf_ref[:, out_cols] = jnp.log(f)
        kk_ref[:, out_cols] = (1.0 - f).astype(BF16)
    v_ref[...] = _dot(h, whg_ref[:, 3 * HG_F:3 * HG_F + HG_V]).astype(BF16)
    og_ref[...] = _silu(_dot(h, whg_ref[:, 3 * HG_F + HG_V:3 * HG_F + 2 * HG_V])).astype(BF16)

    zm = _dot(h, wmla_ref[...])
    cq = zm[:, 0:MLA_Q_RANK]
    mla_ref[:, 0:MLA_Q_RANK] = (cq * _rms_scale(cq, MLA_Q_RANK) * nq_ref[...]).astype(BF16)
    ckv = zm[:, MLA_Q_RANK:MLA_Q_RANK + MLA_KV_RANK]
    mla_ref[:, MLA_Q_RANK:MLA_Q_RANK + MLA_KV_RANK] = (
        ckv * _rms_scale(ckv, MLA_KV_RANK) * nkv_ref[...]).astype(BF16)
    mla_ref[:, MLA_Q_RANK + MLA_KV_RANK:] = zm[:, MLA_Q_RANK + MLA_KV_RANK:].astype(BF16)

    for half in range(2):
        cols = slice(d * half, d * (half + 1))
        gates_ref[:, cols] = _sigmoid(_dot(h, wgate_ref[:, cols])).astype(BF16)


def _inproj(x2, ln_mix, whg, wmla, wgate, lb, nq, nkv):
    t, d = x2.shape
    tm = INPROJ_ROW_TILE
    mla_w = wmla.shape[1]
    row = lambda w: pl.BlockSpec((tm, w), lambda i: (i, 0))
    out_shape = (
        jax.ShapeDtypeStruct((t, HG_F), BF16),
        jax.ShapeDtypeStruct((t, 2 * HG_F), F32),
        jax.ShapeDtypeStruct((t, 2 * HG_F), BF16),
        jax.ShapeDtypeStruct((t, HG_V), BF16),
        jax.ShapeDtypeStruct((t, HG_V), BF16),
        jax.ShapeDtypeStruct((t, mla_w), BF16),
        jax.ShapeDtypeStruct((t, 2 * d), BF16),
    )
    return pl.pallas_call(
        _inproj_kernel,
        grid=(t // tm,),
        in_specs=[row(d), _const_spec((1, d)), _const_spec(whg.shape), _const_spec(wmla.shape),
                  _const_spec(wgate.shape), _const_spec(lb.shape), _const_spec(nq.shape),
                  _const_spec(nkv.shape)],
        out_specs=(row(HG_F), row(2 * HG_F), row(2 * HG_F), row(HG_V), row(HG_V), row(mla_w),
                   row(2 * d)),
        out_shape=out_shape,
        compiler_params=_cparams(("parallel",)),
        name="inproj",
    )(x2, ln_mix, whg, wmla, wgate, lb, nq, nkv)


def _hgrn_kernel(qs_ref, lff_ref, lfb_ref, kf_ref, kb_ref, v_ref, og_ref, onorm_ref, out_ref,
                 state_ref, ofw_ref, obw_ref, qt_scr, u_scr, vec_scr, g_scr, k_scr, v_scr):
    c = HG_CHUNK
    s_len = qs_ref.shape[1]
    n_chunks = s_len // c
    hb = HG_HEADS_PER_STEP
    kd = HG_KDIM

    row = lax.broadcasted_iota(jnp.int32, (c, c), 0)
    col = lax.broadcasted_iota(jnp.int32, (c, c), 1)
    masks = (row >= col, row <= col)
    rowc = lax.broadcasted_iota(jnp.int32, (c, kd), 0)
    o_refs = (ofw_ref, obw_ref)

    def load(j):
        chains = []
        for hh in range(hb):
            lanes = slice(hh * kd, (hh + 1) * kd)
            for direction in range(2):
                chunk = j if direction == 0 else n_chunks - 1 - j
                rows = pl.ds(pl.multiple_of(chunk * c, c), c)
                g = _chunk_cumsum((lff_ref, lfb_ref)[direction][0, rows, lanes], direction, c)
                total = g[c - 1:c, :] if direction == 0 else g[0:1, :]
                chains.append(dict(
                    idx=hh * 2 + direction, direction=direction, rows=rows, lanes=lanes, g=g,
                    total=total,
                    q=qs_ref[0, rows, lanes].astype(F32),
                    k=(kf_ref, kb_ref)[direction][0, rows, lanes].astype(F32),
                    v=v_ref[0, rows, lanes]))
        return chains

    def finish(ch, o, kdec_t_v_scaled):
        (ofw_ref, obw_ref)[ch["direction"]][ch["rows"], ch["lanes"]] = o
        state_ref[ch["idx"]] = kdec_t_v_scaled

    def fast(ch):
        g = ch["g"]
        qt = (ch["q"] * jnp.exp(g)).astype(BF16)
        kt = (ch["k"] * jnp.exp(-g)).astype(BF16)
        st = state_ref[ch["idx"]]
        sc = jnp.where(masks[ch["direction"]], _dot_nt(qt, kt), 0.0)
        o = _dot_nt(qt, st.astype(BF16)) + _dot(sc.astype(BF16), ch["v"])
        finish(ch, o, (st + _dot_tn(ch["v"], kt)) * jnp.exp(ch["total"]))

    def robust(ch):
        g = ch["g"]
        direction = ch["direction"]
        st = state_ref[ch["idx"]]
        o0 = _dot_nt((ch["q"] * jnp.exp(g)).astype(BF16), st.astype(BF16))
        slot = ch["idx"]
        g_scr[slot] = g
        k_scr[slot] = ch["k"]
        v_scr[slot] = ch["v"].astype(F32)
        q = ch["q"]

        def body(s, acc):
            g_s = g_scr[slot, pl.ds(s, 1), :]
            seen = (rowc >= s) if direction == 0 else (rowc <= s)
            decay = jnp.where(seen, jnp.exp(jnp.minimum(g - g_s, 0.0)), 0.0)
            a = jnp.sum(q * decay * k_scr[slot, pl.ds(s, 1), :], axis=-1, keepdims=True)
            return acc + a * v_scr[slot, pl.ds(s, 1), :]

        o = lax.fori_loop(0, c, body, o0)
        kdec = (ch["k"] * jnp.exp(ch["total"] - g)).astype(BF16)
        finish(ch, o, st * jnp.exp(ch["total"]) + _dot_tn(ch["v"], kdec))

    def step(j, carry):
        chains = load(j)
        lowest = chains[0]["total"]
        for ch in chains[1:]:
            lowest = jnp.minimum(lowest, ch["total"])
        safe = jnp.min(lowest) >= HG_SAFE_LOG_DECAY

        @pl.when(safe)
        def _():
            for ch in chains:
                fast(ch)

        @pl.when(jnp.logical_not(safe))
        def _():
            for ch in chains:
                robust(ch)

        return carry

    fc = HG_FAST_CHUNK if s_len % (HG_FAST_CHUNK * HG_CHUNKS_PER_MATMUL) == 0 else c
    grp = HG_CHUNKS_PER_MATMUL if (s_len // fc) % HG_CHUNKS_PER_MATMUL == 0 else 1
    gr = grp * fc
    n_groups = s_len // gr
    mid = fc // 2
    grow = lax.broadcasted_iota(jnp.int32, (gr, gr), 0)
    gcol = lax.broadcasted_iota(jnp.int32, (gr, gr), 1)
    same_chunk = (grow // fc) == (gcol // fc)
    gmasks = (same_chunk & (grow >= gcol), same_chunk & (grow <= gcol))
    chunk_of_row = lax.broadcasted_iota(jnp.int32, (gr, kd), 0) // fc

    def block_diag(a):
        return jnp.concatenate([jnp.where(chunk_of_row == i, a, jnp.zeros_like(a))
                                for i in range(grp)], axis=1)

    def phase_a(gi, lowest):
        rows = pl.ds(pl.multiple_of(gi * gr, gr), gr)
        for hh in range(hb):
            lanes = slice(hh * kd, (hh + 1) * kd)
            v = v_ref[0, rows, lanes]
            v_t = v.T
            q = qs_ref[0, rows, lanes].astype(F32)
            for direction in range(2):
                idx = hh * 2 + direction
                g = _chunk_cumsum((lff_ref, lfb_ref)[direction][0, rows, lanes], direction, fc)
                k = (kf_ref, kb_ref)[direction][0, rows, lanes].astype(F32)
                g_mid = [g[i * fc + mid:i * fc + mid + 1, :] for i in range(grp)]
                centred = g - jnp.concatenate([jnp.broadcast_to(r, (fc, kd)) for r in g_mid], axis=0)
                qt = (q * jnp.exp(centred)).astype(BF16)
                kt = (k * jnp.exp(-centred)).astype(BF16)
                qt_scr[idx, rows, :] = qt
                sc = jnp.where(gmasks[direction], _dot_nt(qt, kt), 0.0)
                o_refs[direction][rows, lanes] = _dot(sc.astype(BF16), v)
                u = _dot(v_t, block_diag(kt))
                for i in range(grp):
                    ci = gi * grp + i
                    u_scr[idx, ci] = u[:, i * kd:(i + 1) * kd]
                    edge = i * fc + (fc - 1 if direction == 0 else 0)
                    total = g[edge:edge + 1, :]
                    lowest = jnp.minimum(lowest, jnp.minimum(g_mid[i], total - g_mid[i]))
                    vec_scr[idx, 0, pl.ds(ci, 1), :] = jnp.exp(total)
                    vec_scr[idx, 1, pl.ds(ci, 1), :] = jnp.exp(g_mid[i])
                    vec_scr[idx, 2, pl.ds(ci, 1), :] = jnp.exp(total - g_mid[i])
        return lowest

    lowest = lax.fori_loop(0, n_groups, phase_a, jnp.zeros((1, kd), F32))
    all_safe = jnp.min(lowest) >= HG_SAFE_LOG_DECAY
    state_ref[...] = jnp.zeros_like(state_ref)

    @pl.when(all_safe)
    def _():
        def phase_b(j, carry):
            for hh in range(hb):
                lanes = slice(hh * kd, (hh + 1) * kd)
                for direction in range(2):
                    idx = hh * 2 + direction
                    gi = j if direction == 0 else n_groups - 1 - j
                    rows = pl.ds(pl.multiple_of(gi * gr, gr), gr)
                    st = state_ref[idx]
                    seen = [None] * grp
                    for i in (range(grp) if direction == 0 else reversed(range(grp))):
                        ci = gi * grp + i
                        at = pl.ds(ci, 1)
                        seen[i] = (st * vec_scr[idx, 1, at, :]).astype(BF16)
                        st = st * vec_scr[idx, 0, at, :] + u_scr[idx, ci] * vec_scr[idx, 2, at, :]
                    state_ref[idx] = st
                    o_refs[direction][rows, lanes] += _dot_nt(
                        block_diag(qt_scr[idx, rows, :]), jnp.concatenate(seen, axis=1))
            return carry

        lax.fori_loop(0, n_groups, phase_b, 0)

    @pl.when(jnp.logical_not(all_safe))
    def _():
        lax.fori_loop(0, n_chunks, step, 0)

    blk = 256 if s_len % 256 == 0 else c

    def epilogue(i, carry):
        rows = pl.ds(pl.multiple_of(i * blk, blk), blk)
        for hh in range(hb):
            lanes = slice(hh * kd, (hh + 1) * kd)
            o = ofw_ref[rows, lanes] + obw_ref[rows, lanes]
            y = o * _rms_scale(o, HG_VDIM) * onorm_ref[...]
            out_ref[0, rows, lanes] = (y * og_ref[0, rows, lanes].astype(F32)).astype(BF16)
        return carry

    lax.fori_loop(0, s_len // blk, epilogue, 0)


def _hgrn(qs, lf, kk, v, og, onorm):
    b, s, _ = qs.shape
    hb = HG_HEADS_PER_STEP
    w = hb * HG_KDIM
    n_hsteps = HG_HEADS // hb
    fwd = pl.BlockSpec((1, s, w), lambda bi, hi: (bi, 0, hi))
    bwd = pl.BlockSpec((1, s, w), lambda bi, hi: (bi, 0, n_hsteps + hi))
    return pl.pallas_call(
        _hgrn_kernel,
        grid=(b, n_hsteps),
        in_specs=[fwd, fwd, bwd, fwd, bwd, fwd, fwd, _const_spec(onorm.shape)],
        out_specs=fwd,
        out_shape=jax.ShapeDtypeStruct((b, s, HG_V), BF16),
        scratch_shapes=[
            pltpu.VMEM((2 * hb, HG_VDIM, HG_KDIM), F32),
            pltpu.VMEM((s, w), F32),
            pltpu.VMEM((s, w), F32),
            pltpu.VMEM((2 * hb, s, HG_KDIM), BF16),
            pltpu.VMEM((2 * hb, s // HG_CHUNK, HG_VDIM, HG_KDIM), F32),
            pltpu.VMEM((2 * hb, 3, max(s // HG_CHUNK, 8), HG_KDIM), F32),
            pltpu.VMEM((2 * hb, HG_CHUNK, HG_KDIM), F32),
            pltpu.VMEM((2 * hb, HG_CHUNK, HG_KDIM), F32),
            pltpu.VMEM((2 * hb, HG_CHUNK, HG_VDIM), F32),
        ],
        compiler_params=_cparams(("parallel", "parallel")),
        name="hgrn",
    )(qs, lf, lf, kk, kk, v, og, onorm)


def _mlaprep_kernel(mla_ref, pos_t_ref, wq_t_ref, wk_t_ref, wv_ref, gq_t_ref, gk_t_ref, freq_t_ref,
                    q_t_ref, k_t_ref, v_ref):
    cq = mla_ref[:, 0:MLA_Q_RANK]
    ckr = mla_ref[:, MLA_Q_RANK:]
    half = MLA_ROPE // 2
    tm = mla_ref.shape[0]

    lane_tile = lambda a: jnp.concatenate([a] * (tm // LANES), axis=1)
    ang_t = lane_tile(freq_t_ref[...]) * pos_t_ref[...]
    cos_t = jnp.cos(ang_t)
    sin_t = jnp.sin(ang_t)

    def norm_rope_t(x_t, gain_ref, out_ref):
        gain_t = lane_tile(gain_ref[...])
        for hd in range(MLA_HEADS):
            r0 = hd * LANES
            x = x_t[r0:r0 + LANES, :]
            y = x * lax.rsqrt(jnp.sum(x * x, axis=0, keepdims=True) * (1.0 / MLA_QK) + EPS) * gain_t
            x1 = y[MLA_NOPE:MLA_NOPE + half, :]
            x2 = y[MLA_NOPE + half:MLA_QK, :]
            out_ref[r0:r0 + MLA_NOPE, :] = y[0:MLA_NOPE, :].astype(BF16)
            out_ref[r0 + MLA_NOPE:r0 + MLA_NOPE + half, :] = (x1 * cos_t - x2 * sin_t).astype(BF16)
            out_ref[r0 + MLA_NOPE + half:r0 + MLA_QK, :] = (x2 * cos_t + x1 * sin_t).astype(BF16)
            out_ref[r0 + MLA_QK:r0 + LANES, :] = jnp.zeros((LANES - MLA_QK, tm), BF16)

    norm_rope_t(_dot_nt(wq_t_ref[...], cq), gq_t_ref, q_t_ref)
    norm_rope_t(_dot_nt(wk_t_ref[...], ckr), gk_t_ref, k_t_ref)
    v_ref[...] = _dot(ckr, wv_ref[...]).astype(BF16)


def _mlaprep(mla_in, pos_t, wq_t, wk_t, wv, gq_t, gk_t, freq_t):
    t = mla_in.shape[0]
    tm = MLAPREP_ROW_TILE
    row = lambda w: pl.BlockSpec((tm, w), lambda i: (i, 0))
    col = lambda h: pl.BlockSpec((h, tm), lambda i: (0, i))
    hw = MLA_HEADS * LANES
    vw = MLA_HEADS * MLA_VDIM
    return pl.pallas_call(
        _mlaprep_kernel,
        grid=(t // tm,),
        in_specs=[row(mla_in.shape[1]), col(1), _const_spec(wq_t.shape), _const_spec(wk_t.shape),
                  _const_spec(wv.shape), _const_spec(gq_t.shape), _const_spec(gk_t.shape),
                  _const_spec(freq_t.shape)],
        out_specs=(col(hw), col(hw), row(vw)),
        out_shape=(jax.ShapeDtypeStruct((hw, t), BF16), jax.ShapeDtypeStruct((hw, t), BF16),
                   jax.ShapeDtypeStruct((t, vw), BF16)),
        compiler_params=_cparams(("parallel",)),
        name="mlaprep",
    )(mla_in, pos_t, wq_t, wk_t, wv, gq_t, gk_t, freq_t)


def _attn_kernel(q_t_ref, k_t_ref, v_ref, o_ref):
    for j in range(ATTN_HEADS_PER_STEP):
        rows = slice(j * LANES, (j + 1) * LANES)
        s = _dot_tn(q_t_ref[rows, :], k_t_ref[rows, :])
        p = jnp.exp(s - jnp.max(s, axis=-1, keepdims=True))
        l = jnp.sum(p, axis=-1, keepdims=True)
        vl = slice(j * MLA_VDIM, (j + 1) * MLA_VDIM)
        o = _dot(p.astype(BF16), v_ref[0, :, vl])
        o_ref[0, :, vl] = (o / l).astype(BF16)


def _attn(q_t, k_t, v):
    b, s, _ = v.shape
    tq = min(ATTN_Q_TILE, s)
    nh = ATTN_HEADS_PER_STEP
    n_q = s // tq
    return pl.pallas_call(
        _attn_kernel,
        grid=(b, MLA_HEADS // nh, n_q),
        in_specs=[pl.BlockSpec((nh * LANES, tq), lambda bi, hp, i: (hp, bi * n_q + i)),
                  pl.BlockSpec((nh * LANES, s), lambda bi, hp, i: (hp, bi)),
                  pl.BlockSpec((1, s, nh * MLA_VDIM), lambda bi, hp, i: (bi, 0, hp))],
        out_specs=pl.BlockSpec((1, tq, nh * MLA_VDIM), lambda bi, hp, i: (bi, i, hp)),
        out_shape=jax.ShapeDtypeStruct((b, s, MLA_HEADS * MLA_VDIM), BF16),
        compiler_params=_cparams(("parallel", "parallel", "arbitrary")),
        name="attn",
    )(q_t, k_t, v)


def _merge_kernel(x_ref, ya_ref, yb_ref, gates_ref, woa_ref, wob_ref, wout_ref, lnm_ref, wr_t_ref,
                  br_t_ref, x1_ref, hm8_ref, route_t_ref, xb_zero_ref):
    d = x_ref.shape[-1]
    xb_zero_ref[...] = jnp.zeros_like(xb_zero_ref)
    y_a = _dot(ya_ref[...], woa_ref[...])
    y_b = _dot(yb_ref[...], wob_ref[...])
    merged = gates_ref[:, 0:d].astype(F32) * y_a + gates_ref[:, d:2 * d].astype(F32) * y_b
    x1 = x_ref[...] + _dot(merged.astype(BF16), wout_ref[...])
    x1_ref[...] = x1
    hm = x1 * _rms_scale(x1, d) * lnm_ref[...]
    _store_token_tiles(hm8_ref, 0, hm)

    tm = hm.shape[0]
    hm_hi = hm.astype(BF16)
    hm_lo = (hm - hm_hi.astype(F32)).astype(BF16)
    hh = _dot_nt(wr_t_ref[...], hm_hi)
    bias = jnp.concatenate([br_t_ref[...]] * (tm // LANES), axis=1)
    logits = hh[0:LANES] + hh[LANES:2 * LANES] + _dot_nt(wr_t_ref[0:LANES, :], hm_lo) + bias
    neg = -jnp.inf
    gl = logits[0:N_GROUPS]
    gmax = jnp.max(gl, axis=0, keepdims=True)
    p_group = 1.0 / jnp.sum(jnp.exp(gl - gmax), axis=0, keepdims=True)
    g_row = lax.broadcasted_iota(jnp.int32, gl.shape, 0)
    g_sel = jnp.min(jnp.where(gl == gmax, g_row, N_GROUPS), axis=0, keepdims=True)
    el_all = logits[N_GROUPS:N_GROUPS + N_EXPERTS]
    e_row = lax.broadcasted_iota(jnp.int32, el_all.shape, 0)
    el = jnp.where(e_row // EXPERTS_PER_GROUP == g_sel, el_all, neg)
    v1 = jnp.max(el, axis=0, keepdims=True)
    i1 = jnp.min(jnp.where(el == v1, e_row, N_EXPERTS), axis=0, keepdims=True)
    el2 = jnp.where(e_row == i1, neg, el)
    v2 = jnp.max(el2, axis=0, keepdims=True)
    i2 = jnp.min(jnp.where(el2 == v2, e_row, N_EXPERTS), axis=0, keepdims=True)
    e21 = jnp.exp(v2 - v1)
    w1 = p_group / (1.0 + e21)
    w2 = w1 * e21
    out_row = lax.broadcasted_iota(jnp.int32, route_t_ref.shape, 0)
    route_t_ref[...] = jnp.where(out_row == 0, i1.astype(F32),
                                 jnp.where(out_row == 1, i2.astype(F32),
                                           jnp.where(out_row == 2, w1,
                                                     jnp.where(out_row == 3, w2, 0.0))))


def _merge(x2, ya, yb, gates, woa, wob, wout, lnm, wr, br):
    t, d = x2.shape
    tm = MERGE_ROW_TILE
    row = lambda w: pl.BlockSpec((tm, w), lambda i: (i, 0))
    xb_rows = _moe_blocks(t) * MOE_ROWS * TOKEN_ROWS
    assert xb_rows % (t // tm) == 0
    return pl.pallas_call(
        _merge_kernel,
        grid=(t // tm,),
        in_specs=[row(d), row(ya.shape[1]), row(yb.shape[1]), row(2 * d), _const_spec(woa.shape),
                  _const_spec(wob.shape), _const_spec(wout.shape), _const_spec(lnm.shape),
                  _const_spec(wr.shape), _const_spec(br.shape)],
        out_specs=(row(d), pl.BlockSpec((tm * TOKEN_ROWS, LANES), lambda i: (i, 0)),
                   pl.BlockSpec((ROUTE_ROWS, tm), lambda i: (0, i)),
                   pl.BlockSpec((xb_rows // (t // tm), LANES), lambda i: (i, 0))),
        out_shape=(jax.ShapeDtypeStruct((t, d), F32),
                   jax.ShapeDtypeStruct((t * TOKEN_ROWS, LANES), jnp.uint32),
                   jax.ShapeDtypeStruct((ROUTE_ROWS, t), F32),
                   jax.ShapeDtypeStruct((xb_rows, LANES), jnp.uint32)),
        compiler_params=_cparams(("parallel",)),
        name="merge",
    )(x2, ya, yb, gates, woa, wob, wout, lnm, wr, br)


_HIGH_HALF = 0xFFFF0000


def _store_token_tiles(ref4, tok0, val):
    n, width = val.shape
    as_bits = lambda a: lax.bitcast_convert_type(a.astype(BF16).astype(F32), jnp.uint32)
    words = (as_bits(val[:, :width // 2]) >> 16) | (as_bits(val[:, width // 2:]) & jnp.uint32(_HIGH_HALF))
    for j in range(TOKEN_ROWS):
        ref4[pl.ds(tok0 * TOKEN_ROWS + j, n, stride=TOKEN_ROWS), :] = words[:, j * LANES:(j + 1) * LANES]


def _load_token_tiles(ref4, tok0, n):
    words = jnp.concatenate(
        [ref4[pl.ds(tok0 * TOKEN_ROWS + j, n, stride=TOKEN_ROWS), :] for j in range(TOKEN_ROWS)],
        axis=1)
    low = lax.bitcast_convert_type(words << 16, F32).astype(BF16)
    high = lax.bitcast_convert_type(words & jnp.uint32(_HIGH_HALF), F32).astype(BF16)
    return jnp.concatenate([low, high], axis=1)


def _token_tile(ref8, tok):
    return ref8.at[pl.ds(pl.multiple_of(tok * TOKEN_ROWS, TOKEN_ROWS), TOKEN_ROWS)]


def _dispatch_kernel(dest_ref, hm8_ref, xb_in_ref, xb8_ref, sem):
    del xb_in_ref
    n_tok = DISPATCH_TOKENS
    for r in range(n_tok):
        src = hm8_ref.at[pl.ds(r * TOKEN_ROWS, TOKEN_ROWS)]
        for slot in range(2):
            pltpu.make_async_copy(src, _token_tile(xb8_ref, dest_ref[2 * r + slot]),
                                  sem).start(priority=slot)
    for _ in range(2):
        pltpu.make_async_copy(hm8_ref, xb8_ref.at[pl.ds(0, n_tok * TOKEN_ROWS)], sem).wait()


def _dispatch(dest_flat, hm8, xb_init):
    t = hm8.shape[0] // TOKEN_ROWS
    return pl.pallas_call(
        _dispatch_kernel,
        grid=(t // DISPATCH_TOKENS,),
        in_specs=[pl.BlockSpec((2 * DISPATCH_TOKENS,), lambda i: (i,), memory_space=pltpu.SMEM),
                  pl.BlockSpec((DISPATCH_TOKENS * TOKEN_ROWS, LANES), lambda i: (i, 0)),
                  pl.BlockSpec(memory_space=pl.ANY)],
        out_specs=pl.BlockSpec(memory_space=pl.ANY),
        out_shape=jax.ShapeDtypeStruct(xb_init.shape, xb_init.dtype),
        scratch_shapes=[pltpu.SemaphoreType.DMA(())],
        input_output_aliases={2: 0},
        compiler_params=_cparams(("arbitrary",)),
        name="dispatch",
    )(dest_flat, hm8, xb_init)


def _experts_kernel(be_ref, nu_ref, nxt_ref, par_ref, xb8_ref, w1_hbm, w3_hbm, w2_hbm, yb8_ref,
                    w1f, w3f, w2f, sems, w1c, w3c, w2c):
    j = pl.program_id(0)

    def fetch(e, slot):
        return [pltpu.make_async_copy(src.at[e], dst.at[slot], sems.at[slot])
                for src, dst in ((w1_hbm, w1f), (w3_hbm, w3f), (w2_hbm, w2f))]

    for half in range(2):
        blk = 2 * j + half
        used = blk < nu_ref[0]
        new_expert = jnp.logical_or(blk == 0, be_ref[blk] != be_ref[jnp.maximum(blk - 1, 0)])
        tok0 = half * MOE_ROWS

        @pl.when(jnp.logical_and(used, new_expert))
        def _():
            slot = par_ref[blk]

            @pl.when(blk == 0)
            def _():
                for cp in fetch(be_ref[0], 0):
                    cp.start()

            for cp in fetch(be_ref[blk], slot):
                cp.wait()

            @pl.when(nxt_ref[blk] >= 0)
            def _():
                for cp in fetch(nxt_ref[blk], 1 - slot):
                    cp.start()

            w1c[...] = w1f[slot].astype(BF16)
            w3c[...] = w3f[slot].astype(BF16)
            w2c[...] = w2f[slot].astype(BF16)

        @pl.when(used)
        def _():
            x = _load_token_tiles(xb8_ref, tok0, MOE_ROWS)
            hmid = (_silu(_dot(x, w1c[...])) * _dot(x, w3c[...])).astype(BF16)
            _store_token_tiles(yb8_ref, tok0, _dot(hmid, w2c[...]))

        @pl.when(jnp.logical_not(used))
        def _():
            rows = pl.ds(tok0 * TOKEN_ROWS, MOE_ROWS * TOKEN_ROWS)
            yb8_ref[rows, :] = jnp.zeros((MOE_ROWS * TOKEN_ROWS, LANES), jnp.uint32)


def _experts(block_expert, n_used, next_expert, slot_parity, xb8, w1, w3, w2):
    nblk = block_expert.shape[0]
    nsteps = nblk // 2
    d = w1.shape[1]
    de = w1.shape[2]
    step_rows = 2 * MOE_ROWS * TOKEN_ROWS
    rows_in = lambda j, be, nu, nx, pa: (jnp.minimum(j, (nu[0] - 1) // 2), 0)
    hbm = pl.BlockSpec(memory_space=pl.ANY)
    return pl.pallas_call(
        _experts_kernel,
        grid_spec=pltpu.PrefetchScalarGridSpec(
            num_scalar_prefetch=4,
            grid=(nsteps,),
            in_specs=[pl.BlockSpec((step_rows, LANES), rows_in), hbm, hbm, hbm],
            out_specs=pl.BlockSpec((step_rows, LANES), lambda j, be, nu, nx, pa: (j, 0)),
            scratch_shapes=[pltpu.VMEM((2, d, de), F32), pltpu.VMEM((2, d, de), F32),
                            pltpu.VMEM((2, de, d), F32), pltpu.SemaphoreType.DMA((2,)),
                            pltpu.VMEM((d, de), BF16), pltpu.VMEM((d, de), BF16),
                            pltpu.VMEM((de, d), BF16)]),
        out_shape=jax.ShapeDtypeStruct((nblk * MOE_ROWS * TOKEN_ROWS, LANES), jnp.uint32),
        compiler_params=_cparams(("arbitrary",)),
        name="experts",
    )(block_expert, n_used, next_expert, slot_parity, xb8, w1, w3, w2)


def _ple_kernel(dest_ref, destn_ref, x1_ref, route_ref, p_ref, ln_ref, wg_ref, wp_ref, yb8_ref,
                out_ref, ya0, ya1, yb0, yb1, sems):
    j = pl.program_id(0)
    last = pl.num_programs(0) - 1
    tm = PLE_ROW_TILE
    d = x1_ref.shape[1]

    def start(idx_ref, base, bufs, sem):
        for r in range(tm):
            for slot in range(2):
                pltpu.make_async_copy(_token_tile(yb8_ref, idx_ref[base + 2 * r + slot]),
                                      bufs[slot].at[pl.ds(r * TOKEN_ROWS, TOKEN_ROWS)],
                                      sem).start(priority=slot)

    def wait(bufs, sem):
        for buf in bufs:
            pltpu.make_async_copy(yb8_ref.at[pl.ds(0, tm * TOKEN_ROWS)], buf, sem).wait()

    def compute(bufs, rows):
        x2 = (x1_ref[rows, :]
              + route_ref[rows, 0:1] * _load_token_tiles(bufs[0], 0, tm).astype(F32)
              + route_ref[rows, 1:2] * _load_token_tiles(bufs[1], 0, tm).astype(F32))
        h = (x2 * _rms_scale(x2, d) * ln_ref[...]).astype(BF16)
        pp = _dot(p_ref[rows, :].astype(BF16), wp_ref[...])
        out_ref[rows, :] = x2 + pp * _sigmoid(_dot(h, wg_ref[...]))

    @pl.when(j == 0)
    def _():
        start(dest_ref, 0, (ya0, ya1), sems.at[0])

    wait((ya0, ya1), sems.at[0])
    start(dest_ref, 2 * tm, (yb0, yb1), sems.at[1])
    compute((ya0, ya1), slice(0, tm))

    wait((yb0, yb1), sems.at[1])
    start(destn_ref, 0, (ya0, ya1), sems.at[0])
    compute((yb0, yb1), slice(tm, 2 * tm))

    @pl.when(j == last)
    def _():
        wait((ya0, ya1), sems.at[0])


def _ple(dest_flat, x1, route, p2, ln, wg, wp, yb):
    t, d = x1.shape
    tm = PLE_ROW_TILE
    nsteps = t // (2 * tm)
    row = lambda w: pl.BlockSpec((2 * tm, w), lambda j: (j, 0))
    return pl.pallas_call(
        _ple_kernel,
        grid=(nsteps,),
        in_specs=[pl.BlockSpec((4 * tm,), lambda j: (j,), memory_space=pltpu.SMEM),
                  pl.BlockSpec((4 * tm,), lambda j: (jnp.minimum(j + 1, nsteps - 1),),
                               memory_space=pltpu.SMEM),
                  row(d), row(route.shape[1]), row(p2.shape[1]), _const_spec(ln.shape),
                  _const_spec(wg.shape), _const_spec(wp.shape),
                  pl.BlockSpec(memory_space=pl.ANY)],
        out_specs=row(d),
        out_shape=jax.ShapeDtypeStruct((t, d), F32),
        scratch_shapes=([pltpu.VMEM((tm * TOKEN_ROWS, LANES), jnp.uint32)] * 4
                        + [pltpu.SemaphoreType.DMA((2,))]),
        compiler_params=_cparams(("arbitrary",)),
        name="ple",
    )(dest_flat, dest_flat, x1, route, p2, ln, wg, wp, yb)


def _head_pad(w, width):
    r = w.shape[0]
    w = w.reshape(r, MLA_HEADS, width)
    return jnp.pad(w, ((0, 0), (0, 0), (0, LANES - width))).reshape(r, MLA_HEADS * LANES)


def _rope_table():
    half = MLA_ROPE // 2
    inv_freq = ROPE_THETA ** (-np.arange(half, dtype=np.float32) / half)
    return jnp.asarray(np.broadcast_to(inv_freq[:, None], (half, LANES)))


def _gain_t(gain, scale):
    return jnp.broadcast_to((jnp.pad(gain, (0, LANES - MLA_QK)) * scale)[:, None], (LANES, LANES))


def _moe_blocks(t):
    nblk = (2 * t + N_EXPERTS * (MOE_ROWS - 1)) // MOE_ROWS
    return nblk + nblk % 2


def _route_tables(ids):
    a = ids.shape[0] * 2
    e = ids.reshape(a)
    onehot = (e[:, None] == jnp.arange(N_EXPERTS, dtype=jnp.int32)[None, :]).astype(jnp.int32)
    csum = jnp.cumsum(onehot, axis=0)
    counts = csum[-1]
    rank = jnp.sum(csum * onehot, axis=1) - 1
    pcounts = (counts + MOE_ROWS - 1) // MOE_ROWS * MOE_ROWS
    pends = jnp.cumsum(pcounts)
    pstarts = pends - pcounts
    dest = (jnp.sum(onehot * pstarts[None, :], axis=1) + rank).astype(jnp.int32)
    nblk = _moe_blocks(ids.shape[0])
    n_used = (pends[-1] // MOE_ROWS).astype(jnp.int32)
    blk_start = jnp.arange(nblk, dtype=jnp.int32) * MOE_ROWS
    blk_start = jnp.minimum(blk_start, pends[-1] - 1)
    block_expert = jnp.sum((pends[None, :] <= blk_start[:, None]).astype(jnp.int32), axis=1)
    block_expert = jnp.minimum(block_expert, N_EXPERTS - 1).astype(jnp.int32)
    block_expert, n_used = lax.optimization_barrier((block_expert, n_used))
    blk = jnp.arange(nblk, dtype=jnp.int32)
    prev = jnp.concatenate([jnp.full((1,), -1, jnp.int32), block_expert[:-1]])
    starts_expert = (blk < n_used) & (block_expert != prev)
    slot_parity = (jnp.cumsum(starts_expert.astype(jnp.int32)) - 1) % 2
    start_at_or_after = jnp.flip(lax.cummin(jnp.flip(jnp.where(starts_expert, blk, nblk))))
    next_start = jnp.concatenate([start_at_or_after[1:], jnp.full((1,), nblk, jnp.int32)])
    next_expert = jnp.where(next_start < nblk, block_expert[jnp.minimum(next_start, nblk - 1)], -1)
    return (dest, block_expert, n_used.reshape(1), next_expert.astype(jnp.int32),
            slot_parity.astype(jnp.int32), nblk)


def kernel(x, p, positions, ln_mix, w_in, hg_lb, hg_onorm, w_oA, mla_qa_norm, mla_kva_norm, w_uq,
           w_ukv, q_norm, k_norm, w_oB, w_out, ln_moe, w_rg, b_rg, w_re, b_re, w1, w3, w2, ln_ple,
           w_ple_gate, w_ple_proj):
    b, s, d = x.shape
    t = b * s
    depth = w_in.shape[0]
    lb_all = jnp.cumsum(jax.nn.softmax(hg_lb.astype(F32), axis=1), axis=1)
    pos_t = positions.astype(F32).reshape(1, t)
    freq_t = _rope_table()
    xc = x.reshape(t, d)

    for layer in range(depth):
        wi = w_in[layer]
        n_hg = 3 * HG_F + 2 * HG_V
        n_mla = MLA_Q_RANK + MLA_KV_RANK + MLA_ROPE
        mla_w = -(-n_mla // LANES) * LANES
        whg = wi[:, :n_hg].astype(BF16)
        wmla = jnp.pad(wi[:, n_hg:n_hg + n_mla], ((0, 0), (0, mla_w - n_mla))).astype(BF16)
        wgate = wi[:, n_hg + n_mla:].astype(BF16)
        lb = lb_all[:, layer, :]

        qs, lf, kk, v, og, mla_in, gates = _inproj(
            xc, ln_mix[layer][None, :], whg, wmla, wgate, lb, mla_qa_norm[layer][None, :],
            mla_kva_norm[layer][None, :])

        r3 = lambda a: a.reshape(b, s, a.shape[-1])
        ya = _hgrn(r3(qs), r3(lf), r3(kk), r3(v), r3(og), hg_onorm[layer][None, :]).reshape(t, HG_V)

        kv_w = MLA_NOPE + MLA_VDIM
        wkv = w_ukv[layer].reshape(MLA_KV_RANK, MLA_HEADS, kv_w)
        ckr_w = mla_w - MLA_Q_RANK
        wq = _head_pad(w_uq[layer], MLA_QK).astype(BF16)
        wk_nope = _head_pad(wkv[:, :, :MLA_NOPE].reshape(MLA_KV_RANK, MLA_HEADS * MLA_NOPE), MLA_NOPE)
        place = np.zeros((ckr_w - MLA_KV_RANK, MLA_HEADS, LANES), np.float32)
        for j in range(MLA_ROPE):
            place[j, :, MLA_NOPE + j] = 1.0
        wk = jnp.concatenate([wk_nope, jnp.asarray(place.reshape(ckr_w - MLA_KV_RANK, -1))],
                             axis=0).astype(BF16)
        wv = jnp.pad(wkv[:, :, MLA_NOPE:].reshape(MLA_KV_RANK, MLA_HEADS * MLA_VDIM),
                     ((0, ckr_w - MLA_KV_RANK), (0, 0))).astype(BF16)
        qh_t, kh_t, vh = _mlaprep(mla_in, pos_t, wq.T, wk.T, wv, _gain_t(q_norm[layer], MLA_QK ** -0.5),
                                  _gain_t(k_norm[layer], 1.0), freq_t)
        yb = _attn(qh_t, kh_t, r3(vh)).reshape(t, MLA_HEADS * MLA_VDIM)

        wr = jnp.pad(jnp.concatenate([w_rg[layer], w_re[layer]], axis=1),
                     ((0, 0), (0, LANES - N_GROUPS - N_EXPERTS)))
        wr_hi = wr.astype(BF16)
        wr_t = jnp.concatenate([wr_hi, (wr - wr_hi.astype(F32)).astype(BF16)], axis=1).T
        br_t = jnp.broadcast_to(jnp.pad(jnp.concatenate([b_rg[layer], b_re[layer]]),
                                        (0, LANES - N_GROUPS - N_EXPERTS))[:, None], (LANES, LANES))
        x1, hm8, route_t, xb_zero = _merge(xc, ya, yb, gates, w_oA[layer].astype(BF16),
                                  w_oB[layer].astype(BF16), w_out[layer].astype(BF16),
                                  ln_moe[layer][None, :], wr_t, br_t)

        ids = route_t[0:2].astype(jnp.int32).T
        route = route_t[2:4].T
        dest, block_expert, n_used, next_expert, slot_parity, nblk = _route_tables(ids)
        xb8 = _dispatch(dest, hm8, xb_zero)
        yexp = _experts(block_expert, n_used, next_expert, slot_parity, xb8,
                        w1[layer], w3[layer], w2[layer])

        xc = _ple(dest, x1, route, p[layer].reshape(t, -1), ln_ple[layer][None, :],
                  w_ple_gate[layer].astype(BF16), w_ple_proj[layer].astype(BF16), yexp)
    return xc.reshape(b, s, d)
```

```python
import functools

import numpy as np
import jax
import jax.numpy as jnp
from jax import lax
from jax.experimental import pallas as pl
from jax.experimental.pallas import tpu as pltpu

F32 = jnp.float32
BF16 = jnp.bfloat16

HG_HEADS = 4
HG_KDIM = 128
HG_VDIM = 128
HG_F = HG_HEADS * HG_KDIM
HG_V = HG_HEADS * HG_VDIM
MLA_HEADS = 8
MLA_NOPE = 64
MLA_ROPE = 32
MLA_VDIM = 64
MLA_QK = MLA_NOPE + MLA_ROPE
MLA_Q_RANK = 256
MLA_KV_RANK = 128
ROPE_THETA = 10000.0
N_GROUPS = 8
EXPERTS_PER_GROUP = 8
N_EXPERTS = N_GROUPS * EXPERTS_PER_GROUP
D_EXPERT = 256
EPS = 1e-6

LANES = 128
ROUTE_ROWS = 8
TOKEN_ROWS = 4
VMEM_LIMIT_BYTES = 56 * 1024 * 1024

PLE_ROW_TILE = 256
MERGE_ROW_TILE = 512
INPROJ_ROW_TILE = 256
MLAPREP_ROW_TILE = 512
HG_CHUNK = 64
HG_HEADS_PER_STEP = 2
HG_FAST_CHUNK = 128
HG_CHUNKS_PER_MATMUL = 2
ATTN_Q_TILE = 512
ATTN_HEADS_PER_STEP = 8
MOE_ROWS = 256
DISPATCH_TOKENS = 1024
HG_SAFE_LOG_DECAY = -75.0


def _cparams(sem):
    return pltpu.CompilerParams(dimension_semantics=sem, vmem_limit_bytes=VMEM_LIMIT_BYTES)


def _const_spec(shape):
    nd = len(shape)
    return pl.BlockSpec(shape, lambda *_: (0,) * nd)


def _sigmoid(x):
    return 1.0 / (1.0 + jnp.exp(-x))


def _silu(x):
    return x * _sigmoid(x)


def _rms_scale(x, n):
    return lax.rsqrt(jnp.sum(x * x, axis=-1, keepdims=True) * (1.0 / n) + EPS)


def _dot(a, b):
    return jnp.dot(a, b, preferred_element_type=F32)


def _dot_nt(a, b):
    return lax.dot_general(a, b, (((1,), (1,)), ((), ())), preferred_element_type=F32)


def _dot_tn(a, b):
    return lax.dot_general(a, b, (((0,), (0,)), ((), ())), preferred_element_type=F32)


def _chunk_cumsum(a, direction, chunk):
    n = a.shape[0]
    in_chunk = lax.broadcasted_iota(jnp.int32, a.shape, 0) % chunk
    k = 1
    while k < chunk:
        if direction == 0:
            a = a + jnp.where(in_chunk >= k, pltpu.roll(a, k, 0), 0.0)
        else:
            a = a + jnp.where(in_chunk < chunk - k, pltpu.roll(a, n - k, 0), 0.0)
        k *= 2
    return a


def _inproj_kernel(x_ref, g_ref, whg_ref, wmla_ref, wgate_ref, lb_ref, nq_ref, nkv_ref,
                   qs_ref, lf_ref, kk_ref, v_ref, og_ref, mla_ref, gates_ref):
    x = x_ref[...]
    d = x.shape[-1]
    h = (x * _rms_scale(x, d) * g_ref[...]).astype(BF16)

    qs_ref[...] = _silu(_dot(h, whg_ref[:, 0:HG_F])).astype(BF16)
    for direction in range(2):
        cols = slice(HG_F * (1 + direction), HG_F * (2 + direction))
        out_cols = slice(HG_F * direction, HG_F * (direction + 1))
        lb = lb_ref[direction:direction + 1, :]
        f = lb + (1.0 - lb) * _sigmoid(_dot(h, whg_ref[:, cols]))
        lf_ref[:, out_cols] = jnp.log(f)
        kk_ref[:, out_cols] = (1.0 - f).astype(BF16)
    v_ref[...] = _dot(h, whg_ref[:, 3 * HG_F:3 * HG_F + HG_V]).astype(BF16)
    og_ref[...] = _silu(_dot(h, whg_ref[:, 3 * HG_F + HG_V:3 * HG_F + 2 * HG_V])).astype(BF16)

    zm = _dot(h, wmla_ref[...])
    cq = zm[:, 0:MLA_Q_RANK]
    mla_ref[:, 0:MLA_Q_RANK] = (cq * _rms_scale(cq, MLA_Q_RANK) * nq_ref[...]).astype(BF16)
    ckv = zm[:, MLA_Q_RANK:MLA_Q_RANK + MLA_KV_RANK]
    mla_ref[:, MLA_Q_RANK:MLA_Q_RANK + MLA_KV_RANK] = (
        ckv * _rms_scale(ckv, MLA_KV_RANK) * nkv_ref[...]).astype(BF16)
    mla_ref[:, MLA_Q_RANK + MLA_KV_RANK:] = zm[:, MLA_Q_RANK + MLA_KV_RANK:].astype(BF16)

    for half in range(2):
        cols = slice(d * half, d * (half + 1))
        gates_ref[:, cols] = _sigmoid(_dot(h, wgate_ref[:, cols])).astype(BF16)


def _inproj(x2, ln_mix, whg, wmla, wgate, lb, nq, nkv):
    t, d = x2.shape
    tm = INPROJ_ROW_TILE
    mla_w = wmla.shape[1]
    row = lambda w: pl.BlockSpec((tm, w), lambda i: (i, 0))
    out_shape = (
        jax.ShapeDtypeStruct((t, HG_F), BF16),
        jax.ShapeDtypeStruct((t, 2 * HG_F), F32),
        jax.ShapeDtypeStruct((t, 2 * HG_F), BF16),
        jax.ShapeDtypeStruct((t, HG_V), BF16),
        jax.ShapeDtypeStruct((t, HG_V), BF16),
        jax.ShapeDtypeStruct((t, mla_w), BF16),
        jax.ShapeDtypeStruct((t, 2 * d), BF16),
    )
    return pl.pallas_call(
        _inproj_kernel,
        grid=(t // tm,),
        in_specs=[row(d), _const_spec((1, d)), _const_spec(whg.shape), _const_spec(wmla.shape),
                  _const_spec(wgate.shape), _const_spec(lb.shape), _const_spec(nq.shape),
                  _const_spec(nkv.shape)],
        out_specs=(row(HG_F), row(2 * HG_F), row(2 * HG_F), row(HG_V), row(HG_V), row(mla_w),
                   row(2 * d)),
        out_shape=out_shape,
        compiler_params=_cparams(("parallel",)),
        name="inproj",
    )(x2, ln_mix, whg, wmla, wgate, lb, nq, nkv)


def _hgrn_kernel(qs_ref, lff_ref, lfb_ref, kf_ref, kb_ref, v_ref, og_ref, onorm_ref, out_ref,
                 state_ref, ofw_ref, obw_ref, qt_scr, u_scr, vec_scr, g_scr, k_scr, v_scr):
    c = HG_CHUNK
    s_len = qs_ref.shape[1]
    n_chunks = s_len // c
    hb = HG_HEADS_PER_STEP
    kd = HG_KDIM

    row = lax.broadcasted_iota(jnp.int32, (c, c), 0)
    col = lax.broadcasted_iota(jnp.int32, (c, c), 1)
    masks = (row >= col, row <= col)
    rowc = lax.broadcasted_iota(jnp.int32, (c, kd), 0)
    o_refs = (ofw_ref, obw_ref)

    def load(j):
        chains = []
        for hh in range(hb):
            lanes = slice(hh * kd, (hh + 1) * kd)
            for direction in range(2):
                chunk = j if direction == 0 else n_chunks - 1 - j
                rows = pl.ds(pl.multiple_of(chunk * c, c), c)
                g = _chunk_cumsum((lff_ref, lfb_ref)[direction][0, rows, lanes], direction, c)
                total = g[c - 1:c, :] if direction == 0 else g[0:1, :]
                chains.append(dict(
                    idx=hh * 2 + direction, direction=direction, rows=rows, lanes=lanes, g=g,
                    total=total,
                    q=qs_ref[0, rows, lanes].astype(F32),
                    k=(kf_ref, kb_ref)[direction][0, rows, lanes].astype(F32),
                    v=v_ref[0, rows, lanes]))
        return chains

    def finish(ch, o, kdec_t_v_scaled):
        (ofw_ref, obw_ref)[ch["direction"]][ch["rows"], ch["lanes"]] = o
        state_ref[ch["idx"]] = kdec_t_v_scaled

    def fast(ch):
        g = ch["g"]
        qt = (ch["q"] * jnp.exp(g)).astype(BF16)
        kt = (ch["k"] * jnp.exp(-g)).astype(BF16)
        st = state_ref[ch["idx"]]
        sc = jnp.where(masks[ch["direction"]], _dot_nt(qt, kt), 0.0)
        o = _dot_nt(qt, st.astype(BF16)) + _dot(sc.astype(BF16), ch["v"])
        finish(ch, o, (st + _dot_tn(ch["v"], kt)) * jnp.exp(ch["total"]))

    def robust(ch):
        g = ch["g"]
        direction = ch["direction"]
        st = state_ref[ch["idx"]]
        o0 = _dot_nt((ch["q"] * jnp.exp(g)).astype(BF16), st.astype(BF16))
        slot = ch["idx"]
        g_scr[slot] = g
        k_scr[slot] = ch["k"]
        v_scr[slot] = ch["v"].astype(F32)
        q = ch["q"]

        def body(s, acc):
            g_s = g_scr[slot, pl.ds(s, 1), :]
            seen = (rowc >= s) if direction == 0 else (rowc <= s)
            decay = jnp.where(seen, jnp.exp(jnp.minimum(g - g_s, 0.0)), 0.0)
            a = jnp.sum(q * decay * k_scr[slot, pl.ds(s, 1), :], axis=-1, keepdims=True)
            return acc + a * v_scr[slot, pl.ds(s, 1), :]

        o = lax.fori_loop(0, c, body, o0)
        kdec = (ch["k"] * jnp.exp(ch["total"] - g)).astype(BF16)
        finish(ch, o, st * jnp.exp(ch["total"]) + _dot_tn(ch["v"], kdec))

    def step(j, carry):
        chains = load(j)
        lowest = chains[0]["total"]
        for ch in chains[1:]:
            lowest = jnp.minimum(lowest, ch["total"])
        safe = jnp.min(lowest) >= HG_SAFE_LOG_DECAY

        @pl.when(safe)
        def _():
            for ch in chains:
                fast(ch)

        @pl.when(jnp.logical_not(safe))
        def _():
            for ch in chains:
                robust(ch)

        return carry

    fc = HG_FAST_CHUNK if s_len % (HG_FAST_CHUNK * HG_CHUNKS_PER_MATMUL) == 0 else c
    grp = HG_CHUNKS_PER_MATMUL if (s_len // fc) % HG_CHUNKS_PER_MATMUL == 0 else 1
    gr = grp * fc
    n_groups = s_len // gr
    mid = fc // 2
    grow = lax.broadcasted_iota(jnp.int32, (gr, gr), 0)
    gcol = lax.broadcasted_iota(jnp.int32, (gr, gr), 1)
    same_chunk = (grow // fc) == (gcol // fc)
    gmasks = (same_chunk & (grow >= gcol), same_chunk & (grow <= gcol))
    chunk_of_row = lax.broadcasted_iota(jnp.int32, (gr, kd), 0) // fc

    def block_diag(a):
        return jnp.concatenate([jnp.where(chunk_of_row == i, a, jnp.zeros_like(a))
                                for i in range(grp)], axis=1)

    def phase_a(gi, lowest):
        rows = pl.ds(pl.multiple_of(gi * gr, gr), gr)
        for hh in range(hb):
            lanes = slice(hh * kd, (hh + 1) * kd)
            v = v_ref[0, rows, lanes]
            v_t = v.T
            q = qs_ref[0, rows, lanes].astype(F32)
            for direction in range(2):
                idx = hh * 2 + direction
                g = _chunk_cumsum((lff_ref, lfb_ref)[direction][0, rows, lanes], direction, fc)
                k = (kf_ref, kb_ref)[direction][0, rows, lanes].astype(F32)
                g_mid = [g[i * fc + mid:i * fc + mid + 1, :] for i in range(grp)]
                centred = g - jnp.concatenate([jnp.broadcast_to(r, (fc, kd)) for r in g_mid], axis=0)
                qt = (q * jnp.exp(centred)).astype(BF16)
                kt = (k * jnp.exp(-centred)).astype(BF16)
                qt_scr[idx, rows, :] = qt
                sc = jnp.where(gmasks[direction], _dot_nt(qt, kt), 0.0)
                o_refs[direction][rows, lanes] = _dot(sc.astype(BF16), v)
                u = _dot(v_t, block_diag(kt))
                for i in range(grp):
                    ci = gi * grp + i
                    u_scr[idx, ci] = u[:, i * kd:(i + 1) * kd]
                    edge = i * fc + (fc - 1 if direction == 0 else 0)
                    total = g[edge:edge + 1, :]
                    lowest = jnp.minimum(lowest, jnp.minimum(g_mid[i], total - g_mid[i]))
                    vec_scr[idx, 0, pl.ds(ci, 1), :] = jnp.exp(total)
                    vec_scr[idx, 1, pl.ds(ci, 1), :] = jnp.exp(g_mid[i])
                    vec_scr[idx, 2, pl.ds(ci, 1), :] = jnp.exp(total - g_mid[i])
        return lowest

    lowest = lax.fori_loop(0, n_groups, phase_a, jnp.zeros((1, kd), F32))
    all_safe = jnp.min(lowest) >= HG_SAFE_LOG_DECAY
    state_ref[...] = jnp.zeros_like(state_ref)

    @pl.when(all_safe)
    def _():
        def phase_b(j, carry):
            for hh in range(hb):
                lanes = slice(hh * kd, (hh + 1) * kd)
                for direction in range(2):
                    idx = hh * 2 + direction
                    gi = j if direction == 0 else n_groups - 1 - j
                    rows = pl.ds(pl.multiple_of(gi * gr, gr), gr)
                    st = state_ref[idx]
                    seen = [None] * grp
                    for i in (range(grp) if direction == 0 else reversed(range(grp))):
                        ci = gi * grp + i
                        at = pl.ds(ci, 1)
                        seen[i] = (st * vec_scr[idx, 1, at, :]).astype(BF16)
                        st = st * vec_scr[idx, 0, at, :] + u_scr[idx, ci] * vec_scr[idx, 2, at, :]
                    state_ref[idx] = st
                    o_refs[direction][rows, lanes] += _dot_nt(
                        block_diag(qt_scr[idx, rows, :]), jnp.concatenate(seen, axis=1))
            return carry

        lax.fori_loop(0, n_groups, phase_b, 0)

    @pl.when(jnp.logical_not(all_safe))
    def _():
        lax.fori_loop(0, n_chunks, step, 0)

    blk = 256 if s_len % 256 == 0 else c

    def epilogue(i, carry):
        rows = pl.ds(pl.multiple_of(i * blk, blk), blk)
        for hh in range(hb):
            lanes = slice(hh * kd, (hh + 1) * kd)
            o = ofw_ref[rows, lanes] + obw_ref[rows, lanes]
            y = o * _rms_scale(o, HG_VDIM) * onorm_ref[...]
            out_ref[0, rows, lanes] = (y * og_ref[0, rows, lanes].astype(F32)).astype(BF16)
        return carry

    lax.fori_loop(0, s_len // blk, epilogue, 0)


def _hgrn(qs, lf, kk, v, og, onorm):
    b, s, _ = qs.shape
    hb = HG_HEADS_PER_STEP
    w = hb * HG_KDIM
    n_hsteps = HG_HEADS // hb
    fwd = pl.BlockSpec((1, s, w), lambda bi, hi: (bi, 0, hi))
    bwd = pl.BlockSpec((1, s, w), lambda bi, hi: (bi, 0, n_hsteps + hi))
    return pl.pallas_call(
        _hgrn_kernel,
        grid=(b, n_hsteps),
        in_specs=[fwd, fwd, bwd, fwd, bwd, fwd, fwd, _const_spec(onorm.shape)],
        out_specs=fwd,
        out_shape=jax.ShapeDtypeStruct((b, s, HG_V), BF16),
        scratch_shapes=[
            pltpu.VMEM((2 * hb, HG_VDIM, HG_KDIM), F32),
            pltpu.VMEM((s, w), F32),
            pltpu.VMEM((s, w), F32),
            pltpu.VMEM((2 * hb, s, HG_KDIM), BF16),
            pltpu.VMEM((2 * hb, s // HG_CHUNK, HG_VDIM, HG_KDIM), F32),
            pltpu.VMEM((2 * hb, 3, max(s // HG_CHUNK, 8), HG_KDIM), F32),
            pltpu.VMEM((2 * hb, HG_CHUNK, HG_KDIM), F32),
            pltpu.VMEM((2 * hb, HG_CHUNK, HG_KDIM), F32),
            pltpu.VMEM((2 * hb, HG_CHUNK, HG_VDIM), F32),
        ],
        compiler_params=_cparams(("parallel", "parallel")),
        name="hgrn",
    )(qs, lf, lf, kk, kk, v, og, onorm)


def _mlaprep_kernel(mla_ref, pos_t_ref, wq_t_ref, wk_t_ref, wv_ref, gq_t_ref, gk_t_ref, freq_t_ref,
                    q_t_ref, k_t_ref, v_ref):
    cq = mla_ref[:, 0:MLA_Q_RANK]
    ckr = mla_ref[:, MLA_Q_RANK:]
    half = MLA_ROPE // 2
    tm = mla_ref.shape[0]

    lane_tile = lambda a: jnp.concatenate([a] * (tm // LANES), axis=1)
    ang_t = lane_tile(freq_t_ref[...]) * pos_t_ref[...]
    cos_t = jnp.cos(ang_t)
    sin_t = jnp.sin(ang_t)

    def norm_rope_t(x_t, gain_ref, out_ref):
        gain_t = lane_tile(gain_ref[...])
        for hd in range(MLA_HEADS):
            r0 = hd * LANES
            x = x_t[r0:r0 + LANES, :]
            y = x * lax.rsqrt(jnp.sum(x * x, axis=0, keepdims=True) * (1.0 / MLA_QK) + EPS) * gain_t
            x1 = y[MLA_NOPE:MLA_NOPE + half, :]
            x2 = y[MLA_NOPE + half:MLA_QK, :]
            out_ref[r0:r0 + MLA_NOPE, :] = y[0:MLA_NOPE, :].astype(BF16)
            out_ref[r0 + MLA_NOPE:r0 + MLA_NOPE + half, :] = (x1 * cos_t - x2 * sin_t).astype(BF16)
            out_ref[r0 + MLA_NOPE + half:r0 + MLA_QK, :] = (x2 * cos_t + x1 * sin_t).astype(BF16)
            out_ref[r0 + MLA_QK:r0 + LANES, :] = jnp.zeros((LANES - MLA_QK, tm), BF16)

    norm_rope_t(_dot_nt(wq_t_ref[...], cq), gq_t_ref, q_t_ref)
    norm_rope_t(_dot_nt(wk_t_ref[...], ckr), gk_t_ref, k_t_ref)
    v_ref[...] = _dot(ckr, wv_ref[...]).astype(BF16)


def _mlaprep(mla_in, pos_t, wq_t, wk_t, wv, gq_t, gk_t, freq_t):
    t = mla_in.shape[0]
    tm = MLAPREP_ROW_TILE
    row = lambda w: pl.BlockSpec((tm, w), lambda i: (i, 0))
    col = lambda h: pl.BlockSpec((h, tm), lambda i: (0, i))
    hw = MLA_HEADS * LANES
    vw = MLA_HEADS * MLA_VDIM
    return pl.pallas_call(
        _mlaprep_kernel,
        grid=(t // tm,),
        in_specs=[row(mla_in.shape[1]), col(1), _const_spec(wq_t.shape), _const_spec(wk_t.shape),
                  _const_spec(wv.shape), _const_spec(gq_t.shape), _const_spec(gk_t.shape),
                  _const_spec(freq_t.shape)],
        out_specs=(col(hw), col(hw), row(vw)),
        out_shape=(jax.ShapeDtypeStruct((hw, t), BF16), jax.ShapeDtypeStruct((hw, t), BF16),
                   jax.ShapeDtypeStruct((t, vw), BF16)),
        compiler_params=_cparams(("parallel",)),
        name="mlaprep",
    )(mla_in, pos_t, wq_t, wk_t, wv, gq_t, gk_t, freq_t)


def _attn_kernel(q_t_ref, k_t_ref, v_ref, o_ref):
    for j in range(ATTN_HEADS_PER_STEP):
        rows = slice(j * LANES, (j + 1) * LANES)
        s = _dot_tn(q_t_ref[rows, :], k_t_ref[rows, :])
        p = jnp.exp(s - jnp.max(s, axis=-1, keepdims=True))
        l = jnp.sum(p, axis=-1, keepdims=True)
        vl = slice(j * MLA_VDIM, (j + 1) * MLA_VDIM)
        o = _dot(p.astype(BF16), v_ref[0, :, vl])
        o_ref[0, :, vl] = (o / l).astype(BF16)


def _attn(q_t, k_t, v):
    b, s, _ = v.shape
    tq = min(ATTN_Q_TILE, s)
    nh = ATTN_HEADS_PER_STEP
    n_q = s // tq
    return pl.pallas_call(
        _attn_kernel,
        grid=(b, MLA_HEADS // nh, n_q),
        in_specs=[pl.BlockSpec((nh * LANES, tq), lambda bi, hp, i: (hp, bi * n_q + i)),
                  pl.BlockSpec((nh * LANES, s), lambda bi, hp, i: (hp, bi)),
                  pl.BlockSpec((1, s, nh * MLA_VDIM), lambda bi, hp, i: (bi, 0, hp))],
        out_specs=pl.BlockSpec((1, tq, nh * MLA_VDIM), lambda bi, hp, i: (bi, i, hp)),
        out_shape=jax.ShapeDtypeStruct((b, s, MLA_HEADS * MLA_VDIM), BF16),
        compiler_params=_cparams(("parallel", "parallel", "arbitrary")),
        name="attn",
    )(q_t, k_t, v)


def _merge_kernel(x_ref, ya_ref, yb_ref, gates_ref, woa_ref, wob_ref, wout_ref, lnm_ref, wr_t_ref,
                  br_t_ref, x1_ref, hm8_ref, route_t_ref, xb_zero_ref):
    d = x_ref.shape[-1]
    xb_zero_ref[...] = jnp.zeros_like(xb_zero_ref)
    y_a = _dot(ya_ref[...], woa_ref[...])
    y_b = _dot(yb_ref[...], wob_ref[...])
    merged = gates_ref[:, 0:d].astype(F32) * y_a + gates_ref[:, d:2 * d].astype(F32) * y_b
    x1 = x_ref[...] + _dot(merged.astype(BF16), wout_ref[...])
    x1_ref[...] = x1
    hm = x1 * _rms_scale(x1, d) * lnm_ref[...]
    _store_token_tiles(hm8_ref, 0, hm)

    tm = hm.shape[0]
    hm_hi = hm.astype(BF16)
    hm_lo = (hm - hm_hi.astype(F32)).astype(BF16)
    hh = _dot_nt(wr_t_ref[...], hm_hi)
    bias = jnp.concatenate([br_t_ref[...]] * (tm // LANES), axis=1)
    logits = hh[0:LANES] + hh[LANES:2 * LANES] + _dot_nt(wr_t_ref[0:LANES, :], hm_lo) + bias
    neg = -jnp.inf
    gl = logits[0:N_GROUPS]
    gmax = jnp.max(gl, axis=0, keepdims=True)
    p_group = 1.0 / jnp.sum(jnp.exp(gl - gmax), axis=0, keepdims=True)
    g_row = lax.broadcasted_iota(jnp.int32, gl.shape, 0)
    g_sel = jnp.min(jnp.where(gl == gmax, g_row, N_GROUPS), axis=0, keepdims=True)
    el_all = logits[N_GROUPS:N_GROUPS + N_EXPERTS]
    e_row = lax.broadcasted_iota(jnp.int32, el_all.shape, 0)
    el = jnp.where(e_row // EXPERTS_PER_GROUP == g_sel, el_all, neg)
    v1 = jnp.max(el, axis=0, keepdims=True)
    i1 = jnp.min(jnp.where(el == v1, e_row, N_EXPERTS), axis=0, keepdims=True)
    el2 = jnp.where(e_row == i1, neg, el)
    v2 = jnp.max(el2, axis=0, keepdims=True)
    i2 = jnp.min(jnp.where(el2 == v2, e_row, N_EXPERTS), axis=0, keepdims=True)
    e21 = jnp.exp(v2 - v1)
    w1 = p_group / (1.0 + e21)
    w2 = w1 * e21
    out_row = lax.broadcasted_iota(jnp.int32, route_t_ref.shape, 0)
    route_t_ref[...] = jnp.where(out_row == 0, i1.astype(F32),
                                 jnp.where(out_row == 1, i2.astype(F32),
                                           jnp.where(out_row == 2, w1,
                                                     jnp.where(out_row == 3, w2, 0.0))))


def _merge(x2, ya, yb, gates, woa, wob, wout, lnm, wr, br):
    t, d = x2.shape
    tm = MERGE_ROW_TILE
    row = lambda w: pl.BlockSpec((tm, w), lambda i: (i, 0))
    xb_rows = _moe_blocks(t) * MOE_ROWS * TOKEN_ROWS
    assert xb_rows % (t // tm) == 0
    return pl.pallas_call(
        _merge_kernel,
        grid=(t // tm,),
        in_specs=[row(d), row(ya.shape[1]), row(yb.shape[1]), row(2 * d), _const_spec(woa.shape),
                  _const_spec(wob.shape), _const_spec(wout.shape), _const_spec(lnm.shape),
                  _const_spec(wr.shape), _const_spec(br.shape)],
        out_specs=(row(d), pl.BlockSpec((tm * TOKEN_ROWS, LANES), lambda i: (i, 0)),
                   pl.BlockSpec((ROUTE_ROWS, tm), lambda i: (0, i)),
                   pl.BlockSpec((xb_rows // (t // tm), LANES), lambda i: (i, 0))),
        out_shape=(jax.ShapeDtypeStruct((t, d), F32),
                   jax.ShapeDtypeStruct((t * TOKEN_ROWS, LANES), jnp.uint32),
                   jax.ShapeDtypeStruct((ROUTE_ROWS, t), F32),
                   jax.ShapeDtypeStruct((xb_rows, LANES), jnp.uint32)),
        compiler_params=_cparams(("parallel",)),
        name="merge",
    )(x2, ya, yb, gates, woa, wob, wout, lnm, wr, br)


_HIGH_HALF = 0xFFFF0000


def _store_token_tiles(ref4, tok0, val):
    n, width = val.shape
    as_bits = lambda a: lax.bitcast_convert_type(a.astype(BF16).astype(F32), jnp.uint32)
    words = (as_bits(val[:, :width // 2]) >> 16) | (as_bits(val[:, width // 2:]) & jnp.uint32(_HIGH_HALF))
    for j in range(TOKEN_ROWS):
        ref4[pl.ds(tok0 * TOKEN_ROWS + j, n, stride=TOKEN_ROWS), :] = words[:, j * LANES:(j + 1) * LANES]


def _load_token_tiles(ref4, tok0, n):
    words = jnp.concatenate(
        [ref4[pl.ds(tok0 * TOKEN_ROWS + j, n, stride=TOKEN_ROWS), :] for j in range(TOKEN_ROWS)],
        axis=1)
    low = lax.bitcast_convert_type(words << 16, F32).astype(BF16)
    high = lax.bitcast_convert_type(words & jnp.uint32(_HIGH_HALF), F32).astype(BF16)
    return jnp.concatenate([low, high], axis=1)


def _token_tile(ref8, tok):
    return ref8.at[pl.ds(pl.multiple_of(tok * TOKEN_ROWS, TOKEN_ROWS), TOKEN_ROWS)]


def _dispatch_kernel(dest0_ref, dest1_ref, hm8_ref, xb_in_ref, xb8_ref, sem):
    del xb_in_ref
    n_tok = DISPATCH_TOKENS
    for r in range(n_tok):
        src = hm8_ref.at[pl.ds(r * TOKEN_ROWS, TOKEN_ROWS)]
        for slot, dest_ref in enumerate((dest0_ref, dest1_ref)):
            pltpu.make_async_copy(src, _token_tile(xb8_ref, dest_ref[r]), sem).start(priority=slot)
    for _ in range(2):
        pltpu.make_async_copy(hm8_ref, xb8_ref.at[pl.ds(0, n_tok * TOKEN_ROWS)], sem).wait()


def _dispatch(dest0, dest1, hm8, xb_init):
    t = hm8.shape[0] // TOKEN_ROWS
    idx = pl.BlockSpec((DISPATCH_TOKENS,), lambda i: (i,), memory_space=pltpu.SMEM)
    return pl.pallas_call(
        _dispatch_kernel,
        grid=(t // DISPATCH_TOKENS,),
        in_specs=[idx, idx,
                  pl.BlockSpec((DISPATCH_TOKENS * TOKEN_ROWS, LANES), lambda i: (i, 0)),
                  pl.BlockSpec(memory_space=pl.ANY)],
        out_specs=pl.BlockSpec(memory_space=pl.ANY),
        out_shape=jax.ShapeDtypeStruct(xb_init.shape, xb_init.dtype),
        scratch_shapes=[pltpu.SemaphoreType.DMA(())],
        input_output_aliases={3: 0},
        compiler_params=_cparams(("arbitrary",)),
        name="dispatch",
    )(dest0, dest1, hm8, xb_init)


def _experts_kernel(be_ref, nu_ref, nxt_ref, par_ref, xb8_ref, w1_hbm, w3_hbm, w2_hbm, yb8_ref,
                    w1f, w3f, w2f, sems, w1c, w3c, w2c):
    j = pl.program_id(0)

    def fetch(e, slot):
        return [pltpu.make_async_copy(src.at[e], dst.at[slot], sems.at[slot])
                for src, dst in ((w1_hbm, w1f), (w3_hbm, w3f), (w2_hbm, w2f))]

    for half in range(2):
        blk = 2 * j + half
        used = blk < nu_ref[0]
        new_expert = jnp.logical_or(blk == 0, be_ref[blk] != be_ref[jnp.maximum(blk - 1, 0)])
        tok0 = half * MOE_ROWS

        @pl.when(jnp.logical_and(used, new_expert))
        def _():
            slot = par_ref[blk]

            @pl.when(blk == 0)
            def _():
                for cp in fetch(be_ref[0], 0):
                    cp.start()

            for cp in fetch(be_ref[blk], slot):
                cp.wait()

            @pl.when(nxt_ref[blk] >= 0)
            def _():
                for cp in fetch(nxt_ref[blk], 1 - slot):
                    cp.start()

            w1c[...] = w1f[slot].astype(BF16)
            w3c[...] = w3f[slot].astype(BF16)
            w2c[...] = w2f[slot].astype(BF16)

        @pl.when(used)
        def _():
            x = _load_token_tiles(xb8_ref, tok0, MOE_ROWS)
            hmid = (_silu(_dot(x, w1c[...])) * _dot(x, w3c[...])).astype(BF16)
            _store_token_tiles(yb8_ref, tok0, _dot(hmid, w2c[...]))

        @pl.when(jnp.logical_not(used))
        def _():
            rows = pl.ds(tok0 * TOKEN_ROWS, MOE_ROWS * TOKEN_ROWS)
            yb8_ref[rows, :] = jnp.zeros((MOE_ROWS * TOKEN_ROWS, LANES), jnp.uint32)


def _experts(block_expert, n_used, next_expert, slot_parity, xb8, w1, w3, w2):
    nblk = block_expert.shape[0]
    nsteps = nblk // 2
    d = w1.shape[1]
    de = w1.shape[2]
    step_rows = 2 * MOE_ROWS * TOKEN_ROWS
    rows_in = lambda j, be, nu, nx, pa: (jnp.minimum(j, (nu[0] - 1) // 2), 0)
    hbm = pl.BlockSpec(memory_space=pl.ANY)
    return pl.pallas_call(
        _experts_kernel,
        grid_spec=pltpu.PrefetchScalarGridSpec(
            num_scalar_prefetch=4,
            grid=(nsteps,),
            in_specs=[pl.BlockSpec((step_rows, LANES), rows_in), hbm, hbm, hbm],
            out_specs=pl.BlockSpec((step_rows, LANES), lambda j, be, nu, nx, pa: (j, 0)),
            scratch_shapes=[pltpu.VMEM((2, d, de), F32), pltpu.VMEM((2, d, de), F32),
                            pltpu.VMEM((2, de, d), F32), pltpu.SemaphoreType.DMA((2,)),
                            pltpu.VMEM((d, de), BF16), pltpu.VMEM((d, de), BF16),
                            pltpu.VMEM((de, d), BF16)]),
        out_shape=jax.ShapeDtypeStruct((nblk * MOE_ROWS * TOKEN_ROWS, LANES), jnp.uint32),
        compiler_params=_cparams(("arbitrary",)),
        name="experts",
    )(block_expert, n_used, next_expert, slot_parity, xb8, w1, w3, w2)


def _ple_kernel(dest0_ref, dest1_ref, dest0n_ref, dest1n_ref, x1_ref, route_ref, p_ref, ln_ref,
                wg_ref, wp_ref, yb8_ref, out_ref, ya0, ya1, yb0, yb1, sems):
    j = pl.program_id(0)
    last = pl.num_programs(0) - 1
    tm = PLE_ROW_TILE
    d = x1_ref.shape[1]

    def start(idx_refs, base, bufs, sem):
        for r in range(tm):
            for slot in range(2):
                pltpu.make_async_copy(_token_tile(yb8_ref, idx_refs[slot][base + r]),
                                      bufs[slot].at[pl.ds(r * TOKEN_ROWS, TOKEN_ROWS)],
                                      sem).start(priority=slot)

    def wait(bufs, sem):
        for buf in bufs:
            pltpu.make_async_copy(yb8_ref.at[pl.ds(0, tm * TOKEN_ROWS)], buf, sem).wait()

    def compute(bufs, rows):
        x2 = (x1_ref[rows, :]
              + route_ref[rows, 0:1] * _load_token_tiles(bufs[0], 0, tm).astype(F32)
              + route_ref[rows, 1:2] * _load_token_tiles(bufs[1], 0, tm).astype(F32))
        h = (x2 * _rms_scale(x2, d) * ln_ref[...]).astype(BF16)
        pp = _dot(p_ref[rows, :].astype(BF16), wp_ref[...])
        out_ref[rows, :] = x2 + pp * _sigmoid(_dot(h, wg_ref[...]))

    dest_now = (dest0_ref, dest1_ref)

    @pl.when(j == 0)
    def _():
        start(dest_now, 0, (ya0, ya1), sems.at[0])

    wait((ya0, ya1), sems.at[0])
    start(dest_now, tm, (yb0, yb1), sems.at[1])
    compute((ya0, ya1), slice(0, tm))

    wait((yb0, yb1), sems.at[1])
    start((dest0n_ref, dest1n_ref), 0, (ya0, ya1), sems.at[0])
    compute((yb0, yb1), slice(tm, 2 * tm))

    @pl.when(j == last)
    def _():
        wait((ya0, ya1), sems.at[0])


def _ple(dest0, dest1, x1, route, p2, ln, wg, wp, yb):
    t, d = x1.shape
    tm = PLE_ROW_TILE
    nsteps = t // (2 * tm)
    row = lambda w: pl.BlockSpec((2 * tm, w), lambda j: (j, 0))
    idx = pl.BlockSpec((2 * tm,), lambda j: (j,), memory_space=pltpu.SMEM)
    idx_next = pl.BlockSpec((2 * tm,), lambda j: (jnp.minimum(j + 1, nsteps - 1),),
                            memory_space=pltpu.SMEM)
    return pl.pallas_call(
        _ple_kernel,
        grid=(nsteps,),
        in_specs=[idx, idx, idx_next, idx_next,
                  row(d), row(route.shape[1]), row(p2.shape[1]), _const_spec(ln.shape),
                  _const_spec(wg.shape), _const_spec(wp.shape),
                  pl.BlockSpec(memory_space=pl.ANY)],
        out_specs=row(d),
        out_shape=jax.ShapeDtypeStruct((t, d), F32),
        scratch_shapes=([pltpu.VMEM((tm * TOKEN_ROWS, LANES), jnp.uint32)] * 4
                        + [pltpu.SemaphoreType.DMA((2,))]),
        compiler_params=_cparams(("arbitrary",)),
        name="ple",
    )(dest0, dest1, dest0, dest1, x1, route, p2, ln, wg, wp, yb)


def _head_pad(w, width):
    r = w.shape[0]
    w = w.reshape(r, MLA_HEADS, width)
    return jnp.pad(w, ((0, 0), (0, 0), (0, LANES - width))).reshape(r, MLA_HEADS * LANES)


def _rope_table():
    half = MLA_ROPE // 2
    inv_freq = ROPE_THETA ** (-np.arange(half, dtype=np.float32) / half)
    return jnp.asarray(np.broadcast_to(inv_freq[:, None], (half, LANES)))


def _gain_t(gain, scale):
    return jnp.broadcast_to((jnp.pad(gain, (0, LANES - MLA_QK)) * scale)[:, None], (LANES, LANES))


def _moe_blocks(t):
    nblk = (2 * t + N_EXPERTS * (MOE_ROWS - 1)) // MOE_ROWS
    return nblk + nblk % 2


def _route_tables(ids_t):
    t = ids_t.shape[1]
    a = 2 * t
    e = ids_t.reshape(a)
    onehot = (e[:, None] == jnp.arange(N_EXPERTS, dtype=jnp.int32)[None, :]).astype(jnp.int32)
    csum = jnp.cumsum(onehot, axis=0)
    counts = csum[-1]
    rank = jnp.sum(csum * onehot, axis=1) - 1
    pcounts = (counts + MOE_ROWS - 1) // MOE_ROWS * MOE_ROWS
    pends = jnp.cumsum(pcounts)
    pstarts = pends - pcounts
    dest = (jnp.sum(onehot * pstarts[None, :], axis=1) + rank).astype(jnp.int32)
    nblk = _moe_blocks(t)
    n_used = (pends[-1] // MOE_ROWS).astype(jnp.int32)
    blk_start = jnp.arange(nblk, dtype=jnp.int32) * MOE_ROWS
    blk_start = jnp.minimum(blk_start, pends[-1] - 1)
    block_expert = jnp.sum((pends[None, :] <= blk_start[:, None]).astype(jnp.int32), axis=1)
    block_expert = jnp.minimum(block_expert, N_EXPERTS - 1).astype(jnp.int32)
    block_expert, n_used = lax.optimization_barrier((block_expert, n_used))
    blk = jnp.arange(nblk, dtype=jnp.int32)
    prev = jnp.concatenate([jnp.full((1,), -1, jnp.int32), block_expert[:-1]])
    starts_expert = (blk < n_used) & (block_expert != prev)
    slot_parity = (jnp.cumsum(starts_expert.astype(jnp.int32)) - 1) % 2
    start_at_or_after = jnp.flip(lax.cummin(jnp.flip(jnp.where(starts_expert, blk, nblk))))
    next_start = jnp.concatenate([start_at_or_after[1:], jnp.full((1,), nblk, jnp.int32)])
    next_expert = jnp.where(next_start < nblk, block_expert[jnp.minimum(next_start, nblk - 1)], -1)
    return (dest, block_expert, n_used.reshape(1), next_expert.astype(jnp.int32),
            slot_parity.astype(jnp.int32), nblk)


def kernel(x, p, positions, ln_mix, w_in, hg_lb, hg_onorm, w_oA, mla_qa_norm, mla_kva_norm, w_uq,
           w_ukv, q_norm, k_norm, w_oB, w_out, ln_moe, w_rg, b_rg, w_re, b_re, w1, w3, w2, ln_ple,
           w_ple_gate, w_ple_proj):
    b, s, d = x.shape
    t = b * s
    depth = w_in.shape[0]
    lb_all = jnp.cumsum(jax.nn.softmax(hg_lb.astype(F32), axis=1), axis=1)
    pos_t = positions.astype(F32).reshape(1, t)
    freq_t = _rope_table()
    xc = x.reshape(t, d)

    for layer in range(depth):
        wi = w_in[layer]
        n_hg = 3 * HG_F + 2 * HG_V
        n_mla = MLA_Q_RANK + MLA_KV_RANK + MLA_ROPE
        mla_w = -(-n_mla // LANES) * LANES
        whg = wi[:, :n_hg].astype(BF16)
        wmla = jnp.pad(wi[:, n_hg:n_hg + n_mla], ((0, 0), (0, mla_w - n_mla))).astype(BF16)
        wgate = wi[:, n_hg + n_mla:].astype(BF16)
        lb = lb_all[:, layer, :]

        qs, lf, kk, v, og, mla_in, gates = _inproj(
            xc, ln_mix[layer][None, :], whg, wmla, wgate, lb, mla_qa_norm[layer][None, :],
            mla_kva_norm[layer][None, :])

        r3 = lambda a: a.reshape(b, s, a.shape[-1])
        ya = _hgrn(r3(qs), r3(lf), r3(kk), r3(v), r3(og), hg_onorm[layer][None, :]).reshape(t, HG_V)

        kv_w = MLA_NOPE + MLA_VDIM
        wkv = w_ukv[layer].reshape(MLA_KV_RANK, MLA_HEADS, kv_w)
        ckr_w = mla_w - MLA_Q_RANK
        wq = _head_pad(w_uq[layer], MLA_QK).astype(BF16)
        wk_nope = _head_pad(wkv[:, :, :MLA_NOPE].reshape(MLA_KV_RANK, MLA_HEADS * MLA_NOPE), MLA_NOPE)
        place = np.zeros((ckr_w - MLA_KV_RANK, MLA_HEADS, LANES), np.float32)
        for j in range(MLA_ROPE):
            place[j, :, MLA_NOPE + j] = 1.0
        wk = jnp.concatenate([wk_nope, jnp.asarray(place.reshape(ckr_w - MLA_KV_RANK, -1))],
                             axis=0).astype(BF16)
        wv = jnp.pad(wkv[:, :, MLA_NOPE:].reshape(MLA_KV_RANK, MLA_HEADS * MLA_VDIM),
                     ((0, ckr_w - MLA_KV_RANK), (0, 0))).astype(BF16)
        qh_t, kh_t, vh = _mlaprep(mla_in, pos_t, wq.T, wk.T, wv, _gain_t(q_norm[layer], MLA_QK ** -0.5),
                                  _gain_t(k_norm[layer], 1.0), freq_t)
        yb = _attn(qh_t, kh_t, r3(vh)).reshape(t, MLA_HEADS * MLA_VDIM)

        wr = jnp.pad(jnp.concatenate([w_rg[layer], w_re[layer]], axis=1),
                     ((0, 0), (0, LANES - N_GROUPS - N_EXPERTS)))
        wr_hi = wr.astype(BF16)
        wr_t = jnp.concatenate([wr_hi, (wr - wr_hi.astype(F32)).astype(BF16)], axis=1).T
        br_t = jnp.broadcast_to(jnp.pad(jnp.concatenate([b_rg[layer], b_re[layer]]),
                                        (0, LANES - N_GROUPS - N_EXPERTS))[:, None], (LANES, LANES))
        x1, hm8, route_t, xb_zero = _merge(xc, ya, yb, gates, w_oA[layer].astype(BF16),
                                  w_oB[layer].astype(BF16), w_out[layer].astype(BF16),
                                  ln_moe[layer][None, :], wr_t, br_t)

        route = route_t[2:4].T
        dest, block_expert, n_used, next_expert, slot_parity, nblk = _route_tables(
            route_t[0:2].astype(jnp.int32))
        dest0, dest1 = dest[:t], dest[t:]
        xb8 = _dispatch(dest0, dest1, hm8, xb_zero)
        yexp = _experts(block_expert, n_used, next_expert, slot_parity, xb8,
                        w1[layer], w3[layer], w2[layer])

        xc = _ple(dest0, dest1, x1, route, p[layer].reshape(t, -1), ln_ple[layer][None, :],
                  w_ple_gate[layer].astype(BF16), w_ple_proj[layer].astype(BF16), yexp)
    return xc.reshape(b, s, d)
```

```python
import numpy as np
import jax
import jax.numpy as jnp
from jax import lax
from jax.experimental import pallas as pl
from jax.experimental.pallas import tpu as pltpu

F32 = jnp.float32
BF16 = jnp.bfloat16

HG_HEADS = 4
HG_KDIM = 128
HG_VDIM = 128
HG_F = HG_HEADS * HG_KDIM
HG_V = HG_HEADS * HG_VDIM
MLA_HEADS = 8
MLA_NOPE = 64
MLA_ROPE = 32
MLA_VDIM = 64
MLA_QK = MLA_NOPE + MLA_ROPE
MLA_Q_RANK = 256
MLA_KV_RANK = 128
ROPE_THETA = 10000.0
N_GROUPS = 8
EXPERTS_PER_GROUP = 8
N_EXPERTS = N_GROUPS * EXPERTS_PER_GROUP
D_EXPERT = 256
EPS = 1e-6

LANES = 128
ROUTE_ROWS = 8
TOKEN_ROWS = 4
VMEM_LIMIT_BYTES = 56 * 1024 * 1024

PLE_ROW_TILE = 256
MERGE_ROW_TILE = 512
INPROJ_ROW_TILE = 256
MLAPREP_ROW_TILE = 512
HG_CHUNK = 64
HG_HEADS_PER_STEP = 2
HG_FAST_CHUNK = 128
HG_CHUNKS_PER_MATMUL = 2
ATTN_Q_TILE = 512
ATTN_HEADS_PER_STEP = 8
MOE_ROWS = 256
DISPATCH_TOKENS = 1024
HG_SAFE_LOG_DECAY = -75.0


def _cparams(sem):
    return pltpu.CompilerParams(dimension_semantics=sem, vmem_limit_bytes=VMEM_LIMIT_BYTES)


def _const_spec(shape):
    nd = len(shape)
    return pl.BlockSpec(shape, lambda *_: (0,) * nd)


def _sigmoid(x):
    return 1.0 / (1.0 + jnp.exp(-x))


def _silu(x):
    return x * _sigmoid(x)


def _rms_scale(x, n):
    return lax.rsqrt(jnp.sum(x * x, axis=-1, keepdims=True) * (1.0 / n) + EPS)


def _dot(a, b):
    return jnp.dot(a, b, preferred_element_type=F32)


def _dot_nt(a, b):
    return lax.dot_general(a, b, (((1,), (1,)), ((), ())), preferred_element_type=F32)


def _dot_tn(a, b):
    return lax.dot_general(a, b, (((0,), (0,)), ((), ())), preferred_element_type=F32)


def _chunk_cumsum(a, direction, chunk):
    n = a.shape[0]
    in_chunk = lax.broadcasted_iota(jnp.int32, a.shape, 0) % chunk
    k = 1
    while k < chunk:
        if direction == 0:
            a = a + jnp.where(in_chunk >= k, pltpu.roll(a, k, 0), 0.0)
        else:
            a = a + jnp.where(in_chunk < chunk - k, pltpu.roll(a, n - k, 0), 0.0)
        k *= 2
    return a


def _inproj_kernel(x_ref, g_ref, whg_ref, wmla_ref, wgate_ref, lb_ref, nq_ref, nkv_ref,
                   qs_ref, lf_ref, kk_ref, v_ref, og_ref, mla_ref, gates_ref):
    x = x_ref[...]
    d = x.shape[-1]
    h = (x * _rms_scale(x, d) * g_ref[...]).astype(BF16)

    qs_ref[...] = _silu(_dot(h, whg_ref[:, 0:HG_F])).astype(BF16)
    for direction in range(2):
        cols = slice(HG_F * (1 + direction), HG_F * (2 + direction))
        out_cols = slice(HG_F * direction, HG_F * (direction + 1))
        lb = lb_ref[direction:direction + 1, :]
        f = lb + (1.0 - lb) * _sigmoid(_dot(h, whg_ref[:, cols]))
        lf_ref[:, out_cols] = jnp.log(f)
        kk_ref[:, out_cols] = (1.0 - f).astype(BF16)
    v_ref[...] = _dot(h, whg_ref[:, 3 * HG_F:3 * HG_F + HG_V]).astype(BF16)
    og_ref[...] = _silu(_dot(h, whg_ref[:, 3 * HG_F + HG_V:3 * HG_F + 2 * HG_V])).astype(BF16)

    zm = _dot(h, wmla_ref[...])
    cq = zm[:, 0:MLA_Q_RANK]
    mla_ref[:, 0:MLA_Q_RANK] = (cq * _rms_scale(cq, MLA_Q_RANK) * nq_ref[...]).astype(BF16)
    ckv = zm[:, MLA_Q_RANK:MLA_Q_RANK + MLA_KV_RANK]
    mla_ref[:, MLA_Q_RANK:MLA_Q_RANK + MLA_KV_RANK] = (
        ckv * _rms_scale(ckv, MLA_KV_RANK) * nkv_ref[...]).astype(BF16)
    mla_ref[:, MLA_Q_RANK + MLA_KV_RANK:] = zm[:, MLA_Q_RANK + MLA_KV_RANK:].astype(BF16)

    for half in range(2):
        cols = slice(d * half, d * (half + 1))
        gates_ref[:, cols] = _sigmoid(_dot(h, wgate_ref[:, cols])).astype(BF16)


def _inproj(x2, ln_mix, whg, wmla, wgate, lb, nq, nkv):
    t, d = x2.shape
    tm = INPROJ_ROW_TILE
    mla_w = wmla.shape[1]
    row = lambda w: pl.BlockSpec((tm, w), lambda i: (i, 0))
    out_shape = (
        jax.ShapeDtypeStruct((t, HG_F), BF16),
        jax.ShapeDtypeStruct((t, 2 * HG_F), F32),
        jax.ShapeDtypeStruct((t, 2 * HG_F), BF16),
        jax.ShapeDtypeStruct((t, HG_V), BF16),
        jax.ShapeDtypeStruct((t, HG_V), BF16),
        jax.ShapeDtypeStruct((t, mla_w), BF16),
        jax.ShapeDtypeStruct((t, 2 * d), BF16),
    )
    return pl.pallas_call(
        _inproj_kernel,
        grid=(t // tm,),
        in_specs=[row(d), _const_spec((1, d)), _const_spec(whg.shape), _const_spec(wmla.shape),
                  _const_spec(wgate.shape), _const_spec(lb.shape), _const_spec(nq.shape),
                  _const_spec(nkv.shape)],
        out_specs=(row(HG_F), row(2 * HG_F), row(2 * HG_F), row(HG_V), row(HG_V), row(mla_w),
                   row(2 * d)),
        out_shape=out_shape,
        compiler_params=_cparams(("parallel",)),
        name="inproj",
    )(x2, ln_mix, whg, wmla, wgate, lb, nq, nkv)


def _hgrn_kernel(qs_ref, lff_ref, lfb_ref, kf_ref, kb_ref, v_ref, og_ref, onorm_ref, out_ref,
                 state_ref, ofw_ref, obw_ref, qt_scr, u_scr, vec_scr, g_scr, k_scr, v_scr):
    c = HG_CHUNK
    s_len = qs_ref.shape[1]
    n_chunks = s_len // c
    hb = HG_HEADS_PER_STEP
    kd = HG_KDIM

    row = lax.broadcasted_iota(jnp.int32, (c, c), 0)
    col = lax.broadcasted_iota(jnp.int32, (c, c), 1)
    masks = (row >= col, row <= col)
    rowc = lax.broadcasted_iota(jnp.int32, (c, kd), 0)
    o_refs = (ofw_ref, obw_ref)

    def load(j):
        chains = []
        for hh in range(hb):
            lanes = slice(hh * kd, (hh + 1) * kd)
            for direction in range(2):
                chunk = j if direction == 0 else n_chunks - 1 - j
                rows = pl.ds(pl.multiple_of(chunk * c, c), c)
                g = _chunk_cumsum((lff_ref, lfb_ref)[direction][0, rows, lanes], direction, c)
                total = g[c - 1:c, :] if direction == 0 else g[0:1, :]
                chains.append(dict(
                    idx=hh * 2 + direction, direction=direction, rows=rows, lanes=lanes, g=g,
                    total=total,
                    q=qs_ref[0, rows, lanes].astype(F32),
                    k=(kf_ref, kb_ref)[direction][0, rows, lanes].astype(F32),
                    v=v_ref[0, rows, lanes]))
        return chains

    def finish(ch, o, kdec_t_v_scaled):
        (ofw_ref, obw_ref)[ch["direction"]][ch["rows"], ch["lanes"]] = o
        state_ref[ch["idx"]] = kdec_t_v_scaled

    def fast(ch):
        g = ch["g"]
        qt = (ch["q"] * jnp.exp(g)).astype(BF16)
        kt = (ch["k"] * jnp.exp(-g)).astype(BF16)
        st = state_ref[ch["idx"]]
        sc = jnp.where(masks[ch["direction"]], _dot_nt(qt, kt), 0.0)
        o = _dot_nt(qt, st.astype(BF16)) + _dot(sc.astype(BF16), ch["v"])
        finish(ch, o, (st + _dot_tn(ch["v"], kt)) * jnp.exp(ch["total"]))

    def robust(ch):
        g = ch["g"]
        direction = ch["direction"]
        st = state_ref[ch["idx"]]
        o0 = _dot_nt((ch["q"] * jnp.exp(g)).astype(BF16), st.astype(BF16))
        slot = ch["idx"]
        g_scr[slot] = g
        k_scr[slot] = ch["k"]
        v_scr[slot] = ch["v"].astype(F32)
        q = ch["q"]

        def body(s, acc):
            g_s = g_scr[slot, pl.ds(s, 1), :]
            seen = (rowc >= s) if direction == 0 else (rowc <= s)
            decay = jnp.where(seen, jnp.exp(jnp.minimum(g - g_s, 0.0)), 0.0)
            a = jnp.sum(q * decay * k_scr[slot, pl.ds(s, 1), :], axis=-1, keepdims=True)
            return acc + a * v_scr[slot, pl.ds(s, 1), :]

        o = lax.fori_loop(0, c, body, o0)
        kdec = (ch["k"] * jnp.exp(ch["total"] - g)).astype(BF16)
        finish(ch, o, st * jnp.exp(ch["total"]) + _dot_tn(ch["v"], kdec))

    def step(j, carry):
        chains = load(j)
        lowest = chains[0]["total"]
        for ch in chains[1:]:
            lowest = jnp.minimum(lowest, ch["total"])
        safe = jnp.min(lowest) >= HG_SAFE_LOG_DECAY

        @pl.when(safe)
        def _():
            for ch in chains:
                fast(ch)

        @pl.when(jnp.logical_not(safe))
        def _():
            for ch in chains:
                robust(ch)

        return carry

    fc = HG_FAST_CHUNK if s_len % (HG_FAST_CHUNK * HG_CHUNKS_PER_MATMUL) == 0 else c
    grp = HG_CHUNKS_PER_MATMUL if (s_len // fc) % HG_CHUNKS_PER_MATMUL == 0 else 1
    gr = grp * fc
    n_groups = s_len // gr
    mid = fc // 2
    grow = lax.broadcasted_iota(jnp.int32, (gr, gr), 0)
    gcol = lax.broadcasted_iota(jnp.int32, (gr, gr), 1)
    same_chunk = (grow // fc) == (gcol // fc)
    gmasks = (same_chunk & (grow >= gcol), same_chunk & (grow <= gcol))
    chunk_of_row = lax.broadcasted_iota(jnp.int32, (gr, kd), 0) // fc

    def block_diag(a):
        return jnp.concatenate([jnp.where(chunk_of_row == i, a, jnp.zeros_like(a))
                                for i in range(grp)], axis=1)

    def phase_a(gi, lowest):
        rows = pl.ds(pl.multiple_of(gi * gr, gr), gr)
        for hh in range(hb):
            lanes = slice(hh * kd, (hh + 1) * kd)
            v = v_ref[0, rows, lanes]
            v_t = v.T
            q = qs_ref[0, rows, lanes].astype(F32)
            for direction in range(2):
                idx = hh * 2 + direction
                g = _chunk_cumsum((lff_ref, lfb_ref)[direction][0, rows, lanes], direction, fc)
                k = (kf_ref, kb_ref)[direction][0, rows, lanes].astype(F32)
                g_mid = [g[i * fc + mid:i * fc + mid + 1, :] for i in range(grp)]
                centred = g - jnp.concatenate([jnp.broadcast_to(r, (fc, kd)) for r in g_mid], axis=0)
                qt = (q * jnp.exp(centred)).astype(BF16)
                kt = (k * jnp.exp(-centred)).astype(BF16)
                qt_scr[idx, rows, :] = qt
                sc = jnp.where(gmasks[direction], _dot_nt(qt, kt), 0.0)
                o_refs[direction][rows, lanes] = _dot(sc.astype(BF16), v)
                u = _dot(v_t, block_diag(kt))
                for i in range(grp):
                    ci = gi * grp + i
                    u_scr[idx, ci] = u[:, i * kd:(i + 1) * kd]
                    edge = i * fc + (fc - 1 if direction == 0 else 0)
                    total = g[edge:edge + 1, :]
                    lowest = jnp.minimum(lowest, jnp.minimum(g_mid[i], total - g_mid[i]))
                    vec_scr[idx, 0, pl.ds(ci, 1), :] = jnp.exp(total)
                    vec_scr[idx, 1, pl.ds(ci, 1), :] = jnp.exp(g_mid[i])
                    vec_scr[idx, 2, pl.ds(ci, 1), :] = jnp.exp(total - g_mid[i])
        return lowest

    lowest = lax.fori_loop(0, n_groups, phase_a, jnp.zeros((1, kd), F32))
    all_safe = jnp.min(lowest) >= HG_SAFE_LOG_DECAY
    state_ref[...] = jnp.zeros_like(state_ref)

    @pl.when(all_safe)
    def _():
        def phase_b(j, carry):
            for hh in range(hb):
                lanes = slice(hh * kd, (hh + 1) * kd)
                for direction in range(2):
                    idx = hh * 2 + direction
                    gi = j if direction == 0 else n_groups - 1 - j
                    rows = pl.ds(pl.multiple_of(gi * gr, gr), gr)
                    st = state_ref[idx]
                    seen = [None] * grp
                    for i in (range(grp) if direction == 0 else reversed(range(grp))):
                        ci = gi * grp + i
                        at = pl.ds(ci, 1)
                        seen[i] = (st * vec_scr[idx, 1, at, :]).astype(BF16)
                        st = st * vec_scr[idx, 0, at, :] + u_scr[idx, ci] * vec_scr[idx, 2, at, :]
                    state_ref[idx] = st
                    o_refs[direction][rows, lanes] += _dot_nt(
                        block_diag(qt_scr[idx, rows, :]), jnp.concatenate(seen, axis=1))
            return carry

        lax.fori_loop(0, n_groups, phase_b, 0)

    @pl.when(jnp.logical_not(all_safe))
    def _():
        lax.fori_loop(0, n_chunks, step, 0)

    blk = 256 if s_len % 256 == 0 else c

    def epilogue(i, carry):
        rows = pl.ds(pl.multiple_of(i * blk, blk), blk)
        for hh in range(hb):
            lanes = slice(hh * kd, (hh + 1) * kd)
            o = ofw_ref[rows, lanes] + obw_ref[rows, lanes]
            y = o * _rms_scale(o, HG_VDIM) * onorm_ref[...]
            out_ref[0, rows, lanes] = (y * og_ref[0, rows, lanes].astype(F32)).astype(BF16)
        return carry

    lax.fori_loop(0, s_len // blk, epilogue, 0)


def _hgrn(qs, lf, kk, v, og, onorm):
    b, s, _ = qs.shape
    hb = HG_HEADS_PER_STEP
    w = hb * HG_KDIM
    n_hsteps = HG_HEADS // hb
    fwd = pl.BlockSpec((1, s, w), lambda bi, hi: (bi, 0, hi))
    bwd = pl.BlockSpec((1, s, w), lambda bi, hi: (bi, 0, n_hsteps + hi))
    return pl.pallas_call(
        _hgrn_kernel,
        grid=(b, n_hsteps),
        in_specs=[fwd, fwd, bwd, fwd, bwd, fwd, fwd, _const_spec(onorm.shape)],
        out_specs=fwd,
        out_shape=jax.ShapeDtypeStruct((b, s, HG_V), BF16),
        scratch_shapes=[
            pltpu.VMEM((2 * hb, HG_VDIM, HG_KDIM), F32),
            pltpu.VMEM((s, w), F32),
            pltpu.VMEM((s, w), F32),
            pltpu.VMEM((2 * hb, s, HG_KDIM), BF16),
            pltpu.VMEM((2 * hb, s // HG_CHUNK, HG_VDIM, HG_KDIM), F32),
            pltpu.VMEM((2 * hb, 3, max(s // HG_CHUNK, 8), HG_KDIM), F32),
            pltpu.VMEM((2 * hb, HG_CHUNK, HG_KDIM), F32),
            pltpu.VMEM((2 * hb, HG_CHUNK, HG_KDIM), F32),
            pltpu.VMEM((2 * hb, HG_CHUNK, HG_VDIM), F32),
        ],
        compiler_params=_cparams(("parallel", "parallel")),
        name="hgrn",
    )(qs, lf, lf, kk, kk, v, og, onorm)


def _mlaprep_kernel(mla_ref, pos_t_ref, wq_t_ref, wk_t_ref, wv_ref, gq_t_ref, gk_t_ref, freq_t_ref,
                    q_t_ref, k_t_ref, v_ref):
    cq = mla_ref[:, 0:MLA_Q_RANK]
    ckr = mla_ref[:, MLA_Q_RANK:]
    half = MLA_ROPE // 2
    tm = mla_ref.shape[0]

    lane_tile = lambda a: jnp.concatenate([a] * (tm // LANES), axis=1)
    ang_t = lane_tile(freq_t_ref[...]) * pos_t_ref[...]
    cos_t = jnp.cos(ang_t)
    sin_t = jnp.sin(ang_t)

    def norm_rope_t(x_t, gain_ref, out_ref):
        gain_t = lane_tile(gain_ref[...])
        for hd in range(MLA_HEADS):
            r0 = hd * LANES
            x = x_t[r0:r0 + LANES, :]
            y = x * lax.rsqrt(jnp.sum(x * x, axis=0, keepdims=True) * (1.0 / MLA_QK) + EPS) * gain_t
            x1 = y[MLA_NOPE:MLA_NOPE + half, :]
            x2 = y[MLA_NOPE + half:MLA_QK, :]
            out_ref[r0:r0 + MLA_NOPE, :] = y[0:MLA_NOPE, :].astype(BF16)
            out_ref[r0 + MLA_NOPE:r0 + MLA_NOPE + half, :] = (x1 * cos_t - x2 * sin_t).astype(BF16)
            out_ref[r0 + MLA_NOPE + half:r0 + MLA_QK, :] = (x2 * cos_t + x1 * sin_t).astype(BF16)
            out_ref[r0 + MLA_QK:r0 + LANES, :] = jnp.zeros((LANES - MLA_QK, tm), BF16)

    norm_rope_t(_dot_nt(wq_t_ref[...], cq), gq_t_ref, q_t_ref)
    norm_rope_t(_dot_nt(wk_t_ref[...], ckr), gk_t_ref, k_t_ref)
    v_ref[...] = _dot(ckr, wv_ref[...]).astype(BF16)


def _mlaprep(mla_in, pos_t, wq_t, wk_t, wv, gq_t, gk_t, freq_t):
    t = mla_in.shape[0]
    tm = MLAPREP_ROW_TILE
    row = lambda w: pl.BlockSpec((tm, w), lambda i: (i, 0))
    col = lambda h: pl.BlockSpec((h, tm), lambda i: (0, i))
    hw = MLA_HEADS * LANES
    vw = MLA_HEADS * MLA_VDIM
    return pl.pallas_call(
        _mlaprep_kernel,
        grid=(t // tm,),
        in_specs=[row(mla_in.shape[1]), col(1), _const_spec(wq_t.shape), _const_spec(wk_t.shape),
                  _const_spec(wv.shape), _const_spec(gq_t.shape), _const_spec(gk_t.shape),
                  _const_spec(freq_t.shape)],
        out_specs=(col(hw), col(hw), row(vw)),
        out_shape=(jax.ShapeDtypeStruct((hw, t), BF16), jax.ShapeDtypeStruct((hw, t), BF16),
                   jax.ShapeDtypeStruct((t, vw), BF16)),
        compiler_params=_cparams(("parallel",)),
        name="mlaprep",
    )(mla_in, pos_t, wq_t, wk_t, wv, gq_t, gk_t, freq_t)


def _attn_kernel(q_t_ref, k_t_ref, v_ref, o_ref):
    for j in range(ATTN_HEADS_PER_STEP):
        rows = slice(j * LANES, (j + 1) * LANES)
        s = _dot_tn(q_t_ref[rows, :], k_t_ref[rows, :])
        p = jnp.exp(s - jnp.max(s, axis=-1, keepdims=True))
        l = jnp.sum(p, axis=-1, keepdims=True)
        vl = slice(j * MLA_VDIM, (j + 1) * MLA_VDIM)
        o = _dot(p.astype(BF16), v_ref[0, :, vl])
        o_ref[0, :, vl] = (o / l).astype(BF16)


def _attn(q_t, k_t, v):
    b, s, _ = v.shape
    tq = min(ATTN_Q_TILE, s)
    nh = ATTN_HEADS_PER_STEP
    n_q = s // tq
    return pl.pallas_call(
        _attn_kernel,
        grid=(b, MLA_HEADS // nh, n_q),
        in_specs=[pl.BlockSpec((nh * LANES, tq), lambda bi, hp, i: (hp, bi * n_q + i)),
                  pl.BlockSpec((nh * LANES, s), lambda bi, hp, i: (hp, bi)),
                  pl.BlockSpec((1, s, nh * MLA_VDIM), lambda bi, hp, i: (bi, 0, hp))],
        out_specs=pl.BlockSpec((1, tq, nh * MLA_VDIM), lambda bi, hp, i: (bi, i, hp)),
        out_shape=jax.ShapeDtypeStruct((b, s, MLA_HEADS * MLA_VDIM), BF16),
        compiler_params=_cparams(("parallel", "parallel", "arbitrary")),
        name="attn",
    )(q_t, k_t, v)


def _merge_kernel(x_ref, ya_ref, yb_ref, gates_ref, woa_ref, wob_ref, wout_ref, lnm_ref, wr_t_ref,
                  br_t_ref, x1_ref, hm8_ref, route_t_ref, xb_zero_ref):
    d = x_ref.shape[-1]
    xb_zero_ref[...] = jnp.zeros_like(xb_zero_ref)
    y_a = _dot(ya_ref[...], woa_ref[...])
    y_b = _dot(yb_ref[...], wob_ref[...])
    merged = gates_ref[:, 0:d].astype(F32) * y_a + gates_ref[:, d:2 * d].astype(F32) * y_b
    x1 = x_ref[...] + _dot(merged.astype(BF16), wout_ref[...])
    x1_ref[...] = x1
    hm = x1 * _rms_scale(x1, d) * lnm_ref[...]
    _store_token_tiles(hm8_ref, 0, hm)

    tm = hm.shape[0]
    hm_hi = hm.astype(BF16)
    hm_lo = (hm - hm_hi.astype(F32)).astype(BF16)
    hh = _dot_nt(wr_t_ref[...], hm_hi)
    bias = jnp.concatenate([br_t_ref[...]] * (tm // LANES), axis=1)
    logits = hh[0:LANES] + hh[LANES:2 * LANES] + _dot_nt(wr_t_ref[0:LANES, :], hm_lo) + bias
    neg = -jnp.inf
    gl = logits[0:N_GROUPS]
    gmax = jnp.max(gl, axis=0, keepdims=True)
    p_group = 1.0 / jnp.sum(jnp.exp(gl - gmax), axis=0, keepdims=True)
    g_row = lax.broadcasted_iota(jnp.int32, gl.shape, 0)
    g_sel = jnp.min(jnp.where(gl == gmax, g_row, N_GROUPS), axis=0, keepdims=True)
    el_all = logits[N_GROUPS:N_GROUPS + N_EXPERTS]
    e_row = lax.broadcasted_iota(jnp.int32, el_all.shape, 0)
    el = jnp.where(e_row // EXPERTS_PER_GROUP == g_sel, el_all, neg)
    v1 = jnp.max(el, axis=0, keepdims=True)
    i1 = jnp.min(jnp.where(el == v1, e_row, N_EXPERTS), axis=0, keepdims=True)
    el2 = jnp.where(e_row == i1, neg, el)
    v2 = jnp.max(el2, axis=0, keepdims=True)
    i2 = jnp.min(jnp.where(el2 == v2, e_row, N_EXPERTS), axis=0, keepdims=True)
    e21 = jnp.exp(v2 - v1)
    w1 = p_group / (1.0 + e21)
    w2 = w1 * e21
    out_row = lax.broadcasted_iota(jnp.int32, route_t_ref.shape, 0)
    route_t_ref[...] = jnp.where(out_row == 0, i1.astype(F32),
                                 jnp.where(out_row == 1, i2.astype(F32),
                                           jnp.where(out_row == 2, w1,
                                                     jnp.where(out_row == 3, w2, 0.0))))


def _merge(x2, ya, yb, gates, woa, wob, wout, lnm, wr, br):
    t, d = x2.shape
    tm = MERGE_ROW_TILE
    row = lambda w: pl.BlockSpec((tm, w), lambda i: (i, 0))
    xb_rows = _moe_blocks(t) * MOE_ROWS * TOKEN_ROWS
    assert xb_rows % (t // tm) == 0
    return pl.pallas_call(
        _merge_kernel,
        grid=(t // tm,),
        in_specs=[row(d), row(ya.shape[1]), row(yb.shape[1]), row(2 * d), _const_spec(woa.shape),
                  _const_spec(wob.shape), _const_spec(wout.shape), _const_spec(lnm.shape),
                  _const_spec(wr.shape), _const_spec(br.shape)],
        out_specs=(row(d), pl.BlockSpec((tm * TOKEN_ROWS, LANES), lambda i: (i, 0)),
                   pl.BlockSpec((ROUTE_ROWS, tm), lambda i: (0, i)),
                   pl.BlockSpec((xb_rows // (t // tm), LANES), lambda i: (i, 0))),
        out_shape=(jax.ShapeDtypeStruct((t, d), F32),
                   jax.ShapeDtypeStruct((t * TOKEN_ROWS, LANES), jnp.uint32),
                   jax.ShapeDtypeStruct((ROUTE_ROWS, t), F32),
                   jax.ShapeDtypeStruct((xb_rows, LANES), jnp.uint32)),
        compiler_params=_cparams(("parallel",)),
        name="merge",
    )(x2, ya, yb, gates, woa, wob, wout, lnm, wr, br)


_HIGH_HALF = 0xFFFF0000


def _store_token_tiles(ref4, tok0, val):
    n, width = val.shape
    as_bits = lambda a: lax.bitcast_convert_type(a.astype(BF16).astype(F32), jnp.uint32)
    words = (as_bits(val[:, :width // 2]) >> 16) | (as_bits(val[:, width // 2:]) & jnp.uint32(_HIGH_HALF))
    for j in range(TOKEN_ROWS):
        ref4[pl.ds(tok0 * TOKEN_ROWS + j, n, stride=TOKEN_ROWS), :] = words[:, j * LANES:(j + 1) * LANES]


def _load_token_tiles(ref4, tok0, n):
    words = jnp.concatenate(
        [ref4[pl.ds(tok0 * TOKEN_ROWS + j, n, stride=TOKEN_ROWS), :] for j in range(TOKEN_ROWS)],
        axis=1)
    low = lax.bitcast_convert_type(words << 16, F32).astype(BF16)
    high = lax.bitcast_convert_type(words & jnp.uint32(_HIGH_HALF), F32).astype(BF16)
    return jnp.concatenate([low, high], axis=1)


def _token_tile(ref8, tok):
    return ref8.at[pl.ds(pl.multiple_of(tok * TOKEN_ROWS, TOKEN_ROWS), TOKEN_ROWS)]


def _dispatch_kernel(dest0_ref, dest1_ref, hm8_ref, xb_in_ref, xb8_ref, sem):
    del xb_in_ref
    n_tok = DISPATCH_TOKENS
    for r in range(n_tok):
        src = hm8_ref.at[pl.ds(r * TOKEN_ROWS, TOKEN_ROWS)]
        for slot, dest_ref in enumerate((dest0_ref, dest1_ref)):
            pltpu.make_async_copy(src, _token_tile(xb8_ref, dest_ref[r]), sem).start(priority=slot)
    for _ in range(2):
        pltpu.make_async_copy(hm8_ref, xb8_ref.at[pl.ds(0, n_tok * TOKEN_ROWS)], sem).wait()


def _dispatch(dest0, dest1, hm8, xb_init):
    t = hm8.shape[0] // TOKEN_ROWS
    idx = pl.BlockSpec((DISPATCH_TOKENS,), lambda i: (i,), memory_space=pltpu.SMEM)
    return pl.pallas_call(
        _dispatch_kernel,
        grid=(t // DISPATCH_TOKENS,),
        in_specs=[idx, idx,
                  pl.BlockSpec((DISPATCH_TOKENS * TOKEN_ROWS, LANES), lambda i: (i, 0)),
                  pl.BlockSpec(memory_space=pl.ANY)],
        out_specs=pl.BlockSpec(memory_space=pl.ANY),
        out_shape=jax.ShapeDtypeStruct(xb_init.shape, xb_init.dtype),
        scratch_shapes=[pltpu.SemaphoreType.DMA(())],
        input_output_aliases={3: 0},
        compiler_params=_cparams(("arbitrary",)),
        name="dispatch",
    )(dest0, dest1, hm8, xb_init)


def _experts_kernel(be_ref, nu_ref, nxt_ref, par_ref, xb8_ref, w1_hbm, w3_hbm, w2_hbm, yb8_ref,
                    w1f, w3f, w2f, sems, w1c, w3c, w2c):
    j = pl.program_id(0)

    def fetch(e, slot):
        return [pltpu.make_async_copy(src.at[e], dst.at[slot], sems.at[slot])
                for src, dst in ((w1_hbm, w1f), (w3_hbm, w3f), (w2_hbm, w2f))]

    for half in range(2):
        blk = 2 * j + half
        used = blk < nu_ref[0]
        new_expert = jnp.logical_or(blk == 0, be_ref[blk] != be_ref[jnp.maximum(blk - 1, 0)])
        tok0 = half * MOE_ROWS

        @pl.when(jnp.logical_and(used, new_expert))
        def _():
            slot = par_ref[blk]

            @pl.when(blk == 0)
            def _():
                for cp in fetch(be_ref[0], 0):
                    cp.start()

            for cp in fetch(be_ref[blk], slot):
                cp.wait()

            @pl.when(nxt_ref[blk] >= 0)
            def _():
                for cp in fetch(nxt_ref[blk], 1 - slot):
                    cp.start()

            w1c[...] = w1f[slot].astype(BF16)
            w3c[...] = w3f[slot].astype(BF16)
            w2c[...] = w2f[slot].astype(BF16)

        @pl.when(used)
        def _():
            x = _load_token_tiles(xb8_ref, tok0, MOE_ROWS)
            hmid = (_silu(_dot(x, w1c[...])) * _dot(x, w3c[...])).astype(BF16)
            _store_token_tiles(yb8_ref, tok0, _dot(hmid, w2c[...]))

        @pl.when(jnp.logical_not(used))
        def _():
            rows = pl.ds(tok0 * TOKEN_ROWS, MOE_ROWS * TOKEN_ROWS)
            yb8_ref[rows, :] = jnp.zeros((MOE_ROWS * TOKEN_ROWS, LANES), jnp.uint32)


def _experts(block_expert, n_used, next_expert, slot_parity, xb8, w1, w3, w2):
    nblk = block_expert.shape[0]
    nsteps = nblk // 2
    d = w1.shape[1]
    de = w1.shape[2]
    step_rows = 2 * MOE_ROWS * TOKEN_ROWS
    rows_in = lambda j, be, nu, nx, pa: (jnp.minimum(j, (nu[0] - 1) // 2), 0)
    hbm = pl.BlockSpec(memory_space=pl.ANY)
    return pl.pallas_call(
        _experts_kernel,
        grid_spec=pltpu.PrefetchScalarGridSpec(
            num_scalar_prefetch=4,
            grid=(nsteps,),
            in_specs=[pl.BlockSpec((step_rows, LANES), rows_in), hbm, hbm, hbm],
            out_specs=pl.BlockSpec((step_rows, LANES), lambda j, be, nu, nx, pa: (j, 0)),
            scratch_shapes=[pltpu.VMEM((2, d, de), F32), pltpu.VMEM((2, d, de), F32),
                            pltpu.VMEM((2, de, d), F32), pltpu.SemaphoreType.DMA((2,)),
                            pltpu.VMEM((d, de), BF16), pltpu.VMEM((d, de), BF16),
                            pltpu.VMEM((de, d), BF16)]),
        out_shape=jax.ShapeDtypeStruct((nblk * MOE_ROWS * TOKEN_ROWS, LANES), jnp.uint32),
        compiler_params=_cparams(("arbitrary",)),
        name="experts",
    )(block_expert, n_used, next_expert, slot_parity, xb8, w1, w3, w2)


def _ple_kernel(dest0_ref, dest1_ref, dest0n_ref, dest1n_ref, x1_ref, route_ref, p_ref, ln_ref,
                wg_ref, wp_ref, yb8_ref, out_ref, ya0, ya1, yb0, yb1, sems):
    j = pl.program_id(0)
    last = pl.num_programs(0) - 1
    tm = PLE_ROW_TILE
    d = x1_ref.shape[1]

    def start(idx_refs, base, bufs, sem):
        for r in range(tm):
            for slot in range(2):
                pltpu.make_async_copy(_token_tile(yb8_ref, idx_refs[slot][base + r]),
                                      bufs[slot].at[pl.ds(r * TOKEN_ROWS, TOKEN_ROWS)],
                                      sem).start(priority=slot)

    def wait(bufs, sem):
        for buf in bufs:
            pltpu.make_async_copy(yb8_ref.at[pl.ds(0, tm * TOKEN_ROWS)], buf, sem).wait()

    def compute(bufs, rows):
        x2 = (x1_ref[rows, :]
              + route_ref[rows, 0:1] * _load_token_tiles(bufs[0], 0, tm).astype(F32)
              + route_ref[rows, 1:2] * _load_token_tiles(bufs[1], 0, tm).astype(F32))
        h = (x2 * _rms_scale(x2, d) * ln_ref[...]).astype(BF16)
        pp = _dot(p_ref[rows, :].astype(BF16), wp_ref[...])
        out_ref[rows, :] = x2 + pp * _sigmoid(_dot(h, wg_ref[...]))

    dest_now = (dest0_ref, dest1_ref)

    @pl.when(j == 0)
    def _():
        start(dest_now, 0, (ya0, ya1), sems.at[0])

    wait((ya0, ya1), sems.at[0])
    start(dest_now, tm, (yb0, yb1), sems.at[1])
    compute((ya0, ya1), slice(0, tm))

    wait((yb0, yb1), sems.at[1])
    start((dest0n_ref, dest1n_ref), 0, (ya0, ya1), sems.at[0])
    compute((yb0, yb1), slice(tm, 2 * tm))

    @pl.when(j == last)
    def _():
        wait((ya0, ya1), sems.at[0])


def _ple(dest0, dest1, x1, route, p2, ln, wg, wp, yb):
    t, d = x1.shape
    tm = PLE_ROW_TILE
    nsteps = t // (2 * tm)
    row = lambda w: pl.BlockSpec((2 * tm, w), lambda j: (j, 0))
    idx = pl.BlockSpec((2 * tm,), lambda j: (j,), memory_space=pltpu.SMEM)
    idx_next = pl.BlockSpec((2 * tm,), lambda j: (jnp.minimum(j + 1, nsteps - 1),),
                            memory_space=pltpu.SMEM)
    return pl.pallas_call(
        _ple_kernel,
        grid=(nsteps,),
        in_specs=[idx, idx, idx_next, idx_next,
                  row(d), row(route.shape[1]), row(p2.shape[1]), _const_spec(ln.shape),
                  _const_spec(wg.shape), _const_spec(wp.shape),
                  pl.BlockSpec(memory_space=pl.ANY)],
        out_specs=row(d),
        out_shape=jax.ShapeDtypeStruct((t, d), F32),
        scratch_shapes=([pltpu.VMEM((tm * TOKEN_ROWS, LANES), jnp.uint32)] * 4
                        + [pltpu.SemaphoreType.DMA((2,))]),
        compiler_params=_cparams(("arbitrary",)),
        name="ple",
    )(dest0, dest1, dest0, dest1, x1, route, p2, ln, wg, wp, yb)


def _head_pad(w, width):
    r = w.shape[0]
    w = w.reshape(r, MLA_HEADS, width)
    return jnp.pad(w, ((0, 0), (0, 0), (0, LANES - width))).reshape(r, MLA_HEADS * LANES)


def _rope_table():
    half = MLA_ROPE // 2
    inv_freq = ROPE_THETA ** (-np.arange(half, dtype=np.float32) / half)
    return jnp.asarray(np.broadcast_to(inv_freq[:, None], (half, LANES)))


def _gain_t(gain, scale):
    return jnp.broadcast_to((jnp.pad(gain, (0, LANES - MLA_QK)) * scale)[:, None], (LANES, LANES))


def _moe_blocks(t):
    nblk = (2 * t + N_EXPERTS * (MOE_ROWS - 1)) // MOE_ROWS
    return nblk + nblk % 2


def _route_tables(ids_t):
    t = ids_t.shape[1]
    a = 2 * t
    e = ids_t.reshape(a)
    onehot = (e[:, None] == jnp.arange(N_EXPERTS, dtype=jnp.int32)[None, :]).astype(jnp.int32)
    csum = jnp.cumsum(onehot, axis=0)
    counts = csum[-1]
    rank = jnp.sum(csum * onehot, axis=1) - 1
    pcounts = (counts + MOE_ROWS - 1) // MOE_ROWS * MOE_ROWS
    pends = jnp.cumsum(pcounts)
    pstarts = pends - pcounts
    dest = (jnp.sum(onehot * pstarts[None, :], axis=1) + rank).astype(jnp.int32)
    nblk = _moe_blocks(t)
    n_used = (pends[-1] // MOE_ROWS).astype(jnp.int32)
    blk_start = jnp.arange(nblk, dtype=jnp.int32) * MOE_ROWS
    blk_start = jnp.minimum(blk_start, pends[-1] - 1)
    block_expert = jnp.sum((pends[None, :] <= blk_start[:, None]).astype(jnp.int32), axis=1)
    block_expert = jnp.minimum(block_expert, N_EXPERTS - 1).astype(jnp.int32)
    block_expert, n_used = lax.optimization_barrier((block_expert, n_used))
    blk = jnp.arange(nblk, dtype=jnp.int32)
    prev = jnp.concatenate([jnp.full((1,), -1, jnp.int32), block_expert[:-1]])
    starts_expert = (blk < n_used) & (block_expert != prev)
    slot_parity = (jnp.cumsum(starts_expert.astype(jnp.int32)) - 1) % 2
    start_at_or_after = jnp.flip(lax.cummin(jnp.flip(jnp.where(starts_expert, blk, nblk))))
    next_start = jnp.concatenate([start_at_or_after[1:], jnp.full((1,), nblk, jnp.int32)])
    next_expert = jnp.where(next_start < nblk, block_expert[jnp.minimum(next_start, nblk - 1)], -1)
    return (dest, block_expert, n_used.reshape(1), next_expert.astype(jnp.int32),
            slot_parity.astype(jnp.int32), nblk)


def kernel(x, p, positions, ln_mix, w_in, hg_lb, hg_onorm, w_oA, mla_qa_norm, mla_kva_norm, w_uq,
           w_ukv, q_norm, k_norm, w_oB, w_out, ln_moe, w_rg, b_rg, w_re, b_re, w1, w3, w2, ln_ple,
           w_ple_gate, w_ple_proj):
    b, s, d = x.shape
    t = b * s
    depth = w_in.shape[0]
    lb_all = jnp.cumsum(jax.nn.softmax(hg_lb.astype(F32), axis=1), axis=1)
    pos_t = positions.astype(F32).reshape(1, t)
    freq_t = _rope_table()
    xc = x.reshape(t, d)

    for layer in range(depth):
        wi = w_in[layer]
        n_hg = 3 * HG_F + 2 * HG_V
        n_mla = MLA_Q_RANK + MLA_KV_RANK + MLA_ROPE
        mla_w = -(-n_mla // LANES) * LANES
        whg = wi[:, :n_hg].astype(BF16)
        wmla = jnp.pad(wi[:, n_hg:n_hg + n_mla], ((0, 0), (0, mla_w - n_mla))).astype(BF16)
        wgate = wi[:, n_hg + n_mla:].astype(BF16)
        lb = lb_all[:, layer, :]

        qs, lf, kk, v, og, mla_in, gates = _inproj(
            xc, ln_mix[layer][None, :], whg, wmla, wgate, lb, mla_qa_norm[layer][None, :],
            mla_kva_norm[layer][None, :])

        r3 = lambda a: a.reshape(b, s, a.shape[-1])
        ya = _hgrn(r3(qs), r3(lf), r3(kk), r3(v), r3(og), hg_onorm[layer][None, :]).reshape(t, HG_V)

        kv_w = MLA_NOPE + MLA_VDIM
        wkv = w_ukv[layer].reshape(MLA_KV_RANK, MLA_HEADS, kv_w)
        ckr_w = mla_w - MLA_Q_RANK
        wq = _head_pad(w_uq[layer], MLA_QK).astype(BF16)
        wk_nope = _head_pad(wkv[:, :, :MLA_NOPE].reshape(MLA_KV_RANK, MLA_HEADS * MLA_NOPE), MLA_NOPE)
        place = np.zeros((ckr_w - MLA_KV_RANK, MLA_HEADS, LANES), np.float32)
        for j in range(MLA_ROPE):
            place[j, :, MLA_NOPE + j] = 1.0
        wk = jnp.concatenate([wk_nope, jnp.asarray(place.reshape(ckr_w - MLA_KV_RANK, -1))],
                             axis=0).astype(BF16)
        wv = jnp.pad(wkv[:, :, MLA_NOPE:].reshape(MLA_KV_RANK, MLA_HEADS * MLA_VDIM),
                     ((0, ckr_w - MLA_KV_RANK), (0, 0))).astype(BF16)
        qh_t, kh_t, vh = _mlaprep(mla_in, pos_t, wq.T, wk.T, wv, _gain_t(q_norm[layer], MLA_QK ** -0.5),
                                  _gain_t(k_norm[layer], 1.0), freq_t)
        yb = _attn(qh_t, kh_t, r3(vh)).reshape(t, MLA_HEADS * MLA_VDIM)

        wr = jnp.pad(jnp.concatenate([w_rg[layer], w_re[layer]], axis=1),
                     ((0, 0), (0, LANES - N_GROUPS - N_EXPERTS)))
        wr_hi = wr.astype(BF16)
        wr_t = jnp.concatenate([wr_hi, (wr - wr_hi.astype(F32)).astype(BF16)], axis=1).T
        br_t = jnp.broadcast_to(jnp.pad(jnp.concatenate([b_rg[layer], b_re[layer]]),
                                        (0, LANES - N_GROUPS - N_EXPERTS))[:, None], (LANES, LANES))
        x1, hm8, route_t, xb_zero = _merge(xc, ya, yb, gates, w_oA[layer].astype(BF16),
                                  w_oB[layer].astype(BF16), w_out[layer].astype(BF16),
                                  ln_moe[layer][None, :], wr_t, br_t)

        route = route_t[2:4].T
        dest, block_expert, n_used, next_expert, slot_parity, nblk = _route_tables(
            route_t[0:2].astype(jnp.int32))
        dest0, dest1 = dest[:t], dest[t:]
        xb8 = _dispatch(dest0, dest1, hm8, xb_zero)
        yexp = _experts(block_expert, n_used, next_expert, slot_parity, xb8,
                        w1[layer], w3[layer], w2[layer])

        xc = _ple(dest0, dest1, x1, route, p[layer].reshape(t, -1), ln_ple[layer][None, :],
                  w_ple_gate[layer].astype(BF16), w_ple_proj[layer].astype(BF16), yexp)
    return xc.reshape(b, s, d)
```
